```python
import math
import jax
import jax.numpy as jnp
from jax import lax
import numpy as np

D_MODEL = 1024
BATCH = 32
SEQ = 256
DEPTH = 2
DEC_BATCH = 8
DEC_SEQ = 1024
PAST_LEN = 512

GRID_W = 64
N_EVEN = (DEPTH + 1) // 2
N_ODD = DEPTH // 2
EPS = 1e-6
H_A = 4
DK_A = 128
DV_A = 128
W_A = H_A * DV_A
CHUNK = 64
C_B = 512
SHORT_W = 3
FILTER_BANDS = 16
FILTER_EMB = 1 + 2 * FILTER_BANDS
FILTER_HIDDEN = 64
DECAY_MIN = math.log(1e-2) / 1.5
DECAY_MAX = math.log(1e-2) / 0.3
H_C = 4
KV_C = 2
G_C = H_C // KV_C
HD_C = 128
H_D = 4
DK_D = 64
DV_D = 128
ROPE_THETA = 10000.0
Q_BLOCK = 128
N_GROUPS = 4
EXP_PER_GROUP = 4
N_EXPERTS = N_GROUPS * EXP_PER_GROUP
TOP_K = 2
D_EXPERT = 512
EVEN_SPLITS = (H_A * DK_A, 2 * H_A * DK_A, 3 * H_A * DK_A, 3 * H_A * DK_A + W_A, 3 * H_A * DK_A + 2 * W_A)
EVEN_IN = 3 * H_A * DK_A + 2 * W_A + 3 * C_B
ODD_SPLITS = (H_C * HD_C, H_C * HD_C + KV_C * HD_C, H_C * HD_C + 2 * KV_C * HD_C, H_C * HD_C + 2 * KV_C * HD_C + 2 * H_D * DK_D, H_C * HD_C + 2 * KV_C * HD_C + 4 * H_D * DK_D)
ODD_IN = H_C * HD_C + 2 * KV_C * HD_C + 4 * H_D * DK_D + H_D * DV_D
MIX_AB = W_A + C_B
MIX_CD = H_C * HD_C + H_D * DV_D

kernel_name = 'hybrid_diffusion_prefix_step'


def rms_norm(x, g):
    xf = x.astype(jnp.float32)
    y = xf * lax.rsqrt(jnp.mean(xf * xf, axis=-1, keepdims=True) + EPS)
    return (y * g.astype(jnp.float32)).astype(x.dtype)


def adaln(cvec, w, b):
    m = jax.nn.silu(cvec) @ w + b
    m = m.reshape(cvec.shape[:-1] + (1, 6, D_MODEL))
    return [m[..., i, :] for i in range(6)]


def modulate(x, shift, scale):
    return x * (1.0 + scale) + shift


def axial_rope(L, dim):
    ROWS = L // GRID_W
    row_idx = jnp.repeat(jnp.arange(ROWS), GRID_W).astype(jnp.float32)
    col_idx = jnp.tile(jnp.arange(GRID_W), ROWS).astype(jnp.float32)
    half = dim // 2
    inv = ROPE_THETA ** (-jnp.arange(0, half, 2, dtype=jnp.float32) / half)
    ang = jnp.concatenate([row_idx[:, None] * inv, col_idx[:, None] * inv], axis=-1)
    return jnp.cos(ang), jnp.sin(ang)


def apply_rope(x, cos, sin):
    shape = (x.shape[1],) + (1,) * (x.ndim - 3) + (x.shape[-1] // 2,)
    cos = cos.reshape(shape)
    sin = sin.reshape(shape)
    xf = x.astype(jnp.float32).reshape(x.shape[:-1] + (x.shape[-1] // 2, 2))
    x1, x2 = xf[..., 0], xf[..., 1]
    out = jnp.stack([x1 * cos - x2 * sin, x1 * sin + x2 * cos], axis=-1).reshape(x.shape)
    return out.astype(x.dtype)


def hgrn2_chunk_scan(q, k, v, logf, s0):
    B, L, H, DK = q.shape
    DV = v.shape[-1]
    n = L // CHUNK

    def to_chunks(a):
        return a.reshape(B, n, CHUNK, H, a.shape[-1]).transpose(1, 0, 3, 2, 4)

    mask = jnp.tril(jnp.ones((CHUNK, CHUNK), dtype=bool))[:, :, None]

    def step(S, inp):
        qc, kc, vc, lf = inp
        b = jnp.cumsum(lf, axis=2)
        diff = b[:, :, :, None, :] - b[:, :, None, :, :]
        decay = jnp.exp(jnp.where(mask, diff, -jnp.inf))
        scores = jnp.einsum('bhtk,bhsk,bhtsk->bhts', qc, kc, decay)
        o = jnp.einsum('bhts,bhsv->bhtv', scores, vc) + jnp.einsum('bhtk,bhkv->bhtv', qc * jnp.exp(b), S)
        b_last = b[:, :, -1:, :]
        S = jnp.exp(b_last[:, :, 0, :])[..., None] * S + jnp.einsum('bhsk,bhsv->bhkv', kc * jnp.exp(b_last - b), vc)
        return S, o

    s_fin, o = lax.scan(step, s0, (to_chunks(q), to_chunks(k), to_chunks(v), to_chunks(logf)))
    o = o.transpose(1, 0, 3, 2, 4).reshape(B, L, H, DV)
    return o, s_fin


def short_conv(u, w, b):
    ch = u.shape[-1]
    y = lax.conv_general_dilated(u, w[:, None, :].astype(u.dtype), window_strides=(1,), padding=[(SHORT_W // 2, SHORT_W // 2)], dimension_numbers=('NWC', 'WIO', 'NWC'), feature_group_count=ch)
    return y + b


def hyena_filters(L, w1, b1, fr1, w2, b2, fr2, w3):
    t = jnp.linspace(0.0, 1.0, L, dtype=jnp.float32)[:, None]
    w = 2.0 * math.pi * jnp.arange(L, dtype=jnp.float32) / L
    f = jnp.linspace(1e-4, FILTER_BANDS - 1, FILTER_BANDS, dtype=jnp.float32)
    ang = w[:, None] * f[None, :]
    z = jnp.concatenate([t, jnp.cos(ang), -jnp.sin(ang)], axis=-1)
    h = jnp.sin(fr1 * (z @ w1 + b1))
    h = jnp.sin(fr2 * (h @ w2 + b2))
    h = (h @ w3).astype(jnp.float32)
    deltas = jnp.abs(jnp.linspace(DECAY_MIN, DECAY_MAX, C_B, dtype=jnp.float32))
    window = jnp.exp(-t * deltas[None, :])
    return h.reshape(L, 2, C_B) * window[:, None, :]


def fft_conv(u, h):
    L = u.shape[1]
    n = 2 * L
    y = jnp.fft.irfft(jnp.fft.rfft(u, n=n, axis=1) * jnp.fft.rfft(h, n=n, axis=0)[None], n=n, axis=1)
    return y[:, :L]


def block_attention(q, k, v):
    B, Hk, G, Lq, d = q.shape
    nb = Lq // Q_BLOCK
    qb = q.reshape(B, Hk, G, nb, Q_BLOCK, d).transpose(3, 0, 1, 2, 4, 5)
    scale = d ** -0.5

    def one(qi):
        s = jnp.einsum('bhgqd,bhkd->bhgqk', qi, k).astype(jnp.float32) * scale
        p = jax.nn.softmax(s, axis=-1)
        return jnp.einsum('bhgqk,bhkv->bhgqv', p.astype(v.dtype), v)

    o = lax.map(one, qb)
    return o.transpose(1, 2, 3, 0, 4, 5).reshape(B, Hk, G, Lq, v.shape[-1])


def even_mixer(h, s0, w_in, w_out, lb, norm_g, conv_w, conv_b, fw1, fb1, ffr1, fw2, fb2, ffr2, fw3, hbias):
    f32 = jnp.float32
    B, L, _ = h.shape
    p = h @ w_in
    qa, ffa, fba, ia, ga, hy = jnp.split(p, EVEN_SPLITS, axis=-1)
    q = jax.nn.silu(qa.astype(f32)).reshape(B, L, H_A, DK_A)
    v = ia.astype(f32).reshape(B, L, H_A, DV_A)

    def forget(z):
        f = lb + (1.0 - lb) * jax.nn.sigmoid(z.astype(f32).reshape(B, L, H_A, DK_A))
        return jnp.log(f), 1.0 - f

    lf_f, k_f = forget(ffa)
    lf_b, k_b = forget(fba)
    o_f, s_f = hgrn2_chunk_scan(q, k_f, v, lf_f, s0[:, 0])
    o_r, s_b = hgrn2_chunk_scan(jnp.flip(q, 1), jnp.flip(k_b, 1), jnp.flip(v, 1), jnp.flip(lf_b, 1), s0[:, 1])
    o_a = rms_norm(o_f + jnp.flip(o_r, 1), norm_g).reshape(B, L, W_A) * jax.nn.silu(ga.astype(f32))
    u = short_conv(hy, conv_w, conv_b).astype(f32)
    x0, x1, vh = jnp.split(u, 3, axis=-1)
    filt = hyena_filters(L, fw1, fb1, ffr1, fw2, fb2, ffr2, fw3)
    z = vh * x1
    z = fft_conv(z, filt[:, 0]) + jnp.flip(fft_conv(jnp.flip(z, 1), filt[:, 1]), 1) + hbias * z
    o_b = z * x0
    out = jnp.concatenate([o_a, o_b], axis=-1).astype(h.dtype) @ w_out
    return out, jnp.stack([s_f, s_b], axis=1)


def odd_mixer(h, ctx_kv, use_rope, w_in, w_out, cqg, ckg, dqg, dkg, lam, lam_init, subln_g):
    B, L, _ = h.shape
    p = h @ w_in
    qc, kc, vc, qd, kd, vd = jnp.split(p, ODD_SPLITS, axis=-1)
    qc = rms_norm(qc.reshape(B, L, H_C, HD_C), cqg)
    kc = rms_norm(kc.reshape(B, L, KV_C, HD_C), ckg)
    qd = rms_norm(qd.reshape(B, L, H_D, 2, DK_D), dqg)
    kd = rms_norm(kd.reshape(B, L, H_D, 2, DK_D), dkg)
    if use_rope:
        cos_c, sin_c = axial_rope(L, HD_C)
        qc = apply_rope(qc, cos_c, sin_c)
        kc = apply_rope(kc, cos_c, sin_c)
        cos_d, sin_d = axial_rope(L, DK_D)
        qd = apply_rope(qd, cos_d, sin_d)
        kd = apply_rope(kd, cos_d, sin_d)
    qc = qc.reshape(B, L, KV_C, G_C, HD_C).transpose(0, 2, 3, 1, 4)
    kc = kc.transpose(0, 2, 1, 3)
    vc = vc.reshape(B, L, KV_C, HD_C).transpose(0, 2, 1, 3)
    qd = qd.transpose(0, 2, 3, 1, 4)
    kd = kd.reshape(B, L, H_D, 2 * DK_D).transpose(0, 2, 1, 3)
    vd = vd.reshape(B, L, H_D, DV_D).transpose(0, 2, 1, 3)
    if ctx_kv is None:
        kc_all, vc_all, kd_all, vd_all = kc, vc, kd, vd
    else:
        kc_all = jnp.concatenate([ctx_kv[0].astype(kc.dtype), kc], axis=2)
        vc_all = jnp.concatenate([ctx_kv[1].astype(vc.dtype), vc], axis=2)
        kd_all = jnp.concatenate([ctx_kv[2].astype(kd.dtype), kd], axis=2)
        vd_all = jnp.concatenate([ctx_kv[3].astype(vd.dtype), vd], axis=2)
    oc = block_attention(qc, kc_all, vc_all).transpose(0, 3, 1, 2, 4).reshape(B, L, H_C * HD_C)
    o1 = block_attention(qd[:, :, 0:1], kd_all[..., :DK_D], vd_all)[:, :, 0]
    o2 = block_attention(qd[:, :, 1:2], kd_all[..., DK_D:], vd_all)[:, :, 0]
    od = rms_norm(o1 - lam * o2, subln_g) * (1.0 - lam_init)
    od = od.transpose(0, 2, 1, 3).reshape(B, L, H_D * DV_D)
    out = jnp.concatenate([oc, od.astype(oc.dtype)], axis=-1) @ w_out
    return out, kc, vc, kd, vd


def hier_moe(h, w_grp, b_grp, w_rt, b_rt, w_gate, w_up, w_down):
    B, L, D = h.shape
    t = h.reshape(B * L, D)
    g_prob = jax.nn.softmax((t @ w_grp + b_grp).astype(jnp.float32), axis=-1)
    g_p, g_i = lax.top_k(g_prob, 1)
    e_logits = (t @ w_rt + b_rt).astype(jnp.float32).reshape(-1, N_GROUPS, EXP_PER_GROUP)
    e_logits = jnp.take_along_axis(e_logits, g_i[:, :, None], axis=1)[:, 0]
    e_p, e_i = lax.top_k(jax.nn.softmax(e_logits, axis=-1), TOP_K)
    wts = g_p * e_p / jnp.sum(e_p, axis=-1, keepdims=True)
    idx = g_i * EXP_PER_GROUP + e_i
    gates = jnp.sum(jax.nn.one_hot(idx, N_EXPERTS, dtype=jnp.float32) * wts[..., None], axis=1)
    a = jnp.einsum('td,edf->tef', t, w_gate)
    u = jnp.einsum('td,edf->tef', t, w_up)
    hid = jax.nn.silu(a) * u * gates[..., None].astype(t.dtype)
    y = jnp.einsum('tef,efd->td', hid, w_down)
    return y.reshape(B, L, D)


def setup_inputs(seed: int = 0) -> dict:
    key = jax.random.key(seed)
    ks = jax.random.split(key, 45)
    f32 = jnp.float32

    def nrm(i, shape, scale):
        return scale * jax.random.normal(ks[i], shape, f32)

    def gain(i, shape):
        return 1.0 + 0.1 * jax.random.normal(ks[i], shape, f32)

    d_in = D_MODEL ** -0.5
    return {
        'x_prompt': nrm(0, (BATCH, SEQ, D_MODEL), 1.0),
        'x_sample': nrm(1, (DEC_BATCH, DEC_SEQ, D_MODEL), 1.0),
        'state_hgrn': nrm(2, (DEC_BATCH, N_EVEN, 2, H_A, DK_A, DV_A), 1.0),
        'cache_c_k': nrm(3, (DEC_BATCH, N_ODD, KV_C, PAST_LEN, HD_C), 1.0),
        'cache_c_v': nrm(4, (DEC_BATCH, N_ODD, KV_C, PAST_LEN, HD_C), 1.0),
        'cache_d_k': nrm(5, (DEC_BATCH, N_ODD, H_D, PAST_LEN, 2 * DK_D), 1.0),
        'cache_d_v': nrm(6, (DEC_BATCH, N_ODD, H_D, PAST_LEN, DV_D), 1.0),
        'c': nrm(7, (DEC_BATCH, D_MODEL), 1.0),
        'c_ctx': nrm(8, (D_MODEL,), 1.0),
        'norm1_g': gain(9, (DEPTH, D_MODEL)),
        'norm2_g': gain(10, (DEPTH, D_MODEL)),
        'w_mod': nrm(11, (DEPTH, D_MODEL, 6 * D_MODEL), 0.5 * d_in),
        'b_mod': nrm(12, (DEPTH, 6 * D_MODEL), 0.02),
        'even_w_in': nrm(13, (N_EVEN, D_MODEL, EVEN_IN), d_in),
        'even_w_out': nrm(14, (N_EVEN, MIX_AB, D_MODEL), MIX_AB ** -0.5),
        'hgrn_lower': nrm(15, (N_EVEN + 1, H_A * DK_A), 0.5),
        'hgrn_norm_g': gain(16, (N_EVEN, DV_A)),
        'hy_conv_w': nrm(17, (N_EVEN, SHORT_W, 3 * C_B), SHORT_W ** -0.5),
        'hy_conv_b': nrm(18, (N_EVEN, 3 * C_B), 0.02),
        'hy_w1': nrm(19, (N_EVEN, FILTER_EMB, FILTER_HIDDEN), FILTER_EMB ** -0.5),
        'hy_b1': nrm(20, (N_EVEN, FILTER_HIDDEN), 0.02),
        'hy_freq1': gain(21, (N_EVEN, FILTER_HIDDEN)),
        'hy_w2': nrm(22, (N_EVEN, FILTER_HIDDEN, FILTER_HIDDEN), FILTER_HIDDEN ** -0.5),
        'hy_b2': nrm(23, (N_EVEN, FILTER_HIDDEN), 0.02),
        'hy_freq2': gain(24, (N_EVEN, FILTER_HIDDEN)),
        'hy_w3': nrm(25, (N_EVEN, FILTER_HIDDEN, 2 * C_B), 0.1 * FILTER_HIDDEN ** -0.5),
        'hy_bias': nrm(26, (N_EVEN, C_B), 0.5),
        'odd_w_in': nrm(27, (N_ODD, D_MODEL, ODD_IN), d_in),
        'odd_w_out': nrm(28, (N_ODD, MIX_CD, D_MODEL), MIX_CD ** -0.5),
        'c_qnorm_g': gain(29, (N_ODD, HD_C)),
        'c_knorm_g': gain(30, (N_ODD, HD_C)),
        'd_qnorm_g': gain(31, (N_ODD, DK_D)),
        'd_knorm_g': gain(32, (N_ODD, DK_D)),
        'd_lambda_q1': nrm(33, (N_ODD, DK_D), 0.1),
        'd_lambda_k1': nrm(34, (N_ODD, DK_D), 0.1),
        'd_lambda_q2': nrm(35, (N_ODD, DK_D), 0.1),
        'd_lambda_k2': nrm(36, (N_ODD, DK_D), 0.1),
        'd_subln_g': gain(37, (N_ODD, DV_D)),
        'moe_w_grp': nrm(38, (DEPTH, D_MODEL, N_GROUPS), d_in),
        'moe_b_grp': nrm(39, (DEPTH, N_GROUPS), 0.01),
        'moe_w_rt': nrm(40, (DEPTH, D_MODEL, N_EXPERTS), d_in),
        'moe_b_rt': nrm(41, (DEPTH, N_EXPERTS), 0.01),
        'moe_w_gate': nrm(42, (DEPTH, N_EXPERTS, D_MODEL, D_EXPERT), d_in),
        'moe_w_up': nrm(43, (DEPTH, N_EXPERTS, D_MODEL, D_EXPERT), d_in),
        'moe_w_down': nrm(44, (DEPTH, N_EXPERTS, D_EXPERT, D_MODEL), D_EXPERT ** -0.5),
    }


def reference(x_prompt, x_sample, state_hgrn, cache_c_k, cache_c_v, cache_d_k, cache_d_v, c, c_ctx, norm1_g, norm2_g, w_mod, b_mod, even_w_in, even_w_out, hgrn_lower, hgrn_norm_g, hy_conv_w, hy_conv_b, hy_w1, hy_b1, hy_freq1, hy_w2, hy_b2, hy_freq2, hy_w3, hy_bias, odd_w_in, odd_w_out, c_qnorm_g, c_knorm_g, d_qnorm_g, d_knorm_g, d_lambda_q1, d_lambda_k1, d_lambda_q2, d_lambda_k2, d_subln_g, moe_w_grp, moe_b_grp, moe_w_rt, moe_b_rt, moe_w_gate, moe_w_up, moe_w_down):
    f32 = jnp.float32
    lower = jnp.cumsum(jax.nn.softmax(hgrn_lower.astype(f32), axis=0), axis=0)
    ctx = x_prompt
    lat = x_sample
    hgrn_states, ck_list, cv_list, dk_list, dv_list = [], [], [], [], []
    for l in range(DEPTH):
        j = l // 2
        m_ctx = adaln(c_ctx, w_mod[l], b_mod[l])
        m_lat = adaln(c, w_mod[l], b_mod[l])
        h_ctx = modulate(rms_norm(ctx, norm1_g[l]), m_ctx[0], m_ctx[1])
        h_lat = modulate(rms_norm(lat, norm1_g[l]), m_lat[0], m_lat[1])
        if l % 2 == 0:
            p = (even_w_in[j], even_w_out[j], lower[j].reshape(H_A, DK_A), hgrn_norm_g[j], hy_conv_w[j], hy_conv_b[j], hy_w1[j], hy_b1[j], hy_freq1[j], hy_w2[j], hy_b2[j], hy_freq2[j], hy_w3[j], hy_bias[j])
            s_zero = jnp.zeros((ctx.shape[0], 2, H_A, DK_A, DV_A), f32)
            o_ctx, s_ctx = even_mixer(h_ctx, s_zero, *p)
            o_lat, _ = even_mixer(h_lat, state_hgrn[:, j].astype(f32), *p)
            hgrn_states.append(s_ctx)
        else:
            lam_init = 0.8 - 0.6 * math.exp(-0.3 * l)
            lam = (jnp.exp(jnp.sum(d_lambda_q1[j].astype(f32) * d_lambda_k1[j].astype(f32)))
                   - jnp.exp(jnp.sum(d_lambda_q2[j].astype(f32) * d_lambda_k2[j].astype(f32))) + lam_init)
            p = (odd_w_in[j], odd_w_out[j], c_qnorm_g[j], c_knorm_g[j], d_qnorm_g[j], d_knorm_g[j], lam, lam_init, d_subln_g[j])
            o_ctx, kc, vc, kd, vd = odd_mixer(h_ctx, None, False, *p)
            o_lat, _, _, _, _ = odd_mixer(h_lat, (cache_c_k[:, j], cache_c_v[:, j], cache_d_k[:, j], cache_d_v[:, j]), True, *p)
            ck_list.append(kc)
            cv_list.append(vc)
            dk_list.append(kd)
            dv_list.append(vd)
        ctx = ctx + m_ctx[2] * o_ctx
        lat = lat + m_lat[2] * o_lat
        mp = (moe_w_grp[l], moe_b_grp[l], moe_w_rt[l], moe_b_rt[l], moe_w_gate[l], moe_w_up[l], moe_w_down[l])
        ctx = ctx + m_ctx[5] * hier_moe(modulate(rms_norm(ctx, norm2_g[l]), m_ctx[3], m_ctx[4]), *mp)
        lat = lat + m_lat[5] * hier_moe(modulate(rms_norm(lat, norm2_g[l]), m_lat[3], m_lat[4]), *mp)
    new_state_hgrn = jnp.stack(hgrn_states, axis=1)
    new_cache_c_k = jnp.stack(ck_list, axis=1)
    new_cache_c_v = jnp.stack(cv_list, axis=1)
    new_cache_d_k = jnp.stack(dk_list, axis=1)
    new_cache_d_v = jnp.stack(dv_list, axis=1)
    return (ctx, lat, new_state_hgrn, new_cache_c_k, new_cache_c_v, new_cache_d_k, new_cache_d_v)
```

```python
import functools
import math

import numpy as np
import jax
import jax.numpy as jnp
from jax import lax
from jax.experimental import pallas as pl
from jax.experimental.pallas import tpu as pltpu

F32 = jnp.float32
BF16 = jnp.bfloat16

D_MODEL = 1024
EPS = 1e-6
GRID_W = 64
ROPE_THETA = 10000.0
H_A = 4
DK_A = 128
W_A = 512
CHUNK = 64
SUB = 8
LEVELS = (16, 32, 64)
C_B = 512
FILTER_BANDS = 16
DECAY_MIN = math.log(1e-2) / 1.5
DECAY_MAX = math.log(1e-2) / 0.3
H_C = 4
KV_C = 2
HD_C = 128
H_D = 4
DK_D = 64
DV_D = 128
N_GROUPS = 4
EXP_PER_GROUP = 4
N_EXPERTS = 16
D_EXPERT = 512

LANES = 128
ROW_GROUP = 1024
VMEM_LIMIT_BYTES = 56 * 1024 * 1024


def _cparams(*sem):
    return pltpu.CompilerParams(dimension_semantics=sem, vmem_limit_bytes=VMEM_LIMIT_BYTES)


def _split_bf16(x):
    hi = x.astype(BF16)
    lo = (x - hi.astype(F32)).astype(BF16)
    return hi, lo


def _dot(a, b):
    return jnp.dot(a, b, preferred_element_type=F32)


def _dot3(a, b):
    ah, al = _split_bf16(a)
    bh, bl = _split_bf16(b)
    return _dot(ah, bh) + _dot(al, bh) + _dot(ah, bl)


def _dot_nt(a, b):
    return lax.dot_general(a, b, (((1,), (1,)), ((), ())), preferred_element_type=F32)


def _dot_tn(a, b):
    return lax.dot_general(a, b, (((0,), (0,)), ((), ())), preferred_element_type=F32)


def _silu(x):
    return x * jax.nn.sigmoid(x)


def _rms(x, eps=EPS):
    return x * lax.rsqrt(jnp.mean(x * x, axis=-1, keepdims=True) + eps)


def _adaln_kernel(c_ref, w_ref, b_ref, o_ref):
    s = _silu(c_ref[...])
    o_ref[0] = _dot(s.astype(BF16), w_ref[0].astype(BF16)) + b_ref[0]


def _adaln(cond, w_mod, b_mod):
    depth, _, n = w_mod.shape
    rows = cond.shape[0]
    tn = 1536
    return pl.pallas_call(
        _adaln_kernel,
        grid=(depth, n // tn),
        in_specs=[
            pl.BlockSpec((rows, D_MODEL), lambda l, j: (0, 0)),
            pl.BlockSpec((1, D_MODEL, tn), lambda l, j: (l, 0, j)),
            pl.BlockSpec((1, 1, tn), lambda l, j: (l, 0, j)),
        ],
        out_specs=pl.BlockSpec((1, rows, tn), lambda l, j: (l, 0, j)),
        out_shape=jax.ShapeDtypeStruct((depth, rows, n), F32),
        compiler_params=_cparams("arbitrary", "arbitrary"),
        name="adaln",
    )(cond, w_mod, b_mod.reshape(depth, 1, n))


def _norm_proj_kernel(x_ref, m_ref, g_ref, w_ref, *o_refs, splits):
    m = m_ref[0]
    h = _rms(x_ref[0]) * g_ref[...] * (1.0 + m[1:2]) + m[0:1]
    hb = h.astype(BF16)
    for o_ref, (a, b) in zip(o_refs, splits):
        o_ref[0] = _dot(hb, w_ref[:, a:b])


def _norm_proj(x, mods, g, w_bf16, splits, tm=256):
    ngrp = x.shape[0]
    n = w_bf16.shape[1]
    kern = functools.partial(_norm_proj_kernel, splits=splits)
    return pl.pallas_call(
        kern,
        grid=(ngrp, ROW_GROUP // tm),
        in_specs=[
            pl.BlockSpec((1, tm, D_MODEL), lambda b, i: (b, i, 0)),
            pl.BlockSpec((1, 6, D_MODEL), lambda b, i: (b, 0, 0)),
            pl.BlockSpec((1, D_MODEL), lambda b, i: (0, 0)),
            pl.BlockSpec((D_MODEL, n), lambda b, i: (0, 0)),
        ],
        out_specs=[pl.BlockSpec((1, tm, hi - lo), lambda b, i: (b, i, 0)) for lo, hi in splits],
        out_shape=[jax.ShapeDtypeStruct((ngrp, ROW_GROUP, hi - lo), F32) for lo, hi in splits],
        compiler_params=_cparams("arbitrary", "arbitrary"),
        name="norm_proj",
    )(x, mods, g.reshape(1, D_MODEL), w_bf16)


def _hgrn_constants():
    c = CHUNK
    nblk = 1 + len(LEVELS)
    w = np.zeros((2, nblk * c, c), np.float32)
    for t in range(c):
        w[0, t, : t + 1] = 1.0
    for li, r in enumerate(LEVELS):
        for t in range(c):
            mid = (t // r) * r + r // 2
            if t >= mid:
                w[0, (li + 1) * c + t, mid : t + 1] = 1.0
            else:
                w[0, (li + 1) * c + t, t + 1 : mid] = 1.0
    for blk in range(nblk):
        w[1, blk * c : (blk + 1) * c] = w[0, blk * c : (blk + 1) * c][::-1, ::-1]
    m = np.zeros((2, nblk, c, c), np.float32)
    for li, r in enumerate(LEVELS):
        for t in range(c):
            for s in range(c):
                if t // r == s // r and (t % r) >= r // 2 and (s % r) < r // 2:
                    m[0, li, t, s] = 1.0
    for t in range(c):
        for s in range(c):
            if t // SUB == s // SUB and s <= t:
                m[0, nblk - 1, t, s] = 1.0
    m[1] = m[0][:, ::-1, ::-1]
    sel = np.zeros((SUB * DK_A, c), np.float32)
    for i in range(SUB):
        sel[i * DK_A : (i + 1) * DK_A, i::SUB] = 1.0
    return w, m, sel


def _hgrn_chunk(direction, r0, qa_ref, gate_ref, ia_ref, lb, wcum, masks, sel, st_ref):
    rows = pl.ds(r0, CHUNK)
    q = _silu(qa_ref[0, rows, :])
    v = ia_ref[0, rows, :]
    f = lb + (1.0 - lb) * jax.nn.sigmoid(gate_ref[0, rows, :])
    lf = jnp.log(f)
    kk = 1.0 - f
    lf_hi, lf_lo = _split_bf16(lf)
    z = _dot(wcum, lf_hi) + _dot(wcum, lf_lo)
    e = z[0:CHUNK]
    edge = e[CHUNK - 1 : CHUNK] if direction == 0 else e[0:1]
    q_in = (q * jnp.exp(e)).astype(BF16)
    k_st = (kk * jnp.exp(edge - e)).astype(BF16)
    st_decay = jnp.exp(edge)
    q_lv, k_lv = [], []
    for li in range(len(LEVELS)):
        ez = jnp.exp(z[(li + 1) * CHUNK : (li + 2) * CHUNK])
        q_lv.append((q * ez).astype(BF16))
        k_lv.append((kk * ez).astype(BF16))
    nsub = CHUNK // SUB
    kk3 = kk.reshape(nsub, SUB, W_A)
    e3 = e.reshape(nsub, SUB, W_A)
    pair = []
    for i in range(SUB):
        kb = jnp.broadcast_to(kk3[:, i : i + 1, :], (nsub, SUB, W_A)).reshape(CHUNK, W_A)
        eb = jnp.broadcast_to(e3[:, i : i + 1, :], (nsub, SUB, W_A)).reshape(CHUNK, W_A)
        pair.append((q * kb * jnp.exp(jnp.minimum(e - eb, 0.0))).astype(BF16))
    vb = v.astype(BF16)
    outs = []
    for h in range(H_A):
        hs = slice(h * DK_A, (h + 1) * DK_A)
        sc = _dot(jnp.concatenate([p[:, hs] for p in pair], axis=1), sel) * masks[len(LEVELS)]
        for li in range(len(LEVELS)):
            sc = sc + _dot_nt(q_lv[li][:, hs], k_lv[li][:, hs]) * masks[li]
        st = st_ref[direction, h]
        o_h = _dot(sc.astype(BF16), vb[:, hs]) + _dot_nt(q_in[:, hs], st.astype(BF16))
        st_ref[direction, h] = st_decay[:, hs] * st + _dot_tn(vb[:, hs], k_st[:, hs])
        outs.append(o_h)
    return jnp.concatenate(outs, axis=1)


def _hgrn_kernel(qa_ref, ff_ref, fb_ref, ia_ref, ga_ref, lb_ref, ng_ref, s0_ref, w_ref, m_ref,
                 sel_ref, o_ref, sout_ref, st_ref, *, seq_len):
    nchunk = seq_len // CHUNK
    st_ref[...] = s0_ref[0]
    lb = lb_ref[...]
    sel = sel_ref[...]

    def fwd(i, carry):
        r0 = pl.multiple_of(i * CHUNK, CHUNK)
        masks = [m_ref[0, j] for j in range(1 + len(LEVELS))]
        o = _hgrn_chunk(0, r0, qa_ref, ff_ref, ia_ref, lb, w_ref[0], masks, sel, st_ref)
        o_ref[0, pl.ds(r0, CHUNK), :] = o
        return carry

    lax.fori_loop(0, nchunk, fwd, 0)

    ng = jnp.concatenate([ng_ref[...]] * H_A, axis=1)

    def bwd(i, carry):
        r0 = pl.multiple_of((nchunk - 1 - i) * CHUNK, CHUNK)
        masks = [m_ref[1, j] for j in range(1 + len(LEVELS))]
        o = _hgrn_chunk(1, r0, qa_ref, fb_ref, ia_ref, lb, w_ref[1], masks, sel, st_ref)
        tot = o + o_ref[0, pl.ds(r0, CHUNK), :]
        normed = jnp.concatenate(
            [_rms(tot[:, h * DK_A : (h + 1) * DK_A]) for h in range(H_A)], axis=1)
        o_ref[0, pl.ds(r0, CHUNK), :] = normed * ng * _silu(ga_ref[0, pl.ds(r0, CHUNK), :])
        return carry

    lax.fori_loop(0, nchunk, bwd, 0)
    sout_ref[0] = st_ref[...]


def _hgrn(qa, ff, fb, ia, ga, lb, ng, s0t, seq_len):
    bsz = qa.shape[0]
    w, m, sel = _hgrn_constants()
    seq = pl.BlockSpec((1, seq_len, W_A), lambda b: (b, 0, 0))
    full = lambda shape: pl.BlockSpec(shape, lambda b: (0,) * len(shape))
    st_spec = pl.BlockSpec((1, 2, H_A, DK_A, DK_A), lambda b: (b, 0, 0, 0, 0))
    return pl.pallas_call(
        functools.partial(_hgrn_kernel, seq_len=seq_len),
        grid=(bsz,),
        in_specs=[seq, seq, seq, seq, seq, full((1, W_A)), full((1, DK_A)), st_spec,
                  full(w.shape), full(m.shape), full(sel.shape)],
        out_specs=[seq, st_spec],
        out_shape=[jax.ShapeDtypeStruct((bsz, seq_len, W_A), F32),
                   jax.ShapeDtypeStruct((bsz, 2, H_A, DK_A, DK_A), F32)],
        scratch_shapes=[pltpu.VMEM((2, H_A, DK_A, DK_A), F32)],
        compiler_params=_cparams("arbitrary"),
        name="hgrn",
    )(qa, ff, fb, ia, ga, lb.reshape(1, W_A), ng.reshape(1, DK_A), s0t,
      jnp.asarray(w, BF16), jnp.asarray(m, F32), jnp.asarray(sel, BF16))


def _dft_constants(seq_len):
    n = 2 * seq_len
    t = np.arange(seq_len, dtype=np.int64)
    wt = (np.arange(seq_len, dtype=np.int64)[:, None] * t[None, :]) % n
    ang = 2.0 * np.pi * wt.astype(np.float64) / n
    cos, sin = np.cos(ang), np.sin(ang)
    nyq = np.where(t % 2 == 0, 1.0, -1.0)
    sin_p = sin.copy()
    sin_p[0] = nyq
    fwd = np.concatenate([cos, sin_p], axis=0)
    icos = 2.0 * cos.T / n
    icos[:, 0] = 1.0 / n
    isin = 2.0 * sin.T / n
    isin[:, 0] = nyq / n
    inv = np.concatenate([icos, isin], axis=1)
    return fwd, inv


def _filter_embedding(seq_len):
    t = np.linspace(0.0, 1.0, seq_len)[:, None]
    w = 2.0 * np.pi * np.arange(seq_len) / seq_len
    f = np.linspace(1e-4, FILTER_BANDS - 1, FILTER_BANDS)
    ang = w[:, None] * f[None, :]
    z = np.concatenate([t, np.cos(ang), -np.sin(ang)], axis=-1)
    zp = np.zeros((seq_len, LANES), np.float64)
    zp[:, : z.shape[1]] = z
    deltas = np.abs(np.linspace(DECAY_MIN, DECAY_MAX, C_B))
    window = np.exp(-t * deltas[None, :])
    return zp.astype(np.float32), window.astype(np.float32)


def _hyena_filter_kernel(z_ref, win_ref, w1_ref, b1_ref, fr1_ref, w2_ref, b2_ref, fr2_ref, w3_ref,
                         hb_ref, fh_ref, fl_ref, gr_ref, gi_ref, *, seq_len):
    h = jnp.sin(fr1_ref[...] * (_dot3(z_ref[...], w1_ref[...]) + b1_ref[...]))
    h = jnp.sin(fr2_ref[...] * (_dot3(h, w2_ref[...]) + b2_ref[...]))
    h = _dot3(h, w3_ref[...])
    win = win_ref[...]
    hf = h[:, :C_B] * win
    hbk = h[:, C_B:] * win

    def dft(x):
        xh, xl = _split_bf16(x)
        return _dot(fh_ref[...], xh) + _dot(fh_ref[...], xl) + _dot(fl_ref[...], xh)

    p_sum = dft(hf + hbk)
    p_dif = dft(hf - hbk)
    row0 = lax.broadcasted_iota(jnp.int32, (seq_len, C_B), 0) == 0
    gr_ref[...] = p_sum[:seq_len] + hb_ref[...]
    gi_ref[...] = jnp.where(row0, p_sum[seq_len:] + hb_ref[...], p_dif[seq_len:])


def _hyena_filter(seq_len, w1, b1, fr1, w2, b2, fr2, w3, hbias):
    zemb, window = _filter_embedding(seq_len)
    fwd, _ = _dft_constants(seq_len)
    f_hi, f_lo = _split_bf16(jnp.asarray(fwd, F32))

    def pad(a, rows, cols):
        return jnp.zeros((rows, cols), F32).at[: a.shape[0], : a.shape[1]].set(a)

    args = (jnp.asarray(zemb), jnp.asarray(window), pad(w1, LANES, LANES), pad(b1[None], 1, LANES),
            pad(fr1[None], 1, LANES), pad(w2, LANES, LANES), pad(b2[None], 1, LANES),
            pad(fr2[None], 1, LANES), pad(w3, LANES, 2 * C_B), hbias.reshape(1, C_B), f_hi, f_lo)
    return pl.pallas_call(
        functools.partial(_hyena_filter_kernel, seq_len=seq_len),
        out_shape=[jax.ShapeDtypeStruct((seq_len, C_B), F32)] * 2,
        compiler_params=pltpu.CompilerParams(vmem_limit_bytes=VMEM_LIMIT_BYTES),
        name="hyena_filter",
    )(*args)


def _hyena_kernel(x0_ref, x1_ref, v_ref, cw_ref, cb_ref, gr_ref, gi_ref, f_ref, fi_ref, o_ref, *,
                  seq_len):
    tc = o_ref.shape[2]
    row = lax.broadcasted_iota(jnp.int32, (seq_len, tc), 0)

    def short_conv(u_ref, j):
        u = u_ref[0]
        prev = jnp.where(row == 0, 0.0, pltpu.roll(u, 1, 0))
        nxt = jnp.where(row == seq_len - 1, 0.0, pltpu.roll(u, seq_len - 1, 0))
        cw = cw_ref[j]
        return cw[0:1] * prev + cw[1:2] * u + cw[2:3] * nxt + cb_ref[j]

    x0 = short_conv(x0_ref, 0)
    z = short_conv(v_ref, 2) * short_conv(x1_ref, 1)
    p = _dot(f_ref[...], z.astype(BF16))
    a, b = p[:seq_len], p[seq_len:]
    gr, gi = gr_ref[...], gi_ref[...]
    row0 = row == 0
    bgi = b * gi
    yr = a * gr - jnp.where(row0, 0.0, bgi)
    yq = jnp.where(row0, bgi, a * gi + b * gr)
    y = _dot(fi_ref[...], jnp.concatenate([yr, yq], axis=0).astype(BF16))
    o_ref[0] = y * x0


def _hyena(hy, conv_w, conv_b, gr, gi, seq_len, tc=256):
    bsz = hy.shape[0]
    nct = C_B // tc
    fwd, inv = _dft_constants(seq_len)
    cw = conv_w.reshape(3, 3, C_B).transpose(1, 0, 2)
    cb = conv_b.reshape(3, 1, C_B)
    part = lambda k: pl.BlockSpec((1, seq_len, tc), lambda b, j, k=k: (b, 0, k * nct + j))
    return pl.pallas_call(
        functools.partial(_hyena_kernel, seq_len=seq_len),
        grid=(bsz, nct),
        in_specs=[part(0), part(1), part(2),
                  pl.BlockSpec((3, 3, tc), lambda b, j: (0, 0, j)),
                  pl.BlockSpec((3, 1, tc), lambda b, j: (0, 0, j)),
                  pl.BlockSpec((seq_len, tc), lambda b, j: (0, j)),
                  pl.BlockSpec((seq_len, tc), lambda b, j: (0, j)),
                  pl.BlockSpec((2 * seq_len, seq_len), lambda b, j: (0, 0)),
                  pl.BlockSpec((seq_len, 2 * seq_len), lambda b, j: (0, 0))],
        out_specs=pl.BlockSpec((1, seq_len, tc), lambda b, j: (b, 0, j)),
        out_shape=jax.ShapeDtypeStruct((bsz, seq_len, C_B), F32),
        compiler_params=_cparams("arbitrary", "arbitrary"),
        name="hyena",
    )(hy, hy, hy, cw, cb, gr, gi, jnp.asarray(fwd, F32).astype(BF16),
      jnp.asarray(inv, F32).astype(BF16))


def _rope_tables(seq_len, dim):
    rows = seq_len // GRID_W
    row_idx = np.repeat(np.arange(rows), GRID_W).astype(np.float64)
    col_idx = np.tile(np.arange(GRID_W), rows).astype(np.float64)
    half = dim // 2
    inv = ROPE_THETA ** (-np.arange(0, half, 2, dtype=np.float64) / half)
    ang = np.concatenate([row_idx[:, None] * inv, col_idx[:, None] * inv], axis=-1)
    cos = np.repeat(np.cos(ang), 2, axis=1)
    sin = np.repeat(np.sin(ang), 2, axis=1)
    sin[:, 0::2] *= -1.0
    reps = LANES // dim
    return (np.tile(cos, (1, reps)).astype(np.float32), np.tile(sin, (1, reps)).astype(np.float32))


def _qk_prep_kernel(p_ref, cqg_ref, ckg_ref, dqg_ref, dkg_ref, *rest, use_rope):
    if use_rope:
        cc_ref, sc_ref, cd_ref, sd_ref = rest[:4]
        rest = rest[4:]
    qc_ref, qd_ref, kc_ref, vc_ref, kd_ref, vd_ref = rest
    p = p_ref[0]
    tm = p.shape[0]
    lane = lax.broadcasted_iota(jnp.int32, (tm, LANES), 1)
    even = (lane % 2) == 0
    low = lane < DK_D

    def rope(x, cos, sin):
        swapped = jnp.where(even, pltpu.roll(x, LANES - 1, 1), pltpu.roll(x, 1, 1))
        return x * cos + swapped * sin

    def norm_c(x, g_ref):
        y = _rms(x) * g_ref[...]
        return rope(y, cc_ref[...], sc_ref[...]) if use_rope else y

    def norm_d(x, g_ref):
        sq = x * x
        s_lo = jnp.sum(jnp.where(low, sq, 0.0), axis=1, keepdims=True)
        s_hi = jnp.sum(jnp.where(low, 0.0, sq), axis=1, keepdims=True)
        ms = jnp.where(low, s_lo, s_hi) * (1.0 / DK_D)
        y = x * lax.rsqrt(ms + EPS) * g_ref[...]
        return rope(y, cd_ref[...], sd_ref[...]) if use_rope else y

    blk = lambda i: p[:, i * LANES : (i + 1) * LANES]
    qc_ref[0] = jnp.concatenate([norm_c(blk(i), cqg_ref) for i in range(H_C)], axis=1)
    for j in range(KV_C):
        kc_ref[0, j] = norm_c(blk(H_C + j), ckg_ref)
        vc_ref[0, j] = blk(H_C + KV_C + j)
    base = H_C + 2 * KV_C
    qd_ref[0] = jnp.concatenate([norm_d(blk(base + i), dqg_ref) for i in range(H_D)], axis=1)
    for j in range(H_D):
        kd_ref[0, j] = norm_d(blk(base + H_D + j), dkg_ref)
        vd_ref[0, j] = blk(base + 2 * H_D + j)


def _qk_prep(p, cqg, ckg, dqg, dkg, seq_len, use_rope, tm=256):
    ngrp = p.shape[0]
    bsz = ngrp * ROW_GROUP // seq_len
    p = p.reshape(bsz, seq_len, p.shape[2])
    tm = min(tm, seq_len)
    vec = lambda: pl.BlockSpec((1, LANES), lambda b, i: (0, 0))
    tab = lambda: pl.BlockSpec((tm, LANES), lambda b, i: (i, 0))
    in_specs = [pl.BlockSpec((1, tm, p.shape[2]), lambda b, i: (b, i, 0)), vec(), vec(), vec(), vec()]
    args = [p, cqg.reshape(1, HD_C), ckg.reshape(1, HD_C),
            jnp.tile(dqg.reshape(1, DK_D), (1, 2)), jnp.tile(dkg.reshape(1, DK_D), (1, 2))]
    if use_rope:
        in_specs += [tab(), tab(), tab(), tab()]
        args += [jnp.asarray(a) for a in _rope_tables(seq_len, HD_C) + _rope_tables(seq_len, DK_D)]
    tok = lambda w: pl.BlockSpec((1, tm, w), lambda b, i: (b, i, 0))
    head = lambda nh: pl.BlockSpec((1, nh, tm, LANES), lambda b, i: (b, 0, i, 0))
    tok_shape = lambda w: jax.ShapeDtypeStruct((bsz, seq_len, w), F32)
    head_shape = lambda nh: jax.ShapeDtypeStruct((bsz, nh, seq_len, LANES), F32)
    return pl.pallas_call(
        functools.partial(_qk_prep_kernel, use_rope=use_rope),
        grid=(bsz, seq_len // tm),
        in_specs=in_specs,
        out_specs=[tok(H_C * HD_C), tok(H_D * 2 * DK_D), head(KV_C), head(KV_C), head(H_D), head(H_D)],
        out_shape=[tok_shape(H_C * HD_C), tok_shape(H_D * 2 * DK_D), head_shape(KV_C),
                   head_shape(KV_C), head_shape(H_D), head_shape(H_D)],
        compiler_params=_cparams("arbitrary", "arbitrary"),
        name="qk_prep",
    )(*args)


def _softmax_pv(q_list, kv_list):
    outs = []
    for q in q_list:
        scores = [_dot_nt(q, k) for k, _ in kv_list]
        mx = scores[0].max(axis=1, keepdims=True)
        for s in scores[1:]:
            mx = jnp.maximum(mx, s.max(axis=1, keepdims=True))
        den = 0.0
        acc = 0.0
        for s, (_, v) in zip(scores, kv_list):
            pexp = jnp.exp(s - mx)
            den = den + pexp.sum(axis=1, keepdims=True)
            acc = acc + _dot(pexp.astype(BF16), v)
        outs.append(acc / den)
    return outs


def _gqa_kernel(q_ref, k_ref, v_ref, *rest, has_cache):
    if has_cache:
        ck_ref, cv_ref, o_ref = rest
        kv = [(ck_ref[0, 0, 0].astype(BF16), cv_ref[0, 0, 0].astype(BF16))]
    else:
        (o_ref,) = rest
        kv = []
    kv.append((k_ref[0, 0].astype(BF16), v_ref[0, 0].astype(BF16)))
    q = q_ref[0] * (HD_C ** -0.5)
    g_c = H_C // KV_C
    qs = [q[:, g * HD_C : (g + 1) * HD_C].astype(BF16) for g in range(g_c)]
    o_ref[0] = jnp.concatenate(_softmax_pv(qs, kv), axis=1)


def _diff_kernel(q_ref, k_ref, v_ref, lam_ref, sg_ref, *rest, has_cache, out_scale):
    if has_cache:
        ck_ref, cv_ref, o_ref = rest
        kv = [(ck_ref[0, 0, 0].astype(BF16), cv_ref[0, 0, 0].astype(BF16))]
    else:
        (o_ref,) = rest
        kv = []
    kv.append((k_ref[0, 0].astype(BF16), v_ref[0, 0].astype(BF16)))
    q = q_ref[0] * (DK_D ** -0.5)
    low = lax.broadcasted_iota(jnp.int32, q.shape, 1) < DK_D
    qs = [jnp.where(low, q, 0.0).astype(BF16), jnp.where(low, 0.0, q).astype(BF16)]
    o1, o2 = _softmax_pv(qs, kv)
    o_ref[0] = _rms(o1 - lam_ref[...] * o2) * sg_ref[...] * out_scale


def _attention(q, k, v, cache_k, cache_v, kernel, extra_args, n_heads, q_width, seq_len, tq=256):
    bsz = q.shape[0]
    tq = min(tq, seq_len)
    in_specs = [pl.BlockSpec((1, tq, q_width), lambda b, h, i: (b, i, h)),
                pl.BlockSpec((1, 1, seq_len, LANES), lambda b, h, i: (b, h, 0, 0)),
                pl.BlockSpec((1, 1, seq_len, LANES), lambda b, h, i: (b, h, 0, 0))]
    in_specs += [pl.BlockSpec((1, LANES), lambda b, h, i: (0, 0)) for _ in extra_args]
    args = [q, k, v, *extra_args]
    if cache_k is not None:
        past = cache_k.shape[3]
        spec = lambda: pl.BlockSpec((1, 1, 1, past, LANES), lambda b, h, i: (b, 0, h, 0, 0))
        in_specs += [spec(), spec()]
        args += [cache_k, cache_v]
    return pl.pallas_call(
        functools.partial(kernel, has_cache=cache_k is not None),
        grid=(bsz, n_heads, seq_len // tq),
        in_specs=in_specs,
        out_specs=pl.BlockSpec((1, tq, q_width), lambda b, h, i: (b, i, h)),
        out_shape=jax.ShapeDtypeStruct((bsz, seq_len, n_heads * q_width), F32),
        compiler_params=_cparams("arbitrary", "arbitrary", "arbitrary"),
        name="attention",
    )(*args)


def _route(lg):
    lane = lax.broadcasted_iota(jnp.int32, lg.shape, 1).astype(F32)
    neg = -1e30
    is_g = (lane >= N_EXPERTS) & (lane < N_EXPERTS + N_GROUPS)
    gl = jnp.where(is_g, lg, neg)
    gmax = gl.max(axis=1, keepdims=True)
    g_p = 1.0 / jnp.where(is_g, jnp.exp(gl - gmax), 0.0).sum(axis=1, keepdims=True)
    g_i = jnp.where(gl == gmax, lane - N_EXPERTS, 1e9).min(axis=1, keepdims=True)
    in_grp = (lane < N_EXPERTS) & (jnp.floor(lane * (1.0 / EXP_PER_GROUP)) == g_i)
    el = jnp.where(in_grp, lg, neg)
    m1 = el.max(axis=1, keepdims=True)
    i1 = jnp.where(in_grp & (el == m1), lane, 1e9).min(axis=1, keepdims=True)
    el2 = jnp.where(lane == i1, neg, el)
    m2 = el2.max(axis=1, keepdims=True)
    i2 = jnp.where(in_grp & (el2 == m2) & (lane != i1), lane, 1e9).min(axis=1, keepdims=True)
    r = jnp.exp(m2 - m1)
    w1 = g_p / (1.0 + r)
    return jnp.where(lane == i1, w1, jnp.where(lane == i2, w1 * r, 0.0))


def _mix_out_kernel(x_ref, a_ref, b_ref, m_ref, g2_ref, w_ref, wrh_ref, wrl_ref, br_ref,
                    x1_ref, h2_ref, gates_ref):
    wa = a_ref.shape[2]
    o = _dot(a_ref[0].astype(BF16), w_ref[:wa]) + _dot(b_ref[0].astype(BF16), w_ref[wa:])
    m = m_ref[0]
    x1 = x_ref[0] + m[2:3] * o
    x1_ref[0] = x1
    h2 = _rms(x1) * g2_ref[...] * (1.0 + m[4:5]) + m[3:4]
    h2_ref[0] = h2.astype(BF16)
    hi, lo = _split_bf16(h2)
    lg = _dot(hi, wrh_ref[...]) + _dot(lo, wrh_ref[...]) + _dot(hi, wrl_ref[...]) + br_ref[...]
    gates_ref[0] = _route(lg)


def _mix_out(x, a, b, mods, g2, w_out_bf16, w_router, b_router, tm=256):
    ngrp = x.shape[0]
    wa, wb = a.shape[2], b.shape[2]
    wr_hi = w_router.astype(BF16)
    wr_lo = (w_router - wr_hi.astype(F32)).astype(BF16)
    row = lambda w: pl.BlockSpec((1, tm, w), lambda g, i: (g, i, 0))
    full = lambda shape: pl.BlockSpec(shape, lambda g, i: (0,) * len(shape))
    return pl.pallas_call(
        _mix_out_kernel,
        grid=(ngrp, ROW_GROUP // tm),
        in_specs=[row(D_MODEL), row(wa), row(wb),
                  pl.BlockSpec((1, 6, D_MODEL), lambda g, i: (g, 0, 0)),
                  full((1, D_MODEL)), full((wa + wb, D_MODEL)),
                  full((D_MODEL, LANES)), full((D_MODEL, LANES)), full((1, LANES))],
        out_specs=[row(D_MODEL), row(D_MODEL), row(LANES)],
        out_shape=[jax.ShapeDtypeStruct((ngrp, ROW_GROUP, D_MODEL), F32),
                   jax.ShapeDtypeStruct((ngrp, ROW_GROUP, D_MODEL), BF16),
                   jax.ShapeDtypeStruct((ngrp, ROW_GROUP, LANES), F32)],
        compiler_params=_cparams("arbitrary", "arbitrary"),
        name="mix_out",
    )(x, a, b, mods, g2.reshape(1, D_MODEL), w_out_bf16, wr_hi, wr_lo, b_router)


def _moe_kernel(h_ref, gates_ref, x_ref, m_ref, wg_ref, wu_ref, wd_ref, o_ref, acc_ref):
    e = pl.program_id(2)

    @pl.when(e == 0)
    def _():
        acc_ref[...] = jnp.zeros_like(acc_ref)

    h = h_ref[0]
    a = _dot(h, wg_ref[0].astype(BF16))
    u = _dot(h, wu_ref[0].astype(BF16))
    gates = gates_ref[0]
    lane = lax.broadcasted_iota(jnp.int32, gates.shape, 1)
    gate = jnp.where(lane == e, gates, 0.0).sum(axis=1, keepdims=True)
    hid = _silu(a) * u * gate
    acc_ref[...] += _dot(hid.astype(BF16), wd_ref[0].astype(BF16))

    @pl.when(e == pl.num_programs(2) - 1)
    def _():
        o_ref[0] = x_ref[0] + m_ref[0][5:6] * acc_ref[...]


def _moe(h2, gates, x1, mods, w_gate, w_up, w_down, tm=1024):
    ngrp = x1.shape[0]
    row = lambda w: pl.BlockSpec((1, tm, w), lambda g, i, e: (g, i, 0))
    return pl.pallas_call(
        _moe_kernel,
        grid=(ngrp, ROW_GROUP // tm, N_EXPERTS),
        in_specs=[row(D_MODEL), row(LANES), row(D_MODEL),
                  pl.BlockSpec((1, 6, D_MODEL), lambda g, i, e: (g, 0, 0)),
                  pl.BlockSpec((1, D_MODEL, D_EXPERT), lambda g, i, e: (e, 0, 0)),
                  pl.BlockSpec((1, D_MODEL, D_EXPERT), lambda g, i, e: (e, 0, 0)),
                  pl.BlockSpec((1, D_EXPERT, D_MODEL), lambda g, i, e: (e, 0, 0))],
        out_specs=row(D_MODEL),
        out_shape=jax.ShapeDtypeStruct((ngrp, ROW_GROUP, D_MODEL), F32),
        scratch_shapes=[pltpu.VMEM((tm, D_MODEL), F32)],
        compiler_params=_cparams("arbitrary", "arbitrary", "arbitrary"),
        name="moe",
    )(h2, gates, x1, mods, w_gate, w_up, w_down)


def kernel(x_prompt, x_sample, state_hgrn, cache_c_k, cache_c_v, cache_d_k, cache_d_v, c, c_ctx, norm1_g, norm2_g, w_mod, b_mod, even_w_in, even_w_out, hgrn_lower, hgrn_norm_g, hy_conv_w, hy_conv_b, hy_w1, hy_b1, hy_freq1, hy_w2, hy_b2, hy_freq2, hy_w3, hy_bias, odd_w_in, odd_w_out, c_qnorm_g, c_knorm_g, d_qnorm_g, d_knorm_g, d_lambda_q1, d_lambda_k1, d_lambda_q2, d_lambda_k2, d_subln_g, moe_w_grp, moe_b_grp, moe_w_rt, moe_b_rt, moe_w_gate, moe_w_up, moe_w_down):
    depth = w_mod.shape[0]
    n_ctx, seq, _ = x_prompt.shape
    n_lat, dec_seq, _ = x_sample.shape
    g_ctx = n_ctx * seq // ROW_GROUP
    g_lat = n_lat * dec_seq // ROW_GROUP
    assert dec_seq == ROW_GROUP and ROW_GROUP % seq == 0

    cond = jnp.zeros((16, D_MODEL), F32).at[0].set(c_ctx).at[1 : 1 + n_lat].set(c)
    mods = _adaln(cond, w_mod, b_mod).reshape(depth, 16, 6, D_MODEL)
    lower = jnp.cumsum(jax.nn.softmax(hgrn_lower.astype(F32), axis=0), axis=0)

    streams = [
        dict(x=x_prompt.reshape(g_ctx, ROW_GROUP, D_MODEL), ngrp=g_ctx, bsz=n_ctx, seq=seq, ctx=True),
        dict(x=x_sample, ngrp=g_lat, bsz=n_lat, seq=dec_seq, ctx=False),
    ]
    new_state, new_ck, new_cv, new_dk, new_dv = [], [], [], [], []

    for l in range(depth):
        j = l // 2
        w_router = jnp.zeros((D_MODEL, LANES), F32)
        w_router = w_router.at[:, :N_EXPERTS].set(moe_w_rt[l])
        w_router = w_router.at[:, N_EXPERTS : N_EXPERTS + N_GROUPS].set(moe_w_grp[l])
        b_router = jnp.zeros((1, LANES), F32)
        b_router = b_router.at[0, :N_EXPERTS].set(moe_b_rt[l])
        b_router = b_router.at[0, N_EXPERTS : N_EXPERTS + N_GROUPS].set(moe_b_grp[l])
        if l % 2 == 0:
            w_in = even_w_in[j].astype(BF16)
            w_out = even_w_out[j].astype(BF16)
        else:
            w_in = odd_w_in[j].astype(BF16)
            w_out = odd_w_out[j].astype(BF16)
            lam_init = 0.8 - 0.6 * math.exp(-0.3 * l)
            lam = (jnp.exp(jnp.sum(d_lambda_q1[j] * d_lambda_k1[j]))
                   - jnp.exp(jnp.sum(d_lambda_q2[j] * d_lambda_k2[j])) + lam_init)
            lam_row = jnp.full((1, LANES), lam, F32)

        for s in streams:
            ngrp, bsz, sl = s["ngrp"], s["bsz"], s["seq"]
            if s["ctx"]:
                m = jnp.broadcast_to(mods[l, 0][None], (ngrp, 6, D_MODEL))
            else:
                m = mods[l, 1 : 1 + ngrp]
            x = s["x"]
            if l % 2 == 0:
                wa = H_A * DK_A
                splits = [(0, wa), (wa, 2 * wa), (2 * wa, 3 * wa), (3 * wa, 3 * wa + W_A),
                          (3 * wa + W_A, 3 * wa + 2 * W_A), (3 * wa + 2 * W_A, w_in.shape[1])]
                qa, ffa, fba, ia, ga, hy = _norm_proj(x, m, norm1_g[l], w_in, splits)
                per_seq = lambda t: t.reshape(bsz, sl, t.shape[-1])
                if s["ctx"]:
                    s0t = jnp.zeros((bsz, 2, H_A, DK_A, DK_A), F32)
                else:
                    s0t = jnp.swapaxes(state_hgrn[:, j].astype(F32), -1, -2)
                mix_a, s_fin = _hgrn(per_seq(qa), per_seq(ffa), per_seq(fba), per_seq(ia),
                                     per_seq(ga), lower[j], hgrn_norm_g[j], s0t, sl)
                gr, gi = _hyena_filter(sl, hy_w1[j], hy_b1[j], hy_freq1[j], hy_w2[j], hy_b2[j],
                                       hy_freq2[j], hy_w3[j], hy_bias[j])
                mix_b = _hyena(per_seq(hy), hy_conv_w[j], hy_conv_b[j], gr, gi, sl)
                if s["ctx"]:
                    new_state.append(jnp.swapaxes(s_fin, -1, -2))
            else:
                (p,) = _norm_proj(x, m, norm1_g[l], w_in, [(0, w_in.shape[1])])
                qc, qd, kc, vc, kd, vd = _qk_prep(p, c_qnorm_g[j], c_knorm_g[j], d_qnorm_g[j],
                                                  d_knorm_g[j], sl, use_rope=not s["ctx"])
                if s["ctx"]:
                    caches = (None, None, None, None)
                    new_ck.append(kc)
                    new_cv.append(vc)
                    new_dk.append(kd)
                    new_dv.append(vd)
                else:
                    caches = (cache_c_k[:, j : j + 1], cache_c_v[:, j : j + 1],
                              cache_d_k[:, j : j + 1], cache_d_v[:, j : j + 1])
                mix_a = _attention(qc, kc, vc, caches[0], caches[1], _gqa_kernel, (), KV_C,
                                   (H_C // KV_C) * HD_C, sl)
                diff = functools.partial(_diff_kernel, out_scale=1.0 - lam_init)
                mix_b = _attention(qd, kd, vd, caches[2], caches[3], diff,
                                   (lam_row, d_subln_g[j].reshape(1, DV_D)), H_D, 2 * DK_D, sl)
            grp = lambda t: t.reshape(ngrp, ROW_GROUP, t.shape[-1])
            x1, h2, gates = _mix_out(x, grp(mix_a), grp(mix_b), m, norm2_g[l], w_out, w_router, b_router)
            s["x"] = _moe(h2, gates, x1, m, moe_w_gate[l], moe_w_up[l], moe_w_down[l])

    y_ctx = streams[0]["x"].reshape(n_ctx, seq, D_MODEL)
    y_lat = streams[1]["x"]
    return (y_ctx, y_lat, jnp.stack(new_state, axis=1), jnp.stack(new_ck, axis=1),
            jnp.stack(new_cv, axis=1), jnp.stack(new_dk, axis=1), jnp.stack(new_dv, axis=1))
```

```python
import functools
import math

import numpy as np
import jax
import jax.numpy as jnp
from jax import lax
from jax.experimental import pallas as pl
from jax.experimental.pallas import tpu as pltpu

F32 = jnp.float32
BF16 = jnp.bfloat16

D_MODEL = 1024
EPS = 1e-6
GRID_W = 64
ROPE_THETA = 10000.0
H_A = 4
DK_A = 128
W_A = 512
CHUNK = 64
SUB = 8
LEVELS = (16, 32, 64)
C_B = 512
FILTER_BANDS = 16
DECAY_MIN = math.log(1e-2) / 1.5
DECAY_MAX = math.log(1e-2) / 0.3
H_C = 4
KV_C = 2
HD_C = 128
H_D = 4
DK_D = 64
DV_D = 128
N_GROUPS = 4
EXP_PER_GROUP = 4
N_EXPERTS = 16
D_EXPERT = 512

ROUTE_LANE = N_EXPERTS + N_GROUPS
MOE_TILE = 512
MOE_CHUNK = 16
MOE_TILE_CHUNKS = 2 * MOE_TILE // MOE_CHUNK + N_EXPERTS
MOE_TILE_ROWS = MOE_TILE_CHUNKS * MOE_CHUNK
MOE_BLOCK_CHUNKS = 16
MOE_ROW_W = D_MODEL + 128

LANES = 128
ROW_GROUP = 1024
VMEM_LIMIT_BYTES = 56 * 1024 * 1024


def _cparams(*sem):
    return pltpu.CompilerParams(dimension_semantics=sem, vmem_limit_bytes=VMEM_LIMIT_BYTES)


def _split_bf16(x):
    hi = x.astype(BF16)
    lo = (x - hi.astype(F32)).astype(BF16)
    return hi, lo


def _dot(a, b):
    return jnp.dot(a, b, preferred_element_type=F32)


def _dot3(a, b):
    ah, al = _split_bf16(a)
    bh, bl = _split_bf16(b)
    return _dot(ah, bh) + _dot(al, bh) + _dot(ah, bl)


def _dot_nt(a, b):
    return lax.dot_general(a, b, (((1,), (1,)), ((), ())), preferred_element_type=F32)


def _dot_tn(a, b):
    return lax.dot_general(a, b, (((0,), (0,)), ((), ())), preferred_element_type=F32)


def _silu(x):
    return x * jax.nn.sigmoid(x)


def _rms(x, eps=EPS):
    return x * lax.rsqrt(jnp.mean(x * x, axis=-1, keepdims=True) + eps)


def _adaln_kernel(c_ref, w_ref, b_ref, o_ref):
    s = _silu(c_ref[...])
    o_ref[0] = _dot(s.astype(BF16), w_ref[0].astype(BF16)) + b_ref[0]


def _adaln(cond, w_mod, b_mod):
    depth, _, n = w_mod.shape
    rows = cond.shape[0]
    tn = 1536
    return pl.pallas_call(
        _adaln_kernel,
        grid=(depth, n // tn),
        in_specs=[
            pl.BlockSpec((rows, D_MODEL), lambda l, j: (0, 0)),
            pl.BlockSpec((1, D_MODEL, tn), lambda l, j: (l, 0, j)),
            pl.BlockSpec((1, 1, tn), lambda l, j: (l, 0, j)),
        ],
        out_specs=pl.BlockSpec((1, rows, tn), lambda l, j: (l, 0, j)),
        out_shape=jax.ShapeDtypeStruct((depth, rows, n), F32),
        compiler_params=_cparams("arbitrary", "arbitrary"),
        name="adaln",
    )(cond, w_mod, b_mod.reshape(depth, 1, n))


def _norm_proj_kernel(x_ref, m_ref, g_ref, w_ref, *o_refs, splits):
    m = m_ref[0]
    h = _rms(x_ref[0]) * g_ref[...] * (1.0 + m[1:2]) + m[0:1]
    hb = h.astype(BF16)
    for o_ref, (a, b) in zip(o_refs, splits):
        o_ref[0] = _dot(hb, w_ref[:, a:b])


def _norm_proj(x, mods, g, w_bf16, splits, tm=256):
    ngrp = x.shape[0]
    n = w_bf16.shape[1]
    kern = functools.partial(_norm_proj_kernel, splits=splits)
    return pl.pallas_call(
        kern,
        grid=(ngrp, ROW_GROUP // tm),
        in_specs=[
            pl.BlockSpec((1, tm, D_MODEL), lambda b, i: (b, i, 0)),
            pl.BlockSpec((1, 6, D_MODEL), lambda b, i: (b, 0, 0)),
            pl.BlockSpec((1, D_MODEL), lambda b, i: (0, 0)),
            pl.BlockSpec((D_MODEL, n), lambda b, i: (0, 0)),
        ],
        out_specs=[pl.BlockSpec((1, tm, hi - lo), lambda b, i: (b, i, 0)) for lo, hi in splits],
        out_shape=[jax.ShapeDtypeStruct((ngrp, ROW_GROUP, hi - lo), F32) for lo, hi in splits],
        compiler_params=_cparams("arbitrary", "arbitrary"),
        name="norm_proj",
    )(x, mods, g.reshape(1, D_MODEL), w_bf16)


def _hgrn_constants():
    c = CHUNK
    nblk = 1 + len(LEVELS)
    w = np.zeros((2, nblk * c, c), np.float32)
    for t in range(c):
        w[0, t, : t + 1] = 1.0
    for li, r in enumerate(LEVELS):
        for t in range(c):
            mid = (t // r) * r + r // 2
            if t >= mid:
                w[0, (li + 1) * c + t, mid : t + 1] = 1.0
            else:
                w[0, (li + 1) * c + t, t + 1 : mid] = 1.0
    for blk in range(nblk):
        w[1, blk * c : (blk + 1) * c] = w[0, blk * c : (blk + 1) * c][::-1, ::-1]
    m = np.zeros((2, nblk, c, c), np.float32)
    for li, r in enumerate(LEVELS):
        for t in range(c):
            for s in range(c):
                if t // r == s // r and (t % r) >= r // 2 and (s % r) < r // 2:
                    m[0, li, t, s] = 1.0
    for t in range(c):
        for s in range(c):
            if t // SUB == s // SUB and s <= t:
                m[0, nblk - 1, t, s] = 1.0
    m[1] = m[0][:, ::-1, ::-1]
    sel = np.zeros((SUB * DK_A, c), np.float32)
    for i in range(SUB):
        sel[i * DK_A : (i + 1) * DK_A, i::SUB] = 1.0
    return w, m, sel


def _hgrn_chunk(direction, r0, qa_ref, gate_ref, ia_ref, lb, wcum, masks, sel, st_ref):
    rows = pl.ds(r0, CHUNK)
    q = _silu(qa_ref[0, rows, :])
    v = ia_ref[0, rows, :]
    f = lb + (1.0 - lb) * jax.nn.sigmoid(gate_ref[0, rows, :])
    lf = jnp.log(f)
    kk = 1.0 - f
    lf_hi, lf_lo = _split_bf16(lf)
    z = _dot(wcum, lf_hi) + _dot(wcum, lf_lo)
    e = z[0:CHUNK]
    edge = e[CHUNK - 1 : CHUNK] if direction == 0 else e[0:1]
    q_in = (q * jnp.exp(e)).astype(BF16)
    k_st = (kk * jnp.exp(edge - e)).astype(BF16)
    st_decay = jnp.exp(edge)
    q_lv, k_lv = [], []
    for li in range(len(LEVELS)):
        ez = jnp.exp(z[(li + 1) * CHUNK : (li + 2) * CHUNK])
        q_lv.append((q * ez).astype(BF16))
        k_lv.append((kk * ez).astype(BF16))
    nsub = CHUNK // SUB
    kk3 = kk.reshape(nsub, SUB, W_A)
    e3 = e.reshape(nsub, SUB, W_A)
    pair = []
    for i in range(SUB):
        kb = jnp.broadcast_to(kk3[:, i : i + 1, :], (nsub, SUB, W_A)).reshape(CHUNK, W_A)
        eb = jnp.broadcast_to(e3[:, i : i + 1, :], (nsub, SUB, W_A)).reshape(CHUNK, W_A)
        pair.append((q * kb * jnp.exp(jnp.minimum(e - eb, 0.0))).astype(BF16))
    vb = v.astype(BF16)
    outs = []
    for h in range(H_A):
        hs = slice(h * DK_A, (h + 1) * DK_A)
        sc = _dot(jnp.concatenate([p[:, hs] for p in pair], axis=1), sel) * masks[len(LEVELS)]
        for li in range(len(LEVELS)):
            sc = sc + _dot_nt(q_lv[li][:, hs], k_lv[li][:, hs]) * masks[li]
        st = st_ref[direction, h]
        o_h = _dot(sc.astype(BF16), vb[:, hs]) + _dot_nt(q_in[:, hs], st.astype(BF16))
        st_ref[direction, h] = st_decay[:, hs] * st + _dot_tn(vb[:, hs], k_st[:, hs])
        outs.append(o_h)
    return jnp.concatenate(outs, axis=1)


def _hgrn_kernel(qa_ref, ff_ref, fb_ref, ia_ref, ga_ref, lb_ref, ng_ref, s0_ref, w_ref, m_ref,
                 sel_ref, o_ref, sout_ref, st_ref, *, seq_len):
    nchunk = seq_len // CHUNK
    st_ref[...] = s0_ref[0]
    lb = lb_ref[...]
    sel = sel_ref[...]

    def fwd(i, carry):
        r0 = pl.multiple_of(i * CHUNK, CHUNK)
        masks = [m_ref[0, j] for j in range(1 + len(LEVELS))]
        o = _hgrn_chunk(0, r0, qa_ref, ff_ref, ia_ref, lb, w_ref[0], masks, sel, st_ref)
        o_ref[0, pl.ds(r0, CHUNK), :] = o
        return carry

    lax.fori_loop(0, nchunk, fwd, 0)

    ng = jnp.concatenate([ng_ref[...]] * H_A, axis=1)

    def bwd(i, carry):
        r0 = pl.multiple_of((nchunk - 1 - i) * CHUNK, CHUNK)
        masks = [m_ref[1, j] for j in range(1 + len(LEVELS))]
        o = _hgrn_chunk(1, r0, qa_ref, fb_ref, ia_ref, lb, w_ref[1], masks, sel, st_ref)
        tot = o + o_ref[0, pl.ds(r0, CHUNK), :]
        normed = jnp.concatenate(
            [_rms(tot[:, h * DK_A : (h + 1) * DK_A]) for h in range(H_A)], axis=1)
        o_ref[0, pl.ds(r0, CHUNK), :] = normed * ng * _silu(ga_ref[0, pl.ds(r0, CHUNK), :])
        return carry

    lax.fori_loop(0, nchunk, bwd, 0)
    sout_ref[0] = st_ref[...]


def _hgrn(qa, ff, fb, ia, ga, lb, ng, s0t, seq_len):
    bsz = qa.shape[0]
    w, m, sel = _hgrn_constants()
    seq = pl.BlockSpec((1, seq_len, W_A), lambda b: (b, 0, 0))
    full = lambda shape: pl.BlockSpec(shape, lambda b: (0,) * len(shape))
    st_spec = pl.BlockSpec((1, 2, H_A, DK_A, DK_A), lambda b: (b, 0, 0, 0, 0))
    return pl.pallas_call(
        functools.partial(_hgrn_kernel, seq_len=seq_len),
        grid=(bsz,),
        in_specs=[seq, seq, seq, seq, seq, full((1, W_A)), full((1, DK_A)), st_spec,
                  full(w.shape), full(m.shape), full(sel.shape)],
        out_specs=[seq, st_spec],
        out_shape=[jax.ShapeDtypeStruct((bsz, seq_len, W_A), F32),
                   jax.ShapeDtypeStruct((bsz, 2, H_A, DK_A, DK_A), F32)],
        scratch_shapes=[pltpu.VMEM((2, H_A, DK_A, DK_A), F32)],
        compiler_params=_cparams("arbitrary"),
        name="hgrn",
    )(qa, ff, fb, ia, ga, lb.reshape(1, W_A), ng.reshape(1, DK_A), s0t,
      jnp.asarray(w, BF16), jnp.asarray(m, F32), jnp.asarray(sel, BF16))


def _dft_constants(seq_len):
    n = 2 * seq_len
    t = np.arange(seq_len, dtype=np.int64)
    wt = (np.arange(seq_len, dtype=np.int64)[:, None] * t[None, :]) % n
    ang = 2.0 * np.pi * wt.astype(np.float64) / n
    cos, sin = np.cos(ang), np.sin(ang)
    nyq = np.where(t % 2 == 0, 1.0, -1.0)
    sin_p = sin.copy()
    sin_p[0] = nyq
    fwd = np.concatenate([cos, sin_p], axis=0)
    icos = 2.0 * cos.T / n
    icos[:, 0] = 1.0 / n
    isin = 2.0 * sin.T / n
    isin[:, 0] = nyq / n
    inv = np.concatenate([icos, isin], axis=1)
    return fwd, inv


def _filter_embedding(seq_len):
    t = np.linspace(0.0, 1.0, seq_len)[:, None]
    w = 2.0 * np.pi * np.arange(seq_len) / seq_len
    f = np.linspace(1e-4, FILTER_BANDS - 1, FILTER_BANDS)
    ang = w[:, None] * f[None, :]
    z = np.concatenate([t, np.cos(ang), -np.sin(ang)], axis=-1)
    zp = np.zeros((seq_len, LANES), np.float64)
    zp[:, : z.shape[1]] = z
    deltas = np.abs(np.linspace(DECAY_MIN, DECAY_MAX, C_B))
    window = np.exp(-t * deltas[None, :])
    return zp.astype(np.float32), window.astype(np.float32)


def _hyena_filter_kernel(z_ref, win_ref, w1_ref, b1_ref, fr1_ref, w2_ref, b2_ref, fr2_ref, w3_ref,
                         hb_ref, fh_ref, fl_ref, gr_ref, gi_ref, *, seq_len):
    h = jnp.sin(fr1_ref[...] * (_dot3(z_ref[...], w1_ref[...]) + b1_ref[...]))
    h = jnp.sin(fr2_ref[...] * (_dot3(h, w2_ref[...]) + b2_ref[...]))
    h = _dot3(h, w3_ref[...])
    win = win_ref[...]
    hf = h[:, :C_B] * win
    hbk = h[:, C_B:] * win

    def dft(x):
        xh, xl = _split_bf16(x)
        return _dot(fh_ref[...], xh) + _dot(fh_ref[...], xl) + _dot(fl_ref[...], xh)

    p_sum = dft(hf + hbk)
    p_dif = dft(hf - hbk)
    row0 = lax.broadcasted_iota(jnp.int32, (seq_len, C_B), 0) == 0
    gr_ref[...] = p_sum[:seq_len] + hb_ref[...]
    gi_ref[...] = jnp.where(row0, p_sum[seq_len:] + hb_ref[...], p_dif[seq_len:])


def _hyena_filter(seq_len, w1, b1, fr1, w2, b2, fr2, w3, hbias):
    zemb, window = _filter_embedding(seq_len)
    fwd, _ = _dft_constants(seq_len)
    f_hi, f_lo = _split_bf16(jnp.asarray(fwd, F32))

    def pad(a, rows, cols):
        return jnp.zeros((rows, cols), F32).at[: a.shape[0], : a.shape[1]].set(a)

    args = (jnp.asarray(zemb), jnp.asarray(window), pad(w1, LANES, LANES), pad(b1[None], 1, LANES),
            pad(fr1[None], 1, LANES), pad(w2, LANES, LANES), pad(b2[None], 1, LANES),
            pad(fr2[None], 1, LANES), pad(w3, LANES, 2 * C_B), hbias.reshape(1, C_B), f_hi, f_lo)
    return pl.pallas_call(
        functools.partial(_hyena_filter_kernel, seq_len=seq_len),
        out_shape=[jax.ShapeDtypeStruct((seq_len, C_B), F32)] * 2,
        compiler_params=pltpu.CompilerParams(vmem_limit_bytes=VMEM_LIMIT_BYTES),
        name="hyena_filter",
    )(*args)


def _hyena_kernel(x0_ref, x1_ref, v_ref, cw_ref, cb_ref, gr_ref, gi_ref, f_ref, fi_ref, o_ref, *,
                  seq_len):
    tc = o_ref.shape[2]
    row = lax.broadcasted_iota(jnp.int32, (seq_len, tc), 0)

    def short_conv(u_ref, j):
        u = u_ref[0]
        prev = jnp.where(row == 0, 0.0, pltpu.roll(u, 1, 0))
        nxt = jnp.where(row == seq_len - 1, 0.0, pltpu.roll(u, seq_len - 1, 0))
        cw = cw_ref[j]
        return cw[0:1] * prev + cw[1:2] * u + cw[2:3] * nxt + cb_ref[j]

    x0 = short_conv(x0_ref, 0)
    z = short_conv(v_ref, 2) * short_conv(x1_ref, 1)
    p = _dot(f_ref[...], z.astype(BF16))
    a, b = p[:seq_len], p[seq_len:]
    gr, gi = gr_ref[...], gi_ref[...]
    row0 = row == 0
    bgi = b * gi
    yr = a * gr - jnp.where(row0, 0.0, bgi)
    yq = jnp.where(row0, bgi, a * gi + b * gr)
    y = _dot(fi_ref[...], jnp.concatenate([yr, yq], axis=0).astype(BF16))
    o_ref[0] = y * x0


def _hyena(hy, conv_w, conv_b, gr, gi, seq_len, tc=256):
    bsz = hy.shape[0]
    nct = C_B // tc
    fwd, inv = _dft_constants(seq_len)
    cw = conv_w.reshape(3, 3, C_B).transpose(1, 0, 2)
    cb = conv_b.reshape(3, 1, C_B)
    part = lambda k: pl.BlockSpec((1, seq_len, tc), lambda b, j, k=k: (b, 0, k * nct + j))
    return pl.pallas_call(
        functools.partial(_hyena_kernel, seq_len=seq_len),
        grid=(bsz, nct),
        in_specs=[part(0), part(1), part(2),
                  pl.BlockSpec((3, 3, tc), lambda b, j: (0, 0, j)),
                  pl.BlockSpec((3, 1, tc), lambda b, j: (0, 0, j)),
                  pl.BlockSpec((seq_len, tc), lambda b, j: (0, j)),
                  pl.BlockSpec((seq_len, tc), lambda b, j: (0, j)),
                  pl.BlockSpec((2 * seq_len, seq_len), lambda b, j: (0, 0)),
                  pl.BlockSpec((seq_len, 2 * seq_len), lambda b, j: (0, 0))],
        out_specs=pl.BlockSpec((1, seq_len, tc), lambda b, j: (b, 0, j)),
        out_shape=jax.ShapeDtypeStruct((bsz, seq_len, C_B), F32),
        compiler_params=_cparams("arbitrary", "arbitrary"),
        name="hyena",
    )(hy, hy, hy, cw, cb, gr, gi, jnp.asarray(fwd, F32).astype(BF16),
      jnp.asarray(inv, F32).astype(BF16))


def _rope_tables(seq_len, dim):
    rows = seq_len // GRID_W
    row_idx = np.repeat(np.arange(rows), GRID_W).astype(np.float64)
    col_idx = np.tile(np.arange(GRID_W), rows).astype(np.float64)
    half = dim // 2
    inv = ROPE_THETA ** (-np.arange(0, half, 2, dtype=np.float64) / half)
    ang = np.concatenate([row_idx[:, None] * inv, col_idx[:, None] * inv], axis=-1)
    cos = np.repeat(np.cos(ang), 2, axis=1)
    sin = np.repeat(np.sin(ang), 2, axis=1)
    sin[:, 0::2] *= -1.0
    reps = LANES // dim
    return (np.tile(cos, (1, reps)).astype(np.float32), np.tile(sin, (1, reps)).astype(np.float32))


def _qk_prep_kernel(p_ref, cqg_ref, ckg_ref, dqg_ref, dkg_ref, *rest, use_rope):
    if use_rope:
        cc_ref, sc_ref, cd_ref, sd_ref = rest[:4]
        rest = rest[4:]
    qc_ref, qd_ref, kc_ref, vc_ref, kd_ref, vd_ref = rest
    p = p_ref[0]
    tm = p.shape[0]
    lane = lax.broadcasted_iota(jnp.int32, (tm, LANES), 1)
    even = (lane % 2) == 0
    low = lane < DK_D

    def rope(x, cos, sin):
        swapped = jnp.where(even, pltpu.roll(x, LANES - 1, 1), pltpu.roll(x, 1, 1))
        return x * cos + swapped * sin

    def norm_c(x, g_ref):
        y = _rms(x) * g_ref[...]
        return rope(y, cc_ref[...], sc_ref[...]) if use_rope else y

    def norm_d(x, g_ref):
        sq = x * x
        s_lo = jnp.sum(jnp.where(low, sq, 0.0), axis=1, keepdims=True)
        s_hi = jnp.sum(jnp.where(low, 0.0, sq), axis=1, keepdims=True)
        ms = jnp.where(low, s_lo, s_hi) * (1.0 / DK_D)
        y = x * lax.rsqrt(ms + EPS) * g_ref[...]
        return rope(y, cd_ref[...], sd_ref[...]) if use_rope else y

    blk = lambda i: p[:, i * LANES : (i + 1) * LANES]
    qc_ref[0] = jnp.concatenate([norm_c(blk(i), cqg_ref) for i in range(H_C)], axis=1)
    for j in range(KV_C):
        kc_ref[0, j] = norm_c(blk(H_C + j), ckg_ref)
        vc_ref[0, j] = blk(H_C + KV_C + j)
    base = H_C + 2 * KV_C
    qd_ref[0] = jnp.concatenate([norm_d(blk(base + i), dqg_ref) for i in range(H_D)], axis=1)
    for j in range(H_D):
        kd_ref[0, j] = norm_d(blk(base + H_D + j), dkg_ref)
        vd_ref[0, j] = blk(base + 2 * H_D + j)


def _qk_prep(p, cqg, ckg, dqg, dkg, seq_len, use_rope, tm=256):
    ngrp = p.shape[0]
    bsz = ngrp * ROW_GROUP // seq_len
    p = p.reshape(bsz, seq_len, p.shape[2])
    tm = min(tm, seq_len)
    vec = lambda: pl.BlockSpec((1, LANES), lambda b, i: (0, 0))
    tab = lambda: pl.BlockSpec((tm, LANES), lambda b, i: (i, 0))
    in_specs = [pl.BlockSpec((1, tm, p.shape[2]), lambda b, i: (b, i, 0)), vec(), vec(), vec(), vec()]
    args = [p, cqg.reshape(1, HD_C), ckg.reshape(1, HD_C),
            jnp.tile(dqg.reshape(1, DK_D), (1, 2)), jnp.tile(dkg.reshape(1, DK_D), (1, 2))]
    if use_rope:
        in_specs += [tab(), tab(), tab(), tab()]
        args += [jnp.asarray(a) for a in _rope_tables(seq_len, HD_C) + _rope_tables(seq_len, DK_D)]
    tok = lambda w: pl.BlockSpec((1, tm, w), lambda b, i: (b, i, 0))
    head = lambda nh: pl.BlockSpec((1, nh, tm, LANES), lambda b, i: (b, 0, i, 0))
    tok_shape = lambda w: jax.ShapeDtypeStruct((bsz, seq_len, w), F32)
    head_shape = lambda nh: jax.ShapeDtypeStruct((bsz, nh, seq_len, LANES), F32)
    return pl.pallas_call(
        functools.partial(_qk_prep_kernel, use_rope=use_rope),
        grid=(bsz, seq_len // tm),
        in_specs=in_specs,
        out_specs=[tok(H_C * HD_C), tok(H_D * 2 * DK_D), head(KV_C), head(KV_C), head(H_D), head(H_D)],
        out_shape=[tok_shape(H_C * HD_C), tok_shape(H_D * 2 * DK_D), head_shape(KV_C),
                   head_shape(KV_C), head_shape(H_D), head_shape(H_D)],
        compiler_params=_cparams("arbitrary", "arbitrary"),
        name="qk_prep",
    )(*args)


def _softmax_pv(q_list, kv_list):
    outs = []
    for q in q_list:
        scores = [_dot_nt(q, k) for k, _ in kv_list]
        mx = scores[0].max(axis=1, keepdims=True)
        for s in scores[1:]:
            mx = jnp.maximum(mx, s.max(axis=1, keepdims=True))
        den = 0.0
        acc = 0.0
        for s, (_, v) in zip(scores, kv_list):
            pexp = jnp.exp(s - mx)
            den = den + pexp.sum(axis=1, keepdims=True)
            acc = acc + _dot(pexp.astype(BF16), v)
        outs.append(acc / den)
    return outs


def _gqa_kernel(q_ref, k_ref, v_ref, *rest, has_cache):
    if has_cache:
        ck_ref, cv_ref, o_ref = rest
        kv = [(ck_ref[0, 0, 0].astype(BF16), cv_ref[0, 0, 0].astype(BF16))]
    else:
        (o_ref,) = rest
        kv = []
    kv.append((k_ref[0, 0].astype(BF16), v_ref[0, 0].astype(BF16)))
    q = q_ref[0] * (HD_C ** -0.5)
    g_c = H_C // KV_C
    qs = [q[:, g * HD_C : (g + 1) * HD_C].astype(BF16) for g in range(g_c)]
    o_ref[0] = jnp.concatenate(_softmax_pv(qs, kv), axis=1)


def _diff_kernel(q_ref, k_ref, v_ref, lam_ref, sg_ref, *rest, has_cache, out_scale):
    if has_cache:
        ck_ref, cv_ref, o_ref = rest
        kv = [(ck_ref[0, 0, 0].astype(BF16), cv_ref[0, 0, 0].astype(BF16))]
    else:
        (o_ref,) = rest
        kv = []
    kv.append((k_ref[0, 0].astype(BF16), v_ref[0, 0].astype(BF16)))
    q = q_ref[0] * (DK_D ** -0.5)
    low = lax.broadcasted_iota(jnp.int32, q.shape, 1) < DK_D
    qs = [jnp.where(low, q, 0.0).astype(BF16), jnp.where(low, 0.0, q).astype(BF16)]
    o1, o2 = _softmax_pv(qs, kv)
    o_ref[0] = _rms(o1 - lam_ref[...] * o2) * sg_ref[...] * out_scale


def _attention(q, k, v, cache_k, cache_v, kernel, extra_args, n_heads, q_width, seq_len, tq=256):
    bsz = q.shape[0]
    tq = min(tq, seq_len)
    in_specs = [pl.BlockSpec((1, tq, q_width), lambda b, h, i: (b, i, h)),
                pl.BlockSpec((1, 1, seq_len, LANES), lambda b, h, i: (b, h, 0, 0)),
                pl.BlockSpec((1, 1, seq_len, LANES), lambda b, h, i: (b, h, 0, 0))]
    in_specs += [pl.BlockSpec((1, LANES), lambda b, h, i: (0, 0)) for _ in extra_args]
    args = [q, k, v, *extra_args]
    if cache_k is not None:
        past = cache_k.shape[3]
        spec = lambda: pl.BlockSpec((1, 1, 1, past, LANES), lambda b, h, i: (b, 0, h, 0, 0))
        in_specs += [spec(), spec()]
        args += [cache_k, cache_v]
    return pl.pallas_call(
        functools.partial(kernel, has_cache=cache_k is not None),
        grid=(bsz, n_heads, seq_len // tq),
        in_specs=in_specs,
        out_specs=pl.BlockSpec((1, tq, q_width), lambda b, h, i: (b, i, h)),
        out_shape=jax.ShapeDtypeStruct((bsz, seq_len, n_heads * q_width), F32),
        compiler_params=_cparams("arbitrary", "arbitrary", "arbitrary"),
        name="attention",
    )(*args)


def _route(lg):
    lane = lax.broadcasted_iota(jnp.int32, lg.shape, 1).astype(F32)
    neg = -1e30
    is_g = (lane >= N_EXPERTS) & (lane < N_EXPERTS + N_GROUPS)
    gl = jnp.where(is_g, lg, neg)
    gmax = gl.max(axis=1, keepdims=True)
    g_p = 1.0 / jnp.where(is_g, jnp.exp(gl - gmax), 0.0).sum(axis=1, keepdims=True)
    g_i = jnp.where(gl == gmax, lane - N_EXPERTS, 1e9).min(axis=1, keepdims=True)
    in_grp = (lane < N_EXPERTS) & (jnp.floor(lane * (1.0 / EXP_PER_GROUP)) == g_i)
    el = jnp.where(in_grp, lg, neg)
    m1 = el.max(axis=1, keepdims=True)
    i1 = jnp.where(in_grp & (el == m1), lane, 1e9).min(axis=1, keepdims=True)
    el2 = jnp.where(lane == i1, neg, el)
    m2 = el2.max(axis=1, keepdims=True)
    i2 = jnp.where(in_grp & (el2 == m2) & (lane != i1), lane, 1e9).min(axis=1, keepdims=True)
    r = jnp.exp(m2 - m1)
    w1 = g_p / (1.0 + r)
    rec = jnp.where(lane == ROUTE_LANE, i1, 0.0)
    rec = jnp.where(lane == ROUTE_LANE + 1, i2, rec)
    rec = jnp.where(lane == ROUTE_LANE + 2, w1, rec)
    return jnp.where(lane == ROUTE_LANE + 3, w1 * r, rec)


def _mix_out_kernel(x_ref, a_ref, b_ref, m_ref, g2_ref, w_ref, wrh_ref, wrl_ref, br_ref,
                    x1_ref, h2_ref, gates_ref):
    wa = a_ref.shape[2]
    o = _dot(a_ref[0].astype(BF16), w_ref[:wa]) + _dot(b_ref[0].astype(BF16), w_ref[wa:])
    m = m_ref[0]
    x1 = x_ref[0] + m[2:3] * o
    x1_ref[0] = x1
    h2 = _rms(x1) * g2_ref[...] * (1.0 + m[4:5]) + m[3:4]
    h2_ref[0] = h2.astype(BF16)
    hi, lo = _split_bf16(h2)
    lg = _dot(hi, wrh_ref[...]) + _dot(lo, wrh_ref[...]) + _dot(hi, wrl_ref[...]) + br_ref[...]
    gates_ref[0] = _route(lg)


def _mix_out(x, a, b, mods, g2, w_out_bf16, w_router, b_router, tm=256):
    ngrp = x.shape[0]
    wa, wb = a.shape[2], b.shape[2]
    wr_hi = w_router.astype(BF16)
    wr_lo = (w_router - wr_hi.astype(F32)).astype(BF16)
    row = lambda w: pl.BlockSpec((1, tm, w), lambda g, i: (g, i, 0))
    full = lambda shape: pl.BlockSpec(shape, lambda g, i: (0,) * len(shape))
    return pl.pallas_call(
        _mix_out_kernel,
        grid=(ngrp, ROW_GROUP // tm),
        in_specs=[row(D_MODEL), row(wa), row(wb),
                  pl.BlockSpec((1, 6, D_MODEL), lambda g, i: (g, 0, 0)),
                  full((1, D_MODEL)), full((wa + wb, D_MODEL)),
                  full((D_MODEL, LANES)), full((D_MODEL, LANES)), full((1, LANES))],
        out_specs=[row(D_MODEL), row(D_MODEL), row(LANES)],
        out_shape=[jax.ShapeDtypeStruct((ngrp, ROW_GROUP, D_MODEL), F32),
                   jax.ShapeDtypeStruct((ngrp, ROW_GROUP, D_MODEL), BF16),
                   jax.ShapeDtypeStruct((ngrp, ROW_GROUP, LANES), F32)],
        compiler_params=_cparams("arbitrary", "arbitrary"),
        name="mix_out",
    )(x, a, b, mods, g2.reshape(1, D_MODEL), w_out_bf16, wr_hi, wr_lo, b_router)


def _lane_col(x, lane, k):
    return jnp.where(lane == k, x, 0.0).sum(axis=1, keepdims=True)


def _moe_dispatch_kernel(h_ref, r_ref, tri_ref, xs_ref, pos_ref, cnt_ref):
    r = r_ref[0]
    lane = lax.broadcasted_iota(jnp.int32, r.shape, 1).astype(F32)
    i1, i2, w1, w2 = [_lane_col(r, lane, ROUTE_LANE + k) for k in range(4)]
    oh1 = lane == i1
    oh2 = lane == i2
    oh = jnp.where(oh1 | oh2, 1.0, 0.0)
    rank = _dot(tri_ref[...], oh.astype(BF16))
    cnt = oh.sum(axis=0, keepdims=True)
    chunks = jnp.floor((cnt + (MOE_CHUNK - 1)) * (1.0 / MOE_CHUNK))
    li = lax.broadcasted_iota(jnp.int32, (LANES, LANES), 0)
    lj = lax.broadcasted_iota(jnp.int32, (LANES, LANES), 1)
    before = jnp.where(li < lj, 1.0, 0.0).astype(BF16)
    seg = _dot(jnp.broadcast_to(chunks, (8, LANES)).astype(BF16), before)[0:1]
    base = seg * MOE_CHUNK + rank
    pos1 = jnp.where(oh1, base, 0.0).sum(axis=1, keepdims=True)
    pos2 = jnp.where(oh2, base, 0.0).sum(axis=1, keepdims=True)
    riota = lax.broadcasted_iota(jnp.int32, (r.shape[0], MOE_TILE_ROWS), 1).astype(F32)
    p1 = jnp.where(riota == pos1, 1.0, 0.0).astype(BF16)
    p2 = jnp.where(riota == pos2, 1.0, 0.0).astype(BF16)

    def pieces(w):
        hi = w.astype(BF16).astype(F32)
        mid = (w - hi).astype(BF16).astype(F32)
        lo = w - hi - mid
        return jnp.where(lane == 0, hi, jnp.where(lane == 1, mid, jnp.where(lane == 2, lo, 0.0))).astype(BF16)

    xs_ref[:, :D_MODEL] = _dot_tn(p1 + p2, h_ref[0]).astype(BF16)
    xs_ref[:, D_MODEL:] = (_dot_tn(p1, pieces(w1)) + _dot_tn(p2, pieces(w2))).astype(BF16)
    pos_ref[0] = jnp.where(lane == 0, pos1, jnp.where(lane == 1, pos2, 0.0))
    cnt_ref[0] = jnp.broadcast_to(cnt, (8, LANES))


def _moe_dispatch(h2, route):
    ngrp = h2.shape[0]
    per = ROW_GROUP // MOE_TILE
    ntile = ngrp * per
    tri = np.tril(np.ones((MOE_TILE, MOE_TILE), np.float32), -1)
    tok = lambda w: pl.BlockSpec((1, MOE_TILE, w), lambda j: (j // per, j % per, 0))
    return pl.pallas_call(
        _moe_dispatch_kernel,
        grid=(ntile,),
        in_specs=[tok(D_MODEL), tok(LANES), pl.BlockSpec((MOE_TILE, MOE_TILE), lambda j: (0, 0))],
        out_specs=[pl.BlockSpec((MOE_TILE_ROWS, MOE_ROW_W), lambda j: (j, 0)),
                   pl.BlockSpec((1, MOE_TILE, LANES), lambda j: (j, 0, 0)),
                   pl.BlockSpec((1, 8, LANES), lambda j: (j, 0, 0))],
        out_shape=[jax.ShapeDtypeStruct((ntile * MOE_TILE_ROWS, MOE_ROW_W), BF16),
                   jax.ShapeDtypeStruct((ntile, MOE_TILE, LANES), F32),
                   jax.ShapeDtypeStruct((ntile, 8, LANES), F32)],
        compiler_params=_cparams("arbitrary"),
        name="moe_dispatch",
    )(h2, route, jnp.asarray(tri, BF16))


def _moe_tables(cnt, ntile):
    nblk = ntile * MOE_TILE_CHUNKS // MOE_BLOCK_CHUNKS + N_EXPERTS
    chunks = (cnt + MOE_CHUNK - 1) // MOE_CHUNK
    seg_start = jnp.cumsum(chunks, axis=1) - chunks
    tile_prefix = jnp.cumsum(chunks, axis=0) - chunks
    per_expert = chunks.sum(axis=0)
    blocks = (per_expert + MOE_BLOCK_CHUNKS - 1) // MOE_BLOCK_CHUNKS
    blk_end = jnp.cumsum(blocks)
    n_used = blk_end[-1]
    b = jnp.arange(nblk, dtype=jnp.int32)
    blk_e = jnp.sum(b[:, None] >= blk_end[None, :], axis=1).astype(jnp.int32)
    last_e = jnp.sum((n_used - 1) >= blk_end).astype(jnp.int32)
    blk_e = jnp.where(b < n_used, blk_e, last_e)
    blk_e = jnp.clip(blk_e, 0, N_EXPERTS - 1)
    k = (b - (blk_end - blocks)[blk_e])[:, None] * MOE_BLOCK_CHUNKS + jnp.arange(MOE_BLOCK_CHUNKS)[None, :]
    valid = (k < per_expert[blk_e][:, None]) & (b < n_used)[:, None]
    incl = (tile_prefix + chunks).T[blk_e]
    tile = jnp.clip(jnp.sum(incl[:, None, :] <= k[:, :, None], axis=-1), 0, ntile - 1)
    e2 = jnp.broadcast_to(blk_e[:, None], tile.shape)
    src = tile * MOE_TILE_CHUNKS + seg_start[tile, e2] + (k - tile_prefix[tile, e2])
    src = jnp.where(valid, src, -1).astype(jnp.int32)
    return blk_e, src.reshape(-1), n_used.astype(jnp.int32).reshape(1), nblk


def _moe_expert_kernel(blk_e_ref, src_ref, nused_ref, xs_hbm, wg_ref, wu_ref, wd_ref, ys_init_hbm,
                       ys_hbm, lhs, obuf, wgb, wub, wdb, in_sem, out_sem):
    del ys_init_hbm
    b = pl.program_id(0)
    nb = pl.num_programs(0)
    n = nused_ref[0]
    rows_per_blk = MOE_BLOCK_CHUNKS * MOE_CHUNK

    def chunk_copy(blk, slot, c, gather):
        src = src_ref[blk * MOE_BLOCK_CHUNKS + c]
        rows = pl.ds(c * MOE_CHUNK, MOE_CHUNK)
        idx = jnp.maximum(src, 0)
        if gather:
            cp = pltpu.make_async_copy(xs_hbm.at[idx], lhs.at[slot, rows], in_sem.at[slot])
        else:
            cp = pltpu.make_async_copy(obuf.at[slot, rows], ys_hbm.at[idx], out_sem.at[slot])
        return src >= 0, cp

    def for_chunks(blk, slot, gather, start):
        for c in range(MOE_BLOCK_CHUNKS):
            valid, cp = chunk_copy(blk, slot, c, gather)

            @pl.when(valid)
            def _(cp=cp):
                if start:
                    cp.start()
                else:
                    cp.wait()

    @pl.when(b == 0)
    def _():
        lhs[...] = jnp.zeros_like(lhs)

        @pl.when(n > 0)
        def _():
            for_chunks(0, 0, True, True)

    @pl.when(b < n)
    def _():
        slot = b % 2
        for_chunks(b, slot, True, False)

        @pl.when(b + 1 < n)
        def _():
            for_chunks(b + 1, 1 - slot, True, True)

        @pl.when((b == 0) | (blk_e_ref[b] != blk_e_ref[jnp.maximum(b - 1, 0)]))
        def _():
            wgb[...] = wg_ref[0, 0].astype(BF16)
            wub[...] = wu_ref[0, 0].astype(BF16)
            wdb[...] = wd_ref[0, 0].astype(BF16)

        @pl.when(b >= 2)
        def _():
            for_chunks(b - 2, slot, False, False)

        xa = lhs[slot]
        x = xa[:, :D_MODEL]
        w = xa[:, D_MODEL:].astype(F32).sum(axis=1, keepdims=True)
        hid = _silu(_dot(x, wgb[...])) * _dot(x, wub[...]) * w
        obuf[slot] = _dot(hid.astype(BF16), wdb[...]).astype(BF16)
        for_chunks(b, slot, False, True)

    @pl.when(b == nb - 1)
    def _():
        @pl.when(n >= 2)
        def _():
            for_chunks(n - 2, n % 2, False, False)

        @pl.when(n >= 1)
        def _():
            for_chunks(n - 1, (n - 1) % 2, False, False)


def _moe_experts(xs, blk_e, src, n_used, nblk, layer, w_gate, w_up, w_down):
    nchunk = xs.shape[0] // MOE_CHUNK
    rows_per_blk = MOE_BLOCK_CHUNKS * MOE_CHUNK
    wspec = lambda shape: pl.BlockSpec((1, 1) + shape, lambda b, be, s, n: (layer, be[b], 0, 0))
    hbm = pl.BlockSpec(memory_space=pl.ANY)
    ys = pl.pallas_call(
        _moe_expert_kernel,
        grid_spec=pltpu.PrefetchScalarGridSpec(
            num_scalar_prefetch=3,
            grid=(nblk,),
            in_specs=[hbm, wspec((D_MODEL, D_EXPERT)), wspec((D_MODEL, D_EXPERT)),
                      wspec((D_EXPERT, D_MODEL)), hbm],
            out_specs=hbm,
            scratch_shapes=[pltpu.VMEM((2, rows_per_blk, MOE_ROW_W), BF16),
                            pltpu.VMEM((2, rows_per_blk, D_MODEL), BF16),
                            pltpu.VMEM((D_MODEL, D_EXPERT), BF16),
                            pltpu.VMEM((D_MODEL, D_EXPERT), BF16),
                            pltpu.VMEM((D_EXPERT, D_MODEL), BF16),
                            pltpu.SemaphoreType.DMA((2,)),
                            pltpu.SemaphoreType.DMA((2,))]),
        out_shape=jax.ShapeDtypeStruct((nchunk, MOE_CHUNK, D_MODEL), BF16),
        input_output_aliases={7: 0},
        compiler_params=_cparams("arbitrary"),
        name="moe_experts",
    )(blk_e, src, n_used, xs.reshape(nchunk, MOE_CHUNK, MOE_ROW_W), w_gate, w_up, w_down,
      jnp.zeros((nchunk, MOE_CHUNK, D_MODEL), BF16))
    return ys.reshape(nchunk * MOE_CHUNK, D_MODEL)


def _moe_combine_kernel(ys_ref, pos_ref, x_ref, m_ref, o_ref):
    pos = pos_ref[0]
    lane = lax.broadcasted_iota(jnp.int32, pos.shape, 1)
    pos1 = _lane_col(pos, lane, 0)
    pos2 = _lane_col(pos, lane, 1)
    riota = lax.broadcasted_iota(jnp.int32, (pos.shape[0], MOE_TILE_ROWS), 1).astype(F32)
    p = jnp.where((riota == pos1) | (riota == pos2), 1.0, 0.0).astype(BF16)
    o_ref[0] = x_ref[0] + m_ref[0][5:6] * _dot(p, ys_ref[...])


def _moe_combine(ys, pos, x1, mods):
    ngrp = x1.shape[0]
    per = ROW_GROUP // MOE_TILE
    tok = pl.BlockSpec((1, MOE_TILE, D_MODEL), lambda j: (j // per, j % per, 0))
    return pl.pallas_call(
        _moe_combine_kernel,
        grid=(ngrp * per,),
        in_specs=[pl.BlockSpec((MOE_TILE_ROWS, D_MODEL), lambda j: (j, 0)),
                  pl.BlockSpec((1, MOE_TILE, LANES), lambda j: (j, 0, 0)),
                  tok, pl.BlockSpec((1, 6, D_MODEL), lambda j: (j // per, 0, 0))],
        out_specs=tok,
        out_shape=jax.ShapeDtypeStruct(x1.shape, F32),
        compiler_params=_cparams("arbitrary"),
        name="moe_combine",
    )(ys, pos, x1, mods)


def _moe(h2, route, x1, mods, layer, w_gate, w_up, w_down):
    ntile = h2.shape[0] * (ROW_GROUP // MOE_TILE)
    xs, pos, cnt = _moe_dispatch(h2, route)
    cnt = cnt[:, 0, :N_EXPERTS].astype(jnp.int32)
    blk_e, src, n_used, nblk = _moe_tables(cnt, ntile)
    ys = _moe_experts(xs, blk_e, src, n_used, nblk, layer, w_gate, w_up, w_down)
    return _moe_combine(ys, pos, x1, mods)


def kernel(x_prompt, x_sample, state_hgrn, cache_c_k, cache_c_v, cache_d_k, cache_d_v, c, c_ctx, norm1_g, norm2_g, w_mod, b_mod, even_w_in, even_w_out, hgrn_lower, hgrn_norm_g, hy_conv_w, hy_conv_b, hy_w1, hy_b1, hy_freq1, hy_w2, hy_b2, hy_freq2, hy_w3, hy_bias, odd_w_in, odd_w_out, c_qnorm_g, c_knorm_g, d_qnorm_g, d_knorm_g, d_lambda_q1, d_lambda_k1, d_lambda_q2, d_lambda_k2, d_subln_g, moe_w_grp, moe_b_grp, moe_w_rt, moe_b_rt, moe_w_gate, moe_w_up, moe_w_down):
    depth = w_mod.shape[0]
    n_ctx, seq, _ = x_prompt.shape
    n_lat, dec_seq, _ = x_sample.shape
    g_ctx = n_ctx * seq // ROW_GROUP
    g_lat = n_lat * dec_seq // ROW_GROUP
    assert dec_seq == ROW_GROUP and ROW_GROUP % seq == 0

    cond = jnp.zeros((16, D_MODEL), F32).at[0].set(c_ctx).at[1 : 1 + n_lat].set(c)
    mods = _adaln(cond, w_mod, b_mod).reshape(depth, 16, 6, D_MODEL)
    lower = jnp.cumsum(jax.nn.softmax(hgrn_lower.astype(F32), axis=0), axis=0)

    streams = [
        dict(x=x_prompt.reshape(g_ctx, ROW_GROUP, D_MODEL), ngrp=g_ctx, bsz=n_ctx, seq=seq, ctx=True),
        dict(x=x_sample, ngrp=g_lat, bsz=n_lat, seq=dec_seq, ctx=False),
    ]
    new_state, new_ck, new_cv, new_dk, new_dv = [], [], [], [], []

    for l in range(depth):
        j = l // 2
        w_router = jnp.zeros((D_MODEL, LANES), F32)
        w_router = w_router.at[:, :N_EXPERTS].set(moe_w_rt[l])
        w_router = w_router.at[:, N_EXPERTS : N_EXPERTS + N_GROUPS].set(moe_w_grp[l])
        b_router = jnp.zeros((1, LANES), F32)
        b_router = b_router.at[0, :N_EXPERTS].set(moe_b_rt[l])
        b_router = b_router.at[0, N_EXPERTS : N_EXPERTS + N_GROUPS].set(moe_b_grp[l])
        if l % 2 == 0:
            w_in = even_w_in[j].astype(BF16)
            w_out = even_w_out[j].astype(BF16)
        else:
            w_in = odd_w_in[j].astype(BF16)
            w_out = odd_w_out[j].astype(BF16)
            lam_init = 0.8 - 0.6 * math.exp(-0.3 * l)
            lam = (jnp.exp(jnp.sum(d_lambda_q1[j] * d_lambda_k1[j]))
                   - jnp.exp(jnp.sum(d_lambda_q2[j] * d_lambda_k2[j])) + lam_init)
            lam_row = jnp.full((1, LANES), lam, F32)

        for s in streams:
            ngrp, bsz, sl = s["ngrp"], s["bsz"], s["seq"]
            if s["ctx"]:
                m = jnp.broadcast_to(mods[l, 0][None], (ngrp, 6, D_MODEL))
            else:
                m = mods[l, 1 : 1 + ngrp]
            x = s["x"]
            if l % 2 == 0:
                wa = H_A * DK_A
                splits = [(0, wa), (wa, 2 * wa), (2 * wa, 3 * wa), (3 * wa, 3 * wa + W_A),
                          (3 * wa + W_A, 3 * wa + 2 * W_A), (3 * wa + 2 * W_A, w_in.shape[1])]
                qa, ffa, fba, ia, ga, hy = _norm_proj(x, m, norm1_g[l], w_in, splits)
                per_seq = lambda t: t.reshape(bsz, sl, t.shape[-1])
                if s["ctx"]:
                    s0t = jnp.zeros((bsz, 2, H_A, DK_A, DK_A), F32)
                else:
                    s0t = jnp.swapaxes(state_hgrn[:, j].astype(F32), -1, -2)
                mix_a, s_fin = _hgrn(per_seq(qa), per_seq(ffa), per_seq(fba), per_seq(ia),
                                     per_seq(ga), lower[j], hgrn_norm_g[j], s0t, sl)
                gr, gi = _hyena_filter(sl, hy_w1[j], hy_b1[j], hy_freq1[j], hy_w2[j], hy_b2[j],
                                       hy_freq2[j], hy_w3[j], hy_bias[j])
                mix_b = _hyena(per_seq(hy), hy_conv_w[j], hy_conv_b[j], gr, gi, sl)
                if s["ctx"]:
                    new_state.append(jnp.swapaxes(s_fin, -1, -2))
            else:
                (p,) = _norm_proj(x, m, norm1_g[l], w_in, [(0, w_in.shape[1])])
                qc, qd, kc, vc, kd, vd = _qk_prep(p, c_qnorm_g[j], c_knorm_g[j], d_qnorm_g[j],
                                                  d_knorm_g[j], sl, use_rope=not s["ctx"])
                if s["ctx"]:
                    caches = (None, None, None, None)
                    new_ck.append(kc)
                    new_cv.append(vc)
                    new_dk.append(kd)
                    new_dv.append(vd)
                else:
                    caches = (cache_c_k[:, j : j + 1], cache_c_v[:, j : j + 1],
                              cache_d_k[:, j : j + 1], cache_d_v[:, j : j + 1])
                mix_a = _attention(qc, kc, vc, caches[0], caches[1], _gqa_kernel, (), KV_C,
                                   (H_C // KV_C) * HD_C, sl)
                diff = functools.partial(_diff_kernel, out_scale=1.0 - lam_init)
                mix_b = _attention(qd, kd, vd, caches[2], caches[3], diff,
                                   (lam_row, d_subln_g[j].reshape(1, DV_D)), H_D, 2 * DK_D, sl)
            grp = lambda t: t.reshape(ngrp, ROW_GROUP, t.shape[-1])
            x1, h2, gates = _mix_out(x, grp(mix_a), grp(mix_b), m, norm2_g[l], w_out, w_router, b_router)
            s["x"] = _moe(h2, gates, x1, m, l, moe_w_gate, moe_w_up, moe_w_down)

    y_ctx = streams[0]["x"].reshape(n_ctx, seq, D_MODEL)
    y_lat = streams[1]["x"]
    return (y_ctx, y_lat, jnp.stack(new_state, axis=1), jnp.stack(new_ck, axis=1),
            jnp.stack(new_cv, axis=1), jnp.stack(new_dk, axis=1), jnp.stack(new_dv, axis=1))
```

```python
import functools
import math

import numpy as np
import jax
import jax.numpy as jnp
from jax import lax
from jax.experimental import pallas as pl
from jax.experimental.pallas import tpu as pltpu

F32 = jnp.float32
BF16 = jnp.bfloat16

D_MODEL = 1024
EPS = 1e-6
LOG2E = 1.0 / math.log(2.0)
GRID_W = 64
ROPE_THETA = 10000.0
H_A = 4
DK_A = 128
W_A = 512
CHUNK = 64
SUB = 8
LEVELS = (16, 32, 64)
C_B = 512
FILTER_BANDS = 16
DECAY_MIN = math.log(1e-2) / 1.5
DECAY_MAX = math.log(1e-2) / 0.3
H_C = 4
KV_C = 2
HD_C = 128
H_D = 4
DK_D = 64
DV_D = 128
N_GROUPS = 4
EXP_PER_GROUP = 4
N_EXPERTS = 16
D_EXPERT = 512

ROUTE_LANE = N_EXPERTS + N_GROUPS
MOE_TILE = 512
MOE_CHUNK = 16
MOE_TILE_CHUNKS = 2 * MOE_TILE // MOE_CHUNK + N_EXPERTS
MOE_TILE_ROWS = MOE_TILE_CHUNKS * MOE_CHUNK
MOE_BLOCK_CHUNKS = 16
MOE_ROW_W = D_MODEL + 128
MOE_W_PIECES = 3

LANES = 128
ROW_GROUP = 1024
VMEM_LIMIT_BYTES = 56 * 1024 * 1024


def _cparams(*sem):
    return pltpu.CompilerParams(dimension_semantics=sem, vmem_limit_bytes=VMEM_LIMIT_BYTES)


def _split_bf16(x):
    hi = x.astype(BF16)
    lo = (x - hi.astype(F32)).astype(BF16)
    return hi, lo


def _dot(a, b):
    return jnp.dot(a, b, preferred_element_type=F32)


def _dot3(a, b):
    ah, al = _split_bf16(a)
    bh, bl = _split_bf16(b)
    return _dot(ah, bh) + _dot(al, bh) + _dot(ah, bl)


def _dot_nt(a, b):
    return lax.dot_general(a, b, (((1,), (1,)), ((), ())), preferred_element_type=F32)


def _dot_tn(a, b):
    return lax.dot_general(a, b, (((0,), (0,)), ((), ())), preferred_element_type=F32)


def _silu(x):
    return x * jax.nn.sigmoid(x)


def _rms(x, eps=EPS):
    return x * lax.rsqrt(jnp.mean(x * x, axis=-1, keepdims=True) + eps)


def _adaln_kernel(c_ref, w_ref, b_ref, o_ref):
    s = _silu(c_ref[...])
    o_ref[0] = _dot(s.astype(BF16), w_ref[0].astype(BF16)) + b_ref[0]


def _adaln(cond, w_mod, b_mod):
    depth, _, n = w_mod.shape
    rows = cond.shape[0]
    tn = 1536
    return pl.pallas_call(
        _adaln_kernel,
        grid=(depth, n // tn),
        in_specs=[
            pl.BlockSpec((rows, D_MODEL), lambda l, j: (0, 0)),
            pl.BlockSpec((1, D_MODEL, tn), lambda l, j: (l, 0, j)),
            pl.BlockSpec((1, 1, tn), lambda l, j: (l, 0, j)),
        ],
        out_specs=pl.BlockSpec((1, rows, tn), lambda l, j: (l, 0, j)),
        out_shape=jax.ShapeDtypeStruct((depth, rows, n), F32),
        compiler_params=_cparams("arbitrary", "arbitrary"),
        name="adaln",
    )(cond, w_mod, b_mod.reshape(depth, 1, n))


def _norm_proj_kernel(x_ref, m_ref, g_ref, w_ref, *o_refs, splits):
    m = m_ref[0]
    h = _rms(x_ref[0]) * g_ref[...] * (1.0 + m[1:2]) + m[0:1]
    hb = h.astype(BF16)
    for o_ref, (a, b) in zip(o_refs, splits):
        o_ref[0] = _dot(hb, w_ref[:, a:b])


def _norm_proj(x, mods, g, w_bf16, splits, tm=256):
    ngrp = x.shape[0]
    n = w_bf16.shape[1]
    kern = functools.partial(_norm_proj_kernel, splits=splits)
    return pl.pallas_call(
        kern,
        grid=(ngrp, ROW_GROUP // tm),
        in_specs=[
            pl.BlockSpec((1, tm, D_MODEL), lambda b, i: (b, i, 0)),
            pl.BlockSpec((1, 6, D_MODEL), lambda b, i: (b, 0, 0)),
            pl.BlockSpec((1, D_MODEL), lambda b, i: (0, 0)),
            pl.BlockSpec((D_MODEL, n), lambda b, i: (0, 0)),
        ],
        out_specs=[pl.BlockSpec((1, tm, hi - lo), lambda b, i: (b, i, 0)) for lo, hi in splits],
        out_shape=[jax.ShapeDtypeStruct((ngrp, ROW_GROUP, hi - lo), F32) for lo, hi in splits],
        compiler_params=_cparams("arbitrary", "arbitrary"),
        name="norm_proj",
    )(x, mods, g.reshape(1, D_MODEL), w_bf16)


def _hgrn_constants():
    c = CHUNK
    nblk = 1 + len(LEVELS)
    w = np.zeros((2, nblk * c, c), np.float32)
    for t in range(c):
        w[0, t, : t + 1] = 1.0
    for li, r in enumerate(LEVELS):
        for t in range(c):
            mid = (t // r) * r + r // 2
            if t >= mid:
                w[0, (li + 1) * c + t, mid : t + 1] = 1.0
            else:
                w[0, (li + 1) * c + t, t + 1 : mid] = 1.0
    for blk in range(nblk):
        w[1, blk * c : (blk + 1) * c] = w[0, blk * c : (blk + 1) * c][::-1, ::-1]
    m = np.zeros((2, nblk, c, c), np.float32)
    for li, r in enumerate(LEVELS):
        for t in range(c):
            for s in range(c):
                if t // r == s // r and (t % r) >= r // 2 and (s % r) < r // 2:
                    m[0, li, t, s] = 1.0
    for t in range(c):
        for s in range(c):
            if t // SUB == s // SUB and s <= t:
                m[0, nblk - 1, t, s] = 1.0
    m[1] = m[0][:, ::-1, ::-1]
    sel = np.zeros((SUB * DK_A, c), np.float32)
    for i in range(SUB):
        sel[i * DK_A : (i + 1) * DK_A, i::SUB] = 1.0
    return w, m, sel


def _hgrn_chunk(direction, r0, qa_ref, gate_ref, ia_ref, lb, wcum, masks, sel, states):
    rows = pl.ds(r0, CHUNK)
    q = _silu(qa_ref[0, rows, :])
    v = ia_ref[0, rows, :]
    f = lb + (1.0 - lb) * jax.nn.sigmoid(gate_ref[0, rows, :])
    kk = jnp.maximum(1.0 - f, 0.0)
    lf = jnp.log(f) * LOG2E
    lk = jnp.log(kk) * LOG2E
    lf_hi, lf_lo = _split_bf16(lf)
    z = _dot(wcum, lf_hi) + _dot(wcum, lf_lo)
    e = z[0:CHUNK]
    edge = e[CHUNK - 1 : CHUNK] if direction == 0 else e[0:1]
    q_in = (q * jnp.exp2(e)).astype(BF16)
    k_st = (kk * jnp.exp2(edge - e)).astype(BF16)
    st_decay = jnp.exp2(edge)
    q_lv, k_lv = [], []
    for li in range(len(LEVELS)):
        ez = jnp.exp2(z[(li + 1) * CHUNK : (li + 2) * CHUNK])
        q_lv.append((q * ez).astype(BF16))
        k_lv.append((kk * ez).astype(BF16))
    nsub = CHUNK // SUB
    c3 = (e - lk).reshape(nsub, SUB, W_A)
    pair = []
    for i in range(SUB):
        cb = jnp.broadcast_to(c3[:, i : i + 1, :], (nsub, SUB, W_A)).reshape(CHUNK, W_A)
        pair.append((q * jnp.exp2(jnp.minimum(e - cb, 0.0))).astype(BF16))
    vb = v.astype(BF16)
    outs, new_states = [], []
    for h in range(H_A):
        hs = slice(h * DK_A, (h + 1) * DK_A)
        sc = _dot(jnp.concatenate([p[:, hs] for p in pair], axis=1), sel) * masks[len(LEVELS)]
        for li in range(len(LEVELS)):
            sc = sc + _dot_nt(q_lv[li][:, hs], k_lv[li][:, hs]) * masks[li]
        st = states[h]
        outs.append(_dot(sc.astype(BF16), vb[:, hs]) + _dot_nt(q_in[:, hs], st.astype(BF16)))
        new_states.append(st_decay[:, hs] * st + _dot_tn(vb[:, hs], k_st[:, hs]))
    return jnp.concatenate(outs, axis=1), tuple(new_states)


def _hgrn_kernel(qa_ref, ff_ref, fb_ref, ia_ref, ga_ref, lb_ref, ng_ref, s0_ref, w_ref, m_ref,
                 sel_ref, o_ref, sout_ref, ob_ref, *, seq_len):
    nchunk = seq_len // CHUNK
    lb = lb_ref[...]
    sel = sel_ref[...]
    nmask = 1 + len(LEVELS)

    def scan(i, states):
        rf = pl.multiple_of(i * CHUNK, CHUNK)
        rb = pl.multiple_of((nchunk - 1 - i) * CHUNK, CHUNK)
        o_f, st_f = _hgrn_chunk(0, rf, qa_ref, ff_ref, ia_ref, lb, w_ref[0],
                                [m_ref[0, j] for j in range(nmask)], sel, states[0])
        o_b, st_b = _hgrn_chunk(1, rb, qa_ref, fb_ref, ia_ref, lb, w_ref[1],
                                [m_ref[1, j] for j in range(nmask)], sel, states[1])
        o_ref[0, pl.ds(rf, CHUNK), :] = o_f
        ob_ref[pl.ds(rb, CHUNK), :] = o_b
        return st_f, st_b

    init = tuple(tuple(s0_ref[0, d, h] for h in range(H_A)) for d in range(2))
    final = lax.fori_loop(0, nchunk, scan, init)

    ng = jnp.concatenate([ng_ref[...]] * H_A, axis=1)

    def gate(i, carry):
        rows = pl.ds(pl.multiple_of(i * CHUNK, CHUNK), CHUNK)
        tot = o_ref[0, rows, :] + ob_ref[rows, :]
        normed = jnp.concatenate(
            [_rms(tot[:, h * DK_A : (h + 1) * DK_A]) for h in range(H_A)], axis=1)
        o_ref[0, rows, :] = normed * ng * _silu(ga_ref[0, rows, :])
        return carry

    lax.fori_loop(0, nchunk, gate, 0)
    for d in range(2):
        for h in range(H_A):
            sout_ref[0, d, h] = final[d][h]


def _hgrn(qa, ff, fb, ia, ga, lb, ng, s0t, seq_len):
    bsz = qa.shape[0]
    w, m, sel = _hgrn_constants()
    seq = pl.BlockSpec((1, seq_len, W_A), lambda b: (b, 0, 0))
    full = lambda shape: pl.BlockSpec(shape, lambda b: (0,) * len(shape))
    st_spec = pl.BlockSpec((1, 2, H_A, DK_A, DK_A), lambda b: (b, 0, 0, 0, 0))
    return pl.pallas_call(
        functools.partial(_hgrn_kernel, seq_len=seq_len),
        grid=(bsz,),
        in_specs=[seq, seq, seq, seq, seq, full((1, W_A)), full((1, DK_A)), st_spec,
                  full(w.shape), full(m.shape), full(sel.shape)],
        out_specs=[seq, st_spec],
        out_shape=[jax.ShapeDtypeStruct((bsz, seq_len, W_A), F32),
                   jax.ShapeDtypeStruct((bsz, 2, H_A, DK_A, DK_A), F32)],
        scratch_shapes=[pltpu.VMEM((seq_len, W_A), F32)],
        compiler_params=_cparams("arbitrary"),
        name="hgrn",
    )(qa, ff, fb, ia, ga, lb.reshape(1, W_A), ng.reshape(1, DK_A), s0t,
      jnp.asarray(w, BF16), jnp.asarray(m, F32), jnp.asarray(sel, BF16))


def _dft_constants(seq_len):
    n = 2 * seq_len
    t = np.arange(seq_len, dtype=np.int64)
    wt = (np.arange(seq_len, dtype=np.int64)[:, None] * t[None, :]) % n
    ang = 2.0 * np.pi * wt.astype(np.float64) / n
    cos, sin = np.cos(ang), np.sin(ang)
    nyq = np.where(t % 2 == 0, 1.0, -1.0)
    sin_p = sin.copy()
    sin_p[0] = nyq
    fwd = np.concatenate([cos, sin_p], axis=0)
    icos = 2.0 * cos.T / n
    icos[:, 0] = 1.0 / n
    isin = 2.0 * sin.T / n
    isin[:, 0] = nyq / n
    inv = np.concatenate([icos, isin], axis=1)
    return fwd, inv


def _filter_embedding(seq_len):
    t = np.linspace(0.0, 1.0, seq_len)[:, None]
    w = 2.0 * np.pi * np.arange(seq_len) / seq_len
    f = np.linspace(1e-4, FILTER_BANDS - 1, FILTER_BANDS)
    ang = w[:, None] * f[None, :]
    z = np.concatenate([t, np.cos(ang), -np.sin(ang)], axis=-1)
    zp = np.zeros((seq_len, LANES), np.float64)
    zp[:, : z.shape[1]] = z
    deltas = np.abs(np.linspace(DECAY_MIN, DECAY_MAX, C_B))
    window = np.exp(-t * deltas[None, :])
    return zp.astype(np.float32), window.astype(np.float32)


def _hyena_filter_kernel(z_ref, win_ref, w1_ref, b1_ref, fr1_ref, w2_ref, b2_ref, fr2_ref, w3_ref,
                         hb_ref, fh_ref, fl_ref, gr_ref, gi_ref, *, seq_len):
    h = jnp.sin(fr1_ref[...] * (_dot3(z_ref[...], w1_ref[...]) + b1_ref[...]))
    h = jnp.sin(fr2_ref[...] * (_dot3(h, w2_ref[...]) + b2_ref[...]))
    h = _dot3(h, w3_ref[...])
    win = win_ref[...]
    hf = h[:, :C_B] * win
    hbk = h[:, C_B:] * win

    def dft(x):
        xh, xl = _split_bf16(x)
        return _dot(fh_ref[...], xh) + _dot(fh_ref[...], xl) + _dot(fl_ref[...], xh)

    p_sum = dft(hf + hbk)
    p_dif = dft(hf - hbk)
    row0 = lax.broadcasted_iota(jnp.int32, (seq_len, C_B), 0) == 0
    gr_ref[...] = p_sum[:seq_len] + hb_ref[...]
    gi_ref[...] = jnp.where(row0, p_sum[seq_len:] + hb_ref[...], p_dif[seq_len:])


def _hyena_filter(seq_len, w1, b1, fr1, w2, b2, fr2, w3, hbias):
    zemb, window = _filter_embedding(seq_len)
    fwd, _ = _dft_constants(seq_len)
    f_hi, f_lo = _split_bf16(jnp.asarray(fwd, F32))

    def pad(a, rows, cols):
        return jnp.zeros((rows, cols), F32).at[: a.shape[0], : a.shape[1]].set(a)

    args = (jnp.asarray(zemb), jnp.asarray(window), pad(w1, LANES, LANES), pad(b1[None], 1, LANES),
            pad(fr1[None], 1, LANES), pad(w2, LANES, LANES), pad(b2[None], 1, LANES),
            pad(fr2[None], 1, LANES), pad(w3, LANES, 2 * C_B), hbias.reshape(1, C_B), f_hi, f_lo)
    return pl.pallas_call(
        functools.partial(_hyena_filter_kernel, seq_len=seq_len),
        out_shape=[jax.ShapeDtypeStruct((seq_len, C_B), F32)] * 2,
        compiler_params=pltpu.CompilerParams(vmem_limit_bytes=VMEM_LIMIT_BYTES),
        name="hyena_filter",
    )(*args)


def _hyena_kernel(x0_ref, x1_ref, v_ref, cw_ref, cb_ref, gr_ref, gi_ref, f_ref, fi_ref, o_ref, *,
                  seq_len):
    tc = o_ref.shape[2]
    row = lax.broadcasted_iota(jnp.int32, (seq_len, tc), 0)

    def short_conv(u_ref, j):
        u = u_ref[0]
        prev = jnp.where(row == 0, 0.0, pltpu.roll(u, 1, 0))
        nxt = jnp.where(row == seq_len - 1, 0.0, pltpu.roll(u, seq_len - 1, 0))
        cw = cw_ref[j]
        return cw[0:1] * prev + cw[1:2] * u + cw[2:3] * nxt + cb_ref[j]

    x0 = short_conv(x0_ref, 0)
    z = short_conv(v_ref, 2) * short_conv(x1_ref, 1)
    p = _dot(f_ref[...], z.astype(BF16))
    a, b = p[:seq_len], p[seq_len:]
    gr, gi = gr_ref[...], gi_ref[...]
    row0 = row == 0
    bgi = b * gi
    yr = a * gr - jnp.where(row0, 0.0, bgi)
    yq = jnp.where(row0, bgi, a * gi + b * gr)
    y = _dot(fi_ref[...], jnp.concatenate([yr, yq], axis=0).astype(BF16))
    o_ref[0] = y * x0


def _hyena(hy, conv_w, conv_b, gr, gi, seq_len, tc=256):
    bsz = hy.shape[0]
    nct = C_B // tc
    fwd, inv = _dft_constants(seq_len)
    cw = conv_w.reshape(3, 3, C_B).transpose(1, 0, 2)
    cb = conv_b.reshape(3, 1, C_B)
    part = lambda k: pl.BlockSpec((1, seq_len, tc), lambda b, j, k=k: (b, 0, k * nct + j))
    return pl.pallas_call(
        functools.partial(_hyena_kernel, seq_len=seq_len),
        grid=(bsz, nct),
        in_specs=[part(0), part(1), part(2),
                  pl.BlockSpec((3, 3, tc), lambda b, j: (0, 0, j)),
                  pl.BlockSpec((3, 1, tc), lambda b, j: (0, 0, j)),
                  pl.BlockSpec((seq_len, tc), lambda b, j: (0, j)),
                  pl.BlockSpec((seq_len, tc), lambda b, j: (0, j)),
                  pl.BlockSpec((2 * seq_len, seq_len), lambda b, j: (0, 0)),
                  pl.BlockSpec((seq_len, 2 * seq_len), lambda b, j: (0, 0))],
        out_specs=pl.BlockSpec((1, seq_len, tc), lambda b, j: (b, 0, j)),
        out_shape=jax.ShapeDtypeStruct((bsz, seq_len, C_B), F32),
        compiler_params=_cparams("arbitrary", "arbitrary"),
        name="hyena",
    )(hy, hy, hy, cw, cb, gr, gi, jnp.asarray(fwd, F32).astype(BF16),
      jnp.asarray(inv, F32).astype(BF16))


def _rope_tables(seq_len, dim):
    rows = seq_len // GRID_W
    row_idx = np.repeat(np.arange(rows), GRID_W).astype(np.float64)
    col_idx = np.tile(np.arange(GRID_W), rows).astype(np.float64)
    half = dim // 2
    inv = ROPE_THETA ** (-np.arange(0, half, 2, dtype=np.float64) / half)
    ang = np.concatenate([row_idx[:, None] * inv, col_idx[:, None] * inv], axis=-1)
    cos = np.repeat(np.cos(ang), 2, axis=1)
    sin = np.repeat(np.sin(ang), 2, axis=1)
    sin[:, 0::2] *= -1.0
    reps = LANES // dim
    return (np.tile(cos, (1, reps)).astype(np.float32), np.tile(sin, (1, reps)).astype(np.float32))


def _qk_prep_kernel(p_ref, cqg_ref, ckg_ref, dqg_ref, dkg_ref, *rest, use_rope):
    if use_rope:
        cc_ref, sc_ref, cd_ref, sd_ref = rest[:4]
        rest = rest[4:]
    qc_ref, qd_ref, kc_ref, vc_ref, kd_ref, vd_ref = rest
    p = p_ref[0]
    tm = p.shape[0]
    lane = lax.broadcasted_iota(jnp.int32, (tm, LANES), 1)
    even = (lane % 2) == 0
    low = lane < DK_D

    def rope(x, cos, sin):
        swapped = jnp.where(even, pltpu.roll(x, LANES - 1, 1), pltpu.roll(x, 1, 1))
        return x * cos + swapped * sin

    def norm_c(x, g_ref):
        y = _rms(x) * g_ref[...]
        return rope(y, cc_ref[...], sc_ref[...]) if use_rope else y

    def norm_d(x, g_ref):
        sq = x * x
        s_lo = jnp.sum(jnp.where(low, sq, 0.0), axis=1, keepdims=True)
        s_hi = jnp.sum(jnp.where(low, 0.0, sq), axis=1, keepdims=True)
        ms = jnp.where(low, s_lo, s_hi) * (1.0 / DK_D)
        y = x * lax.rsqrt(ms + EPS) * g_ref[...]
        return rope(y, cd_ref[...], sd_ref[...]) if use_rope else y

    blk = lambda i: p[:, i * LANES : (i + 1) * LANES]
    qc_ref[0] = jnp.concatenate([norm_c(blk(i), cqg_ref) for i in range(H_C)], axis=1)
    for j in range(KV_C):
        kc_ref[0, j] = norm_c(blk(H_C + j), ckg_ref)
        vc_ref[0, j] = blk(H_C + KV_C + j)
    base = H_C + 2 * KV_C
    qd_ref[0] = jnp.concatenate([norm_d(blk(base + i), dqg_ref) for i in range(H_D)], axis=1)
    for j in range(H_D):
        kd_ref[0, j] = norm_d(blk(base + H_D + j), dkg_ref)
        vd_ref[0, j] = blk(base + 2 * H_D + j)


def _qk_prep(p, cqg, ckg, dqg, dkg, seq_len, use_rope, tm=256):
    ngrp = p.shape[0]
    bsz = ngrp * ROW_GROUP // seq_len
    p = p.reshape(bsz, seq_len, p.shape[2])
    tm = min(tm, seq_len)
    vec = lambda: pl.BlockSpec((1, LANES), lambda b, i: (0, 0))
    tab = lambda: pl.BlockSpec((tm, LANES), lambda b, i: (i, 0))
    in_specs = [pl.BlockSpec((1, tm, p.shape[2]), lambda b, i: (b, i, 0)), vec(), vec(), vec(), vec()]
    args = [p, cqg.reshape(1, HD_C), ckg.reshape(1, HD_C),
            jnp.tile(dqg.reshape(1, DK_D), (1, 2)), jnp.tile(dkg.reshape(1, DK_D), (1, 2))]
    if use_rope:
        in_specs += [tab(), tab(), tab(), tab()]
        args += [jnp.asarray(a) for a in _rope_tables(seq_len, HD_C) + _rope_tables(seq_len, DK_D)]
    tok = lambda w: pl.BlockSpec((1, tm, w), lambda b, i: (b, i, 0))
    head = lambda nh: pl.BlockSpec((1, nh, tm, LANES), lambda b, i: (b, 0, i, 0))
    tok_shape = lambda w: jax.ShapeDtypeStruct((bsz, seq_len, w), F32)
    head_shape = lambda nh: jax.ShapeDtypeStruct((bsz, nh, seq_len, LANES), F32)
    return pl.pallas_call(
        functools.partial(_qk_prep_kernel, use_rope=use_rope),
        grid=(bsz, seq_len // tm),
        in_specs=in_specs,
        out_specs=[tok(H_C * HD_C), tok(H_D * 2 * DK_D), head(KV_C), head(KV_C), head(H_D), head(H_D)],
        out_shape=[tok_shape(H_C * HD_C), tok_shape(H_D * 2 * DK_D), head_shape(KV_C),
                   head_shape(KV_C), head_shape(H_D), head_shape(H_D)],
        compiler_params=_cparams("arbitrary", "arbitrary"),
        name="qk_prep",
    )(*args)


def _softmax_pv(q_list, kv_list):
    outs = []
    for q in q_list:
        scores = [_dot_nt(q, k) for k, _ in kv_list]
        mx = scores[0].max(axis=1, keepdims=True)
        for s in scores[1:]:
            mx = jnp.maximum(mx, s.max(axis=1, keepdims=True))
        den = 0.0
        acc = 0.0
        for s, (_, v) in zip(scores, kv_list):
            pexp = jnp.exp2(s - mx)
            den = den + pexp.sum(axis=1, keepdims=True)
            acc = acc + _dot(pexp.astype(BF16), v)
        outs.append(acc / den)
    return outs


def _head_kv(j, k_ref, v_ref, cache_refs):
    kv = [(r_k[0, 0, j].astype(BF16), r_v[0, 0, j].astype(BF16)) for r_k, r_v in cache_refs]
    kv.append((k_ref[0, j].astype(BF16), v_ref[0, j].astype(BF16)))
    return kv


def _gqa_kernel(q_ref, k_ref, v_ref, *rest, has_cache):
    cache_refs = [rest[:2]] if has_cache else []
    o_ref = rest[-1]
    g_c = H_C // KV_C
    width = g_c * HD_C
    for j in range(k_ref.shape[1]):
        q = q_ref[0, :, j * width : (j + 1) * width] * (HD_C ** -0.5 * LOG2E)
        qs = [q[:, g * HD_C : (g + 1) * HD_C].astype(BF16) for g in range(g_c)]
        outs = _softmax_pv(qs, _head_kv(j, k_ref, v_ref, cache_refs))
        o_ref[0, :, j * width : (j + 1) * width] = jnp.concatenate(outs, axis=1)


def _diff_kernel(q_ref, k_ref, v_ref, lam_ref, sg_ref, *rest, has_cache, out_scale):
    cache_refs = [rest[:2]] if has_cache else []
    o_ref = rest[-1]
    width = 2 * DK_D
    low = lax.broadcasted_iota(jnp.int32, (q_ref.shape[1], width), 1) < DK_D
    for j in range(k_ref.shape[1]):
        q = q_ref[0, :, j * width : (j + 1) * width] * (DK_D ** -0.5 * LOG2E)
        qs = [jnp.where(low, q, 0.0).astype(BF16), jnp.where(low, 0.0, q).astype(BF16)]
        o1, o2 = _softmax_pv(qs, _head_kv(j, k_ref, v_ref, cache_refs))
        o_ref[0, :, j * width : (j + 1) * width] = (
            _rms(o1 - lam_ref[...] * o2) * sg_ref[...] * out_scale)


def _attention(q, k, v, cache_k, cache_v, kernel, extra_args, n_heads, q_width, seq_len, hps, tq):
    bsz = q.shape[0]
    in_specs = [pl.BlockSpec((1, tq, hps * q_width), lambda b, h, i: (b, i, h)),
                pl.BlockSpec((1, hps, seq_len, LANES), lambda b, h, i: (b, h, 0, 0)),
                pl.BlockSpec((1, hps, seq_len, LANES), lambda b, h, i: (b, h, 0, 0))]
    in_specs += [pl.BlockSpec((1, LANES), lambda b, h, i: (0, 0)) for _ in extra_args]
    args = [q, k, v, *extra_args]
    if cache_k is not None:
        past = cache_k.shape[3]
        spec = lambda: pl.BlockSpec((1, 1, hps, past, LANES), lambda b, h, i: (b, 0, h, 0, 0))
        in_specs += [spec(), spec()]
        args += [cache_k, cache_v]
    return pl.pallas_call(
        functools.partial(kernel, has_cache=cache_k is not None),
        grid=(bsz, n_heads // hps, seq_len // tq),
        in_specs=in_specs,
        out_specs=pl.BlockSpec((1, tq, hps * q_width), lambda b, h, i: (b, i, h)),
        out_shape=jax.ShapeDtypeStruct((bsz, seq_len, n_heads * q_width), F32),
        compiler_params=_cparams("arbitrary", "arbitrary", "arbitrary"),
        name="attention",
    )(*args)


def _route(lg):
    lane = lax.broadcasted_iota(jnp.int32, lg.shape, 1).astype(F32)
    neg = -1e30
    is_g = (lane >= N_EXPERTS) & (lane < N_EXPERTS + N_GROUPS)
    gl = jnp.where(is_g, lg, neg)
    gmax = gl.max(axis=1, keepdims=True)
    g_p = 1.0 / jnp.where(is_g, jnp.exp(gl - gmax), 0.0).sum(axis=1, keepdims=True)
    g_i = jnp.where(gl == gmax, lane - N_EXPERTS, 1e9).min(axis=1, keepdims=True)
    in_grp = (lane < N_EXPERTS) & (jnp.floor(lane * (1.0 / EXP_PER_GROUP)) == g_i)
    el = jnp.where(in_grp, lg, neg)
    m1 = el.max(axis=1, keepdims=True)
    i1 = jnp.where(in_grp & (el == m1), lane, 1e9).min(axis=1, keepdims=True)
    el2 = jnp.where(lane == i1, neg, el)
    m2 = el2.max(axis=1, keepdims=True)
    i2 = jnp.where(in_grp & (el2 == m2) & (lane != i1), lane, 1e9).min(axis=1, keepdims=True)
    r = jnp.exp(m2 - m1)
    w1 = g_p / (1.0 + r)
    rec = jnp.where(lane == ROUTE_LANE, i1, 0.0)
    rec = jnp.where(lane == ROUTE_LANE + 1, i2, rec)
    rec = jnp.where(lane == ROUTE_LANE + 2, w1, rec)
    return jnp.where(lane == ROUTE_LANE + 3, w1 * r, rec)


def _mix_out_kernel(x_ref, a_ref, b_ref, m_ref, g2_ref, w_ref, wrh_ref, wrl_ref, br_ref,
                    x1_ref, h2_ref, gates_ref):
    wa = a_ref.shape[2]
    o = _dot(a_ref[0].astype(BF16), w_ref[:wa]) + _dot(b_ref[0].astype(BF16), w_ref[wa:])
    m = m_ref[0]
    x1 = x_ref[0] + m[2:3] * o
    x1_ref[0] = x1
    h2 = _rms(x1) * g2_ref[...] * (1.0 + m[4:5]) + m[3:4]
    h2_ref[0] = h2.astype(BF16)
    hi, lo = _split_bf16(h2)
    lg = _dot(hi, wrh_ref[...]) + _dot(lo, wrh_ref[...]) + _dot(hi, wrl_ref[...]) + br_ref[...]
    gates_ref[0] = _route(lg)


def _mix_out(x, a, b, mods, g2, w_out_bf16, w_router, b_router, tm=256):
    ngrp = x.shape[0]
    wa, wb = a.shape[2], b.shape[2]
    wr_hi = w_router.astype(BF16)
    wr_lo = (w_router - wr_hi.astype(F32)).astype(BF16)
    row = lambda w: pl.BlockSpec((1, tm, w), lambda g, i: (g, i, 0))
    full = lambda shape: pl.BlockSpec(shape, lambda g, i: (0,) * len(shape))
    return pl.pallas_call(
        _mix_out_kernel,
        grid=(ngrp, ROW_GROUP // tm),
        in_specs=[row(D_MODEL), row(wa), row(wb),
                  pl.BlockSpec((1, 6, D_MODEL), lambda g, i: (g, 0, 0)),
                  full((1, D_MODEL)), full((wa + wb, D_MODEL)),
                  full((D_MODEL, LANES)), full((D_MODEL, LANES)), full((1, LANES))],
        out_specs=[row(D_MODEL), row(D_MODEL), row(LANES)],
        out_shape=[jax.ShapeDtypeStruct((ngrp, ROW_GROUP, D_MODEL), F32),
                   jax.ShapeDtypeStruct((ngrp, ROW_GROUP, D_MODEL), BF16),
                   jax.ShapeDtypeStruct((ngrp, ROW_GROUP, LANES), F32)],
        compiler_params=_cparams("arbitrary", "arbitrary"),
        name="mix_out",
    )(x, a, b, mods, g2.reshape(1, D_MODEL), w_out_bf16, wr_hi, wr_lo, b_router)


def _lane_col(x, lane, k):
    return jnp.where(lane == k, x, 0.0).sum(axis=1, keepdims=True)


def _moe_dispatch_kernel(h_ref, r_ref, tri_ref, xs_ref, pos_ref, cnt_ref):
    r = r_ref[0]
    lane = lax.broadcasted_iota(jnp.int32, r.shape, 1).astype(F32)
    i1, i2, w1, w2 = [_lane_col(r, lane, ROUTE_LANE + k) for k in range(4)]
    oh1 = lane == i1
    oh2 = lane == i2
    oh = jnp.where(oh1 | oh2, 1.0, 0.0)
    rank = _dot(tri_ref[...], oh.astype(BF16))
    cnt = oh.sum(axis=0, keepdims=True)
    chunks = jnp.floor((cnt + (MOE_CHUNK - 1)) * (1.0 / MOE_CHUNK))
    li = lax.broadcasted_iota(jnp.int32, (LANES, LANES), 0)
    lj = lax.broadcasted_iota(jnp.int32, (LANES, LANES), 1)
    before = jnp.where(li < lj, 1.0, 0.0).astype(BF16)
    seg = _dot(jnp.broadcast_to(chunks, (8, LANES)).astype(BF16), before)[0:1]
    base = seg * MOE_CHUNK + rank
    pos1 = jnp.where(oh1, base, 0.0).sum(axis=1, keepdims=True)
    pos2 = jnp.where(oh2, base, 0.0).sum(axis=1, keepdims=True)
    riota = lax.broadcasted_iota(jnp.int32, (r.shape[0], MOE_TILE_ROWS), 1).astype(F32)
    p = jnp.where((riota == pos1) | (riota == pos2), 1.0, 0.0).astype(BF16)

    aux = jnp.where(lane == 2 * MOE_W_PIECES, i1, 0.0)
    for k, w in enumerate((w1, w2)):
        rest = w
        for piece in range(MOE_W_PIECES):
            part = rest.astype(BF16).astype(F32)
            aux = jnp.where(lane == k * MOE_W_PIECES + piece, part, aux)
            rest = rest - part
    row = jnp.concatenate([h_ref[0], aux.astype(BF16)], axis=1)
    xs_ref[...] = _dot_tn(p, row).astype(BF16)
    pos_ref[0] = jnp.where(lane == 0, pos1, jnp.where(lane == 1, pos2, 0.0))
    cnt_ref[0] = jnp.broadcast_to(cnt, (8, LANES))


def _moe_dispatch(h2, route):
    ngrp = h2.shape[0]
    per = ROW_GROUP // MOE_TILE
    ntile = ngrp * per
    tri = np.tril(np.ones((MOE_TILE, MOE_TILE), np.float32), -1)
    src_tile = lambda j: jnp.minimum(j, ntile - 1)
    tok = lambda w: pl.BlockSpec((1, MOE_TILE, w), lambda j: (src_tile(j) // per, src_tile(j) % per, 0))
    return pl.pallas_call(
        _moe_dispatch_kernel,
        grid=(ntile + 1,),
        in_specs=[tok(D_MODEL), tok(LANES), pl.BlockSpec((MOE_TILE, MOE_TILE), lambda j: (0, 0))],
        out_specs=[pl.BlockSpec((MOE_TILE_ROWS, MOE_ROW_W), lambda j: (j, 0)),
                   pl.BlockSpec((1, MOE_TILE, LANES), lambda j: (j, 0, 0)),
                   pl.BlockSpec((1, 8, LANES), lambda j: (j, 0, 0))],
        out_shape=[jax.ShapeDtypeStruct(((ntile + 1) * MOE_TILE_ROWS, MOE_ROW_W), BF16),
                   jax.ShapeDtypeStruct((ntile + 1, MOE_TILE, LANES), F32),
                   jax.ShapeDtypeStruct((ntile + 1, 8, LANES), F32)],
        compiler_params=_cparams("arbitrary"),
        name="moe_dispatch",
    )(h2, route, jnp.asarray(tri, BF16))


def _moe_tables(cnt, ntile):
    nblk = ntile * MOE_TILE_CHUNKS // MOE_BLOCK_CHUNKS + N_EXPERTS
    chunks = (cnt + MOE_CHUNK - 1) // MOE_CHUNK
    seg_start = jnp.cumsum(chunks, axis=1) - chunks
    tile_prefix = jnp.cumsum(chunks, axis=0) - chunks
    per_expert = chunks.sum(axis=0)
    blocks = (per_expert + MOE_BLOCK_CHUNKS - 1) // MOE_BLOCK_CHUNKS
    blk_end = jnp.cumsum(blocks)
    n_used = blk_end[-1]
    b = jnp.arange(nblk, dtype=jnp.int32)
    blk_e = jnp.sum(b[:, None] >= blk_end[None, :], axis=1).astype(jnp.int32)
    last_e = jnp.sum((n_used - 1) >= blk_end).astype(jnp.int32)
    blk_e = jnp.clip(jnp.where(b < n_used, blk_e, last_e), 0, N_EXPERTS - 1)
    oh_e = (blk_e[:, None] == jnp.arange(N_EXPERTS)[None, :]).astype(jnp.int32)
    pick = lambda per_tile: jnp.sum(oh_e[:, :, None] * per_tile.T[None], axis=1)
    seg_e, pre_e, chunks_e = pick(seg_start), pick(tile_prefix), pick(chunks)
    first_blk = jnp.sum(oh_e * (blk_end - blocks)[None, :], axis=1)
    k = (b - first_blk)[:, None] * MOE_BLOCK_CHUNKS + jnp.arange(MOE_BLOCK_CHUNKS)[None, :]
    k3 = k[:, :, None]
    in_tile = (pre_e[:, None, :] <= k3) & (k3 < (pre_e + chunks_e)[:, None, :])
    tile_base = (jnp.arange(ntile) * MOE_TILE_CHUNKS)[None, :] + seg_e - pre_e
    src = jnp.sum(jnp.where(in_tile, tile_base[:, None, :] + k3, 0), axis=-1)
    valid = jnp.any(in_tile, axis=-1) & (b < n_used)[:, None]
    spare = ntile * MOE_TILE_CHUNKS + jnp.arange(MOE_BLOCK_CHUNKS)[None, :]
    gather = jnp.where(valid, src, spare + 2 * MOE_BLOCK_CHUNKS).astype(jnp.int32)
    scatter = jnp.where(valid, src, spare + (b % 2)[:, None] * MOE_BLOCK_CHUNKS).astype(jnp.int32)
    return blk_e, gather.reshape(-1), scatter.reshape(-1), n_used.astype(jnp.int32).reshape(1), nblk


def _moe_expert_kernel(blk_e_ref, gather_ref, scatter_ref, nused_ref, xs_hbm, wg_ref, wu_ref, wd_ref,
                       ys_hbm, lhs, obuf, wgb, wub, wdb, in_sem, out_sem):
    b = pl.program_id(0)
    nb = pl.num_programs(0)
    n = nused_ref[0]
    rows_per_blk = MOE_BLOCK_CHUNKS * MOE_CHUNK

    def chunk_copy(blk, slot, c, gather):
        rows = pl.ds(c * MOE_CHUNK, MOE_CHUNK)
        if gather:
            idx = gather_ref[blk * MOE_BLOCK_CHUNKS + c]
            return pltpu.make_async_copy(xs_hbm.at[idx], lhs.at[slot, rows], in_sem.at[slot])
        idx = scatter_ref[blk * MOE_BLOCK_CHUNKS + c]
        dst = ys_hbm.at[idx, pl.ds(0, MOE_CHUNK), pl.ds(0, D_MODEL)]
        return pltpu.make_async_copy(obuf.at[slot, rows], dst, out_sem.at[slot])

    def for_chunks(blk, slot, gather, start):
        for c in range(MOE_BLOCK_CHUNKS):
            cp = chunk_copy(blk, slot, c, gather)
            if start:
                cp.start()
            else:
                cp.wait()

    @pl.when((b == 0) & (n > 0))
    def _():
        for_chunks(0, 0, True, True)

    @pl.when(b < n)
    def _():
        slot = b % 2
        for_chunks(b, slot, True, False)

        @pl.when(b + 1 < n)
        def _():
            for_chunks(b + 1, 1 - slot, True, True)

        @pl.when((b == 0) | (blk_e_ref[b] != blk_e_ref[jnp.maximum(b - 1, 0)]))
        def _():
            wgb[...] = wg_ref[0, 0].astype(BF16)
            wub[...] = wu_ref[0, 0].astype(BF16)
            wdb[...] = wd_ref[0, 0].astype(BF16)

        @pl.when(b >= 2)
        def _():
            for_chunks(b - 2, slot, False, False)

        xa = lhs[slot]
        x = xa[:, :D_MODEL]
        aux = xa[:, D_MODEL:].astype(F32)
        lane = lax.broadcasted_iota(jnp.int32, aux.shape, 1)
        first = lane < MOE_W_PIECES
        w_first = jnp.where(first, aux, 0.0).sum(axis=1, keepdims=True)
        w_second = jnp.where(first | (lane >= 2 * MOE_W_PIECES), 0.0, aux).sum(axis=1, keepdims=True)
        e_first = _lane_col(aux, lane, 2 * MOE_W_PIECES)
        w = jnp.where(e_first == blk_e_ref[b].astype(F32), w_first, w_second)
        hid = _silu(_dot(x, wgb[...])) * _dot(x, wub[...]) * w
        obuf[slot] = _dot(hid.astype(BF16), wdb[...]).astype(BF16)
        for_chunks(b, slot, False, True)

    @pl.when(b == nb - 1)
    def _():
        @pl.when(n >= 2)
        def _():
            for_chunks(n - 2, n % 2, False, False)

        @pl.when(n >= 1)
        def _():
            for_chunks(n - 1, (n - 1) % 2, False, False)


def _moe_experts(xs, blk_e, gather, scatter, n_used, nblk, layer, w_gate, w_up, w_down):
    nchunk = xs.shape[0] // MOE_CHUNK
    rows_per_blk = MOE_BLOCK_CHUNKS * MOE_CHUNK
    wspec = lambda shape: pl.BlockSpec((1, 1) + shape, lambda b, be, g, s, n: (layer, be[b], 0, 0))
    hbm = pl.BlockSpec(memory_space=pl.ANY)
    ys = pl.pallas_call(
        _moe_expert_kernel,
        grid_spec=pltpu.PrefetchScalarGridSpec(
            num_scalar_prefetch=4,
            grid=(nblk,),
            in_specs=[hbm, wspec((D_MODEL, D_EXPERT)), wspec((D_MODEL, D_EXPERT)),
                      wspec((D_EXPERT, D_MODEL))],
            out_specs=hbm,
            scratch_shapes=[pltpu.VMEM((2, rows_per_blk, MOE_ROW_W), BF16),
                            pltpu.VMEM((2, rows_per_blk, D_MODEL), BF16),
                            pltpu.VMEM((D_MODEL, D_EXPERT), BF16),
                            pltpu.VMEM((D_MODEL, D_EXPERT), BF16),
                            pltpu.VMEM((D_EXPERT, D_MODEL), BF16),
                            pltpu.SemaphoreType.DMA((2,)),
                            pltpu.SemaphoreType.DMA((2,))]),
        out_shape=jax.ShapeDtypeStruct((nchunk, MOE_CHUNK, MOE_ROW_W), BF16),
        input_output_aliases={4: 0},
        compiler_params=_cparams("arbitrary"),
        name="moe_experts",
    )(blk_e, gather, scatter, n_used, xs.reshape(nchunk, MOE_CHUNK, MOE_ROW_W), w_gate, w_up, w_down)
    return ys.reshape(nchunk * MOE_CHUNK, MOE_ROW_W)


def _moe_combine_kernel(ys_ref, pos_ref, x_ref, m_ref, o_ref):
    pos = pos_ref[0]
    lane = lax.broadcasted_iota(jnp.int32, pos.shape, 1)
    pos1 = _lane_col(pos, lane, 0)
    pos2 = _lane_col(pos, lane, 1)
    riota = lax.broadcasted_iota(jnp.int32, (pos.shape[0], MOE_TILE_ROWS), 1).astype(F32)
    p = jnp.where((riota == pos1) | (riota == pos2), 1.0, 0.0).astype(BF16)
    o_ref[0] = x_ref[0] + m_ref[0][5:6] * _dot(p, ys_ref[...])


def _moe_combine(ys, pos, x1, mods):
    ngrp = x1.shape[0]
    per = ROW_GROUP // MOE_TILE
    tok = pl.BlockSpec((1, MOE_TILE, D_MODEL), lambda j: (j // per, j % per, 0))
    return pl.pallas_call(
        _moe_combine_kernel,
        grid=(ngrp * per,),
        in_specs=[pl.BlockSpec((MOE_TILE_ROWS, D_MODEL), lambda j: (j, 0)),
                  pl.BlockSpec((1, MOE_TILE, LANES), lambda j: (j, 0, 0)),
                  tok, pl.BlockSpec((1, 6, D_MODEL), lambda j: (j // per, 0, 0))],
        out_specs=tok,
        out_shape=jax.ShapeDtypeStruct(x1.shape, F32),
        compiler_params=_cparams("arbitrary"),
        name="moe_combine",
    )(ys, pos, x1, mods)


def _moe(h2, route, x1, mods, layer, w_gate, w_up, w_down):
    ntile = h2.shape[0] * (ROW_GROUP // MOE_TILE)
    xs, pos, cnt = _moe_dispatch(h2, route)
    cnt = cnt[:ntile, 0, :N_EXPERTS].astype(jnp.int32)
    blk_e, gather, scatter, n_used, nblk = _moe_tables(cnt, ntile)
    ys = _moe_experts(xs, blk_e, gather, scatter, n_used, nblk, layer, w_gate, w_up, w_down)
    return _moe_combine(ys, pos, x1, mods)


def kernel(x_prompt, x_sample, state_hgrn, cache_c_k, cache_c_v, cache_d_k, cache_d_v, c, c_ctx, norm1_g, norm2_g, w_mod, b_mod, even_w_in, even_w_out, hgrn_lower, hgrn_norm_g, hy_conv_w, hy_conv_b, hy_w1, hy_b1, hy_freq1, hy_w2, hy_b2, hy_freq2, hy_w3, hy_bias, odd_w_in, odd_w_out, c_qnorm_g, c_knorm_g, d_qnorm_g, d_knorm_g, d_lambda_q1, d_lambda_k1, d_lambda_q2, d_lambda_k2, d_subln_g, moe_w_grp, moe_b_grp, moe_w_rt, moe_b_rt, moe_w_gate, moe_w_up, moe_w_down):
    depth = w_mod.shape[0]
    n_ctx, seq, _ = x_prompt.shape
    n_lat, dec_seq, _ = x_sample.shape
    g_ctx = n_ctx * seq // ROW_GROUP
    g_lat = n_lat * dec_seq // ROW_GROUP
    assert dec_seq == ROW_GROUP and ROW_GROUP % seq == 0

    cond = jnp.zeros((16, D_MODEL), F32).at[0].set(c_ctx).at[1 : 1 + n_lat].set(c)
    mods = _adaln(cond, w_mod, b_mod).reshape(depth, 16, 6, D_MODEL)
    lower = jnp.cumsum(jax.nn.softmax(hgrn_lower.astype(F32), axis=0), axis=0)

    streams = [
        dict(x=x_prompt.reshape(g_ctx, ROW_GROUP, D_MODEL), ngrp=g_ctx, bsz=n_ctx, seq=seq, ctx=True),
        dict(x=x_sample, ngrp=g_lat, bsz=n_lat, seq=dec_seq, ctx=False),
    ]
    new_state, new_ck, new_cv, new_dk, new_dv = [], [], [], [], []

    for l in range(depth):
        j = l // 2
        w_router = jnp.zeros((D_MODEL, LANES), F32)
        w_router = w_router.at[:, :N_EXPERTS].set(moe_w_rt[l])
        w_router = w_router.at[:, N_EXPERTS : N_EXPERTS + N_GROUPS].set(moe_w_grp[l])
        b_router = jnp.zeros((1, LANES), F32)
        b_router = b_router.at[0, :N_EXPERTS].set(moe_b_rt[l])
        b_router = b_router.at[0, N_EXPERTS : N_EXPERTS + N_GROUPS].set(moe_b_grp[l])
        if l % 2 == 0:
            w_in = even_w_in[j].astype(BF16)
            w_out = even_w_out[j].astype(BF16)
        else:
            w_in = odd_w_in[j].astype(BF16)
            w_out = odd_w_out[j].astype(BF16)
            lam_init = 0.8 - 0.6 * math.exp(-0.3 * l)
            lam = (jnp.exp(jnp.sum(d_lambda_q1[j] * d_lambda_k1[j]))
                   - jnp.exp(jnp.sum(d_lambda_q2[j] * d_lambda_k2[j])) + lam_init)
            lam_row = jnp.full((1, LANES), lam, F32)

        for s in streams:
            ngrp, bsz, sl = s["ngrp"], s["bsz"], s["seq"]
            if s["ctx"]:
                m = jnp.broadcast_to(mods[l, 0][None], (ngrp, 6, D_MODEL))
            else:
                m = mods[l, 1 : 1 + ngrp]
            x = s["x"]
            if l % 2 == 0:
                wa = H_A * DK_A
                splits = [(0, wa), (wa, 2 * wa), (2 * wa, 3 * wa), (3 * wa, 3 * wa + W_A),
                          (3 * wa + W_A, 3 * wa + 2 * W_A), (3 * wa + 2 * W_A, w_in.shape[1])]
                qa, ffa, fba, ia, ga, hy = _norm_proj(x, m, norm1_g[l], w_in, splits)
                per_seq = lambda t: t.reshape(bsz, sl, t.shape[-1])
                if s["ctx"]:
                    s0t = jnp.zeros((bsz, 2, H_A, DK_A, DK_A), F32)
                else:
                    s0t = jnp.swapaxes(state_hgrn[:, j].astype(F32), -1, -2)
                mix_a, s_fin = _hgrn(per_seq(qa), per_seq(ffa), per_seq(fba), per_seq(ia),
                                     per_seq(ga), lower[j], hgrn_norm_g[j], s0t, sl)
                gr, gi = _hyena_filter(sl, hy_w1[j], hy_b1[j], hy_freq1[j], hy_w2[j], hy_b2[j],
                                       hy_freq2[j], hy_w3[j], hy_bias[j])
                mix_b = _hyena(per_seq(hy), hy_conv_w[j], hy_conv_b[j], gr, gi, sl)
                if s["ctx"]:
                    new_state.append(jnp.swapaxes(s_fin, -1, -2))
            else:
                (p,) = _norm_proj(x, m, norm1_g[l], w_in, [(0, w_in.shape[1])])
                qc, qd, kc, vc, kd, vd = _qk_prep(p, c_qnorm_g[j], c_knorm_g[j], d_qnorm_g[j],
                                                  d_knorm_g[j], sl, use_rope=not s["ctx"])
                if s["ctx"]:
                    caches = (None, None, None, None)
                    new_ck.append(kc)
                    new_cv.append(vc)
                    new_dk.append(kd)
                    new_dv.append(vd)
                else:
                    caches = (cache_c_k[:, j : j + 1], cache_c_v[:, j : j + 1],
                              cache_d_k[:, j : j + 1], cache_d_v[:, j : j + 1])
                hps_c, hps_d, tq = (KV_C, H_D, sl) if s["ctx"] else (1, 1, 512)
                mix_a = _attention(qc, kc, vc, caches[0], caches[1], _gqa_kernel, (), KV_C,
                                   (H_C // KV_C) * HD_C, sl, hps_c, tq)
                diff = functools.partial(_diff_kernel, out_scale=1.0 - lam_init)
                mix_b = _attention(qd, kd, vd, caches[2], caches[3], diff,
                                   (lam_row, d_subln_g[j].reshape(1, DV_D)), H_D, 2 * DK_D, sl,
                                   hps_d, tq)
            grp = lambda t: t.reshape(ngrp, ROW_GROUP, t.shape[-1])
            x1, h2, gates = _mix_out(x, grp(mix_a), grp(mix_b), m, norm2_g[l], w_out, w_router, b_router)
            s["x"] = _moe(h2, gates, x1, m, l, moe_w_gate, moe_w_up, moe_w_down)

    y_ctx = streams[0]["x"].reshape(n_ctx, seq, D_MODEL)
    y_lat = streams[1]["x"]
    return (y_ctx, y_lat, jnp.stack(new_state, axis=1), jnp.stack(new_ck, axis=1),
            jnp.stack(new_cv, axis=1), jnp.stack(new_dk, axis=1), jnp.stack(new_dv, axis=1))
```

```python
import functools
import math

import numpy as np
import jax
import jax.numpy as jnp
from jax import lax
from jax.experimental import pallas as pl
from jax.experimental.pallas import tpu as pltpu

F32 = jnp.float32
BF16 = jnp.bfloat16

D_MODEL = 1024
EPS = 1e-6
LOG2E = 1.0 / math.log(2.0)
GRID_W = 64
ROPE_THETA = 10000.0
H_A = 4
DK_A = 128
W_A = 512
CHUNK = 64
SUB = 8
LEVELS = (16, 32, 64)
C_B = 512
FILTER_BANDS = 16
DECAY_MIN = math.log(1e-2) / 1.5
DECAY_MAX = math.log(1e-2) / 0.3
H_C = 4
KV_C = 2
HD_C = 128
H_D = 4
DK_D = 64
DV_D = 128
N_GROUPS = 4
EXP_PER_GROUP = 4
N_EXPERTS = 16
D_EXPERT = 512

ROUTE_LANE = N_EXPERTS + N_GROUPS
MOE_TILE = 512
MOE_CHUNK = 16
MOE_TILE_CHUNKS = 2 * MOE_TILE // MOE_CHUNK + N_EXPERTS
MOE_TILE_ROWS = MOE_TILE_CHUNKS * MOE_CHUNK
MOE_BLOCK_CHUNKS = 16
MOE_ROW_W = D_MODEL + 128
MOE_W_PIECES = 3

LANES = 128
ROW_GROUP = 1024
VMEM_LIMIT_BYTES = 56 * 1024 * 1024


def _cparams(*sem):
    return pltpu.CompilerParams(dimension_semantics=sem, vmem_limit_bytes=VMEM_LIMIT_BYTES)


def _split_bf16(x):
    hi = x.astype(BF16)
    lo = (x - hi.astype(F32)).astype(BF16)
    return hi, lo


def _dot(a, b):
    return jnp.dot(a, b, preferred_element_type=F32)


def _dot3(a, b):
    ah, al = _split_bf16(a)
    bh, bl = _split_bf16(b)
    return _dot(ah, bh) + _dot(al, bh) + _dot(ah, bl)


def _dot_nt(a, b):
    return lax.dot_general(a, b, (((1,), (1,)), ((), ())), preferred_element_type=F32)


def _dot_tn(a, b):
    return lax.dot_general(a, b, (((0,), (0,)), ((), ())), preferred_element_type=F32)


def _silu(x):
    return x * jax.nn.sigmoid(x)


def _rms(x, eps=EPS):
    return x * lax.rsqrt(jnp.mean(x * x, axis=-1, keepdims=True) + eps)


def _adaln_kernel(c_ref, w_ref, b_ref, o_ref):
    s = _silu(c_ref[...])
    o_ref[0] = _dot(s.astype(BF16), w_ref[0].astype(BF16)) + b_ref[0]


def _adaln(cond, w_mod, b_mod):
    depth, _, n = w_mod.shape
    rows = cond.shape[0]
    tn = 1536
    return pl.pallas_call(
        _adaln_kernel,
        grid=(depth, n // tn),
        in_specs=[
            pl.BlockSpec((rows, D_MODEL), lambda l, j: (0, 0)),
            pl.BlockSpec((1, D_MODEL, tn), lambda l, j: (l, 0, j)),
            pl.BlockSpec((1, 1, tn), lambda l, j: (l, 0, j)),
        ],
        out_specs=pl.BlockSpec((1, rows, tn), lambda l, j: (l, 0, j)),
        out_shape=jax.ShapeDtypeStruct((depth, rows, n), F32),
        compiler_params=_cparams("arbitrary", "arbitrary"),
        name="adaln",
    )(cond, w_mod, b_mod.reshape(depth, 1, n))


def _norm_proj_kernel(x_ref, m_ref, g_ref, w_ref, *o_refs, splits):
    m = m_ref[0]
    h = _rms(x_ref[0]) * g_ref[...] * (1.0 + m[1:2]) + m[0:1]
    hb = h.astype(BF16)
    for o_ref, (a, b) in zip(o_refs, splits):
        o_ref[0] = _dot(hb, w_ref[:, a:b])


def _norm_proj(x, mods, g, w_bf16, splits, tm=256):
    ngrp = x.shape[0]
    n = w_bf16.shape[1]
    kern = functools.partial(_norm_proj_kernel, splits=splits)
    return pl.pallas_call(
        kern,
        grid=(ngrp, ROW_GROUP // tm),
        in_specs=[
            pl.BlockSpec((1, tm, D_MODEL), lambda b, i: (b, i, 0)),
            pl.BlockSpec((1, 6, D_MODEL), lambda b, i: (b, 0, 0)),
            pl.BlockSpec((1, D_MODEL), lambda b, i: (0, 0)),
            pl.BlockSpec((D_MODEL, n), lambda b, i: (0, 0)),
        ],
        out_specs=[pl.BlockSpec((1, tm, hi - lo), lambda b, i: (b, i, 0)) for lo, hi in splits],
        out_shape=[jax.ShapeDtypeStruct((ngrp, ROW_GROUP, hi - lo), F32) for lo, hi in splits],
        compiler_params=_cparams("arbitrary", "arbitrary"),
        name="norm_proj",
    )(x, mods, g.reshape(1, D_MODEL), w_bf16)


def _hgrn_constants():
    c = CHUNK
    nblk = 1 + len(LEVELS)
    w = np.zeros((2, nblk * c, c), np.float32)
    for t in range(c):
        w[0, t, : t + 1] = 1.0
    for li, r in enumerate(LEVELS):
        for t in range(c):
            mid = (t // r) * r + r // 2
            if t >= mid:
                w[0, (li + 1) * c + t, mid : t + 1] = 1.0
            else:
                w[0, (li + 1) * c + t, t + 1 : mid] = 1.0
    for blk in range(nblk):
        w[1, blk * c : (blk + 1) * c] = w[0, blk * c : (blk + 1) * c][::-1, ::-1]
    m = np.zeros((2, nblk, c, c), np.float32)
    for li, r in enumerate(LEVELS):
        for t in range(c):
            for s in range(c):
                if t // r == s // r and (t % r) >= r // 2 and (s % r) < r // 2:
                    m[0, li, t, s] = 1.0
    for t in range(c):
        for s in range(c):
            if t // SUB == s // SUB and s <= t:
                m[0, nblk - 1, t, s] = 1.0
    m[1] = m[0][:, ::-1, ::-1]
    sel = np.zeros((SUB * DK_A, c), np.float32)
    for i in range(SUB):
        sel[i * DK_A : (i + 1) * DK_A, i::SUB] = 1.0
    return w, m, sel


def _hgrn_chunk(direction, r0, qa_ref, gate_ref, ia_ref, lb, wcum, masks, sel, states):
    rows = pl.ds(r0, CHUNK)
    q = _silu(qa_ref[0, rows, :])
    v = ia_ref[0, rows, :]
    f = lb + (1.0 - lb) * jax.nn.sigmoid(gate_ref[0, rows, :])
    kk = jnp.maximum(1.0 - f, 0.0)
    lf = jnp.log(f) * LOG2E
    lk = jnp.log(kk) * LOG2E
    lf_hi, lf_lo = _split_bf16(lf)
    z = _dot(wcum, lf_hi) + _dot(wcum, lf_lo)
    e = z[0:CHUNK]
    edge = e[CHUNK - 1 : CHUNK] if direction == 0 else e[0:1]
    q_in = (q * jnp.exp2(e)).astype(BF16)
    k_st = (kk * jnp.exp2(edge - e)).astype(BF16)
    st_decay = jnp.exp2(edge)
    q_lv, k_lv = [], []
    for li in range(len(LEVELS)):
        ez = jnp.exp2(z[(li + 1) * CHUNK : (li + 2) * CHUNK])
        q_lv.append((q * ez).astype(BF16))
        k_lv.append((kk * ez).astype(BF16))
    nsub = CHUNK // SUB
    c3 = (e - lk).reshape(nsub, SUB, W_A)
    pair = []
    for i in range(SUB):
        cb = jnp.broadcast_to(c3[:, i : i + 1, :], (nsub, SUB, W_A)).reshape(CHUNK, W_A)
        pair.append((q * jnp.exp2(jnp.minimum(e - cb, 0.0))).astype(BF16))
    vb = v.astype(BF16)
    outs, new_states = [], []
    for h in range(H_A):
        hs = slice(h * DK_A, (h + 1) * DK_A)
        sc = _dot(jnp.concatenate([p[:, hs] for p in pair], axis=1), sel) * masks[len(LEVELS)]
        for li in range(len(LEVELS)):
            sc = sc + _dot_nt(q_lv[li][:, hs], k_lv[li][:, hs]) * masks[li]
        st = states[h]
        outs.append(_dot(sc.astype(BF16), vb[:, hs]) + _dot_nt(q_in[:, hs], st.astype(BF16)))
        new_states.append(st_decay[:, hs] * st + _dot_tn(vb[:, hs], k_st[:, hs]))
    return jnp.concatenate(outs, axis=1), tuple(new_states)


def _hgrn_kernel(qa_ref, ff_ref, fb_ref, ia_ref, ga_ref, lb_ref, ng_ref, s0_ref, w_ref, m_ref,
                 sel_ref, o_ref, sout_ref, ob_ref, *, seq_len):
    nchunk = seq_len // CHUNK
    lb = lb_ref[...]
    sel = sel_ref[...]
    nmask = 1 + len(LEVELS)

    def scan(i, states):
        rf = pl.multiple_of(i * CHUNK, CHUNK)
        rb = pl.multiple_of((nchunk - 1 - i) * CHUNK, CHUNK)
        o_f, st_f = _hgrn_chunk(0, rf, qa_ref, ff_ref, ia_ref, lb, w_ref[0],
                                [m_ref[0, j] for j in range(nmask)], sel, states[0])
        o_b, st_b = _hgrn_chunk(1, rb, qa_ref, fb_ref, ia_ref, lb, w_ref[1],
                                [m_ref[1, j] for j in range(nmask)], sel, states[1])
        o_ref[0, pl.ds(rf, CHUNK), :] = o_f
        ob_ref[pl.ds(rb, CHUNK), :] = o_b
        return st_f, st_b

    init = tuple(tuple(s0_ref[0, d, h] for h in range(H_A)) for d in range(2))
    final = lax.fori_loop(0, nchunk, scan, init)

    ng = jnp.concatenate([ng_ref[...]] * H_A, axis=1)

    def gate(i, carry):
        rows = pl.ds(pl.multiple_of(i * CHUNK, CHUNK), CHUNK)
        tot = o_ref[0, rows, :] + ob_ref[rows, :]
        normed = jnp.concatenate(
            [_rms(tot[:, h * DK_A : (h + 1) * DK_A]) for h in range(H_A)], axis=1)
        o_ref[0, rows, :] = normed * ng * _silu(ga_ref[0, rows, :])
        return carry

    lax.fori_loop(0, nchunk, gate, 0)
    for d in range(2):
        for h in range(H_A):
            sout_ref[0, d, h] = final[d][h]


def _hgrn(qa, ff, fb, ia, ga, lb, ng, s0t, seq_len):
    bsz = qa.shape[0]
    w, m, sel = _hgrn_constants()
    seq = pl.BlockSpec((1, seq_len, W_A), lambda b: (b, 0, 0))
    full = lambda shape: pl.BlockSpec(shape, lambda b: (0,) * len(shape))
    st_spec = pl.BlockSpec((1, 2, H_A, DK_A, DK_A), lambda b: (b, 0, 0, 0, 0))
    return pl.pallas_call(
        functools.partial(_hgrn_kernel, seq_len=seq_len),
        grid=(bsz,),
        in_specs=[seq, seq, seq, seq, seq, full((1, W_A)), full((1, DK_A)), st_spec,
                  full(w.shape), full(m.shape), full(sel.shape)],
        out_specs=[seq, st_spec],
        out_shape=[jax.ShapeDtypeStruct((bsz, seq_len, W_A), F32),
                   jax.ShapeDtypeStruct((bsz, 2, H_A, DK_A, DK_A), F32)],
        scratch_shapes=[pltpu.VMEM((seq_len, W_A), F32)],
        compiler_params=_cparams("arbitrary"),
        name="hgrn",
    )(qa, ff, fb, ia, ga, lb.reshape(1, W_A), ng.reshape(1, DK_A), s0t,
      jnp.asarray(w, BF16), jnp.asarray(m, F32), jnp.asarray(sel, BF16))


def _dft_constants(seq_len):
    n = 2 * seq_len
    t = np.arange(seq_len, dtype=np.int64)
    wt = (np.arange(seq_len, dtype=np.int64)[:, None] * t[None, :]) % n
    ang = 2.0 * np.pi * wt.astype(np.float64) / n
    cos, sin = np.cos(ang), np.sin(ang)
    nyq = np.where(t % 2 == 0, 1.0, -1.0)
    sin_p = sin.copy()
    sin_p[0] = nyq
    fwd = np.concatenate([cos, sin_p], axis=0)
    icos = 2.0 * cos.T / n
    icos[:, 0] = 1.0 / n
    isin = 2.0 * sin.T / n
    isin[:, 0] = nyq / n
    inv = np.concatenate([icos, isin], axis=1)
    return fwd, inv


def _filter_embedding(seq_len):
    t = np.linspace(0.0, 1.0, seq_len)[:, None]
    w = 2.0 * np.pi * np.arange(seq_len) / seq_len
    f = np.linspace(1e-4, FILTER_BANDS - 1, FILTER_BANDS)
    ang = w[:, None] * f[None, :]
    z = np.concatenate([t, np.cos(ang), -np.sin(ang)], axis=-1)
    zp = np.zeros((seq_len, LANES), np.float64)
    zp[:, : z.shape[1]] = z
    deltas = np.abs(np.linspace(DECAY_MIN, DECAY_MAX, C_B))
    window = np.exp(-t * deltas[None, :])
    return zp.astype(np.float32), window.astype(np.float32)


def _hyena_filter_kernel(z_ref, win_ref, w1_ref, b1_ref, fr1_ref, w2_ref, b2_ref, fr2_ref, w3_ref,
                         hb_ref, fh_ref, fl_ref, gr_ref, gi_ref, *, seq_len):
    h = jnp.sin(fr1_ref[...] * (_dot3(z_ref[...], w1_ref[...]) + b1_ref[...]))
    h = jnp.sin(fr2_ref[...] * (_dot3(h, w2_ref[...]) + b2_ref[...]))
    h = _dot3(h, w3_ref[...])
    win = win_ref[...]
    hf = h[:, :C_B] * win
    hbk = h[:, C_B:] * win

    def dft(x):
        xh, xl = _split_bf16(x)
        return _dot(fh_ref[...], xh) + _dot(fh_ref[...], xl) + _dot(fl_ref[...], xh)

    p_sum = dft(hf + hbk)
    p_dif = dft(hf - hbk)
    row0 = lax.broadcasted_iota(jnp.int32, (seq_len, C_B), 0) == 0
    gr_ref[...] = p_sum[:seq_len] + hb_ref[...]
    gi_ref[...] = jnp.where(row0, p_sum[seq_len:] + hb_ref[...], p_dif[seq_len:])


def _hyena_filter(seq_len, w1, b1, fr1, w2, b2, fr2, w3, hbias):
    zemb, window = _filter_embedding(seq_len)
    fwd, _ = _dft_constants(seq_len)
    f_hi, f_lo = _split_bf16(jnp.asarray(fwd, F32))

    def pad(a, rows, cols):
        return jnp.zeros((rows, cols), F32).at[: a.shape[0], : a.shape[1]].set(a)

    args = (jnp.asarray(zemb), jnp.asarray(window), pad(w1, LANES, LANES), pad(b1[None], 1, LANES),
            pad(fr1[None], 1, LANES), pad(w2, LANES, LANES), pad(b2[None], 1, LANES),
            pad(fr2[None], 1, LANES), pad(w3, LANES, 2 * C_B), hbias.reshape(1, C_B), f_hi, f_lo)
    return pl.pallas_call(
        functools.partial(_hyena_filter_kernel, seq_len=seq_len),
        out_shape=[jax.ShapeDtypeStruct((seq_len, C_B), F32)] * 2,
        compiler_params=pltpu.CompilerParams(vmem_limit_bytes=VMEM_LIMIT_BYTES),
        name="hyena_filter",
    )(*args)


def _hyena_kernel(x0_ref, x1_ref, v_ref, cw_ref, cb_ref, gr_ref, gi_ref, f_ref, fi_ref, o_ref, *,
                  seq_len):
    tc = o_ref.shape[2]
    row = lax.broadcasted_iota(jnp.int32, (seq_len, tc), 0)

    def short_conv(u_ref, j, sq):
        u = u_ref[sq]
        prev = jnp.where(row == 0, 0.0, pltpu.roll(u, 1, 0))
        nxt = jnp.where(row == seq_len - 1, 0.0, pltpu.roll(u, seq_len - 1, 0))
        cw = cw_ref[j]
        return cw[0:1] * prev + cw[1:2] * u + cw[2:3] * nxt + cb_ref[j]

    gr, gi = gr_ref[...], gi_ref[...]
    row0 = row == 0
    for sq in range(o_ref.shape[0]):
        x0 = short_conv(x0_ref, 0, sq)
        z = short_conv(v_ref, 2, sq) * short_conv(x1_ref, 1, sq)
        p = _dot(f_ref[...], z.astype(BF16))
        a, b = p[:seq_len], p[seq_len:]
        bgi = b * gi
        yr = a * gr - jnp.where(row0, 0.0, bgi)
        yq = jnp.where(row0, bgi, a * gi + b * gr)
        y = _dot(fi_ref[...], jnp.concatenate([yr, yq], axis=0).astype(BF16))
        o_ref[sq] = y * x0


def _hyena(hy, conv_w, conv_b, gr, gi, seq_len, nseq, tc=256):
    bsz = hy.shape[0]
    nct = C_B // tc
    fwd, inv = _dft_constants(seq_len)
    cw = conv_w.reshape(3, 3, C_B).transpose(1, 0, 2)
    cb = conv_b.reshape(3, 1, C_B)
    part = lambda k: pl.BlockSpec((nseq, seq_len, tc), lambda b, j, k=k: (b, 0, k * nct + j))
    return pl.pallas_call(
        functools.partial(_hyena_kernel, seq_len=seq_len),
        grid=(bsz // nseq, nct),
        in_specs=[part(0), part(1), part(2),
                  pl.BlockSpec((3, 3, tc), lambda b, j: (0, 0, j)),
                  pl.BlockSpec((3, 1, tc), lambda b, j: (0, 0, j)),
                  pl.BlockSpec((seq_len, tc), lambda b, j: (0, j)),
                  pl.BlockSpec((seq_len, tc), lambda b, j: (0, j)),
                  pl.BlockSpec((2 * seq_len, seq_len), lambda b, j: (0, 0)),
                  pl.BlockSpec((seq_len, 2 * seq_len), lambda b, j: (0, 0))],
        out_specs=pl.BlockSpec((nseq, seq_len, tc), lambda b, j: (b, 0, j)),
        out_shape=jax.ShapeDtypeStruct((bsz, seq_len, C_B), F32),
        compiler_params=_cparams("arbitrary", "arbitrary"),
        name="hyena",
    )(hy, hy, hy, cw, cb, gr, gi, jnp.asarray(fwd, F32).astype(BF16),
      jnp.asarray(inv, F32).astype(BF16))


def _rope_tables(seq_len, dim):
    rows = seq_len // GRID_W
    row_idx = np.repeat(np.arange(rows), GRID_W).astype(np.float64)
    col_idx = np.tile(np.arange(GRID_W), rows).astype(np.float64)
    half = dim // 2
    inv = ROPE_THETA ** (-np.arange(0, half, 2, dtype=np.float64) / half)
    ang = np.concatenate([row_idx[:, None] * inv, col_idx[:, None] * inv], axis=-1)
    cos = np.repeat(np.cos(ang), 2, axis=1)
    sin = np.repeat(np.sin(ang), 2, axis=1)
    sin[:, 0::2] *= -1.0
    reps = LANES // dim
    return (np.tile(cos, (1, reps)).astype(np.float32), np.tile(sin, (1, reps)).astype(np.float32))


def _lane_group_matrices():
    i = np.arange(2 * LANES)
    same = lambda width: (i[:, None] // width == i[None, :] // width).astype(np.float32)
    swap = (i[:, None] == (i[None, :] ^ 1)).astype(np.float32)
    return same(HD_C), same(DK_D), swap


def _qk_prep_kernel(p_ref, cqg_ref, ckg_ref, dqg_ref, dkg_ref, grp_c_ref, grp_d_ref, swap_ref, *rest,
                    use_rope):
    if use_rope:
        cc_ref, sc_ref, cd_ref, sd_ref = rest[:4]
        rest = rest[4:]
    qc_ref, qd_ref, kc_ref, vc_ref, kd_ref, vd_ref = rest
    p = p_ref[0]
    pair = 2 * LANES
    two = lambda r: jnp.concatenate([r[...], r[...]], axis=1)

    def norm(x, grp_ref, width, g_ref, cos_ref, sin_ref):
        ms = _dot((x * x).astype(BF16), grp_ref[...]) * (1.0 / width)
        y = x * lax.rsqrt(ms + EPS) * two(g_ref)
        if not use_rope:
            return y
        return y * two(cos_ref) + _dot(y.astype(BF16), swap_ref[...]) * two(sin_ref)

    norm_c = lambda x, g_ref: norm(x, grp_c_ref, HD_C, g_ref, cc_ref if use_rope else None,
                                   sc_ref if use_rope else None)
    norm_d = lambda x, g_ref: norm(x, grp_d_ref, DK_D, g_ref, cd_ref if use_rope else None,
                                   sd_ref if use_rope else None)
    cols = lambda start, n: p[:, start * LANES : (start + n) * LANES]
    qc_ref[0] = jnp.concatenate([norm_c(cols(2 * i, 2), cqg_ref) for i in range(H_C // 2)], axis=1)
    kc = norm_c(cols(H_C, KV_C), ckg_ref)
    for j in range(KV_C):
        kc_ref[0, j] = kc[:, j * LANES : (j + 1) * LANES]
        vc_ref[0, j] = cols(H_C + KV_C + j, 1)
    base = H_C + 2 * KV_C
    qd_ref[0] = jnp.concatenate(
        [norm_d(cols(base + 2 * i, 2), dqg_ref) for i in range(H_D // 2)], axis=1)
    for i in range(H_D // 2):
        kd = norm_d(cols(base + H_D + 2 * i, 2), dkg_ref)
        for j in range(2):
            kd_ref[0, 2 * i + j] = kd[:, j * LANES : (j + 1) * LANES]
    for j in range(H_D):
        vd_ref[0, j] = cols(base + 2 * H_D + j, 1)
    assert pair == KV_C * HD_C


def _qk_prep(p, cqg, ckg, dqg, dkg, seq_len, use_rope, tm=256):
    ngrp = p.shape[0]
    bsz = ngrp * ROW_GROUP // seq_len
    p = p.reshape(bsz, seq_len, p.shape[2])
    tm = min(tm, seq_len)
    vec = lambda: pl.BlockSpec((1, LANES), lambda b, i: (0, 0))
    mat = lambda: pl.BlockSpec((2 * LANES, 2 * LANES), lambda b, i: (0, 0))
    tab = lambda: pl.BlockSpec((tm, LANES), lambda b, i: (i, 0))
    in_specs = [pl.BlockSpec((1, tm, p.shape[2]), lambda b, i: (b, i, 0)), vec(), vec(), vec(), vec(),
                mat(), mat(), mat()]
    args = [p, cqg.reshape(1, HD_C), ckg.reshape(1, HD_C),
            jnp.tile(dqg.reshape(1, DK_D), (1, 2)), jnp.tile(dkg.reshape(1, DK_D), (1, 2))]
    args += [jnp.asarray(m, BF16) for m in _lane_group_matrices()]
    if use_rope:
        in_specs += [tab(), tab(), tab(), tab()]
        args += [jnp.asarray(a) for a in _rope_tables(seq_len, HD_C) + _rope_tables(seq_len, DK_D)]
    tok = lambda w: pl.BlockSpec((1, tm, w), lambda b, i: (b, i, 0))
    head = lambda nh: pl.BlockSpec((1, nh, tm, LANES), lambda b, i: (b, 0, i, 0))
    tok_shape = lambda w: jax.ShapeDtypeStruct((bsz, seq_len, w), F32)
    head_shape = lambda nh: jax.ShapeDtypeStruct((bsz, nh, seq_len, LANES), F32)
    return pl.pallas_call(
        functools.partial(_qk_prep_kernel, use_rope=use_rope),
        grid=(bsz, seq_len // tm),
        in_specs=in_specs,
        out_specs=[tok(H_C * HD_C), tok(H_D * 2 * DK_D), head(KV_C), head(KV_C), head(H_D), head(H_D)],
        out_shape=[tok_shape(H_C * HD_C), tok_shape(H_D * 2 * DK_D), head_shape(KV_C),
                   head_shape(KV_C), head_shape(H_D), head_shape(H_D)],
        compiler_params=_cparams("arbitrary", "arbitrary"),
        name="qk_prep",
    )(*args)


def _softmax_pv(q_list, kv_list):
    outs = []
    for q in q_list:
        scores = [_dot_nt(q, k) for k, _ in kv_list]
        mx = scores[0].max(axis=1, keepdims=True)
        for s in scores[1:]:
            mx = jnp.maximum(mx, s.max(axis=1, keepdims=True))
        den = 0.0
        acc = 0.0
        for s, (_, v) in zip(scores, kv_list):
            pexp = jnp.exp2(s - mx)
            den = den + pexp.sum(axis=1, keepdims=True)
            acc = acc + _dot(pexp.astype(BF16), v)
        outs.append(acc / den)
    return outs


def _head_kv(j, k_ref, v_ref, cache_refs):
    kv = [(r_k[0, 0, j].astype(BF16), r_v[0, 0, j].astype(BF16)) for r_k, r_v in cache_refs]
    kv.append((k_ref[0, j].astype(BF16), v_ref[0, j].astype(BF16)))
    return kv


def _gqa_kernel(q_ref, k_ref, v_ref, *rest, has_cache):
    cache_refs = [rest[:2]] if has_cache else []
    o_ref = rest[-1]
    g_c = H_C // KV_C
    width = g_c * HD_C
    for j in range(k_ref.shape[1]):
        q = q_ref[0, :, j * width : (j + 1) * width] * (HD_C ** -0.5 * LOG2E)
        qs = [q[:, g * HD_C : (g + 1) * HD_C].astype(BF16) for g in range(g_c)]
        outs = _softmax_pv(qs, _head_kv(j, k_ref, v_ref, cache_refs))
        o_ref[0, :, j * width : (j + 1) * width] = jnp.concatenate(outs, axis=1)


def _diff_kernel(q_ref, k_ref, v_ref, lam_ref, sg_ref, *rest, has_cache, out_scale):
    cache_refs = [rest[:2]] if has_cache else []
    o_ref = rest[-1]
    width = 2 * DK_D
    low = lax.broadcasted_iota(jnp.int32, (q_ref.shape[1], width), 1) < DK_D
    for j in range(k_ref.shape[1]):
        q = q_ref[0, :, j * width : (j + 1) * width] * (DK_D ** -0.5 * LOG2E)
        qs = [jnp.where(low, q, 0.0).astype(BF16), jnp.where(low, 0.0, q).astype(BF16)]
        o1, o2 = _softmax_pv(qs, _head_kv(j, k_ref, v_ref, cache_refs))
        o_ref[0, :, j * width : (j + 1) * width] = (
            _rms(o1 - lam_ref[...] * o2) * sg_ref[...] * out_scale)


def _attention(q, k, v, cache_k, cache_v, kernel, extra_args, n_heads, q_width, seq_len, hps, tq):
    bsz = q.shape[0]
    in_specs = [pl.BlockSpec((1, tq, hps * q_width), lambda b, h, i: (b, i, h)),
                pl.BlockSpec((1, hps, seq_len, LANES), lambda b, h, i: (b, h, 0, 0)),
                pl.BlockSpec((1, hps, seq_len, LANES), lambda b, h, i: (b, h, 0, 0))]
    in_specs += [pl.BlockSpec((1, LANES), lambda b, h, i: (0, 0)) for _ in extra_args]
    args = [q, k, v, *extra_args]
    if cache_k is not None:
        past = cache_k.shape[3]
        spec = lambda: pl.BlockSpec((1, 1, hps, past, LANES), lambda b, h, i: (b, 0, h, 0, 0))
        in_specs += [spec(), spec()]
        args += [cache_k, cache_v]
    return pl.pallas_call(
        functools.partial(kernel, has_cache=cache_k is not None),
        grid=(bsz, n_heads // hps, seq_len // tq),
        in_specs=in_specs,
        out_specs=pl.BlockSpec((1, tq, hps * q_width), lambda b, h, i: (b, i, h)),
        out_shape=jax.ShapeDtypeStruct((bsz, seq_len, n_heads * q_width), F32),
        compiler_params=_cparams("arbitrary", "arbitrary", "arbitrary"),
        name="attention",
    )(*args)


def _route(lg):
    lane = lax.broadcasted_iota(jnp.int32, lg.shape, 1).astype(F32)
    neg = -1e30
    is_g = (lane >= N_EXPERTS) & (lane < N_EXPERTS + N_GROUPS)
    gl = jnp.where(is_g, lg, neg)
    gmax = gl.max(axis=1, keepdims=True)
    g_p = 1.0 / jnp.where(is_g, jnp.exp(gl - gmax), 0.0).sum(axis=1, keepdims=True)
    g_i = jnp.where(gl == gmax, lane - N_EXPERTS, 1e9).min(axis=1, keepdims=True)
    in_grp = (lane < N_EXPERTS) & (jnp.floor(lane * (1.0 / EXP_PER_GROUP)) == g_i)
    el = jnp.where(in_grp, lg, neg)
    m1 = el.max(axis=1, keepdims=True)
    i1 = jnp.where(in_grp & (el == m1), lane, 1e9).min(axis=1, keepdims=True)
    el2 = jnp.where(lane == i1, neg, el)
    m2 = el2.max(axis=1, keepdims=True)
    i2 = jnp.where(in_grp & (el2 == m2) & (lane != i1), lane, 1e9).min(axis=1, keepdims=True)
    r = jnp.exp(m2 - m1)
    w1 = g_p / (1.0 + r)
    rec = jnp.where(lane == ROUTE_LANE, i1, 0.0)
    rec = jnp.where(lane == ROUTE_LANE + 1, i2, rec)
    rec = jnp.where(lane == ROUTE_LANE + 2, w1, rec)
    return jnp.where(lane == ROUTE_LANE + 3, w1 * r, rec)


def _mix_out_kernel(x_ref, a_ref, b_ref, m_ref, g2_ref, w_ref, wrh_ref, wrl_ref, br_ref,
                    x1_ref, h2_ref, gates_ref):
    wa = a_ref.shape[2]
    o = _dot(a_ref[0].astype(BF16), w_ref[:wa]) + _dot(b_ref[0].astype(BF16), w_ref[wa:])
    m = m_ref[0]
    x1 = x_ref[0] + m[2:3] * o
    x1_ref[0] = x1
    h2 = _rms(x1) * g2_ref[...] * (1.0 + m[4:5]) + m[3:4]
    h2_ref[0] = h2.astype(BF16)
    hi, lo = _split_bf16(h2)
    lg = _dot(hi, wrh_ref[...]) + _dot(lo, wrh_ref[...]) + _dot(hi, wrl_ref[...]) + br_ref[...]
    gates_ref[0] = _route(lg)


def _mix_out(x, a, b, mods, g2, w_out_bf16, w_router, b_router, tm=256):
    ngrp = x.shape[0]
    wa, wb = a.shape[2], b.shape[2]
    wr_hi = w_router.astype(BF16)
    wr_lo = (w_router - wr_hi.astype(F32)).astype(BF16)
    row = lambda w: pl.BlockSpec((1, tm, w), lambda g, i: (g, i, 0))
    full = lambda shape: pl.BlockSpec(shape, lambda g, i: (0,) * len(shape))
    return pl.pallas_call(
        _mix_out_kernel,
        grid=(ngrp, ROW_GROUP // tm),
        in_specs=[row(D_MODEL), row(wa), row(wb),
                  pl.BlockSpec((1, 6, D_MODEL), lambda g, i: (g, 0, 0)),
                  full((1, D_MODEL)), full((wa + wb, D_MODEL)),
                  full((D_MODEL, LANES)), full((D_MODEL, LANES)), full((1, LANES))],
        out_specs=[row(D_MODEL), row(D_MODEL), row(LANES)],
        out_shape=[jax.ShapeDtypeStruct((ngrp, ROW_GROUP, D_MODEL), F32),
                   jax.ShapeDtypeStruct((ngrp, ROW_GROUP, D_MODEL), BF16),
                   jax.ShapeDtypeStruct((ngrp, ROW_GROUP, LANES), F32)],
        compiler_params=_cparams("arbitrary", "arbitrary"),
        name="mix_out",
    )(x, a, b, mods, g2.reshape(1, D_MODEL), w_out_bf16, wr_hi, wr_lo, b_router)


def _lane_col(x, lane, k):
    return jnp.where(lane == k, x, 0.0).sum(axis=1, keepdims=True)


def _moe_dispatch_kernel(h_ref, r_ref, tri_ref, xs_ref, pos_ref, cnt_ref):
    r = r_ref[0]
    lane = lax.broadcasted_iota(jnp.int32, r.shape, 1).astype(F32)
    i1, i2, w1, w2 = [_lane_col(r, lane, ROUTE_LANE + k) for k in range(4)]
    oh1 = lane == i1
    oh2 = lane == i2
    oh = jnp.where(oh1 | oh2, 1.0, 0.0)
    rank = _dot(tri_ref[...], oh.astype(BF16))
    cnt = oh.sum(axis=0, keepdims=True)
    chunks = jnp.floor((cnt + (MOE_CHUNK - 1)) * (1.0 / MOE_CHUNK))
    li = lax.broadcasted_iota(jnp.int32, (LANES, LANES), 0)
    lj = lax.broadcasted_iota(jnp.int32, (LANES, LANES), 1)
    before = jnp.where(li < lj, 1.0, 0.0).astype(BF16)
    seg = _dot(jnp.broadcast_to(chunks, (8, LANES)).astype(BF16), before)[0:1]
    base = seg * MOE_CHUNK + rank
    pos1 = jnp.where(oh1, base, 0.0).sum(axis=1, keepdims=True)
    pos2 = jnp.where(oh2, base, 0.0).sum(axis=1, keepdims=True)
    riota = lax.broadcasted_iota(jnp.int32, (r.shape[0], MOE_TILE_ROWS), 1).astype(F32)
    p = jnp.where((riota == pos1) | (riota == pos2), 1.0, 0.0).astype(BF16)

    aux = jnp.where(lane == 2 * MOE_W_PIECES, i1, 0.0)
    for k, w in enumerate((w1, w2)):
        rest = w
        for piece in range(MOE_W_PIECES):
            part = rest.astype(BF16).astype(F32)
            aux = jnp.where(lane == k * MOE_W_PIECES + piece, part, aux)
            rest = rest - part
    row = jnp.concatenate([h_ref[0], aux.astype(BF16)], axis=1)
    xs_ref[...] = _dot_tn(p, row).astype(BF16)
    pos_ref[0] = jnp.where(lane == 0, pos1, jnp.where(lane == 1, pos2, 0.0))
    cnt_ref[0] = jnp.broadcast_to(cnt, (8, LANES))


def _moe_dispatch(h2, route):
    ngrp = h2.shape[0]
    per = ROW_GROUP // MOE_TILE
    ntile = ngrp * per
    tri = np.tril(np.ones((MOE_TILE, MOE_TILE), np.float32), -1)
    src_tile = lambda j: jnp.minimum(j, ntile - 1)
    tok = lambda w: pl.BlockSpec((1, MOE_TILE, w), lambda j: (src_tile(j) // per, src_tile(j) % per, 0))
    return pl.pallas_call(
        _moe_dispatch_kernel,
        grid=(ntile + 1,),
        in_specs=[tok(D_MODEL), tok(LANES), pl.BlockSpec((MOE_TILE, MOE_TILE), lambda j: (0, 0))],
        out_specs=[pl.BlockSpec((MOE_TILE_ROWS, MOE_ROW_W), lambda j: (j, 0)),
                   pl.BlockSpec((1, MOE_TILE, LANES), lambda j: (j, 0, 0)),
                   pl.BlockSpec((1, 8, LANES), lambda j: (j, 0, 0))],
        out_shape=[jax.ShapeDtypeStruct(((ntile + 1) * MOE_TILE_ROWS, MOE_ROW_W), BF16),
                   jax.ShapeDtypeStruct((ntile + 1, MOE_TILE, LANES), F32),
                   jax.ShapeDtypeStruct((ntile + 1, 8, LANES), F32)],
        compiler_params=_cparams("arbitrary"),
        name="moe_dispatch",
    )(h2, route, jnp.asarray(tri, BF16))


def _moe_tables(cnt, ntile):
    nblk = ntile * MOE_TILE_CHUNKS // MOE_BLOCK_CHUNKS + N_EXPERTS
    chunks = (cnt + MOE_CHUNK - 1) // MOE_CHUNK
    seg_start = jnp.cumsum(chunks, axis=1) - chunks
    tile_prefix = jnp.cumsum(chunks, axis=0) - chunks
    per_expert = chunks.sum(axis=0)
    blocks = (per_expert + MOE_BLOCK_CHUNKS - 1) // MOE_BLOCK_CHUNKS
    blk_end = jnp.cumsum(blocks)
    n_used = blk_end[-1]
    b = jnp.arange(nblk + 1, dtype=jnp.int32)
    blk_e = jnp.sum(b[:, None] >= blk_end[None, :], axis=1).astype(jnp.int32)
    last_e = jnp.sum((n_used - 1) >= blk_end).astype(jnp.int32)
    blk_e = jnp.clip(jnp.where(b < n_used, blk_e, last_e), 0, N_EXPERTS - 1)
    oh_e = (blk_e[:, None] == jnp.arange(N_EXPERTS)[None, :]).astype(jnp.int32)
    pick = lambda per_tile: jnp.sum(oh_e[:, :, None] * per_tile.T[None], axis=1)
    seg_e, pre_e, chunks_e = pick(seg_start), pick(tile_prefix), pick(chunks)
    first_blk = jnp.sum(oh_e * (blk_end - blocks)[None, :], axis=1)
    k = (b - first_blk)[:, None] * MOE_BLOCK_CHUNKS + jnp.arange(MOE_BLOCK_CHUNKS)[None, :]
    k3 = k[:, :, None]
    in_tile = (pre_e[:, None, :] <= k3) & (k3 < (pre_e + chunks_e)[:, None, :])
    tile_base = (jnp.arange(ntile) * MOE_TILE_CHUNKS)[None, :] + seg_e - pre_e
    src = jnp.sum(jnp.where(in_tile, tile_base[:, None, :] + k3, 0), axis=-1)
    valid = jnp.any(in_tile, axis=-1) & (b < n_used)[:, None]
    slot_c = jnp.arange(MOE_BLOCK_CHUNKS)[None, :]
    n_read = MOE_TILE_CHUNKS - 2 * MOE_BLOCK_CHUNKS
    assert n_read > 0
    spare = ntile * MOE_TILE_CHUNKS
    gather = jnp.where(valid, src, spare + 2 * MOE_BLOCK_CHUNKS + slot_c % n_read).astype(jnp.int32)
    scatter = jnp.where(valid, src, spare + (b % 2)[:, None] * MOE_BLOCK_CHUNKS + slot_c).astype(jnp.int32)
    return blk_e, gather.reshape(-1), scatter.reshape(-1), n_used.astype(jnp.int32).reshape(1), nblk


def _moe_expert_kernel(blk_e_ref, gather_ref, scatter_ref, nused_ref, xs_hbm, wg_ref, wu_ref, wd_ref,
                       ys_hbm, lhs, obuf, wgb, wub, wdb, in_sem, out_sem):
    b = pl.program_id(0)
    nb = pl.num_programs(0)
    n = nused_ref[0]
    rows_per_blk = MOE_BLOCK_CHUNKS * MOE_CHUNK

    def chunk_copy(blk, slot, c, gather):
        rows = pl.ds(c * MOE_CHUNK, MOE_CHUNK)
        if gather:
            idx = gather_ref[blk * MOE_BLOCK_CHUNKS + c]
            return pltpu.make_async_copy(xs_hbm.at[idx], lhs.at[slot, rows], in_sem.at[slot])
        idx = scatter_ref[blk * MOE_BLOCK_CHUNKS + c]
        dst = ys_hbm.at[idx, pl.ds(0, MOE_CHUNK), pl.ds(0, D_MODEL)]
        return pltpu.make_async_copy(obuf.at[slot, rows], dst, out_sem.at[slot])

    def for_chunks(blk, slot, gather, start):
        for c in range(MOE_BLOCK_CHUNKS):
            cp = chunk_copy(blk, slot, c, gather)
            if start:
                cp.start()
            else:
                cp.wait()

    @pl.when((b == 0) & (n > 0))
    def _():
        for_chunks(0, 0, True, True)

    @pl.when(b < n)
    def _():
        slot = b % 2

        @pl.when(b >= 2)
        def _():
            for_chunks(b - 2, slot, False, False)

        @pl.when((b == 0) | (blk_e_ref[b] != blk_e_ref[jnp.maximum(b - 1, 0)]))
        def _():
            wgb[...] = wg_ref[0, 0].astype(BF16)
            wub[...] = wu_ref[0, 0].astype(BF16)
            wdb[...] = wd_ref[0, 0].astype(BF16)

        for_chunks(b, slot, True, False)
        for_chunks(b + 1, 1 - slot, True, True)
        xa = lhs[slot]
        x = xa[:, :D_MODEL]
        aux = xa[:, D_MODEL:].astype(F32)
        lane = lax.broadcasted_iota(jnp.int32, aux.shape, 1)
        first = lane < MOE_W_PIECES
        w_first = jnp.where(first, aux, 0.0).sum(axis=1, keepdims=True)
        w_second = jnp.where(first | (lane >= 2 * MOE_W_PIECES), 0.0, aux).sum(axis=1, keepdims=True)
        e_first = _lane_col(aux, lane, 2 * MOE_W_PIECES)
        w = jnp.where(e_first == blk_e_ref[b].astype(F32), w_first, w_second)
        hid = _silu(_dot(x, wgb[...])) * _dot(x, wub[...]) * w
        obuf[slot] = _dot(hid.astype(BF16), wdb[...]).astype(BF16)
        for_chunks(b, slot, False, True)

    @pl.when(b == nb - 1)
    def _():
        @pl.when(n >= 2)
        def _():
            for_chunks(n - 2, n % 2, False, False)

        @pl.when(n >= 1)
        def _():
            for_chunks(n, n % 2, True, False)
            for_chunks(n - 1, (n - 1) % 2, False, False)


def _moe_experts(xs, blk_e, gather, scatter, n_used, nblk, layer, w_gate, w_up, w_down):
    nchunk = xs.shape[0] // MOE_CHUNK
    rows_per_blk = MOE_BLOCK_CHUNKS * MOE_CHUNK
    wspec = lambda shape: pl.BlockSpec((1, 1) + shape, lambda b, be, g, s, n: (layer, be[b], 0, 0))
    hbm = pl.BlockSpec(memory_space=pl.ANY)
    ys = pl.pallas_call(
        _moe_expert_kernel,
        grid_spec=pltpu.PrefetchScalarGridSpec(
            num_scalar_prefetch=4,
            grid=(nblk,),
            in_specs=[hbm, wspec((D_MODEL, D_EXPERT)), wspec((D_MODEL, D_EXPERT)),
                      wspec((D_EXPERT, D_MODEL))],
            out_specs=hbm,
            scratch_shapes=[pltpu.VMEM((2, rows_per_blk, MOE_ROW_W), BF16),
                            pltpu.VMEM((2, rows_per_blk, D_MODEL), BF16),
                            pltpu.VMEM((D_MODEL, D_EXPERT), BF16),
                            pltpu.VMEM((D_MODEL, D_EXPERT), BF16),
                            pltpu.VMEM((D_EXPERT, D_MODEL), BF16),
                            pltpu.SemaphoreType.DMA((2,)),
                            pltpu.SemaphoreType.DMA((2,))]),
        out_shape=jax.ShapeDtypeStruct((nchunk, MOE_CHUNK, MOE_ROW_W), BF16),
        input_output_aliases={4: 0},
        compiler_params=_cparams("arbitrary"),
        name="moe_experts",
    )(blk_e, gather, scatter, n_used, xs.reshape(nchunk, MOE_CHUNK, MOE_ROW_W), w_gate, w_up, w_down)
    return ys.reshape(nchunk * MOE_CHUNK, MOE_ROW_W)


def _moe_combine_kernel(ys_ref, pos_ref, x_ref, m_ref, o_ref):
    pos = pos_ref[0]
    lane = lax.broadcasted_iota(jnp.int32, pos.shape, 1)
    pos1 = _lane_col(pos, lane, 0)
    pos2 = _lane_col(pos, lane, 1)
    riota = lax.broadcasted_iota(jnp.int32, (pos.shape[0], MOE_TILE_ROWS), 1).astype(F32)
    p = jnp.where((riota == pos1) | (riota == pos2), 1.0, 0.0).astype(BF16)
    o_ref[0] = x_ref[0] + m_ref[0][5:6] * _dot(p, ys_ref[...])


def _moe_combine(ys, pos, x1, mods):
    ngrp = x1.shape[0]
    per = ROW_GROUP // MOE_TILE
    tok = pl.BlockSpec((1, MOE_TILE, D_MODEL), lambda j: (j // per, j % per, 0))
    return pl.pallas_call(
        _moe_combine_kernel,
        grid=(ngrp * per,),
        in_specs=[pl.BlockSpec((MOE_TILE_ROWS, D_MODEL), lambda j: (j, 0)),
                  pl.BlockSpec((1, MOE_TILE, LANES), lambda j: (j, 0, 0)),
                  tok, pl.BlockSpec((1, 6, D_MODEL), lambda j: (j // per, 0, 0))],
        out_specs=tok,
        out_shape=jax.ShapeDtypeStruct(x1.shape, F32),
        compiler_params=_cparams("arbitrary"),
        name="moe_combine",
    )(ys, pos, x1, mods)


def _moe(h2, route, x1, mods, layer, w_gate, w_up, w_down):
    ntile = h2.shape[0] * (ROW_GROUP // MOE_TILE)
    xs, pos, cnt = _moe_dispatch(h2, route)
    cnt = cnt[:ntile, 0, :N_EXPERTS].astype(jnp.int32)
    blk_e, gather, scatter, n_used, nblk = _moe_tables(cnt, ntile)
    ys = _moe_experts(xs, blk_e, gather, scatter, n_used, nblk, layer, w_gate, w_up, w_down)
    return _moe_combine(ys, pos, x1, mods)


def kernel(x_prompt, x_sample, state_hgrn, cache_c_k, cache_c_v, cache_d_k, cache_d_v, c, c_ctx, norm1_g, norm2_g, w_mod, b_mod, even_w_in, even_w_out, hgrn_lower, hgrn_norm_g, hy_conv_w, hy_conv_b, hy_w1, hy_b1, hy_freq1, hy_w2, hy_b2, hy_freq2, hy_w3, hy_bias, odd_w_in, odd_w_out, c_qnorm_g, c_knorm_g, d_qnorm_g, d_knorm_g, d_lambda_q1, d_lambda_k1, d_lambda_q2, d_lambda_k2, d_subln_g, moe_w_grp, moe_b_grp, moe_w_rt, moe_b_rt, moe_w_gate, moe_w_up, moe_w_down):
    depth = w_mod.shape[0]
    n_ctx, seq, _ = x_prompt.shape
    n_lat, dec_seq, _ = x_sample.shape
    g_ctx = n_ctx * seq // ROW_GROUP
    g_lat = n_lat * dec_seq // ROW_GROUP
    assert dec_seq == ROW_GROUP and ROW_GROUP % seq == 0

    cond = jnp.zeros((16, D_MODEL), F32).at[0].set(c_ctx).at[1 : 1 + n_lat].set(c)
    mods = _adaln(cond, w_mod, b_mod).reshape(depth, 16, 6, D_MODEL)
    lower = jnp.cumsum(jax.nn.softmax(hgrn_lower.astype(F32), axis=0), axis=0)

    streams = [
        dict(x=x_prompt.reshape(g_ctx, ROW_GROUP, D_MODEL), ngrp=g_ctx, bsz=n_ctx, seq=seq, ctx=True),
        dict(x=x_sample, ngrp=g_lat, bsz=n_lat, seq=dec_seq, ctx=False),
    ]
    new_state, new_ck, new_cv, new_dk, new_dv = [], [], [], [], []

    for l in range(depth):
        j = l // 2
        w_router = jnp.zeros((D_MODEL, LANES), F32)
        w_router = w_router.at[:, :N_EXPERTS].set(moe_w_rt[l])
        w_router = w_router.at[:, N_EXPERTS : N_EXPERTS + N_GROUPS].set(moe_w_grp[l])
        b_router = jnp.zeros((1, LANES), F32)
        b_router = b_router.at[0, :N_EXPERTS].set(moe_b_rt[l])
        b_router = b_router.at[0, N_EXPERTS : N_EXPERTS + N_GROUPS].set(moe_b_grp[l])
        if l % 2 == 0:
            w_in = even_w_in[j].astype(BF16)
            w_out = even_w_out[j].astype(BF16)
        else:
            w_in = odd_w_in[j].astype(BF16)
            w_out = odd_w_out[j].astype(BF16)
            lam_init = 0.8 - 0.6 * math.exp(-0.3 * l)
            lam = (jnp.exp(jnp.sum(d_lambda_q1[j] * d_lambda_k1[j]))
                   - jnp.exp(jnp.sum(d_lambda_q2[j] * d_lambda_k2[j])) + lam_init)
            lam_row = jnp.full((1, LANES), lam, F32)

        for s in streams:
            ngrp, bsz, sl = s["ngrp"], s["bsz"], s["seq"]
            if s["ctx"]:
                m = jnp.broadcast_to(mods[l, 0][None], (ngrp, 6, D_MODEL))
            else:
                m = mods[l, 1 : 1 + ngrp]
            x = s["x"]
            if l % 2 == 0:
                wa = H_A * DK_A
                splits = [(0, wa), (wa, 2 * wa), (2 * wa, 3 * wa), (3 * wa, 3 * wa + W_A),
                          (3 * wa + W_A, 3 * wa + 2 * W_A), (3 * wa + 2 * W_A, w_in.shape[1])]
                qa, ffa, fba, ia, ga, hy = _norm_proj(x, m, norm1_g[l], w_in, splits)
                per_seq = lambda t: t.reshape(bsz, sl, t.shape[-1])
                if s["ctx"]:
                    s0t = jnp.zeros((bsz, 2, H_A, DK_A, DK_A), F32)
                else:
                    s0t = jnp.swapaxes(state_hgrn[:, j].astype(F32), -1, -2)
                mix_a, s_fin = _hgrn(per_seq(qa), per_seq(ffa), per_seq(fba), per_seq(ia),
                                     per_seq(ga), lower[j], hgrn_norm_g[j], s0t, sl)
                gr, gi = _hyena_filter(sl, hy_w1[j], hy_b1[j], hy_freq1[j], hy_w2[j], hy_b2[j],
                                       hy_freq2[j], hy_w3[j], hy_bias[j])
                mix_b = _hyena(per_seq(hy), hy_conv_w[j], hy_conv_b[j], gr, gi, sl,
                               nseq=4 if s["ctx"] else 2)
                if s["ctx"]:
                    new_state.append(jnp.swapaxes(s_fin, -1, -2))
            else:
                (p,) = _norm_proj(x, m, norm1_g[l], w_in, [(0, w_in.shape[1])])
                qc, qd, kc, vc, kd, vd = _qk_prep(p, c_qnorm_g[j], c_knorm_g[j], d_qnorm_g[j],
                                                  d_knorm_g[j], sl, use_rope=not s["ctx"])
                if s["ctx"]:
                    caches = (None, None, None, None)
                    new_ck.append(kc)
                    new_cv.append(vc)
                    new_dk.append(kd)
                    new_dv.append(vd)
                else:
                    caches = (cache_c_k[:, j : j + 1], cache_c_v[:, j : j + 1],
                              cache_d_k[:, j : j + 1], cache_d_v[:, j : j + 1])
                hps_c, hps_d, tq = (KV_C, H_D, sl) if s["ctx"] else (1, 1, 512)
                mix_a = _attention(qc, kc, vc, caches[0], caches[1], _gqa_kernel, (), KV_C,
                                   (H_C // KV_C) * HD_C, sl, hps_c, tq)
                diff = functools.partial(_diff_kernel, out_scale=1.0 - lam_init)
                mix_b = _attention(qd, kd, vd, caches[2], caches[3], diff,
                                   (lam_row, d_subln_g[j].reshape(1, DV_D)), H_D, 2 * DK_D, sl,
                                   hps_d, tq)
            grp = lambda t: t.reshape(ngrp, ROW_GROUP, t.shape[-1])
            x1, h2, gates = _mix_out(x, grp(mix_a), grp(mix_b), m, norm2_g[l], w_out, w_router, b_router)
            s["x"] = _moe(h2, gates, x1, m, l, moe_w_gate, moe_w_up, moe_w_down)

    y_ctx = streams[0]["x"].reshape(n_ctx, seq, D_MODEL)
    y_lat = streams[1]["x"]
    return (y_ctx, y_lat, jnp.stack(new_state, axis=1), jnp.stack(new_ck, axis=1),
            jnp.stack(new_cv, axis=1), jnp.stack(new_dk, axis=1), jnp.stack(new_dv, axis=1))
```

```python
import functools
import math

import numpy as np
import jax
import jax.numpy as jnp
from jax import lax
from jax.experimental import pallas as pl
from jax.experimental.pallas import tpu as pltpu

F32 = jnp.float32
BF16 = jnp.bfloat16

D_MODEL = 1024
EPS = 1e-6
LOG2E = 1.0 / math.log(2.0)
GRID_W = 64
ROPE_THETA = 10000.0
H_A = 4
DK_A = 128
W_A = 512
CHUNK = 64
SUB = 8
LEVELS = (16, 32, 64)
C_B = 512
FILTER_BANDS = 16
DECAY_MIN = math.log(1e-2) / 1.5
DECAY_MAX = math.log(1e-2) / 0.3
H_C = 4
KV_C = 2
HD_C = 128
H_D = 4
DK_D = 64
DV_D = 128
N_GROUPS = 4
EXP_PER_GROUP = 4
N_EXPERTS = 16
D_EXPERT = 512

ROUTE_LANE = N_EXPERTS + N_GROUPS
MOE_TILE = 512
MOE_CHUNK = 16
MOE_TILE_CHUNKS = 2 * MOE_TILE // MOE_CHUNK + N_EXPERTS
MOE_TILE_ROWS = MOE_TILE_CHUNKS * MOE_CHUNK
MOE_BLOCK_CHUNKS = 16
MOE_ROW_W = D_MODEL + 128
MOE_W_PIECES = 3

LANES = 128
ROW_GROUP = 1024
VMEM_LIMIT_BYTES = 56 * 1024 * 1024


def _cparams(*sem):
    return pltpu.CompilerParams(dimension_semantics=sem, vmem_limit_bytes=VMEM_LIMIT_BYTES)


def _split_bf16(x):
    hi = x.astype(BF16)
    lo = (x - hi.astype(F32)).astype(BF16)
    return hi, lo


def _dot(a, b):
    return jnp.dot(a, b, preferred_element_type=F32)


def _dot3(a, b):
    ah, al = _split_bf16(a)
    bh, bl = _split_bf16(b)
    return _dot(ah, bh) + _dot(al, bh) + _dot(ah, bl)


def _dot_nt(a, b):
    return lax.dot_general(a, b, (((1,), (1,)), ((), ())), preferred_element_type=F32)


def _dot_tn(a, b):
    return lax.dot_general(a, b, (((0,), (0,)), ((), ())), preferred_element_type=F32)


def _silu(x):
    return x * jax.nn.sigmoid(x)


def _rms(x, eps=EPS):
    return x * lax.rsqrt(jnp.mean(x * x, axis=-1, keepdims=True) + eps)


def _adaln_kernel(c_ref, w_ref, b_ref, o_ref):
    s = _silu(c_ref[...])
    o_ref[0] = _dot(s.astype(BF16), w_ref[0].astype(BF16)) + b_ref[0]


def _adaln(cond, w_mod, b_mod):
    depth, _, n = w_mod.shape
    rows = cond.shape[0]
    tn = 1536
    return pl.pallas_call(
        _adaln_kernel,
        grid=(depth, n // tn),
        in_specs=[
            pl.BlockSpec((rows, D_MODEL), lambda l, j: (0, 0)),
            pl.BlockSpec((1, D_MODEL, tn), lambda l, j: (l, 0, j)),
            pl.BlockSpec((1, 1, tn), lambda l, j: (l, 0, j)),
        ],
        out_specs=pl.BlockSpec((1, rows, tn), lambda l, j: (l, 0, j)),
        out_shape=jax.ShapeDtypeStruct((depth, rows, n), F32),
        compiler_params=_cparams("arbitrary", "arbitrary"),
        name="adaln",
    )(cond, w_mod, b_mod.reshape(depth, 1, n))


def _norm_proj_kernel(x_ref, m_ref, g_ref, w_ref, *o_refs, splits):
    m = m_ref[0]
    h = _rms(x_ref[0]) * g_ref[...] * (1.0 + m[1:2]) + m[0:1]
    hb = h.astype(BF16)
    for o_ref, (a, b) in zip(o_refs, splits):
        o_ref[0] = _dot(hb, w_ref[:, a:b])


def _norm_proj(x, mods, g, w_bf16, splits, tm=256):
    ngrp = x.shape[0]
    n = w_bf16.shape[1]
    kern = functools.partial(_norm_proj_kernel, splits=splits)
    return pl.pallas_call(
        kern,
        grid=(ngrp, ROW_GROUP // tm),
        in_specs=[
            pl.BlockSpec((1, tm, D_MODEL), lambda b, i: (b, i, 0)),
            pl.BlockSpec((1, 6, D_MODEL), lambda b, i: (b, 0, 0)),
            pl.BlockSpec((1, D_MODEL), lambda b, i: (0, 0)),
            pl.BlockSpec((D_MODEL, n), lambda b, i: (0, 0)),
        ],
        out_specs=[pl.BlockSpec((1, tm, hi - lo), lambda b, i: (b, i, 0)) for lo, hi in splits],
        out_shape=[jax.ShapeDtypeStruct((ngrp, ROW_GROUP, hi - lo), F32) for lo, hi in splits],
        compiler_params=_cparams("arbitrary", "arbitrary"),
        name="norm_proj",
    )(x, mods, g.reshape(1, D_MODEL), w_bf16)


def _hgrn_constants():
    c = CHUNK
    nblk = 1 + len(LEVELS)
    w = np.zeros((2, nblk * c, c), np.float32)
    for t in range(c):
        w[0, t, : t + 1] = 1.0
    for li, r in enumerate(LEVELS):
        for t in range(c):
            mid = (t // r) * r + r // 2
            if t >= mid:
                w[0, (li + 1) * c + t, mid : t + 1] = 1.0
            else:
                w[0, (li + 1) * c + t, t + 1 : mid] = 1.0
    for blk in range(nblk):
        w[1, blk * c : (blk + 1) * c] = w[0, blk * c : (blk + 1) * c][::-1, ::-1]
    m = np.zeros((2, nblk, c, c), np.float32)
    for li, r in enumerate(LEVELS):
        for t in range(c):
            for s in range(c):
                if t // r == s // r and (t % r) >= r // 2 and (s % r) < r // 2:
                    m[0, li, t, s] = 1.0
    for t in range(c):
        for s in range(c):
            if t // SUB == s // SUB and s <= t:
                m[0, nblk - 1, t, s] = 1.0
    m[1] = m[0][:, ::-1, ::-1]
    sel = np.zeros((SUB * DK_A, c), np.float32)
    for i in range(SUB):
        sel[i * DK_A : (i + 1) * DK_A, i::SUB] = 1.0
    return w, m, sel


def _hgrn_chunk(direction, r0, qa_ref, gate_ref, ia_ref, lb, wcum, masks, sel, states):
    rows = pl.ds(r0, CHUNK)
    q = _silu(qa_ref[0, rows, :])
    v = ia_ref[0, rows, :]
    f = lb + (1.0 - lb) * jax.nn.sigmoid(gate_ref[0, rows, :])
    kk = jnp.maximum(1.0 - f, 0.0)
    lf = jnp.log(f) * LOG2E
    lk = jnp.log(kk) * LOG2E
    lf_hi, lf_lo = _split_bf16(lf)
    z = _dot(wcum, lf_hi) + _dot(wcum, lf_lo)
    e = z[0:CHUNK]
    edge = e[CHUNK - 1 : CHUNK] if direction == 0 else e[0:1]
    q_in = (q * jnp.exp2(e)).astype(BF16)
    k_st = (kk * jnp.exp2(edge - e)).astype(BF16)
    st_decay = jnp.exp2(edge)
    q_lv, k_lv = [], []
    for li in range(len(LEVELS)):
        ez = jnp.exp2(z[(li + 1) * CHUNK : (li + 2) * CHUNK])
        q_lv.append((q * ez).astype(BF16))
        k_lv.append((kk * ez).astype(BF16))
    nsub = CHUNK // SUB
    c3 = (e - lk).reshape(nsub, SUB, W_A)
    pair = []
    for i in range(SUB):
        cb = jnp.broadcast_to(c3[:, i : i + 1, :], (nsub, SUB, W_A)).reshape(CHUNK, W_A)
        pair.append((q * jnp.exp2(jnp.minimum(e - cb, 0.0))).astype(BF16))
    vb = v.astype(BF16)
    outs, new_states = [], []
    for h in range(H_A):
        hs = slice(h * DK_A, (h + 1) * DK_A)
        sc = _dot(jnp.concatenate([p[:, hs] for p in pair], axis=1), sel) * masks[len(LEVELS)]
        for li in range(len(LEVELS)):
            sc = sc + _dot_nt(q_lv[li][:, hs], k_lv[li][:, hs]) * masks[li]
        st = states[h]
        outs.append(_dot(sc.astype(BF16), vb[:, hs]) + _dot_nt(q_in[:, hs], st.astype(BF16)))
        new_states.append(st_decay[:, hs] * st + _dot_tn(vb[:, hs], k_st[:, hs]))
    return jnp.concatenate(outs, axis=1), tuple(new_states)


def _hgrn_kernel(qa_ref, ff_ref, fb_ref, ia_ref, ga_ref, lb_ref, ng_ref, s0_ref, w_ref, m_ref,
                 sel_ref, o_ref, sout_ref, ob_ref, *, seq_len):
    nchunk = seq_len // CHUNK
    lb = lb_ref[...]
    sel = sel_ref[...]
    nmask = 1 + len(LEVELS)

    def scan(i, states):
        rf = pl.multiple_of(i * CHUNK, CHUNK)
        rb = pl.multiple_of((nchunk - 1 - i) * CHUNK, CHUNK)
        o_f, st_f = _hgrn_chunk(0, rf, qa_ref, ff_ref, ia_ref, lb, w_ref[0],
                                [m_ref[0, j] for j in range(nmask)], sel, states[0])
        o_b, st_b = _hgrn_chunk(1, rb, qa_ref, fb_ref, ia_ref, lb, w_ref[1],
                                [m_ref[1, j] for j in range(nmask)], sel, states[1])
        o_ref[0, pl.ds(rf, CHUNK), :] = o_f
        ob_ref[pl.ds(rb, CHUNK), :] = o_b
        return st_f, st_b

    init = tuple(tuple(s0_ref[0, d, h] for h in range(H_A)) for d in range(2))
    final = lax.fori_loop(0, nchunk, scan, init)

    ng = jnp.concatenate([ng_ref[...]] * H_A, axis=1)

    def gate(i, carry):
        rows = pl.ds(pl.multiple_of(i * CHUNK, CHUNK), CHUNK)
        tot = o_ref[0, rows, :] + ob_ref[rows, :]
        normed = jnp.concatenate(
            [_rms(tot[:, h * DK_A : (h + 1) * DK_A]) for h in range(H_A)], axis=1)
        o_ref[0, rows, :] = normed * ng * _silu(ga_ref[0, rows, :])
        return carry

    lax.fori_loop(0, nchunk, gate, 0)
    for d in range(2):
        for h in range(H_A):
            sout_ref[0, d, h] = final[d][h]


def _hgrn(qa, ff, fb, ia, ga, lb, ng, s0t, seq_len):
    bsz = qa.shape[0]
    w, m, sel = _hgrn_constants()
    seq = pl.BlockSpec((1, seq_len, W_A), lambda b: (b, 0, 0))
    full = lambda shape: pl.BlockSpec(shape, lambda b: (0,) * len(shape))
    st_spec = pl.BlockSpec((1, 2, H_A, DK_A, DK_A), lambda b: (b, 0, 0, 0, 0))
    return pl.pallas_call(
        functools.partial(_hgrn_kernel, seq_len=seq_len),
        grid=(bsz,),
        in_specs=[seq, seq, seq, seq, seq, full((1, W_A)), full((1, DK_A)), st_spec,
                  full(w.shape), full(m.shape), full(sel.shape)],
        out_specs=[seq, st_spec],
        out_shape=[jax.ShapeDtypeStruct((bsz, seq_len, W_A), F32),
                   jax.ShapeDtypeStruct((bsz, 2, H_A, DK_A, DK_A), F32)],
        scratch_shapes=[pltpu.VMEM((seq_len, W_A), F32)],
        compiler_params=_cparams("arbitrary"),
        name="hgrn",
    )(qa, ff, fb, ia, ga, lb.reshape(1, W_A), ng.reshape(1, DK_A), s0t,
      jnp.asarray(w, BF16), jnp.asarray(m, F32), jnp.asarray(sel, BF16))


def _dft_constants(seq_len):
    n = 2 * seq_len
    t = np.arange(seq_len, dtype=np.int64)
    wt = (np.arange(seq_len, dtype=np.int64)[:, None] * t[None, :]) % n
    ang = 2.0 * np.pi * wt.astype(np.float64) / n
    cos, sin = np.cos(ang), np.sin(ang)
    nyq = np.where(t % 2 == 0, 1.0, -1.0)
    sin_p = sin.copy()
    sin_p[0] = nyq
    fwd = np.concatenate([cos, sin_p], axis=0)
    icos = 2.0 * cos.T / n
    icos[:, 0] = 1.0 / n
    isin = 2.0 * sin.T / n
    isin[:, 0] = nyq / n
    inv = np.concatenate([icos, isin], axis=1)
    return fwd, inv


def _filter_embedding(seq_len):
    t = np.linspace(0.0, 1.0, seq_len)[:, None]
    w = 2.0 * np.pi * np.arange(seq_len) / seq_len
    f = np.linspace(1e-4, FILTER_BANDS - 1, FILTER_BANDS)
    ang = w[:, None] * f[None, :]
    z = np.concatenate([t, np.cos(ang), -np.sin(ang)], axis=-1)
    zp = np.zeros((seq_len, LANES), np.float64)
    zp[:, : z.shape[1]] = z
    deltas = np.abs(np.linspace(DECAY_MIN, DECAY_MAX, C_B))
    window = np.exp(-t * deltas[None, :])
    return zp.astype(np.float32), window.astype(np.float32)


def _hyena_filter_kernel(z_ref, win_ref, w1_ref, b1_ref, fr1_ref, w2_ref, b2_ref, fr2_ref, w3_ref,
                         hb_ref, fh_ref, fl_ref, gr_ref, gi_ref, *, seq_len):
    h = jnp.sin(fr1_ref[...] * (_dot3(z_ref[...], w1_ref[...]) + b1_ref[...]))
    h = jnp.sin(fr2_ref[...] * (_dot3(h, w2_ref[...]) + b2_ref[...]))
    h = _dot3(h, w3_ref[...])
    win = win_ref[...]
    hf = h[:, :C_B] * win
    hbk = h[:, C_B:] * win

    def dft(x):
        xh, xl = _split_bf16(x)
        return _dot(fh_ref[...], xh) + _dot(fh_ref[...], xl) + _dot(fl_ref[...], xh)

    p_sum = dft(hf + hbk)
    p_dif = dft(hf - hbk)
    row0 = lax.broadcasted_iota(jnp.int32, (seq_len, C_B), 0) == 0
    gr_ref[...] = p_sum[:seq_len] + hb_ref[...]
    gi_ref[...] = jnp.where(row0, p_sum[seq_len:] + hb_ref[...], p_dif[seq_len:])


def _hyena_filter(seq_len, w1, b1, fr1, w2, b2, fr2, w3, hbias):
    zemb, window = _filter_embedding(seq_len)
    fwd, _ = _dft_constants(seq_len)
    f_hi, f_lo = _split_bf16(jnp.asarray(fwd, F32))

    def pad(a, rows, cols):
        return jnp.zeros((rows, cols), F32).at[: a.shape[0], : a.shape[1]].set(a)

    args = (jnp.asarray(zemb), jnp.asarray(window), pad(w1, LANES, LANES), pad(b1[None], 1, LANES),
            pad(fr1[None], 1, LANES), pad(w2, LANES, LANES), pad(b2[None], 1, LANES),
            pad(fr2[None], 1, LANES), pad(w3, LANES, 2 * C_B), hbias.reshape(1, C_B), f_hi, f_lo)
    return pl.pallas_call(
        functools.partial(_hyena_filter_kernel, seq_len=seq_len),
        out_shape=[jax.ShapeDtypeStruct((seq_len, C_B), F32)] * 2,
        compiler_params=pltpu.CompilerParams(vmem_limit_bytes=VMEM_LIMIT_BYTES),
        name="hyena_filter",
    )(*args)


def _hyena_kernel(x0_ref, x1_ref, v_ref, cw_ref, cb_ref, gr_ref, gi_ref, f_ref, fi_ref, o_ref, *,
                  seq_len):
    tc = o_ref.shape[2]
    row = lax.broadcasted_iota(jnp.int32, (seq_len, tc), 0)

    def short_conv(u_ref, j, sq):
        u = u_ref[sq]
        prev = jnp.where(row == 0, 0.0, pltpu.roll(u, 1, 0))
        nxt = jnp.where(row == seq_len - 1, 0.0, pltpu.roll(u, seq_len - 1, 0))
        cw = cw_ref[j]
        return cw[0:1] * prev + cw[1:2] * u + cw[2:3] * nxt + cb_ref[j]

    gr, gi = gr_ref[...], gi_ref[...]
    row0 = row == 0
    for sq in range(o_ref.shape[0]):
        x0 = short_conv(x0_ref, 0, sq)
        z = short_conv(v_ref, 2, sq) * short_conv(x1_ref, 1, sq)
        p = _dot(f_ref[...], z.astype(BF16))
        a, b = p[:seq_len], p[seq_len:]
        bgi = b * gi
        yr = a * gr - jnp.where(row0, 0.0, bgi)
        yq = jnp.where(row0, bgi, a * gi + b * gr)
        y = _dot(fi_ref[...], jnp.concatenate([yr, yq], axis=0).astype(BF16))
        o_ref[sq] = y * x0


def _hyena(hy, conv_w, conv_b, gr, gi, seq_len, nseq, tc=256):
    bsz = hy.shape[0]
    nct = C_B // tc
    fwd, inv = _dft_constants(seq_len)
    cw = conv_w.reshape(3, 3, C_B).transpose(1, 0, 2)
    cb = conv_b.reshape(3, 1, C_B)
    part = lambda k: pl.BlockSpec((nseq, seq_len, tc), lambda b, j, k=k: (b, 0, k * nct + j))
    return pl.pallas_call(
        functools.partial(_hyena_kernel, seq_len=seq_len),
        grid=(bsz // nseq, nct),
        in_specs=[part(0), part(1), part(2),
                  pl.BlockSpec((3, 3, tc), lambda b, j: (0, 0, j)),
                  pl.BlockSpec((3, 1, tc), lambda b, j: (0, 0, j)),
                  pl.BlockSpec((seq_len, tc), lambda b, j: (0, j)),
                  pl.BlockSpec((seq_len, tc), lambda b, j: (0, j)),
                  pl.BlockSpec((2 * seq_len, seq_len), lambda b, j: (0, 0)),
                  pl.BlockSpec((seq_len, 2 * seq_len), lambda b, j: (0, 0))],
        out_specs=pl.BlockSpec((nseq, seq_len, tc), lambda b, j: (b, 0, j)),
        out_shape=jax.ShapeDtypeStruct((bsz, seq_len, C_B), F32),
        compiler_params=_cparams("arbitrary", "arbitrary"),
        name="hyena",
    )(hy, hy, hy, cw, cb, gr, gi, jnp.asarray(fwd, F32).astype(BF16),
      jnp.asarray(inv, F32).astype(BF16))


def _rope_tables(seq_len, dim):
    rows = seq_len // GRID_W
    row_idx = np.repeat(np.arange(rows), GRID_W).astype(np.float64)
    col_idx = np.tile(np.arange(GRID_W), rows).astype(np.float64)
    half = dim // 2
    inv = ROPE_THETA ** (-np.arange(0, half, 2, dtype=np.float64) / half)
    ang = np.concatenate([row_idx[:, None] * inv, col_idx[:, None] * inv], axis=-1)
    cos = np.repeat(np.cos(ang), 2, axis=1)
    sin = np.repeat(np.sin(ang), 2, axis=1)
    sin[:, 0::2] *= -1.0
    reps = LANES // dim
    return (np.tile(cos, (1, reps)).astype(np.float32), np.tile(sin, (1, reps)).astype(np.float32))


def _lane_group_matrices():
    i = np.arange(2 * LANES)
    same = lambda width: (i[:, None] // width == i[None, :] // width).astype(np.float32)
    swap = (i[:, None] == (i[None, :] ^ 1)).astype(np.float32)
    return same(HD_C), same(DK_D), swap


def _qk_prep_kernel(p_ref, cqg_ref, ckg_ref, dqg_ref, dkg_ref, grp_c_ref, grp_d_ref, swap_ref, *rest,
                    use_rope):
    if use_rope:
        cc_ref, sc_ref, cd_ref, sd_ref = rest[:4]
        rest = rest[4:]
    qc_ref, qd_ref, kc_ref, vc_ref, kd_ref, vd_ref = rest
    p = p_ref[0]
    pair = 2 * LANES
    two = lambda r: jnp.concatenate([r[...], r[...]], axis=1)

    def norm(x, grp_ref, width, g_ref, cos_ref, sin_ref):
        ms = _dot((x * x).astype(BF16), grp_ref[...]) * (1.0 / width)
        y = x * lax.rsqrt(ms + EPS) * two(g_ref)
        if not use_rope:
            return y
        return y * two(cos_ref) + _dot(y.astype(BF16), swap_ref[...]) * two(sin_ref)

    norm_c = lambda x, g_ref: norm(x, grp_c_ref, HD_C, g_ref, cc_ref if use_rope else None,
                                   sc_ref if use_rope else None)
    norm_d = lambda x, g_ref: norm(x, grp_d_ref, DK_D, g_ref, cd_ref if use_rope else None,
                                   sd_ref if use_rope else None)
    cols = lambda start, n: p[:, start * LANES : (start + n) * LANES]
    qc_ref[0] = jnp.concatenate([norm_c(cols(2 * i, 2), cqg_ref) for i in range(H_C // 2)], axis=1)
    kc = norm_c(cols(H_C, KV_C), ckg_ref)
    for j in range(KV_C):
        kc_ref[0, j] = kc[:, j * LANES : (j + 1) * LANES]
        vc_ref[0, j] = cols(H_C + KV_C + j, 1)
    base = H_C + 2 * KV_C
    qd_ref[0] = jnp.concatenate(
        [norm_d(cols(base + 2 * i, 2), dqg_ref) for i in range(H_D // 2)], axis=1)
    for i in range(H_D // 2):
        kd = norm_d(cols(base + H_D + 2 * i, 2), dkg_ref)
        for j in range(2):
            kd_ref[0, 2 * i + j] = kd[:, j * LANES : (j + 1) * LANES]
    for j in range(H_D):
        vd_ref[0, j] = cols(base + 2 * H_D + j, 1)
    assert pair == KV_C * HD_C


def _qk_prep(p, cqg, ckg, dqg, dkg, seq_len, use_rope, tm=256):
    ngrp = p.shape[0]
    bsz = ngrp * ROW_GROUP // seq_len
    p = p.reshape(bsz, seq_len, p.shape[2])
    tm = min(tm, seq_len)
    vec = lambda: pl.BlockSpec((1, LANES), lambda b, i: (0, 0))
    mat = lambda: pl.BlockSpec((2 * LANES, 2 * LANES), lambda b, i: (0, 0))
    tab = lambda: pl.BlockSpec((tm, LANES), lambda b, i: (i, 0))
    in_specs = [pl.BlockSpec((1, tm, p.shape[2]), lambda b, i: (b, i, 0)), vec(), vec(), vec(), vec(),
                mat(), mat(), mat()]
    args = [p, cqg.reshape(1, HD_C), ckg.reshape(1, HD_C),
            jnp.tile(dqg.reshape(1, DK_D), (1, 2)), jnp.tile(dkg.reshape(1, DK_D), (1, 2))]
    args += [jnp.asarray(m, BF16) for m in _lane_group_matrices()]
    if use_rope:
        in_specs += [tab(), tab(), tab(), tab()]
        args += [jnp.asarray(a) for a in _rope_tables(seq_len, HD_C) + _rope_tables(seq_len, DK_D)]
    tok = lambda w: pl.BlockSpec((1, tm, w), lambda b, i: (b, i, 0))
    head = lambda nh: pl.BlockSpec((1, nh, tm, LANES), lambda b, i: (b, 0, i, 0))
    tok_shape = lambda w: jax.ShapeDtypeStruct((bsz, seq_len, w), F32)
    head_shape = lambda nh: jax.ShapeDtypeStruct((bsz, nh, seq_len, LANES), F32)
    return pl.pallas_call(
        functools.partial(_qk_prep_kernel, use_rope=use_rope),
        grid=(bsz, seq_len // tm),
        in_specs=in_specs,
        out_specs=[tok(H_C * HD_C), tok(H_D * 2 * DK_D), head(KV_C), head(KV_C), head(H_D), head(H_D)],
        out_shape=[tok_shape(H_C * HD_C), tok_shape(H_D * 2 * DK_D), head_shape(KV_C),
                   head_shape(KV_C), head_shape(H_D), head_shape(H_D)],
        compiler_params=_cparams("arbitrary", "arbitrary"),
        name="qk_prep",
    )(*args)


def _softmax_pv(q_list, kv_list):
    outs = []
    for q in q_list:
        scores = [_dot_nt(q, k) for k, _ in kv_list]
        mx = scores[0].max(axis=1, keepdims=True)
        for s in scores[1:]:
            mx = jnp.maximum(mx, s.max(axis=1, keepdims=True))
        den = 0.0
        acc = 0.0
        for s, (_, v) in zip(scores, kv_list):
            pexp = jnp.exp2(s - mx)
            den = den + pexp.sum(axis=1, keepdims=True)
            acc = acc + _dot(pexp.astype(BF16), v)
        outs.append(acc / den)
    return outs


def _head_kv(j, k_ref, v_ref, cache_refs):
    kv = [(r_k[0, 0, j].astype(BF16), r_v[0, 0, j].astype(BF16)) for r_k, r_v in cache_refs]
    kv.append((k_ref[0, j].astype(BF16), v_ref[0, j].astype(BF16)))
    return kv


def _gqa_kernel(q_ref, k_ref, v_ref, *rest, has_cache):
    cache_refs = [rest[:2]] if has_cache else []
    o_ref = rest[-1]
    g_c = H_C // KV_C
    width = g_c * HD_C
    for j in range(k_ref.shape[1]):
        q = q_ref[0, :, j * width : (j + 1) * width] * (HD_C ** -0.5 * LOG2E)
        qs = [q[:, g * HD_C : (g + 1) * HD_C].astype(BF16) for g in range(g_c)]
        outs = _softmax_pv(qs, _head_kv(j, k_ref, v_ref, cache_refs))
        o_ref[0, :, j * width : (j + 1) * width] = jnp.concatenate(outs, axis=1)


def _diff_kernel(q_ref, k_ref, v_ref, lam_ref, sg_ref, *rest, has_cache, out_scale):
    cache_refs = [rest[:2]] if has_cache else []
    o_ref = rest[-1]
    width = 2 * DK_D
    low = lax.broadcasted_iota(jnp.int32, (q_ref.shape[1], width), 1) < DK_D
    for j in range(k_ref.shape[1]):
        q = q_ref[0, :, j * width : (j + 1) * width] * (DK_D ** -0.5 * LOG2E)
        qs = [jnp.where(low, q, 0.0).astype(BF16), jnp.where(low, 0.0, q).astype(BF16)]
        o1, o2 = _softmax_pv(qs, _head_kv(j, k_ref, v_ref, cache_refs))
        o_ref[0, :, j * width : (j + 1) * width] = (
            _rms(o1 - lam_ref[...] * o2) * sg_ref[...] * out_scale)


def _attention(q, k, v, cache_k, cache_v, kernel, extra_args, n_heads, q_width, seq_len, hps, tq):
    bsz = q.shape[0]
    in_specs = [pl.BlockSpec((1, tq, hps * q_width), lambda b, h, i: (b, i, h)),
                pl.BlockSpec((1, hps, seq_len, LANES), lambda b, h, i: (b, h, 0, 0)),
                pl.BlockSpec((1, hps, seq_len, LANES), lambda b, h, i: (b, h, 0, 0))]
    in_specs += [pl.BlockSpec((1, LANES), lambda b, h, i: (0, 0)) for _ in extra_args]
    args = [q, k, v, *extra_args]
    if cache_k is not None:
        past = cache_k.shape[3]
        spec = lambda: pl.BlockSpec((1, 1, hps, past, LANES), lambda b, h, i: (b, 0, h, 0, 0))
        in_specs += [spec(), spec()]
        args += [cache_k, cache_v]
    return pl.pallas_call(
        functools.partial(kernel, has_cache=cache_k is not None),
        grid=(bsz, n_heads // hps, seq_len // tq),
        in_specs=in_specs,
        out_specs=pl.BlockSpec((1, tq, hps * q_width), lambda b, h, i: (b, i, h)),
        out_shape=jax.ShapeDtypeStruct((bsz, seq_len, n_heads * q_width), F32),
        compiler_params=_cparams("arbitrary", "arbitrary", "arbitrary"),
        name="attention",
    )(*args)


def _route(lg):
    lane = lax.broadcasted_iota(jnp.int32, lg.shape, 1).astype(F32)
    neg = -1e30
    is_g = (lane >= N_EXPERTS) & (lane < N_EXPERTS + N_GROUPS)
    gl = jnp.where(is_g, lg, neg)
    gmax = gl.max(axis=1, keepdims=True)
    g_p = 1.0 / jnp.where(is_g, jnp.exp(gl - gmax), 0.0).sum(axis=1, keepdims=True)
    g_i = jnp.where(gl == gmax, lane - N_EXPERTS, 1e9).min(axis=1, keepdims=True)
    in_grp = (lane < N_EXPERTS) & (jnp.floor(lane * (1.0 / EXP_PER_GROUP)) == g_i)
    el = jnp.where(in_grp, lg, neg)
    m1 = el.max(axis=1, keepdims=True)
    i1 = jnp.where(in_grp & (el == m1), lane, 1e9).min(axis=1, keepdims=True)
    el2 = jnp.where(lane == i1, neg, el)
    m2 = el2.max(axis=1, keepdims=True)
    i2 = jnp.where(in_grp & (el2 == m2) & (lane != i1), lane, 1e9).min(axis=1, keepdims=True)
    r = jnp.exp(m2 - m1)
    w1 = g_p / (1.0 + r)
    rec = jnp.where(lane == ROUTE_LANE, i1, 0.0)
    rec = jnp.where(lane == ROUTE_LANE + 1, i2, rec)
    rec = jnp.where(lane == ROUTE_LANE + 2, w1, rec)
    return jnp.where(lane == ROUTE_LANE + 3, w1 * r, rec)


def _mix_out_kernel(x_ref, a_ref, b_ref, m_ref, g2_ref, w_ref, wr_ref, br_ref,
                    x1_ref, h2_ref, gates_ref, *, parts):
    wa = a_ref.shape[2]
    m = m_ref[0]
    rows_per_part = x_ref.shape[1] // parts
    for part in range(parts):
        rows = pl.ds(part * rows_per_part, rows_per_part)
        o = (_dot(a_ref[0, rows, :].astype(BF16), w_ref[:wa])
             + _dot(b_ref[0, rows, :].astype(BF16), w_ref[wa:]))
        x1 = x_ref[0, rows, :] + m[2:3] * o
        x1_ref[0, rows, :] = x1
        h2 = _rms(x1) * g2_ref[...] * (1.0 + m[4:5]) + m[3:4]
        h2_ref[0, rows, :] = h2.astype(BF16)
        r = _dot(jnp.concatenate(_split_bf16(h2), axis=0), wr_ref[...])
        hi_rows, lo_rows = r[:rows_per_part], r[rows_per_part:]
        lg = hi_rows[:, :LANES] + hi_rows[:, LANES:] + lo_rows[:, :LANES] + br_ref[...]
        gates_ref[0, rows, :] = _route(lg)


def _mix_out(x, a, b, mods, g2, w_out_bf16, w_router, b_router, tm=512, parts=2):
    ngrp = x.shape[0]
    wa, wb = a.shape[2], b.shape[2]
    wr = jnp.concatenate(_split_bf16(w_router), axis=1)
    row = lambda w: pl.BlockSpec((1, tm, w), lambda g, i: (g, i, 0))
    full = lambda shape: pl.BlockSpec(shape, lambda g, i: (0,) * len(shape))
    return pl.pallas_call(
        functools.partial(_mix_out_kernel, parts=parts),
        grid=(ngrp, ROW_GROUP // tm),
        in_specs=[row(D_MODEL), row(wa), row(wb),
                  pl.BlockSpec((1, 6, D_MODEL), lambda g, i: (g, 0, 0)),
                  full((1, D_MODEL)), full((wa + wb, D_MODEL)),
                  full((D_MODEL, 2 * LANES)), full((1, LANES))],
        out_specs=[row(D_MODEL), row(D_MODEL), row(LANES)],
        out_shape=[jax.ShapeDtypeStruct((ngrp, ROW_GROUP, D_MODEL), F32),
                   jax.ShapeDtypeStruct((ngrp, ROW_GROUP, D_MODEL), BF16),
                   jax.ShapeDtypeStruct((ngrp, ROW_GROUP, LANES), F32)],
        compiler_params=_cparams("arbitrary", "arbitrary"),
        name="mix_out",
    )(x, a, b, mods, g2.reshape(1, D_MODEL), w_out_bf16, wr, b_router)


def _lane_col(x, lane, k):
    return jnp.where(lane == k, x, 0.0).sum(axis=1, keepdims=True)


def _moe_dispatch_kernel(h_ref, r_ref, tri_ref, xs_ref, pos_ref, cnt_ref):
    for t in range(pos_ref.shape[0]):
        _moe_dispatch_tile(t, h_ref, r_ref, tri_ref, xs_ref, pos_ref, cnt_ref)


def _moe_dispatch_tile(t, h_ref, r_ref, tri_ref, xs_ref, pos_ref, cnt_ref):
    r = r_ref[0, pl.ds(t * MOE_TILE, MOE_TILE), :]
    lane = lax.broadcasted_iota(jnp.int32, r.shape, 1).astype(F32)
    i1, i2, w1, w2 = [_lane_col(r, lane, ROUTE_LANE + k) for k in range(4)]
    oh1 = lane == i1
    oh2 = lane == i2
    oh = jnp.where(oh1 | oh2, 1.0, 0.0)
    rank = _dot(tri_ref[...], oh.astype(BF16))
    cnt = oh.sum(axis=0, keepdims=True)
    chunks = jnp.floor((cnt + (MOE_CHUNK - 1)) * (1.0 / MOE_CHUNK))
    li = lax.broadcasted_iota(jnp.int32, (LANES, LANES), 0)
    lj = lax.broadcasted_iota(jnp.int32, (LANES, LANES), 1)
    before = jnp.where(li < lj, 1.0, 0.0).astype(BF16)
    seg = _dot(jnp.broadcast_to(chunks, (8, LANES)).astype(BF16), before)[0:1]
    base = seg * MOE_CHUNK + rank
    pos1 = jnp.where(oh1, base, 0.0).sum(axis=1, keepdims=True)
    pos2 = jnp.where(oh2, base, 0.0).sum(axis=1, keepdims=True)
    riota = lax.broadcasted_iota(jnp.int32, (r.shape[0], MOE_TILE_ROWS), 1).astype(F32)
    p = jnp.where((riota == pos1) | (riota == pos2), 1.0, 0.0).astype(BF16)

    aux = jnp.where(lane == 2 * MOE_W_PIECES, i1, 0.0)
    for k, w in enumerate((w1, w2)):
        rest = w
        for piece in range(MOE_W_PIECES):
            part = rest.astype(BF16).astype(F32)
            aux = jnp.where(lane == k * MOE_W_PIECES + piece, part, aux)
            rest = rest - part
    row = jnp.concatenate([h_ref[0, pl.ds(t * MOE_TILE, MOE_TILE), :], aux.astype(BF16)], axis=1)
    xs_ref[pl.ds(t * MOE_TILE_ROWS, MOE_TILE_ROWS), :] = _dot_tn(p, row).astype(BF16)
    pos_ref[t] = jnp.where(lane == 0, pos1, jnp.where(lane == 1, pos2, 0.0))
    cnt_ref[t] = jnp.broadcast_to(cnt, (8, LANES))


def _moe_dispatch(h2, route):
    ngrp = h2.shape[0]
    per = ROW_GROUP // MOE_TILE
    ntile = ngrp * per
    tri = np.tril(np.ones((MOE_TILE, MOE_TILE), np.float32), -1)
    tok = lambda w: pl.BlockSpec((1, ROW_GROUP, w), lambda g: (jnp.minimum(g, ngrp - 1), 0, 0))
    return pl.pallas_call(
        _moe_dispatch_kernel,
        grid=(ngrp + 1,),
        in_specs=[tok(D_MODEL), tok(LANES), pl.BlockSpec((MOE_TILE, MOE_TILE), lambda g: (0, 0))],
        out_specs=[pl.BlockSpec((per * MOE_TILE_ROWS, MOE_ROW_W), lambda g: (g, 0)),
                   pl.BlockSpec((per, MOE_TILE, LANES), lambda g: (g, 0, 0)),
                   pl.BlockSpec((per, 8, LANES), lambda g: (g, 0, 0))],
        out_shape=[jax.ShapeDtypeStruct(((ntile + per) * MOE_TILE_ROWS, MOE_ROW_W), BF16),
                   jax.ShapeDtypeStruct((ntile + per, MOE_TILE, LANES), F32),
                   jax.ShapeDtypeStruct((ntile + per, 8, LANES), F32)],
        compiler_params=_cparams("arbitrary"),
        name="moe_dispatch",
    )(h2, route, jnp.asarray(tri, BF16))


def _moe_tables(cnt, ntile):
    nblk = ntile * MOE_TILE_CHUNKS // MOE_BLOCK_CHUNKS + N_EXPERTS
    chunks = (cnt + MOE_CHUNK - 1) // MOE_CHUNK
    seg_start = jnp.cumsum(chunks, axis=1) - chunks
    tile_prefix = jnp.cumsum(chunks, axis=0) - chunks
    per_expert = chunks.sum(axis=0)
    blocks = (per_expert + MOE_BLOCK_CHUNKS - 1) // MOE_BLOCK_CHUNKS
    blk_end = jnp.cumsum(blocks)
    n_used = blk_end[-1]
    b = jnp.arange(nblk + 1, dtype=jnp.int32)
    blk_e = jnp.sum(b[:, None] >= blk_end[None, :], axis=1).astype(jnp.int32)
    last_e = jnp.sum((n_used - 1) >= blk_end).astype(jnp.int32)
    blk_e = jnp.clip(jnp.where(b < n_used, blk_e, last_e), 0, N_EXPERTS - 1)
    oh_e = (blk_e[:, None] == jnp.arange(N_EXPERTS)[None, :]).astype(jnp.int32)
    pick = lambda per_tile: jnp.sum(oh_e[:, :, None] * per_tile.T[None], axis=1)
    seg_e, pre_e, chunks_e = pick(seg_start), pick(tile_prefix), pick(chunks)
    first_blk = jnp.sum(oh_e * (blk_end - blocks)[None, :], axis=1)
    k = (b - first_blk)[:, None] * MOE_BLOCK_CHUNKS + jnp.arange(MOE_BLOCK_CHUNKS)[None, :]
    k3 = k[:, :, None]
    in_tile = (pre_e[:, None, :] <= k3) & (k3 < (pre_e + chunks_e)[:, None, :])
    tile_base = (jnp.arange(ntile) * MOE_TILE_CHUNKS)[None, :] + seg_e - pre_e
    src = jnp.sum(jnp.where(in_tile, tile_base[:, None, :] + k3, 0), axis=-1)
    valid = jnp.any(in_tile, axis=-1) & (b < n_used)[:, None]
    slot_c = jnp.arange(MOE_BLOCK_CHUNKS)[None, :]
    n_read = MOE_TILE_CHUNKS - 2 * MOE_BLOCK_CHUNKS
    assert n_read > 0
    spare = ntile * MOE_TILE_CHUNKS
    gather = jnp.where(valid, src, spare + 2 * MOE_BLOCK_CHUNKS + slot_c % n_read).astype(jnp.int32)
    scatter = jnp.where(valid, src, spare + (b % 2)[:, None] * MOE_BLOCK_CHUNKS + slot_c).astype(jnp.int32)
    blk_start = jnp.concatenate([blk_end - blocks, n_used[None]]).astype(jnp.int32)
    return blk_start, gather.reshape(-1), scatter.reshape(-1)


def _moe_expert_kernel(start_ref, gather_ref, scatter_ref, xs_hbm, wg_ref, wu_ref, wd_ref, ys_hbm,
                       lhs, obuf, wgb, wub, wdb, in_sem, out_sem):
    e = pl.program_id(0)
    ne = pl.num_programs(0)
    b0 = start_ref[e]
    b1 = start_ref[e + 1]
    n = start_ref[ne]

    def chunk_copy(blk, slot, c, gather):
        rows = pl.ds(c * MOE_CHUNK, MOE_CHUNK)
        if gather:
            idx = gather_ref[blk * MOE_BLOCK_CHUNKS + c]
            return pltpu.make_async_copy(xs_hbm.at[idx], lhs.at[slot, rows], in_sem.at[slot])
        idx = scatter_ref[blk * MOE_BLOCK_CHUNKS + c]
        dst = ys_hbm.at[idx, pl.ds(0, MOE_CHUNK), pl.ds(0, D_MODEL)]
        return pltpu.make_async_copy(obuf.at[slot, rows], dst, out_sem.at[slot])

    def for_chunks(blk, slot, gather, start):
        for c in range(MOE_BLOCK_CHUNKS):
            cp = chunk_copy(blk, slot, c, gather)
            if start:
                cp.start()
            else:
                cp.wait()

    @pl.when((e == 0) & (n > 0))
    def _():
        for_chunks(0, 0, True, True)

    @pl.when(b1 > b0)
    def _():
        wgb[...] = wg_ref[0, 0].astype(BF16)
        wub[...] = wu_ref[0, 0].astype(BF16)
        wdb[...] = wd_ref[0, 0].astype(BF16)
        e_f32 = e.astype(F32)

        def block(b, carry):
            slot = b % 2

            @pl.when(b >= 2)
            def _():
                for_chunks(b - 2, slot, False, False)

            for_chunks(b, slot, True, False)
            for_chunks(b + 1, 1 - slot, True, True)
            xa = lhs[slot]
            x = xa[:, :D_MODEL]
            aux = xa[:, D_MODEL:].astype(F32)
            lane = lax.broadcasted_iota(jnp.int32, aux.shape, 1)
            first = lane < MOE_W_PIECES
            w_first = jnp.where(first, aux, 0.0).sum(axis=1, keepdims=True)
            w_second = jnp.where(first | (lane >= 2 * MOE_W_PIECES), 0.0, aux).sum(axis=1, keepdims=True)
            e_first = _lane_col(aux, lane, 2 * MOE_W_PIECES)
            w = jnp.where(e_first == e_f32, w_first, w_second)
            hid = _silu(_dot(x, wgb[...])) * _dot(x, wub[...]) * w
            obuf[slot] = _dot(hid.astype(BF16), wdb[...]).astype(BF16)
            for_chunks(b, slot, False, True)
            return carry

        lax.fori_loop(b0, b1, block, 0)

    @pl.when(e == ne - 1)
    def _():
        @pl.when(n >= 2)
        def _():
            for_chunks(n - 2, n % 2, False, False)

        @pl.when(n >= 1)
        def _():
            for_chunks(n, n % 2, True, False)
            for_chunks(n - 1, (n - 1) % 2, False, False)


def _moe_experts(xs, blk_start, gather, scatter, layer, w_gate, w_up, w_down):
    nchunk = xs.shape[0] // MOE_CHUNK
    rows_per_blk = MOE_BLOCK_CHUNKS * MOE_CHUNK
    wspec = lambda shape: pl.BlockSpec((1, 1) + shape, lambda e, st, g, s: (layer, e, 0, 0))
    hbm = pl.BlockSpec(memory_space=pl.ANY)
    ys = pl.pallas_call(
        _moe_expert_kernel,
        grid_spec=pltpu.PrefetchScalarGridSpec(
            num_scalar_prefetch=3,
            grid=(N_EXPERTS,),
            in_specs=[hbm, wspec((D_MODEL, D_EXPERT)), wspec((D_MODEL, D_EXPERT)),
                      wspec((D_EXPERT, D_MODEL))],
            out_specs=hbm,
            scratch_shapes=[pltpu.VMEM((2, rows_per_blk, MOE_ROW_W), BF16),
                            pltpu.VMEM((2, rows_per_blk, D_MODEL), BF16),
                            pltpu.VMEM((D_MODEL, D_EXPERT), BF16),
                            pltpu.VMEM((D_MODEL, D_EXPERT), BF16),
                            pltpu.VMEM((D_EXPERT, D_MODEL), BF16),
                            pltpu.SemaphoreType.DMA((2,)),
                            pltpu.SemaphoreType.DMA((2,))]),
        out_shape=jax.ShapeDtypeStruct((nchunk, MOE_CHUNK, MOE_ROW_W), BF16),
        input_output_aliases={3: 0},
        compiler_params=_cparams("arbitrary"),
        name="moe_experts",
    )(blk_start, gather, scatter, xs.reshape(nchunk, MOE_CHUNK, MOE_ROW_W), w_gate, w_up, w_down)
    return ys.reshape(nchunk * MOE_CHUNK, MOE_ROW_W)


def _moe_combine_kernel(ys_ref, pos_ref, x_ref, m_ref, o_ref):
    pos = pos_ref[0]
    lane = lax.broadcasted_iota(jnp.int32, pos.shape, 1)
    pos1 = _lane_col(pos, lane, 0)
    pos2 = _lane_col(pos, lane, 1)
    riota = lax.broadcasted_iota(jnp.int32, (pos.shape[0], MOE_TILE_ROWS), 1).astype(F32)
    p = jnp.where((riota == pos1) | (riota == pos2), 1.0, 0.0).astype(BF16)
    o_ref[0] = x_ref[0] + m_ref[0][5:6] * _dot(p, ys_ref[...])


def _moe_combine(ys, pos, x1, mods):
    ngrp = x1.shape[0]
    per = ROW_GROUP // MOE_TILE
    tok = pl.BlockSpec((1, MOE_TILE, D_MODEL), lambda j: (j // per, j % per, 0))
    return pl.pallas_call(
        _moe_combine_kernel,
        grid=(ngrp * per,),
        in_specs=[pl.BlockSpec((MOE_TILE_ROWS, D_MODEL), lambda j: (j, 0)),
                  pl.BlockSpec((1, MOE_TILE, LANES), lambda j: (j, 0, 0)),
                  tok, pl.BlockSpec((1, 6, D_MODEL), lambda j: (j // per, 0, 0))],
        out_specs=tok,
        out_shape=jax.ShapeDtypeStruct(x1.shape, F32),
        compiler_params=_cparams("arbitrary"),
        name="moe_combine",
    )(ys, pos, x1, mods)


def _moe(h2, route, x1, mods, layer, w_gate, w_up, w_down):
    ntile = h2.shape[0] * (ROW_GROUP // MOE_TILE)
    xs, pos, cnt = _moe_dispatch(h2, route)
    cnt = cnt[:ntile, 0, :N_EXPERTS].astype(jnp.int32)
    blk_start, gather, scatter = _moe_tables(cnt, ntile)
    ys = _moe_experts(xs, blk_start, gather, scatter, layer, w_gate, w_up, w_down)
    return _moe_combine(ys, pos, x1, mods)


def kernel(x_prompt, x_sample, state_hgrn, cache_c_k, cache_c_v, cache_d_k, cache_d_v, c, c_ctx, norm1_g, norm2_g, w_mod, b_mod, even_w_in, even_w_out, hgrn_lower, hgrn_norm_g, hy_conv_w, hy_conv_b, hy_w1, hy_b1, hy_freq1, hy_w2, hy_b2, hy_freq2, hy_w3, hy_bias, odd_w_in, odd_w_out, c_qnorm_g, c_knorm_g, d_qnorm_g, d_knorm_g, d_lambda_q1, d_lambda_k1, d_lambda_q2, d_lambda_k2, d_subln_g, moe_w_grp, moe_b_grp, moe_w_rt, moe_b_rt, moe_w_gate, moe_w_up, moe_w_down):
    depth = w_mod.shape[0]
    n_ctx, seq, _ = x_prompt.shape
    n_lat, dec_seq, _ = x_sample.shape
    g_ctx = n_ctx * seq // ROW_GROUP
    g_lat = n_lat * dec_seq // ROW_GROUP
    assert dec_seq == ROW_GROUP and ROW_GROUP % seq == 0

    cond = jnp.zeros((16, D_MODEL), F32).at[0].set(c_ctx).at[1 : 1 + n_lat].set(c)
    mods = _adaln(cond, w_mod, b_mod).reshape(depth, 16, 6, D_MODEL)
    lower = jnp.cumsum(jax.nn.softmax(hgrn_lower.astype(F32), axis=0), axis=0)

    streams = [
        dict(x=x_prompt.reshape(g_ctx, ROW_GROUP, D_MODEL), ngrp=g_ctx, bsz=n_ctx, seq=seq, ctx=True),
        dict(x=x_sample, ngrp=g_lat, bsz=n_lat, seq=dec_seq, ctx=False),
    ]
    new_state, new_ck, new_cv, new_dk, new_dv = [], [], [], [], []

    for l in range(depth):
        j = l // 2
        w_router = jnp.zeros((D_MODEL, LANES), F32)
        w_router = w_router.at[:, :N_EXPERTS].set(moe_w_rt[l])
        w_router = w_router.at[:, N_EXPERTS : N_EXPERTS + N_GROUPS].set(moe_w_grp[l])
        b_router = jnp.zeros((1, LANES), F32)
        b_router = b_router.at[0, :N_EXPERTS].set(moe_b_rt[l])
        b_router = b_router.at[0, N_EXPERTS : N_EXPERTS + N_GROUPS].set(moe_b_grp[l])
        if l % 2 == 0:
            w_in = even_w_in[j].astype(BF16)
            w_out = even_w_out[j].astype(BF16)
        else:
            w_in = odd_w_in[j].astype(BF16)
            w_out = odd_w_out[j].astype(BF16)
            lam_init = 0.8 - 0.6 * math.exp(-0.3 * l)
            lam = (jnp.exp(jnp.sum(d_lambda_q1[j] * d_lambda_k1[j]))
                   - jnp.exp(jnp.sum(d_lambda_q2[j] * d_lambda_k2[j])) + lam_init)
            lam_row = jnp.full((1, LANES), lam, F32)

        for s in streams:
            ngrp, bsz, sl = s["ngrp"], s["bsz"], s["seq"]
            if s["ctx"]:
                m = jnp.broadcast_to(mods[l, 0][None], (ngrp, 6, D_MODEL))
            else:
                m = mods[l, 1 : 1 + ngrp]
            x = s["x"]
            if l % 2 == 0:
                wa = H_A * DK_A
                splits = [(0, wa), (wa, 2 * wa), (2 * wa, 3 * wa), (3 * wa, 3 * wa + W_A),
                          (3 * wa + W_A, 3 * wa + 2 * W_A), (3 * wa + 2 * W_A, w_in.shape[1])]
                qa, ffa, fba, ia, ga, hy = _norm_proj(x, m, norm1_g[l], w_in, splits)
                per_seq = lambda t: t.reshape(bsz, sl, t.shape[-1])
                if s["ctx"]:
                    s0t = jnp.zeros((bsz, 2, H_A, DK_A, DK_A), F32)
                else:
                    s0t = jnp.swapaxes(state_hgrn[:, j].astype(F32), -1, -2)
                mix_a, s_fin = _hgrn(per_seq(qa), per_seq(ffa), per_seq(fba), per_seq(ia),
                                     per_seq(ga), lower[j], hgrn_norm_g[j], s0t, sl)
                gr, gi = _hyena_filter(sl, hy_w1[j], hy_b1[j], hy_freq1[j], hy_w2[j], hy_b2[j],
                                       hy_freq2[j], hy_w3[j], hy_bias[j])
                mix_b = _hyena(per_seq(hy), hy_conv_w[j], hy_conv_b[j], gr, gi, sl,
                               nseq=4 if s["ctx"] else 2)
                if s["ctx"]:
                    new_state.append(jnp.swapaxes(s_fin, -1, -2))
            else:
                (p,) = _norm_proj(x, m, norm1_g[l], w_in, [(0, w_in.shape[1])])
                qc, qd, kc, vc, kd, vd = _qk_prep(p, c_qnorm_g[j], c_knorm_g[j], d_qnorm_g[j],
                                                  d_knorm_g[j], sl, use_rope=not s["ctx"])
                if s["ctx"]:
                    caches = (None, None, None, None)
                    new_ck.append(kc)
                    new_cv.append(vc)
                    new_dk.append(kd)
                    new_dv.append(vd)
                else:
                    caches = (cache_c_k[:, j : j + 1], cache_c_v[:, j : j + 1],
                              cache_d_k[:, j : j + 1], cache_d_v[:, j : j + 1])
                hps_c, hps_d, tq = (KV_C, H_D, sl) if s["ctx"] else (1, 1, 512)
                mix_a = _attention(qc, kc, vc, caches[0], caches[1], _gqa_kernel, (), KV_C,
                                   (H_C // KV_C) * HD_C, sl, hps_c, tq)
                diff = functools.partial(_diff_kernel, out_scale=1.0 - lam_init)
                mix_b = _attention(qd, kd, vd, caches[2], caches[3], diff,
                                   (lam_row, d_subln_g[j].reshape(1, DV_D)), H_D, 2 * DK_D, sl,
                                   hps_d, tq)
            grp = lambda t: t.reshape(ngrp, ROW_GROUP, t.shape[-1])
            x1, h2, gates = _mix_out(x, grp(mix_a), grp(mix_b), m, norm2_g[l], w_out, w_router, b_router)
            s["x"] = _moe(h2, gates, x1, m, l, moe_w_gate, moe_w_up, moe_w_down)

    y_ctx = streams[0]["x"].reshape(n_ctx, seq, D_MODEL)
    y_lat = streams[1]["x"]
    return (y_ctx, y_lat, jnp.stack(new_state, axis=1), jnp.stack(new_ck, axis=1),
            jnp.stack(new_cv, axis=1), jnp.stack(new_dk, axis=1), jnp.stack(new_dv, axis=1))
```

```python
import functools
import math

import numpy as np
import jax
import jax.numpy as jnp
from jax import lax
from jax.experimental import pallas as pl
from jax.experimental.pallas import tpu as pltpu

F32 = jnp.float32
BF16 = jnp.bfloat16

D_MODEL = 1024
EPS = 1e-6
LOG2E = 1.0 / math.log(2.0)
GRID_W = 64
ROPE_THETA = 10000.0
H_A = 4
DK_A = 128
W_A = 512
CHUNK = 128
SUB = 8
LEVELS = (16, 32, 64, 128)
C_B = 512
FILTER_BANDS = 16
DECAY_MIN = math.log(1e-2) / 1.5
DECAY_MAX = math.log(1e-2) / 0.3
H_C = 4
KV_C = 2
HD_C = 128
H_D = 4
DK_D = 64
DV_D = 128
N_GROUPS = 4
EXP_PER_GROUP = 4
N_EXPERTS = 16
D_EXPERT = 512

ROUTE_LANE = N_EXPERTS + N_GROUPS
MOE_TILE = 512
MOE_CHUNK = 16
MOE_TILE_CHUNKS = 2 * MOE_TILE // MOE_CHUNK + N_EXPERTS
MOE_TILE_ROWS = MOE_TILE_CHUNKS * MOE_CHUNK
MOE_BLOCK_CHUNKS = 16
MOE_ROW_W = D_MODEL + 128
MOE_W_PIECES = 3
ATT_SLAB = 256

LANES = 128
ROW_GROUP = 1024
VMEM_LIMIT_BYTES = 56 * 1024 * 1024


def _cparams(*sem):
    return pltpu.CompilerParams(dimension_semantics=sem, vmem_limit_bytes=VMEM_LIMIT_BYTES)


def _split_bf16(x):
    hi = x.astype(BF16)
    lo = (x - hi.astype(F32)).astype(BF16)
    return hi, lo


def _dot(a, b):
    return jnp.dot(a, b, preferred_element_type=F32)


def _dot3(a, b):
    ah, al = _split_bf16(a)
    bh, bl = _split_bf16(b)
    return _dot(ah, bh) + _dot(al, bh) + _dot(ah, bl)


def _dot_nt(a, b):
    return lax.dot_general(a, b, (((1,), (1,)), ((), ())), preferred_element_type=F32)


def _dot_tn(a, b):
    return lax.dot_general(a, b, (((0,), (0,)), ((), ())), preferred_element_type=F32)


def _silu(x):
    return x * jax.nn.sigmoid(x)


def _rms(x, eps=EPS):
    return x * lax.rsqrt(jnp.mean(x * x, axis=-1, keepdims=True) + eps)


def _adaln_kernel(c_ref, w_ref, b_ref, o_ref):
    s = _silu(c_ref[...])
    o_ref[0] = _dot(s.astype(BF16), w_ref[0].astype(BF16)) + b_ref[0]


def _adaln(cond, w_mod, b_mod):
    depth, _, n = w_mod.shape
    rows = cond.shape[0]
    tn = 1536
    return pl.pallas_call(
        _adaln_kernel,
        grid=(depth, n // tn),
        in_specs=[
            pl.BlockSpec((rows, D_MODEL), lambda l, j: (0, 0)),
            pl.BlockSpec((1, D_MODEL, tn), lambda l, j: (l, 0, j)),
            pl.BlockSpec((1, 1, tn), lambda l, j: (l, 0, j)),
        ],
        out_specs=pl.BlockSpec((1, rows, tn), lambda l, j: (l, 0, j)),
        out_shape=jax.ShapeDtypeStruct((depth, rows, n), F32),
        compiler_params=_cparams("arbitrary", "arbitrary"),
        name="adaln",
    )(cond, w_mod, b_mod.reshape(depth, 1, n))


def _norm_proj_kernel(x_ref, m_ref, g_ref, w_ref, *o_refs, splits):
    m = m_ref[0]
    h = _rms(x_ref[0]) * g_ref[...] * (1.0 + m[1:2]) + m[0:1]
    hb = h.astype(BF16)
    for o_ref, (a, b) in zip(o_refs, splits):
        o_ref[0] = _dot(hb, w_ref[:, a:b])


def _norm_proj(x, mods, g, w_bf16, splits, tm=256):
    ngrp = x.shape[0]
    n = w_bf16.shape[1]
    kern = functools.partial(_norm_proj_kernel, splits=splits)
    return pl.pallas_call(
        kern,
        grid=(ngrp, ROW_GROUP // tm),
        in_specs=[
            pl.BlockSpec((1, tm, D_MODEL), lambda b, i: (b, i, 0)),
            pl.BlockSpec((1, 6, D_MODEL), lambda b, i: (b, 0, 0)),
            pl.BlockSpec((1, D_MODEL), lambda b, i: (0, 0)),
            pl.BlockSpec((D_MODEL, n), lambda b, i: (0, 0)),
        ],
        out_specs=[pl.BlockSpec((1, tm, hi - lo), lambda b, i: (b, i, 0)) for lo, hi in splits],
        out_shape=[jax.ShapeDtypeStruct((ngrp, ROW_GROUP, hi - lo), F32) for lo, hi in splits],
        compiler_params=_cparams("arbitrary", "arbitrary"),
        name="norm_proj",
    )(x, mods, g.reshape(1, D_MODEL), w_bf16)


def _hgrn_constants():
    c = CHUNK
    nblk = 1 + len(LEVELS)
    w = np.zeros((2, nblk * c, c), np.float32)
    for t in range(c):
        w[0, t, : t + 1] = 1.0
    for li, r in enumerate(LEVELS):
        for t in range(c):
            mid = (t // r) * r + r // 2
            if t >= mid:
                w[0, (li + 1) * c + t, mid : t + 1] = 1.0
            else:
                w[0, (li + 1) * c + t, t + 1 : mid] = 1.0
    for blk in range(nblk):
        w[1, blk * c : (blk + 1) * c] = w[0, blk * c : (blk + 1) * c][::-1, ::-1]
    m = np.zeros((2, nblk, c, c), np.float32)
    for li, r in enumerate(LEVELS):
        for t in range(c):
            for s in range(c):
                if t // r == s // r and (t % r) >= r // 2 and (s % r) < r // 2:
                    m[0, li, t, s] = 1.0
    for t in range(c):
        for s in range(c):
            if t // SUB == s // SUB and s <= t:
                m[0, nblk - 1, t, s] = 1.0
    m[1] = m[0][:, ::-1, ::-1]
    sel = np.zeros((SUB * DK_A, c), np.float32)
    for i in range(SUB):
        sel[i * DK_A : (i + 1) * DK_A, i::SUB] = 1.0
    return w, m, sel


def _hgrn_chunk(direction, r0, qa_ref, gate_ref, ia_ref, lb, wcum, masks, sel, states):
    rows = pl.ds(r0, CHUNK)
    q = _silu(qa_ref[0, rows, :])
    v = ia_ref[0, rows, :]
    f = lb + (1.0 - lb) * jax.nn.sigmoid(gate_ref[0, rows, :])
    kk = jnp.maximum(1.0 - f, 0.0)
    lf = jnp.log(f) * LOG2E
    lk = jnp.log(kk) * LOG2E
    lf_hi, lf_lo = _split_bf16(lf)
    z = _dot(wcum, lf_hi) + _dot(wcum, lf_lo)
    e = z[0:CHUNK]
    edge = e[CHUNK - 1 : CHUNK] if direction == 0 else e[0:1]
    q_in = (q * jnp.exp2(e)).astype(BF16)
    k_st = (kk * jnp.exp2(edge - e)).astype(BF16)
    st_decay = jnp.exp2(edge)
    q_lv, k_lv = [], []
    for li in range(len(LEVELS)):
        ez = jnp.exp2(z[(li + 1) * CHUNK : (li + 2) * CHUNK])
        q_lv.append((q * ez).astype(BF16))
        k_lv.append((kk * ez).astype(BF16))
    nsub = CHUNK // SUB
    c3 = (e - lk).reshape(nsub, SUB, W_A)
    pair = []
    for i in range(SUB):
        cb = jnp.broadcast_to(c3[:, i : i + 1, :], (nsub, SUB, W_A)).reshape(CHUNK, W_A)
        pair.append((q * jnp.exp2(jnp.minimum(e - cb, 0.0))).astype(BF16))
    vb = v.astype(BF16)
    outs, new_states = [], []
    for h in range(H_A):
        hs = slice(h * DK_A, (h + 1) * DK_A)
        sc = _dot(jnp.concatenate([p[:, hs] for p in pair], axis=1), sel) * masks[len(LEVELS)]
        for li in range(len(LEVELS)):
            sc = sc + _dot_nt(q_lv[li][:, hs], k_lv[li][:, hs]) * masks[li]
        st = states[h]
        outs.append(_dot(sc.astype(BF16), vb[:, hs]) + _dot_nt(q_in[:, hs], st.astype(BF16)))
        new_states.append(st_decay[:, hs] * st + _dot_tn(vb[:, hs], k_st[:, hs]))
    return jnp.concatenate(outs, axis=1), tuple(new_states)


def _hgrn_kernel(qa_ref, ff_ref, fb_ref, ia_ref, ga_ref, lb_ref, ng_ref, s0_ref, w_ref, m_ref,
                 sel_ref, o_ref, sout_ref, ob_ref, *, seq_len):
    nchunk = seq_len // CHUNK
    lb = lb_ref[...]
    sel = sel_ref[...]
    nmask = 1 + len(LEVELS)

    def scan(i, states):
        rf = pl.multiple_of(i * CHUNK, CHUNK)
        rb = pl.multiple_of((nchunk - 1 - i) * CHUNK, CHUNK)
        o_f, st_f = _hgrn_chunk(0, rf, qa_ref, ff_ref, ia_ref, lb, w_ref[0],
                                [m_ref[0, j] for j in range(nmask)], sel, states[0])
        o_b, st_b = _hgrn_chunk(1, rb, qa_ref, fb_ref, ia_ref, lb, w_ref[1],
                                [m_ref[1, j] for j in range(nmask)], sel, states[1])
        o_ref[0, pl.ds(rf, CHUNK), :] = o_f
        ob_ref[pl.ds(rb, CHUNK), :] = o_b
        return st_f, st_b

    init = tuple(tuple(s0_ref[0, d, h] for h in range(H_A)) for d in range(2))
    final = lax.fori_loop(0, nchunk, scan, init)

    ng = jnp.concatenate([ng_ref[...]] * H_A, axis=1)

    def gate(i, carry):
        rows = pl.ds(pl.multiple_of(i * CHUNK, CHUNK), CHUNK)
        tot = o_ref[0, rows, :] + ob_ref[rows, :]
        normed = jnp.concatenate(
            [_rms(tot[:, h * DK_A : (h + 1) * DK_A]) for h in range(H_A)], axis=1)
        o_ref[0, rows, :] = normed * ng * _silu(ga_ref[0, rows, :])
        return carry

    lax.fori_loop(0, nchunk, gate, 0)
    for d in range(2):
        for h in range(H_A):
            sout_ref[0, d, h] = final[d][h]


def _hgrn(qa, ff, fb, ia, ga, lb, ng, s0t, seq_len):
    bsz = qa.shape[0]
    w, m, sel = _hgrn_constants()
    seq = pl.BlockSpec((1, seq_len, W_A), lambda b: (b, 0, 0))
    full = lambda shape: pl.BlockSpec(shape, lambda b: (0,) * len(shape))
    st_spec = pl.BlockSpec((1, 2, H_A, DK_A, DK_A), lambda b: (b, 0, 0, 0, 0))
    return pl.pallas_call(
        functools.partial(_hgrn_kernel, seq_len=seq_len),
        grid=(bsz,),
        in_specs=[seq, seq, seq, seq, seq, full((1, W_A)), full((1, DK_A)), st_spec,
                  full(w.shape), full(m.shape), full(sel.shape)],
        out_specs=[seq, st_spec],
        out_shape=[jax.ShapeDtypeStruct((bsz, seq_len, W_A), F32),
                   jax.ShapeDtypeStruct((bsz, 2, H_A, DK_A, DK_A), F32)],
        scratch_shapes=[pltpu.VMEM((seq_len, W_A), F32)],
        compiler_params=_cparams("arbitrary"),
        name="hgrn",
    )(qa, ff, fb, ia, ga, lb.reshape(1, W_A), ng.reshape(1, DK_A), s0t,
      jnp.asarray(w, BF16), jnp.asarray(m, F32), jnp.asarray(sel, BF16))


def _dft_constants(seq_len):
    n = 2 * seq_len
    t = np.arange(seq_len, dtype=np.int64)
    wt = (np.arange(seq_len, dtype=np.int64)[:, None] * t[None, :]) % n
    ang = 2.0 * np.pi * wt.astype(np.float64) / n
    cos, sin = np.cos(ang), np.sin(ang)
    nyq = np.where(t % 2 == 0, 1.0, -1.0)
    sin_p = sin.copy()
    sin_p[0] = nyq
    fwd = np.concatenate([cos, sin_p], axis=0)
    icos = 2.0 * cos.T / n
    icos[:, 0] = 1.0 / n
    isin = 2.0 * sin.T / n
    isin[:, 0] = nyq / n
    inv = np.concatenate([icos, isin], axis=1)
    return fwd, inv


def _filter_embedding(seq_len):
    t = np.linspace(0.0, 1.0, seq_len)[:, None]
    w = 2.0 * np.pi * np.arange(seq_len) / seq_len
    f = np.linspace(1e-4, FILTER_BANDS - 1, FILTER_BANDS)
    ang = w[:, None] * f[None, :]
    z = np.concatenate([t, np.cos(ang), -np.sin(ang)], axis=-1)
    zp = np.zeros((seq_len, LANES), np.float64)
    zp[:, : z.shape[1]] = z
    deltas = np.abs(np.linspace(DECAY_MIN, DECAY_MAX, C_B))
    window = np.exp(-t * deltas[None, :])
    return zp.astype(np.float32), window.astype(np.float32)


def _hyena_filter_kernel(z_ref, win_ref, w1_ref, b1_ref, fr1_ref, w2_ref, b2_ref, fr2_ref, w3_ref,
                         hb_ref, fh_ref, fl_ref, gr_ref, gi_ref, *, seq_len):
    h = jnp.sin(fr1_ref[...] * (_dot3(z_ref[...], w1_ref[...]) + b1_ref[...]))
    h = jnp.sin(fr2_ref[...] * (_dot3(h, w2_ref[...]) + b2_ref[...]))
    h = _dot3(h, w3_ref[...])
    win = win_ref[...]
    hf = h[:, :C_B] * win
    hbk = h[:, C_B:] * win

    def dft(x):
        xh, xl = _split_bf16(x)
        return _dot(fh_ref[...], xh) + _dot(fh_ref[...], xl) + _dot(fl_ref[...], xh)

    p_sum = dft(hf + hbk)
    p_dif = dft(hf - hbk)
    row0 = lax.broadcasted_iota(jnp.int32, (seq_len, C_B), 0) == 0
    gr_ref[...] = p_sum[:seq_len] + hb_ref[...]
    gi_ref[...] = jnp.where(row0, p_sum[seq_len:] + hb_ref[...], p_dif[seq_len:])


def _hyena_filter(seq_len, w1, b1, fr1, w2, b2, fr2, w3, hbias):
    zemb, window = _filter_embedding(seq_len)
    fwd, _ = _dft_constants(seq_len)
    f_hi, f_lo = _split_bf16(jnp.asarray(fwd, F32))

    def pad(a, rows, cols):
        return jnp.zeros((rows, cols), F32).at[: a.shape[0], : a.shape[1]].set(a)

    args = (jnp.asarray(zemb), jnp.asarray(window), pad(w1, LANES, LANES), pad(b1[None], 1, LANES),
            pad(fr1[None], 1, LANES), pad(w2, LANES, LANES), pad(b2[None], 1, LANES),
            pad(fr2[None], 1, LANES), pad(w3, LANES, 2 * C_B), hbias.reshape(1, C_B), f_hi, f_lo)
    return pl.pallas_call(
        functools.partial(_hyena_filter_kernel, seq_len=seq_len),
        out_shape=[jax.ShapeDtypeStruct((seq_len, C_B), F32)] * 2,
        compiler_params=pltpu.CompilerParams(vmem_limit_bytes=VMEM_LIMIT_BYTES),
        name="hyena_filter",
    )(*args)


def _hyena_kernel(x0_ref, x1_ref, v_ref, cw_ref, cb_ref, gr_ref, gi_ref, f_ref, fi_ref, o_ref, *,
                  seq_len):
    tc = o_ref.shape[2]
    row = lax.broadcasted_iota(jnp.int32, (seq_len, tc), 0)

    def short_conv(u_ref, j, sq):
        u = u_ref[sq]
        prev = jnp.where(row == 0, 0.0, pltpu.roll(u, 1, 0))
        nxt = jnp.where(row == seq_len - 1, 0.0, pltpu.roll(u, seq_len - 1, 0))
        cw = cw_ref[j]
        return cw[0:1] * prev + cw[1:2] * u + cw[2:3] * nxt + cb_ref[j]

    gr, gi = gr_ref[...], gi_ref[...]
    row0 = row == 0
    for sq in range(o_ref.shape[0]):
        x0 = short_conv(x0_ref, 0, sq)
        z = short_conv(v_ref, 2, sq) * short_conv(x1_ref, 1, sq)
        p = _dot(f_ref[...], z.astype(BF16))
        a, b = p[:seq_len], p[seq_len:]
        bgi = b * gi
        yr = a * gr - jnp.where(row0, 0.0, bgi)
        yq = jnp.where(row0, bgi, a * gi + b * gr)
        y = _dot(fi_ref[...], jnp.concatenate([yr, yq], axis=0).astype(BF16))
        o_ref[sq] = y * x0


def _hyena(hy, conv_w, conv_b, gr, gi, seq_len, nseq, tc=256):
    bsz = hy.shape[0]
    nct = C_B // tc
    fwd, inv = _dft_constants(seq_len)
    cw = conv_w.reshape(3, 3, C_B).transpose(1, 0, 2)
    cb = conv_b.reshape(3, 1, C_B)
    part = lambda k: pl.BlockSpec((nseq, seq_len, tc), lambda b, j, k=k: (b, 0, k * nct + j))
    return pl.pallas_call(
        functools.partial(_hyena_kernel, seq_len=seq_len),
        grid=(bsz // nseq, nct),
        in_specs=[part(0), part(1), part(2),
                  pl.BlockSpec((3, 3, tc), lambda b, j: (0, 0, j)),
                  pl.BlockSpec((3, 1, tc), lambda b, j: (0, 0, j)),
                  pl.BlockSpec((seq_len, tc), lambda b, j: (0, j)),
                  pl.BlockSpec((seq_len, tc), lambda b, j: (0, j)),
                  pl.BlockSpec((2 * seq_len, seq_len), lambda b, j: (0, 0)),
                  pl.BlockSpec((seq_len, 2 * seq_len), lambda b, j: (0, 0))],
        out_specs=pl.BlockSpec((nseq, seq_len, tc), lambda b, j: (b, 0, j)),
        out_shape=jax.ShapeDtypeStruct((bsz, seq_len, C_B), F32),
        compiler_params=_cparams("arbitrary", "arbitrary"),
        name="hyena",
    )(hy, hy, hy, cw, cb, gr, gi, jnp.asarray(fwd, F32).astype(BF16),
      jnp.asarray(inv, F32).astype(BF16))


def _rope_tables(seq_len, dim):
    rows = seq_len // GRID_W
    row_idx = np.repeat(np.arange(rows), GRID_W).astype(np.float64)
    col_idx = np.tile(np.arange(GRID_W), rows).astype(np.float64)
    half = dim // 2
    inv = ROPE_THETA ** (-np.arange(0, half, 2, dtype=np.float64) / half)
    ang = np.concatenate([row_idx[:, None] * inv, col_idx[:, None] * inv], axis=-1)
    cos = np.repeat(np.cos(ang), 2, axis=1)
    sin = np.repeat(np.sin(ang), 2, axis=1)
    sin[:, 0::2] *= -1.0
    reps = LANES // dim
    return (np.tile(cos, (1, reps)).astype(np.float32), np.tile(sin, (1, reps)).astype(np.float32))


def _lane_group_matrices():
    i = np.arange(2 * LANES)
    same = lambda width: (i[:, None] // width == i[None, :] // width).astype(np.float32)
    swap = (i[:, None] == (i[None, :] ^ 1)).astype(np.float32)
    return same(HD_C), same(DK_D), swap


def _qk_prep_kernel(p_ref, cqg_ref, ckg_ref, dqg_ref, dkg_ref, grp_c_ref, grp_d_ref, swap_ref, *rest,
                    use_rope):
    if use_rope:
        cc_ref, sc_ref, cd_ref, sd_ref = rest[:4]
        rest = rest[4:]
    qc_ref, qd_ref, kc_ref, vc_ref, kd_ref, vd_ref = rest
    p = p_ref[0]
    pair = 2 * LANES
    two = lambda r: jnp.concatenate([r[...], r[...]], axis=1)

    def norm(x, grp_ref, width, g_ref, cos_ref, sin_ref):
        ms = _dot((x * x).astype(BF16), grp_ref[...]) * (1.0 / width)
        y = x * lax.rsqrt(ms + EPS) * two(g_ref)
        if not use_rope:
            return y
        return y * two(cos_ref) + _dot(y.astype(BF16), swap_ref[...]) * two(sin_ref)

    norm_c = lambda x, g_ref: norm(x, grp_c_ref, HD_C, g_ref, cc_ref if use_rope else None,
                                   sc_ref if use_rope else None)
    norm_d = lambda x, g_ref: norm(x, grp_d_ref, DK_D, g_ref, cd_ref if use_rope else None,
                                   sd_ref if use_rope else None)
    cols = lambda start, n: p[:, start * LANES : (start + n) * LANES]
    qc_ref[0] = jnp.concatenate([norm_c(cols(2 * i, 2), cqg_ref) for i in range(H_C // 2)], axis=1)
    kc = norm_c(cols(H_C, KV_C), ckg_ref)
    for j in range(KV_C):
        kc_ref[0, j] = kc[:, j * LANES : (j + 1) * LANES]
        vc_ref[0, j] = cols(H_C + KV_C + j, 1)
    base = H_C + 2 * KV_C
    qd_ref[0] = jnp.concatenate(
        [norm_d(cols(base + 2 * i, 2), dqg_ref) for i in range(H_D // 2)], axis=1)
    for i in range(H_D // 2):
        kd = norm_d(cols(base + H_D + 2 * i, 2), dkg_ref)
        for j in range(2):
            kd_ref[0, 2 * i + j] = kd[:, j * LANES : (j + 1) * LANES]
    for j in range(H_D):
        vd_ref[0, j] = cols(base + 2 * H_D + j, 1)
    assert pair == KV_C * HD_C


def _qk_prep(p, cqg, ckg, dqg, dkg, seq_len, use_rope, tm=256):
    ngrp = p.shape[0]
    bsz = ngrp * ROW_GROUP // seq_len
    p = p.reshape(bsz, seq_len, p.shape[2])
    tm = min(tm, seq_len)
    vec = lambda: pl.BlockSpec((1, LANES), lambda b, i: (0, 0))
    mat = lambda: pl.BlockSpec((2 * LANES, 2 * LANES), lambda b, i: (0, 0))
    tab = lambda: pl.BlockSpec((tm, LANES), lambda b, i: (i, 0))
    in_specs = [pl.BlockSpec((1, tm, p.shape[2]), lambda b, i: (b, i, 0)), vec(), vec(), vec(), vec(),
                mat(), mat(), mat()]
    args = [p, cqg.reshape(1, HD_C), ckg.reshape(1, HD_C),
            jnp.tile(dqg.reshape(1, DK_D), (1, 2)), jnp.tile(dkg.reshape(1, DK_D), (1, 2))]
    args += [jnp.asarray(m, BF16) for m in _lane_group_matrices()]
    if use_rope:
        in_specs += [tab(), tab(), tab(), tab()]
        args += [jnp.asarray(a) for a in _rope_tables(seq_len, HD_C) + _rope_tables(seq_len, DK_D)]
    tok = lambda w: pl.BlockSpec((1, tm, w), lambda b, i: (b, i, 0))
    head = lambda nh: pl.BlockSpec((1, nh, tm, LANES), lambda b, i: (b, 0, i, 0))
    tok_shape = lambda w: jax.ShapeDtypeStruct((bsz, seq_len, w), F32)
    head_shape = lambda nh: jax.ShapeDtypeStruct((bsz, nh, seq_len, LANES), F32)
    return pl.pallas_call(
        functools.partial(_qk_prep_kernel, use_rope=use_rope),
        grid=(bsz, seq_len // tm),
        in_specs=in_specs,
        out_specs=[tok(H_C * HD_C), tok(H_D * 2 * DK_D), head(KV_C), head(KV_C), head(H_D), head(H_D)],
        out_shape=[tok_shape(H_C * HD_C), tok_shape(H_D * 2 * DK_D), head_shape(KV_C),
                   head_shape(KV_C), head_shape(H_D), head_shape(H_D)],
        compiler_params=_cparams("arbitrary", "arbitrary"),
        name="qk_prep",
    )(*args)


def _softmax_pv(q_list, kv_list):
    outs = []
    for q in q_list:
        scores = [_dot_nt(q, k) for k, _ in kv_list]
        mx = scores[0].max(axis=1, keepdims=True)
        for s in scores[1:]:
            mx = jnp.maximum(mx, s.max(axis=1, keepdims=True))
        den = 0.0
        acc = 0.0
        for s, (_, v) in zip(scores, kv_list):
            pexp = jnp.exp2(s - mx)
            den = den + pexp.sum(axis=1, keepdims=True)
            acc = acc + _dot(pexp.astype(BF16), v)
        outs.append(acc / den)
    return outs


def _head_kv(j, k_ref, v_ref, cache_refs):
    kv = [(r_k[0, 0, j].astype(BF16), r_v[0, 0, j].astype(BF16)) for r_k, r_v in cache_refs]
    kv.append((k_ref[0, j].astype(BF16), v_ref[0, j].astype(BF16)))
    return kv


def _per_slab(q_ref, o_ref, cols, fn):
    for r0 in range(0, q_ref.shape[1], ATT_SLAB):
        rows = pl.ds(r0, min(ATT_SLAB, q_ref.shape[1]))
        o_ref[0, rows, cols] = fn(q_ref[0, rows, cols])


def _gqa_kernel(q_ref, k_ref, v_ref, *rest, has_cache):
    cache_refs = [rest[:2]] if has_cache else []
    o_ref = rest[-1]
    g_c = H_C // KV_C
    width = g_c * HD_C
    for j in range(k_ref.shape[1]):
        kv = _head_kv(j, k_ref, v_ref, cache_refs)

        def head(q):
            q = q * (HD_C ** -0.5 * LOG2E)
            qs = [q[:, g * HD_C : (g + 1) * HD_C].astype(BF16) for g in range(g_c)]
            return jnp.concatenate(_softmax_pv(qs, kv), axis=1)

        _per_slab(q_ref, o_ref, slice(j * width, (j + 1) * width), head)


def _diff_kernel(q_ref, k_ref, v_ref, lam_ref, sg_ref, *rest, has_cache, out_scale):
    cache_refs = [rest[:2]] if has_cache else []
    o_ref = rest[-1]
    width = 2 * DK_D
    for j in range(k_ref.shape[1]):
        kv = _head_kv(j, k_ref, v_ref, cache_refs)

        def head(q):
            q = q * (DK_D ** -0.5 * LOG2E)
            low = lax.broadcasted_iota(jnp.int32, q.shape, 1) < DK_D
            qs = [jnp.where(low, q, 0.0).astype(BF16), jnp.where(low, 0.0, q).astype(BF16)]
            o1, o2 = _softmax_pv(qs, kv)
            return _rms(o1 - lam_ref[...] * o2) * sg_ref[...] * out_scale

        _per_slab(q_ref, o_ref, slice(j * width, (j + 1) * width), head)


def _attention(q, k, v, cache_k, cache_v, kernel, extra_args, n_heads, q_width, seq_len, hps, tq):
    bsz = q.shape[0]
    in_specs = [pl.BlockSpec((1, tq, hps * q_width), lambda b, h, i: (b, i, h)),
                pl.BlockSpec((1, hps, seq_len, LANES), lambda b, h, i: (b, h, 0, 0)),
                pl.BlockSpec((1, hps, seq_len, LANES), lambda b, h, i: (b, h, 0, 0))]
    in_specs += [pl.BlockSpec((1, LANES), lambda b, h, i: (0, 0)) for _ in extra_args]
    args = [q, k, v, *extra_args]
    if cache_k is not None:
        past = cache_k.shape[3]
        spec = lambda: pl.BlockSpec((1, 1, hps, past, LANES), lambda b, h, i: (b, 0, h, 0, 0))
        in_specs += [spec(), spec()]
        args += [cache_k, cache_v]
    return pl.pallas_call(
        functools.partial(kernel, has_cache=cache_k is not None),
        grid=(bsz, n_heads // hps, seq_len // tq),
        in_specs=in_specs,
        out_specs=pl.BlockSpec((1, tq, hps * q_width), lambda b, h, i: (b, i, h)),
        out_shape=jax.ShapeDtypeStruct((bsz, seq_len, n_heads * q_width), F32),
        compiler_params=_cparams("arbitrary", "arbitrary", "arbitrary"),
        name="attention",
    )(*args)


def _route(lg):
    lane = lax.broadcasted_iota(jnp.int32, lg.shape, 1).astype(F32)
    neg = -1e30
    is_g = (lane >= N_EXPERTS) & (lane < N_EXPERTS + N_GROUPS)
    gl = jnp.where(is_g, lg, neg)
    gmax = gl.max(axis=1, keepdims=True)
    g_p = 1.0 / jnp.where(is_g, jnp.exp(gl - gmax), 0.0).sum(axis=1, keepdims=True)
    g_i = jnp.where(gl == gmax, lane - N_EXPERTS, 1e9).min(axis=1, keepdims=True)
    in_grp = (lane < N_EXPERTS) & (jnp.floor(lane * (1.0 / EXP_PER_GROUP)) == g_i)
    el = jnp.where(in_grp, lg, neg)
    m1 = el.max(axis=1, keepdims=True)
    i1 = jnp.where(in_grp & (el == m1), lane, 1e9).min(axis=1, keepdims=True)
    el2 = jnp.where(lane == i1, neg, el)
    m2 = el2.max(axis=1, keepdims=True)
    i2 = jnp.where(in_grp & (el2 == m2) & (lane != i1), lane, 1e9).min(axis=1, keepdims=True)
    r = jnp.exp(m2 - m1)
    w1 = g_p / (1.0 + r)
    rec = jnp.where(lane == ROUTE_LANE, i1, 0.0)
    rec = jnp.where(lane == ROUTE_LANE + 1, i2, rec)
    rec = jnp.where(lane == ROUTE_LANE + 2, w1, rec)
    return jnp.where(lane == ROUTE_LANE + 3, w1 * r, rec)


def _mix_out_kernel(x_ref, a_ref, b_ref, m_ref, g2_ref, w_ref, wr_ref, br_ref,
                    x1_ref, h2_ref, gates_ref, *, parts):
    wa = a_ref.shape[2]
    m = m_ref[0]
    rows_per_part = x_ref.shape[1] // parts
    for part in range(parts):
        rows = pl.ds(part * rows_per_part, rows_per_part)
        o = (_dot(a_ref[0, rows, :].astype(BF16), w_ref[:wa])
             + _dot(b_ref[0, rows, :].astype(BF16), w_ref[wa:]))
        x1 = x_ref[0, rows, :] + m[2:3] * o
        x1_ref[0, rows, :] = x1
        h2 = _rms(x1) * g2_ref[...] * (1.0 + m[4:5]) + m[3:4]
        h2_ref[0, rows, :] = h2.astype(BF16)
        r = _dot(jnp.concatenate(_split_bf16(h2), axis=0), wr_ref[...])
        hi_rows, lo_rows = r[:rows_per_part], r[rows_per_part:]
        lg = hi_rows[:, :LANES] + hi_rows[:, LANES:] + lo_rows[:, :LANES] + br_ref[...]
        gates_ref[0, rows, :] = _route(lg)


def _mix_out(x, a, b, mods, g2, w_out_bf16, w_router, b_router, tm=512, parts=2):
    ngrp = x.shape[0]
    wa, wb = a.shape[2], b.shape[2]
    wr = jnp.concatenate(_split_bf16(w_router), axis=1)
    row = lambda w: pl.BlockSpec((1, tm, w), lambda g, i: (g, i, 0))
    full = lambda shape: pl.BlockSpec(shape, lambda g, i: (0,) * len(shape))
    return pl.pallas_call(
        functools.partial(_mix_out_kernel, parts=parts),
        grid=(ngrp, ROW_GROUP // tm),
        in_specs=[row(D_MODEL), row(wa), row(wb),
                  pl.BlockSpec((1, 6, D_MODEL), lambda g, i: (g, 0, 0)),
                  full((1, D_MODEL)), full((wa + wb, D_MODEL)),
                  full((D_MODEL, 2 * LANES)), full((1, LANES))],
        out_specs=[row(D_MODEL), row(D_MODEL), row(LANES)],
        out_shape=[jax.ShapeDtypeStruct((ngrp, ROW_GROUP, D_MODEL), F32),
                   jax.ShapeDtypeStruct((ngrp, ROW_GROUP, D_MODEL), BF16),
                   jax.ShapeDtypeStruct((ngrp, ROW_GROUP, LANES), F32)],
        compiler_params=_cparams("arbitrary", "arbitrary"),
        name="mix_out",
    )(x, a, b, mods, g2.reshape(1, D_MODEL), w_out_bf16, wr, b_router)


def _lane_col(x, lane, k):
    return jnp.where(lane == k, x, 0.0).sum(axis=1, keepdims=True)


def _moe_dispatch_kernel(h_ref, r_ref, tri_ref, xs_ref, pos_ref, cnt_ref):
    for t in range(pos_ref.shape[0]):
        _moe_dispatch_tile(t, h_ref, r_ref, tri_ref, xs_ref, pos_ref, cnt_ref)


def _moe_dispatch_tile(t, h_ref, r_ref, tri_ref, xs_ref, pos_ref, cnt_ref):
    r = r_ref[0, pl.ds(t * MOE_TILE, MOE_TILE), :]
    lane = lax.broadcasted_iota(jnp.int32, r.shape, 1).astype(F32)
    i1, i2, w1, w2 = [_lane_col(r, lane, ROUTE_LANE + k) for k in range(4)]
    oh1 = lane == i1
    oh2 = lane == i2
    oh = jnp.where(oh1 | oh2, 1.0, 0.0)
    rank = _dot(tri_ref[...], oh.astype(BF16))
    cnt = oh.sum(axis=0, keepdims=True)
    chunks = jnp.floor((cnt + (MOE_CHUNK - 1)) * (1.0 / MOE_CHUNK))
    li = lax.broadcasted_iota(jnp.int32, (LANES, LANES), 0)
    lj = lax.broadcasted_iota(jnp.int32, (LANES, LANES), 1)
    before = jnp.where(li < lj, 1.0, 0.0).astype(BF16)
    seg = _dot(jnp.broadcast_to(chunks, (8, LANES)).astype(BF16), before)[0:1]
    base = seg * MOE_CHUNK + rank
    pos1 = jnp.where(oh1, base, 0.0).sum(axis=1, keepdims=True)
    pos2 = jnp.where(oh2, base, 0.0).sum(axis=1, keepdims=True)
    riota = lax.broadcasted_iota(jnp.int32, (r.shape[0], MOE_TILE_ROWS), 1).astype(F32)
    p = jnp.where((riota == pos1) | (riota == pos2), 1.0, 0.0).astype(BF16)

    aux = jnp.where(lane == 2 * MOE_W_PIECES, i1, 0.0)
    for k, w in enumerate((w1, w2)):
        rest = w
        for piece in range(MOE_W_PIECES):
            part = rest.astype(BF16).astype(F32)
            aux = jnp.where(lane == k * MOE_W_PIECES + piece, part, aux)
            rest = rest - part
    row = jnp.concatenate([h_ref[0, pl.ds(t * MOE_TILE, MOE_TILE), :], aux.astype(BF16)], axis=1)
    xs_ref[pl.ds(t * MOE_TILE_ROWS, MOE_TILE_ROWS), :] = _dot_tn(p, row).astype(BF16)
    pos_ref[t] = jnp.where(lane == 0, pos1, jnp.where(lane == 1, pos2, 0.0))
    cnt_ref[t] = jnp.broadcast_to(cnt, (8, LANES))


def _moe_dispatch(h2, route):
    ngrp = h2.shape[0]
    per = ROW_GROUP // MOE_TILE
    ntile = ngrp * per
    tri = np.tril(np.ones((MOE_TILE, MOE_TILE), np.float32), -1)
    tok = lambda w: pl.BlockSpec((1, ROW_GROUP, w), lambda g: (jnp.minimum(g, ngrp - 1), 0, 0))
    return pl.pallas_call(
        _moe_dispatch_kernel,
        grid=(ngrp + 1,),
        in_specs=[tok(D_MODEL), tok(LANES), pl.BlockSpec((MOE_TILE, MOE_TILE), lambda g: (0, 0))],
        out_specs=[pl.BlockSpec((per * MOE_TILE_ROWS, MOE_ROW_W), lambda g: (g, 0)),
                   pl.BlockSpec((per, MOE_TILE, LANES), lambda g: (g, 0, 0)),
                   pl.BlockSpec((per, 8, LANES), lambda g: (g, 0, 0))],
        out_shape=[jax.ShapeDtypeStruct(((ntile + per) * MOE_TILE_ROWS, MOE_ROW_W), BF16),
                   jax.ShapeDtypeStruct((ntile + per, MOE_TILE, LANES), F32),
                   jax.ShapeDtypeStruct((ntile + per, 8, LANES), F32)],
        compiler_params=_cparams("arbitrary"),
        name="moe_dispatch",
    )(h2, route, jnp.asarray(tri, BF16))


def _moe_tables(cnt, ntile):
    nblk = ntile * MOE_TILE_CHUNKS // MOE_BLOCK_CHUNKS + N_EXPERTS
    chunks = (cnt + MOE_CHUNK - 1) // MOE_CHUNK
    seg_start = jnp.cumsum(chunks, axis=1) - chunks
    tile_prefix = jnp.cumsum(chunks, axis=0) - chunks
    per_expert = chunks.sum(axis=0)
    blocks = (per_expert + MOE_BLOCK_CHUNKS - 1) // MOE_BLOCK_CHUNKS
    blk_end = jnp.cumsum(blocks)
    n_used = blk_end[-1]
    b = jnp.arange(nblk + 1, dtype=jnp.int32)
    blk_e = jnp.sum(b[:, None] >= blk_end[None, :], axis=1).astype(jnp.int32)
    last_e = jnp.sum((n_used - 1) >= blk_end).astype(jnp.int32)
    blk_e = jnp.clip(jnp.where(b < n_used, blk_e, last_e), 0, N_EXPERTS - 1)
    oh_e = (blk_e[:, None] == jnp.arange(N_EXPERTS)[None, :]).astype(jnp.int32)
    pick = lambda per_tile: jnp.sum(oh_e[:, :, None] * per_tile.T[None], axis=1)
    seg_e, pre_e, chunks_e = pick(seg_start), pick(tile_prefix), pick(chunks)
    first_blk = jnp.sum(oh_e * (blk_end - blocks)[None, :], axis=1)
    k = (b - first_blk)[:, None] * MOE_BLOCK_CHUNKS + jnp.arange(MOE_BLOCK_CHUNKS)[None, :]
    k3 = k[:, :, None]
    in_tile = (pre_e[:, None, :] <= k3) & (k3 < (pre_e + chunks_e)[:, None, :])
    tile_base = (jnp.arange(ntile) * MOE_TILE_CHUNKS)[None, :] + seg_e - pre_e
    src = jnp.sum(jnp.where(in_tile, tile_base[:, None, :] + k3, 0), axis=-1)
    valid = jnp.any(in_tile, axis=-1) & (b < n_used)[:, None]
    slot_c = jnp.arange(MOE_BLOCK_CHUNKS)[None, :]
    n_read = MOE_TILE_CHUNKS - 2 * MOE_BLOCK_CHUNKS
    assert n_read > 0
    spare = ntile * MOE_TILE_CHUNKS
    gather = jnp.where(valid, src, spare + 2 * MOE_BLOCK_CHUNKS + slot_c % n_read).astype(jnp.int32)
    scatter = jnp.where(valid, src, spare + (b % 2)[:, None] * MOE_BLOCK_CHUNKS + slot_c).astype(jnp.int32)
    blk_start = jnp.concatenate([blk_end - blocks, n_used[None]]).astype(jnp.int32)
    return blk_start, gather.reshape(-1), scatter.reshape(-1)


def _moe_expert_kernel(start_ref, gather_ref, scatter_ref, xs_hbm, wg_ref, wu_ref, wd_ref, ys_hbm,
                       lhs, obuf, wgb, wub, wdb, in_sem, out_sem):
    e = pl.program_id(0)
    ne = pl.num_programs(0)
    b0 = start_ref[e]
    b1 = start_ref[e + 1]
    n = start_ref[ne]

    def chunk_copy(blk, slot, c, gather):
        rows = pl.ds(c * MOE_CHUNK, MOE_CHUNK)
        if gather:
            idx = gather_ref[blk * MOE_BLOCK_CHUNKS + c]
            return pltpu.make_async_copy(xs_hbm.at[idx], lhs.at[slot, rows], in_sem.at[slot])
        idx = scatter_ref[blk * MOE_BLOCK_CHUNKS + c]
        dst = ys_hbm.at[idx, pl.ds(0, MOE_CHUNK), pl.ds(0, D_MODEL)]
        return pltpu.make_async_copy(obuf.at[slot, rows], dst, out_sem.at[slot])

    def for_chunks(blk, slot, gather, start):
        for c in range(MOE_BLOCK_CHUNKS):
            cp = chunk_copy(blk, slot, c, gather)
            if start:
                cp.start()
            else:
                cp.wait()

    @pl.when((e == 0) & (n > 0))
    def _():
        for_chunks(0, 0, True, True)

    @pl.when(b1 > b0)
    def _():
        wgb[...] = wg_ref[0, 0].astype(BF16)
        wub[...] = wu_ref[0, 0].astype(BF16)
        wdb[...] = wd_ref[0, 0].astype(BF16)
        e_f32 = e.astype(F32)

        def block(b, carry):
            slot = b % 2

            @pl.when(b >= 2)
            def _():
                for_chunks(b - 2, slot, False, False)

            for_chunks(b, slot, True, False)
            for_chunks(b + 1, 1 - slot, True, True)
            xa = lhs[slot]
            x = xa[:, :D_MODEL]
            aux = xa[:, D_MODEL:].astype(F32)
            lane = lax.broadcasted_iota(jnp.int32, aux.shape, 1)
            first = lane < MOE_W_PIECES
            w_first = jnp.where(first, aux, 0.0).sum(axis=1, keepdims=True)
            w_second = jnp.where(first | (lane >= 2 * MOE_W_PIECES), 0.0, aux).sum(axis=1, keepdims=True)
            e_first = _lane_col(aux, lane, 2 * MOE_W_PIECES)
            w = jnp.where(e_first == e_f32, w_first, w_second)
            hid = _silu(_dot(x, wgb[...])) * _dot(x, wub[...]) * w
            obuf[slot] = _dot(hid.astype(BF16), wdb[...]).astype(BF16)
            for_chunks(b, slot, False, True)
            return carry

        lax.fori_loop(b0, b1, block, 0)

    @pl.when(e == ne - 1)
    def _():
        @pl.when(n >= 2)
        def _():
            for_chunks(n - 2, n % 2, False, False)

        @pl.when(n >= 1)
        def _():
            for_chunks(n, n % 2, True, False)
            for_chunks(n - 1, (n - 1) % 2, False, False)


def _moe_experts(xs, blk_start, gather, scatter, layer, w_gate, w_up, w_down):
    nchunk = xs.shape[0] // MOE_CHUNK
    rows_per_blk = MOE_BLOCK_CHUNKS * MOE_CHUNK
    wspec = lambda shape: pl.BlockSpec((1, 1) + shape, lambda e, st, g, s: (layer, e, 0, 0))
    hbm = pl.BlockSpec(memory_space=pl.ANY)
    ys = pl.pallas_call(
        _moe_expert_kernel,
        grid_spec=pltpu.PrefetchScalarGridSpec(
            num_scalar_prefetch=3,
            grid=(N_EXPERTS,),
            in_specs=[hbm, wspec((D_MODEL, D_EXPERT)), wspec((D_MODEL, D_EXPERT)),
                      wspec((D_EXPERT, D_MODEL))],
            out_specs=hbm,
            scratch_shapes=[pltpu.VMEM((2, rows_per_blk, MOE_ROW_W), BF16),
                            pltpu.VMEM((2, rows_per_blk, D_MODEL), BF16),
                            pltpu.VMEM((D_MODEL, D_EXPERT), BF16),
                            pltpu.VMEM((D_MODEL, D_EXPERT), BF16),
                            pltpu.VMEM((D_EXPERT, D_MODEL), BF16),
                            pltpu.SemaphoreType.DMA((2,)),
                            pltpu.SemaphoreType.DMA((2,))]),
        out_shape=jax.ShapeDtypeStruct((nchunk, MOE_CHUNK, MOE_ROW_W), BF16),
        input_output_aliases={3: 0},
        compiler_params=_cparams("arbitrary"),
        name="moe_experts",
    )(blk_start, gather, scatter, xs.reshape(nchunk, MOE_CHUNK, MOE_ROW_W), w_gate, w_up, w_down)
    return ys.reshape(nchunk * MOE_CHUNK, MOE_ROW_W)


def _moe_combine_kernel(ys_ref, pos_ref, x_ref, m_ref, o_ref):
    pos = pos_ref[0]
    lane = lax.broadcasted_iota(jnp.int32, pos.shape, 1)
    pos1 = _lane_col(pos, lane, 0)
    pos2 = _lane_col(pos, lane, 1)
    riota = lax.broadcasted_iota(jnp.int32, (pos.shape[0], MOE_TILE_ROWS), 1).astype(F32)
    p = jnp.where((riota == pos1) | (riota == pos2), 1.0, 0.0).astype(BF16)
    o_ref[0] = x_ref[0] + m_ref[0][5:6] * _dot(p, ys_ref[...])


def _moe_combine(ys, pos, x1, mods):
    ngrp = x1.shape[0]
    per = ROW_GROUP // MOE_TILE
    tok = pl.BlockSpec((1, MOE_TILE, D_MODEL), lambda j: (j // per, j % per, 0))
    return pl.pallas_call(
        _moe_combine_kernel,
        grid=(ngrp * per,),
        in_specs=[pl.BlockSpec((MOE_TILE_ROWS, D_MODEL), lambda j: (j, 0)),
                  pl.BlockSpec((1, MOE_TILE, LANES), lambda j: (j, 0, 0)),
                  tok, pl.BlockSpec((1, 6, D_MODEL), lambda j: (j // per, 0, 0))],
        out_specs=tok,
        out_shape=jax.ShapeDtypeStruct(x1.shape, F32),
        compiler_params=_cparams("arbitrary"),
        name="moe_combine",
    )(ys, pos, x1, mods)


def _moe(h2, route, x1, mods, layer, w_gate, w_up, w_down):
    ntile = h2.shape[0] * (ROW_GROUP // MOE_TILE)
    xs, pos, cnt = _moe_dispatch(h2, route)
    cnt = cnt[:ntile, 0, :N_EXPERTS].astype(jnp.int32)
    blk_start, gather, scatter = _moe_tables(cnt, ntile)
    ys = _moe_experts(xs, blk_start, gather, scatter, layer, w_gate, w_up, w_down)
    return _moe_combine(ys, pos, x1, mods)


def kernel(x_prompt, x_sample, state_hgrn, cache_c_k, cache_c_v, cache_d_k, cache_d_v, c, c_ctx, norm1_g, norm2_g, w_mod, b_mod, even_w_in, even_w_out, hgrn_lower, hgrn_norm_g, hy_conv_w, hy_conv_b, hy_w1, hy_b1, hy_freq1, hy_w2, hy_b2, hy_freq2, hy_w3, hy_bias, odd_w_in, odd_w_out, c_qnorm_g, c_knorm_g, d_qnorm_g, d_knorm_g, d_lambda_q1, d_lambda_k1, d_lambda_q2, d_lambda_k2, d_subln_g, moe_w_grp, moe_b_grp, moe_w_rt, moe_b_rt, moe_w_gate, moe_w_up, moe_w_down):
    depth = w_mod.shape[0]
    n_ctx, seq, _ = x_prompt.shape
    n_lat, dec_seq, _ = x_sample.shape
    g_ctx = n_ctx * seq // ROW_GROUP
    g_lat = n_lat * dec_seq // ROW_GROUP
    assert dec_seq == ROW_GROUP and ROW_GROUP % seq == 0

    cond = jnp.zeros((16, D_MODEL), F32).at[0].set(c_ctx).at[1 : 1 + n_lat].set(c)
    mods = _adaln(cond, w_mod, b_mod).reshape(depth, 16, 6, D_MODEL)
    lower = jnp.cumsum(jax.nn.softmax(hgrn_lower.astype(F32), axis=0), axis=0)

    streams = [
        dict(x=x_prompt.reshape(g_ctx, ROW_GROUP, D_MODEL), ngrp=g_ctx, bsz=n_ctx, seq=seq, ctx=True),
        dict(x=x_sample, ngrp=g_lat, bsz=n_lat, seq=dec_seq, ctx=False),
    ]
    new_state, new_ck, new_cv, new_dk, new_dv = [], [], [], [], []

    for l in range(depth):
        j = l // 2
        w_router = jnp.zeros((D_MODEL, LANES), F32)
        w_router = w_router.at[:, :N_EXPERTS].set(moe_w_rt[l])
        w_router = w_router.at[:, N_EXPERTS : N_EXPERTS + N_GROUPS].set(moe_w_grp[l])
        b_router = jnp.zeros((1, LANES), F32)
        b_router = b_router.at[0, :N_EXPERTS].set(moe_b_rt[l])
        b_router = b_router.at[0, N_EXPERTS : N_EXPERTS + N_GROUPS].set(moe_b_grp[l])
        if l % 2 == 0:
            w_in = even_w_in[j].astype(BF16)
            w_out = even_w_out[j].astype(BF16)
        else:
            w_in = odd_w_in[j].astype(BF16)
            w_out = odd_w_out[j].astype(BF16)
            lam_init = 0.8 - 0.6 * math.exp(-0.3 * l)
            lam = (jnp.exp(jnp.sum(d_lambda_q1[j] * d_lambda_k1[j]))
                   - jnp.exp(jnp.sum(d_lambda_q2[j] * d_lambda_k2[j])) + lam_init)
            lam_row = jnp.full((1, LANES), lam, F32)

        for s in streams:
            ngrp, bsz, sl = s["ngrp"], s["bsz"], s["seq"]
            if s["ctx"]:
                m = jnp.broadcast_to(mods[l, 0][None], (ngrp, 6, D_MODEL))
            else:
                m = mods[l, 1 : 1 + ngrp]
            x = s["x"]
            if l % 2 == 0:
                wa = H_A * DK_A
                splits = [(0, wa), (wa, 2 * wa), (2 * wa, 3 * wa), (3 * wa, 3 * wa + W_A),
                          (3 * wa + W_A, 3 * wa + 2 * W_A), (3 * wa + 2 * W_A, w_in.shape[1])]
                qa, ffa, fba, ia, ga, hy = _norm_proj(x, m, norm1_g[l], w_in, splits)
                per_seq = lambda t: t.reshape(bsz, sl, t.shape[-1])
                if s["ctx"]:
                    s0t = jnp.zeros((bsz, 2, H_A, DK_A, DK_A), F32)
                else:
                    s0t = jnp.swapaxes(state_hgrn[:, j].astype(F32), -1, -2)
                mix_a, s_fin = _hgrn(per_seq(qa), per_seq(ffa), per_seq(fba), per_seq(ia),
                                     per_seq(ga), lower[j], hgrn_norm_g[j], s0t, sl)
                gr, gi = _hyena_filter(sl, hy_w1[j], hy_b1[j], hy_freq1[j], hy_w2[j], hy_b2[j],
                                       hy_freq2[j], hy_w3[j], hy_bias[j])
                mix_b = _hyena(per_seq(hy), hy_conv_w[j], hy_conv_b[j], gr, gi, sl,
                               nseq=4 if s["ctx"] else 2)
                if s["ctx"]:
                    new_state.append(jnp.swapaxes(s_fin, -1, -2))
            else:
                (p,) = _norm_proj(x, m, norm1_g[l], w_in, [(0, w_in.shape[1])])
                qc, qd, kc, vc, kd, vd = _qk_prep(p, c_qnorm_g[j], c_knorm_g[j], d_qnorm_g[j],
                                                  d_knorm_g[j], sl, use_rope=not s["ctx"])
                if s["ctx"]:
                    caches = (None, None, None, None)
                    new_ck.append(kc)
                    new_cv.append(vc)
                    new_dk.append(kd)
                    new_dv.append(vd)
                else:
                    caches = (cache_c_k[:, j : j + 1], cache_c_v[:, j : j + 1],
                              cache_d_k[:, j : j + 1], cache_d_v[:, j : j + 1])
                hps_c, hps_d, tq = (KV_C, H_D, sl) if s["ctx"] else (1, 1, sl)
                mix_a = _attention(qc, kc, vc, caches[0], caches[1], _gqa_kernel, (), KV_C,
                                   (H_C // KV_C) * HD_C, sl, hps_c, tq)
                diff = functools.partial(_diff_kernel, out_scale=1.0 - lam_init)
                mix_b = _attention(qd, kd, vd, caches[2], caches[3], diff,
                                   (lam_row, d_subln_g[j].reshape(1, DV_D)), H_D, 2 * DK_D, sl,
                                   hps_d, tq)
            grp = lambda t: t.reshape(ngrp, ROW_GROUP, t.shape[-1])
            x1, h2, gates = _mix_out(x, grp(mix_a), grp(mix_b), m, norm2_g[l], w_out, w_router, b_router)
            s["x"] = _moe(h2, gates, x1, m, l, moe_w_gate, moe_w_up, moe_w_down)

    y_ctx = streams[0]["x"].reshape(n_ctx, seq, D_MODEL)
    y_lat = streams[1]["x"]
    return (y_ctx, y_lat, jnp.stack(new_state, axis=1), jnp.stack(new_ck, axis=1),
            jnp.stack(new_cv, axis=1), jnp.stack(new_dk, axis=1), jnp.stack(new_dv, axis=1))
```

```python
import functools
import math

import numpy as np
import jax
import jax.numpy as jnp
from jax import lax
from jax.experimental import pallas as pl
from jax.experimental.pallas import tpu as pltpu

F32 = jnp.float32
BF16 = jnp.bfloat16

D_MODEL = 1024
EPS = 1e-6
LOG2E = 1.0 / math.log(2.0)
GRID_W = 64
ROPE_THETA = 10000.0
H_A = 4
DK_A = 128
W_A = 512
CHUNK = 128
SUB = 8
LEVELS = (16, 32, 64, 128)
C_B = 512
FILTER_BANDS = 16
DECAY_MIN = math.log(1e-2) / 1.5
DECAY_MAX = math.log(1e-2) / 0.3
H_C = 4
KV_C = 2
HD_C = 128
H_D = 4
DK_D = 64
DV_D = 128
N_GROUPS = 4
EXP_PER_GROUP = 4
N_EXPERTS = 16
D_EXPERT = 512

ROUTE_LANE = N_EXPERTS + N_GROUPS
MOE_TILE = 512
MOE_CHUNK = 16
MOE_TILE_CHUNKS = 2 * MOE_TILE // MOE_CHUNK + N_EXPERTS
MOE_TILE_ROWS = MOE_TILE_CHUNKS * MOE_CHUNK
MOE_BLOCK_CHUNKS = 16
MOE_ROW_W = D_MODEL + 128
MOE_W_PIECES = 3
ATT_SLAB = 256

LANES = 128
ROW_GROUP = 1024
VMEM_LIMIT_BYTES = 56 * 1024 * 1024


def _cparams(*sem):
    return pltpu.CompilerParams(dimension_semantics=sem, vmem_limit_bytes=VMEM_LIMIT_BYTES)


def _split_bf16(x):
    hi = x.astype(BF16)
    lo = (x - hi.astype(F32)).astype(BF16)
    return hi, lo


def _dot(a, b):
    return jnp.dot(a, b, preferred_element_type=F32)


def _dot3(a, b):
    ah, al = _split_bf16(a)
    bh, bl = _split_bf16(b)
    return _dot(ah, bh) + _dot(al, bh) + _dot(ah, bl)


def _dot_nt(a, b):
    return lax.dot_general(a, b, (((1,), (1,)), ((), ())), preferred_element_type=F32)


def _dot_tn(a, b):
    return lax.dot_general(a, b, (((0,), (0,)), ((), ())), preferred_element_type=F32)


def _silu(x):
    return x * jax.nn.sigmoid(x)


def _rms(x, eps=EPS):
    return x * lax.rsqrt(jnp.mean(x * x, axis=-1, keepdims=True) + eps)


def _adaln_kernel(c_ref, w_ref, b_ref, o_ref):
    s = _silu(c_ref[...])
    o_ref[0] = _dot(s.astype(BF16), w_ref[0].astype(BF16)) + b_ref[0]


def _adaln(cond, w_mod, b_mod):
    depth, _, n = w_mod.shape
    rows = cond.shape[0]
    tn = 1536
    return pl.pallas_call(
        _adaln_kernel,
        grid=(depth, n // tn),
        in_specs=[
            pl.BlockSpec((rows, D_MODEL), lambda l, j: (0, 0)),
            pl.BlockSpec((1, D_MODEL, tn), lambda l, j: (l, 0, j)),
            pl.BlockSpec((1, 1, tn), lambda l, j: (l, 0, j)),
        ],
        out_specs=pl.BlockSpec((1, rows, tn), lambda l, j: (l, 0, j)),
        out_shape=jax.ShapeDtypeStruct((depth, rows, n), F32),
        compiler_params=_cparams("arbitrary", "arbitrary"),
        name="adaln",
    )(cond, w_mod, b_mod.reshape(depth, 1, n))


def _norm_proj_kernel(x_ref, m_ref, g_ref, w_ref, *o_refs, splits):
    m = m_ref[0]
    h = _rms(x_ref[0]) * g_ref[...] * (1.0 + m[1:2]) + m[0:1]
    hb = h.astype(BF16)
    for o_ref, (a, b) in zip(o_refs, splits):
        o_ref[0] = _dot(hb, w_ref[:, a:b])


def _norm_proj(x, mods, g, w_bf16, splits, tm=512):
    ngrp = x.shape[0]
    n = w_bf16.shape[1]
    kern = functools.partial(_norm_proj_kernel, splits=splits)
    return pl.pallas_call(
        kern,
        grid=(ngrp, ROW_GROUP // tm),
        in_specs=[
            pl.BlockSpec((1, tm, D_MODEL), lambda b, i: (b, i, 0)),
            pl.BlockSpec((1, 6, D_MODEL), lambda b, i: (b, 0, 0)),
            pl.BlockSpec((1, D_MODEL), lambda b, i: (0, 0)),
            pl.BlockSpec((D_MODEL, n), lambda b, i: (0, 0)),
        ],
        out_specs=[pl.BlockSpec((1, tm, hi - lo), lambda b, i: (b, i, 0)) for lo, hi in splits],
        out_shape=[jax.ShapeDtypeStruct((ngrp, ROW_GROUP, hi - lo), F32) for lo, hi in splits],
        compiler_params=_cparams("arbitrary", "arbitrary"),
        name="norm_proj",
    )(x, mods, g.reshape(1, D_MODEL), w_bf16)


def _hgrn_constants():
    c = CHUNK
    nblk = 1 + len(LEVELS)
    w = np.zeros((2, nblk * c, c), np.float32)
    for t in range(c):
        w[0, t, : t + 1] = 1.0
    for li, r in enumerate(LEVELS):
        for t in range(c):
            mid = (t // r) * r + r // 2
            if t >= mid:
                w[0, (li + 1) * c + t, mid : t + 1] = 1.0
            else:
                w[0, (li + 1) * c + t, t + 1 : mid] = 1.0
    for blk in range(nblk):
        w[1, blk * c : (blk + 1) * c] = w[0, blk * c : (blk + 1) * c][::-1, ::-1]
    m = np.zeros((2, nblk, c, c), np.float32)
    for li, r in enumerate(LEVELS):
        for t in range(c):
            for s in range(c):
                if t // r == s // r and (t % r) >= r // 2 and (s % r) < r // 2:
                    m[0, li, t, s] = 1.0
    for t in range(c):
        for s in range(c):
            if t // SUB == s // SUB and s <= t:
                m[0, nblk - 1, t, s] = 1.0
    m[1] = m[0][:, ::-1, ::-1]
    sel = np.zeros((SUB * DK_A, c), np.float32)
    for i in range(SUB):
        sel[i * DK_A : (i + 1) * DK_A, i::SUB] = 1.0
    return w, m, sel


def _hgrn_chunk(direction, r0, qa_ref, gate_ref, ia_ref, lb, wcum, masks, sel, states):
    rows = pl.ds(r0, CHUNK)
    q = _silu(qa_ref[0, rows, :])
    v = ia_ref[0, rows, :]
    f = lb + (1.0 - lb) * jax.nn.sigmoid(gate_ref[0, rows, :])
    kk = jnp.maximum(1.0 - f, 0.0)
    lf = jnp.log(f) * LOG2E
    lk = jnp.log(kk) * LOG2E
    lf_hi, lf_lo = _split_bf16(lf)
    z = _dot(wcum, lf_hi) + _dot(wcum, lf_lo)
    e = z[0:CHUNK]
    edge = e[CHUNK - 1 : CHUNK] if direction == 0 else e[0:1]
    q_in = (q * jnp.exp2(e)).astype(BF16)
    k_st = (kk * jnp.exp2(edge - e)).astype(BF16)
    st_decay = jnp.exp2(edge)
    q_lv, k_lv = [], []
    for li in range(len(LEVELS)):
        ez = jnp.exp2(z[(li + 1) * CHUNK : (li + 2) * CHUNK])
        q_lv.append((q * ez).astype(BF16))
        k_lv.append((kk * ez).astype(BF16))
    nsub = CHUNK // SUB
    c3 = (e - lk).reshape(nsub, SUB, W_A)
    pair = []
    for i in range(SUB):
        cb = jnp.broadcast_to(c3[:, i : i + 1, :], (nsub, SUB, W_A)).reshape(CHUNK, W_A)
        pair.append((q * jnp.exp2(jnp.minimum(e - cb, 0.0))).astype(BF16))
    vb = v.astype(BF16)
    outs, new_states = [], []
    for h in range(H_A):
        hs = slice(h * DK_A, (h + 1) * DK_A)
        sc = _dot(jnp.concatenate([p[:, hs] for p in pair], axis=1), sel) * masks[len(LEVELS)]
        for li in range(len(LEVELS)):
            sc = sc + _dot_nt(q_lv[li][:, hs], k_lv[li][:, hs]) * masks[li]
        st = states[h]
        outs.append(_dot(sc.astype(BF16), vb[:, hs]) + _dot_nt(q_in[:, hs], st.astype(BF16)))
        new_states.append(st_decay[:, hs] * st + _dot_tn(vb[:, hs], k_st[:, hs]))
    return jnp.concatenate(outs, axis=1), tuple(new_states)


def _hgrn_kernel(qa_ref, ff_ref, fb_ref, ia_ref, ga_ref, lb_ref, ng_ref, s0_ref, w_ref, m_ref,
                 sel_ref, o_ref, sout_ref, ob_ref, *, seq_len):
    nchunk = seq_len // CHUNK
    lb = lb_ref[...]
    sel = sel_ref[...]
    nmask = 1 + len(LEVELS)

    def scan(i, states):
        rf = pl.multiple_of(i * CHUNK, CHUNK)
        rb = pl.multiple_of((nchunk - 1 - i) * CHUNK, CHUNK)
        o_f, st_f = _hgrn_chunk(0, rf, qa_ref, ff_ref, ia_ref, lb, w_ref[0],
                                [m_ref[0, j] for j in range(nmask)], sel, states[0])
        o_b, st_b = _hgrn_chunk(1, rb, qa_ref, fb_ref, ia_ref, lb, w_ref[1],
                                [m_ref[1, j] for j in range(nmask)], sel, states[1])
        o_ref[0, pl.ds(rf, CHUNK), :] = o_f
        ob_ref[pl.ds(rb, CHUNK), :] = o_b
        return st_f, st_b

    init = tuple(tuple(s0_ref[0, d, h] for h in range(H_A)) for d in range(2))
    final = lax.fori_loop(0, nchunk, scan, init)

    ng = jnp.concatenate([ng_ref[...]] * H_A, axis=1)

    def gate(i, carry):
        rows = pl.ds(pl.multiple_of(i * CHUNK, CHUNK), CHUNK)
        tot = o_ref[0, rows, :] + ob_ref[rows, :]
        normed = jnp.concatenate(
            [_rms(tot[:, h * DK_A : (h + 1) * DK_A]) for h in range(H_A)], axis=1)
        o_ref[0, rows, :] = normed * ng * _silu(ga_ref[0, rows, :])
        return carry

    lax.fori_loop(0, nchunk, gate, 0)
    for d in range(2):
        for h in range(H_A):
            sout_ref[0, d, h] = final[d][h]


def _hgrn(qa, ff, fb, ia, ga, lb, ng, s0t, seq_len):
    bsz = qa.shape[0]
    w, m, sel = _hgrn_constants()
    seq = pl.BlockSpec((1, seq_len, W_A), lambda b: (b, 0, 0))
    full = lambda shape: pl.BlockSpec(shape, lambda b: (0,) * len(shape))
    st_spec = pl.BlockSpec((1, 2, H_A, DK_A, DK_A), lambda b: (b, 0, 0, 0, 0))
    return pl.pallas_call(
        functools.partial(_hgrn_kernel, seq_len=seq_len),
        grid=(bsz,),
        in_specs=[seq, seq, seq, seq, seq, full((1, W_A)), full((1, DK_A)), st_spec,
                  full(w.shape), full(m.shape), full(sel.shape)],
        out_specs=[seq, st_spec],
        out_shape=[jax.ShapeDtypeStruct((bsz, seq_len, W_A), F32),
                   jax.ShapeDtypeStruct((bsz, 2, H_A, DK_A, DK_A), F32)],
        scratch_shapes=[pltpu.VMEM((seq_len, W_A), F32)],
        compiler_params=_cparams("arbitrary"),
        name="hgrn",
    )(qa, ff, fb, ia, ga, lb.reshape(1, W_A), ng.reshape(1, DK_A), s0t,
      jnp.asarray(w, BF16), jnp.asarray(m, F32), jnp.asarray(sel, BF16))


def _dft_constants(seq_len):
    n = 2 * seq_len
    t = np.arange(seq_len, dtype=np.int64)
    wt = (np.arange(seq_len, dtype=np.int64)[:, None] * t[None, :]) % n
    ang = 2.0 * np.pi * wt.astype(np.float64) / n
    cos, sin = np.cos(ang), np.sin(ang)
    nyq = np.where(t % 2 == 0, 1.0, -1.0)
    sin_p = sin.copy()
    sin_p[0] = nyq
    fwd = np.concatenate([cos, sin_p], axis=0)
    icos = 2.0 * cos.T / n
    icos[:, 0] = 1.0 / n
    isin = 2.0 * sin.T / n
    isin[:, 0] = nyq / n
    inv = np.concatenate([icos, isin], axis=1)
    return fwd, inv


def _filter_embedding(seq_len):
    t = np.linspace(0.0, 1.0, seq_len)[:, None]
    w = 2.0 * np.pi * np.arange(seq_len) / seq_len
    f = np.linspace(1e-4, FILTER_BANDS - 1, FILTER_BANDS)
    ang = w[:, None] * f[None, :]
    z = np.concatenate([t, np.cos(ang), -np.sin(ang)], axis=-1)
    zp = np.zeros((seq_len, LANES), np.float64)
    zp[:, : z.shape[1]] = z
    deltas = np.abs(np.linspace(DECAY_MIN, DECAY_MAX, C_B))
    window = np.exp(-t * deltas[None, :])
    return zp.astype(np.float32), window.astype(np.float32)


def _hyena_filter_kernel(z_ref, win_ref, w1_ref, b1_ref, fr1_ref, w2_ref, b2_ref, fr2_ref, w3_ref,
                         hb_ref, fh_ref, fl_ref, gr_ref, gi_ref, *, seq_len):
    h = jnp.sin(fr1_ref[...] * (_dot3(z_ref[...], w1_ref[...]) + b1_ref[...]))
    h = jnp.sin(fr2_ref[...] * (_dot3(h, w2_ref[...]) + b2_ref[...]))
    h = _dot3(h, w3_ref[...])
    win = win_ref[...]
    hf = h[:, :C_B] * win
    hbk = h[:, C_B:] * win

    def dft(x):
        xh, xl = _split_bf16(x)
        return _dot(fh_ref[...], xh) + _dot(fh_ref[...], xl) + _dot(fl_ref[...], xh)

    p_sum = dft(hf + hbk)
    p_dif = dft(hf - hbk)
    row0 = lax.broadcasted_iota(jnp.int32, (seq_len, C_B), 0) == 0
    gr_ref[...] = p_sum[:seq_len] + hb_ref[...]
    gi_ref[...] = jnp.where(row0, p_sum[seq_len:] + hb_ref[...], p_dif[seq_len:])


def _hyena_filter(seq_len, w1, b1, fr1, w2, b2, fr2, w3, hbias):
    zemb, window = _filter_embedding(seq_len)
    fwd, _ = _dft_constants(seq_len)
    f_hi, f_lo = _split_bf16(jnp.asarray(fwd, F32))

    def pad(a, rows, cols):
        return jnp.zeros((rows, cols), F32).at[: a.shape[0], : a.shape[1]].set(a)

    args = (jnp.asarray(zemb), jnp.asarray(window), pad(w1, LANES, LANES), pad(b1[None], 1, LANES),
            pad(fr1[None], 1, LANES), pad(w2, LANES, LANES), pad(b2[None], 1, LANES),
            pad(fr2[None], 1, LANES), pad(w3, LANES, 2 * C_B), hbias.reshape(1, C_B), f_hi, f_lo)
    return pl.pallas_call(
        functools.partial(_hyena_filter_kernel, seq_len=seq_len),
        out_shape=[jax.ShapeDtypeStruct((seq_len, C_B), F32)] * 2,
        compiler_params=pltpu.CompilerParams(vmem_limit_bytes=VMEM_LIMIT_BYTES),
        name="hyena_filter",
    )(*args)


def _hyena_kernel(x0_ref, x1_ref, v_ref, cw_ref, cb_ref, gr_ref, gi_ref, f_ref, fi_ref, o_ref, *,
                  seq_len):
    tc = o_ref.shape[2]
    row = lax.broadcasted_iota(jnp.int32, (seq_len, tc), 0)

    def short_conv(u_ref, j, sq):
        u = u_ref[sq]
        prev = jnp.where(row == 0, 0.0, pltpu.roll(u, 1, 0))
        nxt = jnp.where(row == seq_len - 1, 0.0, pltpu.roll(u, seq_len - 1, 0))
        cw = cw_ref[j]
        return cw[0:1] * prev + cw[1:2] * u + cw[2:3] * nxt + cb_ref[j]

    gr, gi = gr_ref[...], gi_ref[...]
    row0 = row == 0
    for sq in range(o_ref.shape[0]):
        x0 = short_conv(x0_ref, 0, sq)
        z = short_conv(v_ref, 2, sq) * short_conv(x1_ref, 1, sq)
        p = _dot(f_ref[...], z.astype(BF16))
        a, b = p[:seq_len], p[seq_len:]
        bgi = b * gi
        yr = a * gr - jnp.where(row0, 0.0, bgi)
        yq = jnp.where(row0, bgi, a * gi + b * gr)
        y = _dot(fi_ref[...], jnp.concatenate([yr, yq], axis=0).astype(BF16))
        o_ref[sq] = y * x0


def _hyena(hy, conv_w, conv_b, gr, gi, seq_len, nseq, tc=256):
    bsz = hy.shape[0]
    nct = C_B // tc
    fwd, inv = _dft_constants(seq_len)
    cw = conv_w.reshape(3, 3, C_B).transpose(1, 0, 2)
    cb = conv_b.reshape(3, 1, C_B)
    part = lambda k: pl.BlockSpec((nseq, seq_len, tc), lambda b, j, k=k: (b, 0, k * nct + j))
    return pl.pallas_call(
        functools.partial(_hyena_kernel, seq_len=seq_len),
        grid=(bsz // nseq, nct),
        in_specs=[part(0), part(1), part(2),
                  pl.BlockSpec((3, 3, tc), lambda b, j: (0, 0, j)),
                  pl.BlockSpec((3, 1, tc), lambda b, j: (0, 0, j)),
                  pl.BlockSpec((seq_len, tc), lambda b, j: (0, j)),
                  pl.BlockSpec((seq_len, tc), lambda b, j: (0, j)),
                  pl.BlockSpec((2 * seq_len, seq_len), lambda b, j: (0, 0)),
                  pl.BlockSpec((seq_len, 2 * seq_len), lambda b, j: (0, 0))],
        out_specs=pl.BlockSpec((nseq, seq_len, tc), lambda b, j: (b, 0, j)),
        out_shape=jax.ShapeDtypeStruct((bsz, seq_len, C_B), F32),
        compiler_params=_cparams("arbitrary", "arbitrary"),
        name="hyena",
    )(hy, hy, hy, cw, cb, gr, gi, jnp.asarray(fwd, F32).astype(BF16),
      jnp.asarray(inv, F32).astype(BF16))


def _rope_tables(seq_len, dim):
    rows = seq_len // GRID_W
    row_idx = np.repeat(np.arange(rows), GRID_W).astype(np.float64)
    col_idx = np.tile(np.arange(GRID_W), rows).astype(np.float64)
    half = dim // 2
    inv = ROPE_THETA ** (-np.arange(0, half, 2, dtype=np.float64) / half)
    ang = np.concatenate([row_idx[:, None] * inv, col_idx[:, None] * inv], axis=-1)
    cos = np.repeat(np.cos(ang), 2, axis=1)
    sin = np.repeat(np.sin(ang), 2, axis=1)
    sin[:, 0::2] *= -1.0
    reps = LANES // dim
    return (np.tile(cos, (1, reps)).astype(np.float32), np.tile(sin, (1, reps)).astype(np.float32))


def _lane_group_matrices():
    i = np.arange(2 * LANES)
    same = lambda width: (i[:, None] // width == i[None, :] // width).astype(np.float32)
    swap = (i[:, None] == (i[None, :] ^ 1)).astype(np.float32)
    return same(HD_C), same(DK_D), swap


def _qk_proj_kernel(x_ref, m_ref, g_ref, w_ref, cqg_ref, ckg_ref, dqg_ref, dkg_ref, grp_c_ref,
                    grp_d_ref, swap_ref, *rest, use_rope, slabs):
    if use_rope:
        cc_ref, sc_ref, cd_ref, sd_ref = rest[:4]
        rest = rest[4:]
    qc_ref, qd_ref, kc_ref, vc_ref, kd_ref, vd_ref = rest
    m = m_ref[0]
    pair = 2 * LANES
    assert pair == KV_C * HD_C
    two = lambda r: jnp.concatenate([r, r], axis=1)
    rows_per_slab = x_ref.shape[1] // slabs

    for slab in range(slabs):
        rows = pl.ds(slab * rows_per_slab, rows_per_slab)
        hb = (_rms(x_ref[0, rows, :]) * g_ref[...] * (1.0 + m[1:2]) + m[0:1]).astype(BF16)
        cols = lambda start, n: _dot(hb, w_ref[:, start * LANES : (start + n) * LANES])

        def norm(x, grp_ref, width, g_ref, cos_ref, sin_ref):
            ms = _dot((x * x).astype(BF16), grp_ref[...]) * (1.0 / width)
            y = x * lax.rsqrt(ms + EPS) * two(g_ref[...])
            if not use_rope:
                return y
            swapped = _dot(y.astype(BF16), swap_ref[...])
            return y * two(cos_ref[rows, :]) + swapped * two(sin_ref[rows, :])

        norm_c = lambda x, g_ref: norm(x, grp_c_ref, HD_C, g_ref, cc_ref if use_rope else None,
                                       sc_ref if use_rope else None)
        norm_d = lambda x, g_ref: norm(x, grp_d_ref, DK_D, g_ref, cd_ref if use_rope else None,
                                       sd_ref if use_rope else None)
        for i in range(H_C // 2):
            qc_ref[0, rows, i * pair : (i + 1) * pair] = norm_c(cols(2 * i, 2), cqg_ref)
        kc = norm_c(cols(H_C, KV_C), ckg_ref)
        vc = cols(H_C + KV_C, KV_C)
        for j in range(KV_C):
            kc_ref[0, j, rows, :] = kc[:, j * LANES : (j + 1) * LANES]
            vc_ref[0, j, rows, :] = vc[:, j * LANES : (j + 1) * LANES]
        base = H_C + 2 * KV_C
        for i in range(H_D // 2):
            qd_ref[0, rows, i * pair : (i + 1) * pair] = norm_d(cols(base + 2 * i, 2), dqg_ref)
            kd = norm_d(cols(base + H_D + 2 * i, 2), dkg_ref)
            vd = cols(base + 2 * H_D + 2 * i, 2)
            for j in range(2):
                kd_ref[0, 2 * i + j, rows, :] = kd[:, j * LANES : (j + 1) * LANES]
                vd_ref[0, 2 * i + j, rows, :] = vd[:, j * LANES : (j + 1) * LANES]


def _qk_proj(x, mods, g, w_bf16, cqg, ckg, dqg, dkg, seq_len, use_rope, tm, slabs):
    ngrp = x.shape[0]
    spg = ROW_GROUP // seq_len
    bsz = ngrp * spg
    assert seq_len % tm == 0
    seq_of = lambda gi, i: gi * spg + (i * tm) // seq_len
    blk_of = lambda i: ((i * tm) % seq_len) // tm
    vec = lambda: pl.BlockSpec((1, LANES), lambda gi, i: (0, 0))
    mat = lambda: pl.BlockSpec((2 * LANES, 2 * LANES), lambda gi, i: (0, 0))
    tab = lambda: pl.BlockSpec((tm, LANES), lambda gi, i: (blk_of(i), 0))
    in_specs = [pl.BlockSpec((1, tm, D_MODEL), lambda gi, i: (gi, i, 0)),
                pl.BlockSpec((1, 6, D_MODEL), lambda gi, i: (gi, 0, 0)),
                pl.BlockSpec((1, D_MODEL), lambda gi, i: (0, 0)),
                pl.BlockSpec(w_bf16.shape, lambda gi, i: (0, 0)),
                vec(), vec(), vec(), vec(), mat(), mat(), mat()]
    args = [x, mods, g.reshape(1, D_MODEL), w_bf16, cqg.reshape(1, HD_C), ckg.reshape(1, HD_C),
            jnp.tile(dqg.reshape(1, DK_D), (1, 2)), jnp.tile(dkg.reshape(1, DK_D), (1, 2))]
    args += [jnp.asarray(mm, BF16) for mm in _lane_group_matrices()]
    if use_rope:
        in_specs += [tab(), tab(), tab(), tab()]
        args += [jnp.asarray(t) for t in _rope_tables(seq_len, HD_C) + _rope_tables(seq_len, DK_D)]
    tok = lambda w: pl.BlockSpec((1, tm, w), lambda gi, i: (seq_of(gi, i), blk_of(i), 0))
    head = lambda nh: pl.BlockSpec((1, nh, tm, LANES), lambda gi, i: (seq_of(gi, i), 0, blk_of(i), 0))
    tok_shape = lambda w: jax.ShapeDtypeStruct((bsz, seq_len, w), F32)
    head_shape = lambda nh: jax.ShapeDtypeStruct((bsz, nh, seq_len, LANES), F32)
    return pl.pallas_call(
        functools.partial(_qk_proj_kernel, use_rope=use_rope, slabs=slabs),
        grid=(ngrp, ROW_GROUP // tm),
        in_specs=in_specs,
        out_specs=[tok(H_C * HD_C), tok(H_D * 2 * DK_D), head(KV_C), head(KV_C), head(H_D), head(H_D)],
        out_shape=[tok_shape(H_C * HD_C), tok_shape(H_D * 2 * DK_D), head_shape(KV_C),
                   head_shape(KV_C), head_shape(H_D), head_shape(H_D)],
        compiler_params=_cparams("arbitrary", "arbitrary"),
        name="qk_proj",
    )(*args)


def _softmax_pv(q_list, kv_list):
    outs = []
    for q in q_list:
        scores = [_dot_nt(q, k) for k, _ in kv_list]
        mx = scores[0].max(axis=1, keepdims=True)
        for s in scores[1:]:
            mx = jnp.maximum(mx, s.max(axis=1, keepdims=True))
        den = 0.0
        acc = 0.0
        for s, (_, v) in zip(scores, kv_list):
            pexp = jnp.exp2(s - mx)
            den = den + pexp.sum(axis=1, keepdims=True)
            acc = acc + _dot(pexp.astype(BF16), v)
        outs.append(acc / den)
    return outs


def _head_kv(j, k_ref, v_ref, cache_refs):
    kv = [(r_k[0, 0, j].astype(BF16), r_v[0, 0, j].astype(BF16)) for r_k, r_v in cache_refs]
    kv.append((k_ref[0, j].astype(BF16), v_ref[0, j].astype(BF16)))
    return kv


def _per_slab(q_ref, o_ref, cols, fn):
    for r0 in range(0, q_ref.shape[1], ATT_SLAB):
        rows = pl.ds(r0, min(ATT_SLAB, q_ref.shape[1]))
        o_ref[0, rows, cols] = fn(q_ref[0, rows, cols])


def _gqa_kernel(q_ref, k_ref, v_ref, *rest, has_cache):
    cache_refs = [rest[:2]] if has_cache else []
    o_ref = rest[-1]
    g_c = H_C // KV_C
    width = g_c * HD_C
    for j in range(k_ref.shape[1]):
        kv = _head_kv(j, k_ref, v_ref, cache_refs)

        def head(q):
            q = q * (HD_C ** -0.5 * LOG2E)
            qs = [q[:, g * HD_C : (g + 1) * HD_C].astype(BF16) for g in range(g_c)]
            return jnp.concatenate(_softmax_pv(qs, kv), axis=1)

        _per_slab(q_ref, o_ref, slice(j * width, (j + 1) * width), head)


def _diff_kernel(q_ref, k_ref, v_ref, lam_ref, sg_ref, *rest, has_cache, out_scale):
    cache_refs = [rest[:2]] if has_cache else []
    o_ref = rest[-1]
    width = 2 * DK_D
    for j in range(k_ref.shape[1]):
        kv = _head_kv(j, k_ref, v_ref, cache_refs)

        def head(q):
            q = q * (DK_D ** -0.5 * LOG2E)
            low = lax.broadcasted_iota(jnp.int32, q.shape, 1) < DK_D
            qs = [jnp.where(low, q, 0.0).astype(BF16), jnp.where(low, 0.0, q).astype(BF16)]
            o1, o2 = _softmax_pv(qs, kv)
            return _rms(o1 - lam_ref[...] * o2) * sg_ref[...] * out_scale

        _per_slab(q_ref, o_ref, slice(j * width, (j + 1) * width), head)


def _attention(q, k, v, cache_k, cache_v, kernel, extra_args, n_heads, q_width, seq_len, hps, tq):
    bsz = q.shape[0]
    in_specs = [pl.BlockSpec((1, tq, hps * q_width), lambda b, h, i: (b, i, h)),
                pl.BlockSpec((1, hps, seq_len, LANES), lambda b, h, i: (b, h, 0, 0)),
                pl.BlockSpec((1, hps, seq_len, LANES), lambda b, h, i: (b, h, 0, 0))]
    in_specs += [pl.BlockSpec((1, LANES), lambda b, h, i: (0, 0)) for _ in extra_args]
    args = [q, k, v, *extra_args]
    if cache_k is not None:
        past = cache_k.shape[3]
        spec = lambda: pl.BlockSpec((1, 1, hps, past, LANES), lambda b, h, i: (b, 0, h, 0, 0))
        in_specs += [spec(), spec()]
        args += [cache_k, cache_v]
    return pl.pallas_call(
        functools.partial(kernel, has_cache=cache_k is not None),
        grid=(bsz, n_heads // hps, seq_len // tq),
        in_specs=in_specs,
        out_specs=pl.BlockSpec((1, tq, hps * q_width), lambda b, h, i: (b, i, h)),
        out_shape=jax.ShapeDtypeStruct((bsz, seq_len, n_heads * q_width), F32),
        compiler_params=_cparams("arbitrary", "arbitrary", "arbitrary"),
        name="attention",
    )(*args)


def _route(lg):
    lane = lax.broadcasted_iota(jnp.int32, lg.shape, 1).astype(F32)
    neg = -1e30
    is_g = (lane >= N_EXPERTS) & (lane < N_EXPERTS + N_GROUPS)
    gl = jnp.where(is_g, lg, neg)
    gmax = gl.max(axis=1, keepdims=True)
    g_p = 1.0 / jnp.where(is_g, jnp.exp(gl - gmax), 0.0).sum(axis=1, keepdims=True)
    g_i = jnp.where(gl == gmax, lane - N_EXPERTS, 1e9).min(axis=1, keepdims=True)
    in_grp = (lane < N_EXPERTS) & (jnp.floor(lane * (1.0 / EXP_PER_GROUP)) == g_i)
    el = jnp.where(in_grp, lg, neg)
    m1 = el.max(axis=1, keepdims=True)
    i1 = jnp.where(in_grp & (el == m1), lane, 1e9).min(axis=1, keepdims=True)
    el2 = jnp.where(lane == i1, neg, el)
    m2 = el2.max(axis=1, keepdims=True)
    i2 = jnp.where(in_grp & (el2 == m2) & (lane != i1), lane, 1e9).min(axis=1, keepdims=True)
    r = jnp.exp(m2 - m1)
    w1 = g_p / (1.0 + r)
    rec = jnp.where(lane == ROUTE_LANE, i1, 0.0)
    rec = jnp.where(lane == ROUTE_LANE + 1, i2, rec)
    rec = jnp.where(lane == ROUTE_LANE + 2, w1, rec)
    return jnp.where(lane == ROUTE_LANE + 3, w1 * r, rec)


def _mix_out_kernel(x_ref, a_ref, b_ref, m_ref, g2_ref, w_ref, wr_ref, br_ref,
                    x1_ref, h2_ref, gates_ref, *, parts):
    wa = a_ref.shape[2]
    m = m_ref[0]
    rows_per_part = x_ref.shape[1] // parts
    for part in range(parts):
        rows = pl.ds(part * rows_per_part, rows_per_part)
        o = (_dot(a_ref[0, rows, :].astype(BF16), w_ref[:wa])
             + _dot(b_ref[0, rows, :].astype(BF16), w_ref[wa:]))
        x1 = x_ref[0, rows, :] + m[2:3] * o
        x1_ref[0, rows, :] = x1
        h2 = _rms(x1) * g2_ref[...] * (1.0 + m[4:5]) + m[3:4]
        h2_ref[0, rows, :] = h2.astype(BF16)
        r = _dot(jnp.concatenate(_split_bf16(h2), axis=0), wr_ref[...])
        hi_rows, lo_rows = r[:rows_per_part], r[rows_per_part:]
        lg = hi_rows[:, :LANES] + hi_rows[:, LANES:] + lo_rows[:, :LANES] + br_ref[...]
        gates_ref[0, rows, :] = _route(lg)


def _mix_out(x, a, b, mods, g2, w_out_bf16, w_router, b_router, tm=512, parts=2):
    ngrp = x.shape[0]
    wa, wb = a.shape[2], b.shape[2]
    wr = jnp.concatenate(_split_bf16(w_router), axis=1)
    row = lambda w: pl.BlockSpec((1, tm, w), lambda g, i: (g, i, 0))
    full = lambda shape: pl.BlockSpec(shape, lambda g, i: (0,) * len(shape))
    return pl.pallas_call(
        functools.partial(_mix_out_kernel, parts=parts),
        grid=(ngrp, ROW_GROUP // tm),
        in_specs=[row(D_MODEL), row(wa), row(wb),
                  pl.BlockSpec((1, 6, D_MODEL), lambda g, i: (g, 0, 0)),
                  full((1, D_MODEL)), full((wa + wb, D_MODEL)),
                  full((D_MODEL, 2 * LANES)), full((1, LANES))],
        out_specs=[row(D_MODEL), row(D_MODEL), row(LANES)],
        out_shape=[jax.ShapeDtypeStruct((ngrp, ROW_GROUP, D_MODEL), F32),
                   jax.ShapeDtypeStruct((ngrp, ROW_GROUP, D_MODEL), BF16),
                   jax.ShapeDtypeStruct((ngrp, ROW_GROUP, LANES), F32)],
        compiler_params=_cparams("arbitrary", "arbitrary"),
        name="mix_out",
    )(x, a, b, mods, g2.reshape(1, D_MODEL), w_out_bf16, wr, b_router)


def _lane_col(x, lane, k):
    return jnp.where(lane == k, x, 0.0).sum(axis=1, keepdims=True)


def _moe_dispatch_kernel(h_ref, r_ref, tri_ref, xs_ref, pos_ref, cnt_ref):
    for t in range(pos_ref.shape[0]):
        _moe_dispatch_tile(t, h_ref, r_ref, tri_ref, xs_ref, pos_ref, cnt_ref)


def _moe_dispatch_tile(t, h_ref, r_ref, tri_ref, xs_ref, pos_ref, cnt_ref):
    r = r_ref[0, pl.ds(t * MOE_TILE, MOE_TILE), :]
    lane = lax.broadcasted_iota(jnp.int32, r.shape, 1).astype(F32)
    i1, i2, w1, w2 = [_lane_col(r, lane, ROUTE_LANE + k) for k in range(4)]
    oh1 = lane == i1
    oh2 = lane == i2
    oh = jnp.where(oh1 | oh2, 1.0, 0.0)
    rank = _dot(tri_ref[...], oh.astype(BF16))
    cnt = oh.sum(axis=0, keepdims=True)
    chunks = jnp.floor((cnt + (MOE_CHUNK - 1)) * (1.0 / MOE_CHUNK))
    li = lax.broadcasted_iota(jnp.int32, (LANES, LANES), 0)
    lj = lax.broadcasted_iota(jnp.int32, (LANES, LANES), 1)
    before = jnp.where(li < lj, 1.0, 0.0).astype(BF16)
    seg = _dot(jnp.broadcast_to(chunks, (8, LANES)).astype(BF16), before)[0:1]
    base = seg * MOE_CHUNK + rank
    pos1 = jnp.where(oh1, base, 0.0).sum(axis=1, keepdims=True)
    pos2 = jnp.where(oh2, base, 0.0).sum(axis=1, keepdims=True)
    riota = lax.broadcasted_iota(jnp.int32, (r.shape[0], MOE_TILE_ROWS), 1).astype(F32)
    p = jnp.where((riota == pos1) | (riota == pos2), 1.0, 0.0).astype(BF16)

    aux = jnp.where(lane == 2 * MOE_W_PIECES, i1, 0.0)
    for k, w in enumerate((w1, w2)):
        rest = w
        for piece in range(MOE_W_PIECES):
            part = rest.astype(BF16).astype(F32)
            aux = jnp.where(lane == k * MOE_W_PIECES + piece, part, aux)
            rest = rest - part
    row = jnp.concatenate([h_ref[0, pl.ds(t * MOE_TILE, MOE_TILE), :], aux.astype(BF16)], axis=1)
    xs_ref[pl.ds(t * MOE_TILE_ROWS, MOE_TILE_ROWS), :] = _dot_tn(p, row).astype(BF16)
    pos_ref[t] = jnp.where(lane == 0, pos1, jnp.where(lane == 1, pos2, 0.0))
    cnt_ref[t] = jnp.broadcast_to(cnt, (8, LANES))


def _moe_dispatch(h2, route):
    ngrp = h2.shape[0]
    per = ROW_GROUP // MOE_TILE
    ntile = ngrp * per
    tri = np.tril(np.ones((MOE_TILE, MOE_TILE), np.float32), -1)
    tok = lambda w: pl.BlockSpec((1, ROW_GROUP, w), lambda g: (jnp.minimum(g, ngrp - 1), 0, 0))
    return pl.pallas_call(
        _moe_dispatch_kernel,
        grid=(ngrp + 1,),
        in_specs=[tok(D_MODEL), tok(LANES), pl.BlockSpec((MOE_TILE, MOE_TILE), lambda g: (0, 0))],
        out_specs=[pl.BlockSpec((per * MOE_TILE_ROWS, MOE_ROW_W), lambda g: (g, 0)),
                   pl.BlockSpec((per, MOE_TILE, LANES), lambda g: (g, 0, 0)),
                   pl.BlockSpec((per, 8, LANES), lambda g: (g, 0, 0))],
        out_shape=[jax.ShapeDtypeStruct(((ntile + per) * MOE_TILE_ROWS, MOE_ROW_W), BF16),
                   jax.ShapeDtypeStruct((ntile + per, MOE_TILE, LANES), F32),
                   jax.ShapeDtypeStruct((ntile + per, 8, LANES), F32)],
        compiler_params=_cparams("arbitrary"),
        name="moe_dispatch",
    )(h2, route, jnp.asarray(tri, BF16))


def _moe_tables(cnt, ntile):
    nblk = ntile * MOE_TILE_CHUNKS // MOE_BLOCK_CHUNKS + N_EXPERTS
    chunks = (cnt + MOE_CHUNK - 1) // MOE_CHUNK
    seg_start = jnp.cumsum(chunks, axis=1) - chunks
    tile_prefix = jnp.cumsum(chunks, axis=0) - chunks
    per_expert = chunks.sum(axis=0)
    blocks = (per_expert + MOE_BLOCK_CHUNKS - 1) // MOE_BLOCK_CHUNKS
    blk_end = jnp.cumsum(blocks)
    n_used = blk_end[-1]
    b = jnp.arange(nblk + 1, dtype=jnp.int32)
    blk_e = jnp.sum(b[:, None] >= blk_end[None, :], axis=1).astype(jnp.int32)
    last_e = jnp.sum((n_used - 1) >= blk_end).astype(jnp.int32)
    blk_e = jnp.clip(jnp.where(b < n_used, blk_e, last_e), 0, N_EXPERTS - 1)
    oh_e = (blk_e[:, None] == jnp.arange(N_EXPERTS)[None, :]).astype(jnp.int32)
    pick = lambda per_tile: jnp.sum(oh_e[:, :, None] * per_tile.T[None], axis=1)
    seg_e, pre_e, chunks_e = pick(seg_start), pick(tile_prefix), pick(chunks)
    first_blk = jnp.sum(oh_e * (blk_end - blocks)[None, :], axis=1)
    k = (b - first_blk)[:, None] * MOE_BLOCK_CHUNKS + jnp.arange(MOE_BLOCK_CHUNKS)[None, :]
    k3 = k[:, :, None]
    in_tile = (pre_e[:, None, :] <= k3) & (k3 < (pre_e + chunks_e)[:, None, :])
    tile_base = (jnp.arange(ntile) * MOE_TILE_CHUNKS)[None, :] + seg_e - pre_e
    src = jnp.sum(jnp.where(in_tile, tile_base[:, None, :] + k3, 0), axis=-1)
    valid = jnp.any(in_tile, axis=-1) & (b < n_used)[:, None]
    slot_c = jnp.arange(MOE_BLOCK_CHUNKS)[None, :]
    n_read = MOE_TILE_CHUNKS - 2 * MOE_BLOCK_CHUNKS
    assert n_read > 0
    spare = ntile * MOE_TILE_CHUNKS
    gather = jnp.where(valid, src, spare + 2 * MOE_BLOCK_CHUNKS + slot_c % n_read).astype(jnp.int32)
    scatter = jnp.where(valid, src, spare + (b % 2)[:, None] * MOE_BLOCK_CHUNKS + slot_c).astype(jnp.int32)
    blk_start = jnp.concatenate([blk_end - blocks, n_used[None]]).astype(jnp.int32)
    return blk_start, gather.reshape(-1), scatter.reshape(-1)


def _moe_expert_kernel(start_ref, gather_ref, scatter_ref, xs_hbm, wg_ref, wu_ref, wd_ref, ys_hbm,
                       lhs, obuf, wgb, wub, wdb, in_sem, out_sem):
    e = pl.program_id(0)
    ne = pl.num_programs(0)
    b0 = start_ref[e]
    b1 = start_ref[e + 1]
    n = start_ref[ne]

    def chunk_copy(blk, slot, c, gather):
        rows = pl.ds(c * MOE_CHUNK, MOE_CHUNK)
        if gather:
            idx = gather_ref[blk * MOE_BLOCK_CHUNKS + c]
            return pltpu.make_async_copy(xs_hbm.at[idx], lhs.at[slot, rows], in_sem.at[slot])
        idx = scatter_ref[blk * MOE_BLOCK_CHUNKS + c]
        dst = ys_hbm.at[idx, pl.ds(0, MOE_CHUNK), pl.ds(0, D_MODEL)]
        return pltpu.make_async_copy(obuf.at[slot, rows], dst, out_sem.at[slot])

    def for_chunks(blk, slot, gather, start):
        for c in range(MOE_BLOCK_CHUNKS):
            cp = chunk_copy(blk, slot, c, gather)
            if start:
                cp.start()
            else:
                cp.wait()

    @pl.when((e == 0) & (n > 0))
    def _():
        for_chunks(0, 0, True, True)

    @pl.when(b1 > b0)
    def _():
        wgb[...] = wg_ref[0, 0].astype(BF16)
        wub[...] = wu_ref[0, 0].astype(BF16)
        wdb[...] = wd_ref[0, 0].astype(BF16)
        e_f32 = e.astype(F32)

        def block(b, carry):
            slot = b % 2

            @pl.when(b >= 2)
            def _():
                for_chunks(b - 2, slot, False, False)

            for_chunks(b, slot, True, False)
            for_chunks(b + 1, 1 - slot, True, True)
            xa = lhs[slot]
            x = xa[:, :D_MODEL]
            aux = xa[:, D_MODEL:].astype(F32)
            lane = lax.broadcasted_iota(jnp.int32, aux.shape, 1)
            first = lane < MOE_W_PIECES
            w_first = jnp.where(first, aux, 0.0).sum(axis=1, keepdims=True)
            w_second = jnp.where(first | (lane >= 2 * MOE_W_PIECES), 0.0, aux).sum(axis=1, keepdims=True)
            e_first = _lane_col(aux, lane, 2 * MOE_W_PIECES)
            w = jnp.where(e_first == e_f32, w_first, w_second)
            hid = _silu(_dot(x, wgb[...])) * _dot(x, wub[...]) * w
            obuf[slot] = _dot(hid.astype(BF16), wdb[...]).astype(BF16)
            for_chunks(b, slot, False, True)
            return carry

        lax.fori_loop(b0, b1, block, 0)

    @pl.when(e == ne - 1)
    def _():
        @pl.when(n >= 2)
        def _():
            for_chunks(n - 2, n % 2, False, False)

        @pl.when(n >= 1)
        def _():
            for_chunks(n, n % 2, True, False)
            for_chunks(n - 1, (n - 1) % 2, False, False)


def _moe_experts(xs, blk_start, gather, scatter, layer, w_gate, w_up, w_down):
    nchunk = xs.shape[0] // MOE_CHUNK
    rows_per_blk = MOE_BLOCK_CHUNKS * MOE_CHUNK
    wspec = lambda shape: pl.BlockSpec((1, 1) + shape, lambda e, st, g, s: (layer, e, 0, 0))
    hbm = pl.BlockSpec(memory_space=pl.ANY)
    ys = pl.pallas_call(
        _moe_expert_kernel,
        grid_spec=pltpu.PrefetchScalarGridSpec(
            num_scalar_prefetch=3,
            grid=(N_EXPERTS,),
            in_specs=[hbm, wspec((D_MODEL, D_EXPERT)), wspec((D_MODEL, D_EXPERT)),
                      wspec((D_EXPERT, D_MODEL))],
            out_specs=hbm,
            scratch_shapes=[pltpu.VMEM((2, rows_per_blk, MOE_ROW_W), BF16),
                            pltpu.VMEM((2, rows_per_blk, D_MODEL), BF16),
                            pltpu.VMEM((D_MODEL, D_EXPERT), BF16),
                            pltpu.VMEM((D_MODEL, D_EXPERT), BF16),
                            pltpu.VMEM((D_EXPERT, D_MODEL), BF16),
                            pltpu.SemaphoreType.DMA((2,)),
                            pltpu.SemaphoreType.DMA((2,))]),
        out_shape=jax.ShapeDtypeStruct((nchunk, MOE_CHUNK, MOE_ROW_W), BF16),
        input_output_aliases={3: 0},
        compiler_params=_cparams("arbitrary"),
        name="moe_experts",
    )(blk_start, gather, scatter, xs.reshape(nchunk, MOE_CHUNK, MOE_ROW_W), w_gate, w_up, w_down)
    return ys.reshape(nchunk * MOE_CHUNK, MOE_ROW_W)


def _moe_combine_kernel(ys_ref, pos_ref, x_ref, m_ref, o_ref):
    pos = pos_ref[0]
    lane = lax.broadcasted_iota(jnp.int32, pos.shape, 1)
    pos1 = _lane_col(pos, lane, 0)
    pos2 = _lane_col(pos, lane, 1)
    riota = lax.broadcasted_iota(jnp.int32, (pos.shape[0], MOE_TILE_ROWS), 1).astype(F32)
    p = jnp.where((riota == pos1) | (riota == pos2), 1.0, 0.0).astype(BF16)
    o_ref[0] = x_ref[0] + m_ref[0][5:6] * _dot(p, ys_ref[...])


def _moe_combine(ys, pos, x1, mods):
    ngrp = x1.shape[0]
    per = ROW_GROUP // MOE_TILE
    tok = pl.BlockSpec((1, MOE_TILE, D_MODEL), lambda j: (j // per, j % per, 0))
    return pl.pallas_call(
        _moe_combine_kernel,
        grid=(ngrp * per,),
        in_specs=[pl.BlockSpec((MOE_TILE_ROWS, D_MODEL), lambda j: (j, 0)),
                  pl.BlockSpec((1, MOE_TILE, LANES), lambda j: (j, 0, 0)),
                  tok, pl.BlockSpec((1, 6, D_MODEL), lambda j: (j // per, 0, 0))],
        out_specs=tok,
        out_shape=jax.ShapeDtypeStruct(x1.shape, F32),
        compiler_params=_cparams("arbitrary"),
        name="moe_combine",
    )(ys, pos, x1, mods)


def _moe(h2, route, x1, mods, layer, w_gate, w_up, w_down):
    ntile = h2.shape[0] * (ROW_GROUP // MOE_TILE)
    xs, pos, cnt = _moe_dispatch(h2, route)
    cnt = cnt[:ntile, 0, :N_EXPERTS].astype(jnp.int32)
    blk_start, gather, scatter = _moe_tables(cnt, ntile)
    ys = _moe_experts(xs, blk_start, gather, scatter, layer, w_gate, w_up, w_down)
    return _moe_combine(ys, pos, x1, mods)


def kernel(x_prompt, x_sample, state_hgrn, cache_c_k, cache_c_v, cache_d_k, cache_d_v, c, c_ctx, norm1_g, norm2_g, w_mod, b_mod, even_w_in, even_w_out, hgrn_lower, hgrn_norm_g, hy_conv_w, hy_conv_b, hy_w1, hy_b1, hy_freq1, hy_w2, hy_b2, hy_freq2, hy_w3, hy_bias, odd_w_in, odd_w_out, c_qnorm_g, c_knorm_g, d_qnorm_g, d_knorm_g, d_lambda_q1, d_lambda_k1, d_lambda_q2, d_lambda_k2, d_subln_g, moe_w_grp, moe_b_grp, moe_w_rt, moe_b_rt, moe_w_gate, moe_w_up, moe_w_down):
    depth = w_mod.shape[0]
    n_ctx, seq, _ = x_prompt.shape
    n_lat, dec_seq, _ = x_sample.shape
    g_ctx = n_ctx * seq // ROW_GROUP
    g_lat = n_lat * dec_seq // ROW_GROUP
    assert dec_seq == ROW_GROUP and ROW_GROUP % seq == 0

    cond = jnp.zeros((16, D_MODEL), F32).at[0].set(c_ctx).at[1 : 1 + n_lat].set(c)
    mods = _adaln(cond, w_mod, b_mod).reshape(depth, 16, 6, D_MODEL)
    lower = jnp.cumsum(jax.nn.softmax(hgrn_lower.astype(F32), axis=0), axis=0)

    streams = [
        dict(x=x_prompt.reshape(g_ctx, ROW_GROUP, D_MODEL), ngrp=g_ctx, bsz=n_ctx, seq=seq, ctx=True),
        dict(x=x_sample, ngrp=g_lat, bsz=n_lat, seq=dec_seq, ctx=False),
    ]
    new_state, new_ck, new_cv, new_dk, new_dv = [], [], [], [], []

    for l in range(depth):
        j = l // 2
        w_router = jnp.zeros((D_MODEL, LANES), F32)
        w_router = w_router.at[:, :N_EXPERTS].set(moe_w_rt[l])
        w_router = w_router.at[:, N_EXPERTS : N_EXPERTS + N_GROUPS].set(moe_w_grp[l])
        b_router = jnp.zeros((1, LANES), F32)
        b_router = b_router.at[0, :N_EXPERTS].set(moe_b_rt[l])
        b_router = b_router.at[0, N_EXPERTS : N_EXPERTS + N_GROUPS].set(moe_b_grp[l])
        if l % 2 == 0:
            w_in = even_w_in[j].astype(BF16)
            w_out = even_w_out[j].astype(BF16)
        else:
            w_in = odd_w_in[j].astype(BF16)
            w_out = odd_w_out[j].astype(BF16)
            lam_init = 0.8 - 0.6 * math.exp(-0.3 * l)
            lam = (jnp.exp(jnp.sum(d_lambda_q1[j] * d_lambda_k1[j]))
                   - jnp.exp(jnp.sum(d_lambda_q2[j] * d_lambda_k2[j])) + lam_init)
            lam_row = jnp.full((1, LANES), lam, F32)

        for s in streams:
            ngrp, bsz, sl = s["ngrp"], s["bsz"], s["seq"]
            if s["ctx"]:
                m = jnp.broadcast_to(mods[l, 0][None], (ngrp, 6, D_MODEL))
            else:
                m = mods[l, 1 : 1 + ngrp]
            x = s["x"]
            if l % 2 == 0:
                wa = H_A * DK_A
                splits = [(0, wa), (wa, 2 * wa), (2 * wa, 3 * wa), (3 * wa, 3 * wa + W_A),
                          (3 * wa + W_A, 3 * wa + 2 * W_A), (3 * wa + 2 * W_A, w_in.shape[1])]
                qa, ffa, fba, ia, ga, hy = _norm_proj(x, m, norm1_g[l], w_in, splits)
                per_seq = lambda t: t.reshape(bsz, sl, t.shape[-1])
                if s["ctx"]:
                    s0t = jnp.zeros((bsz, 2, H_A, DK_A, DK_A), F32)
                else:
                    s0t = jnp.swapaxes(state_hgrn[:, j].astype(F32), -1, -2)
                mix_a, s_fin = _hgrn(per_seq(qa), per_seq(ffa), per_seq(fba), per_seq(ia),
                                     per_seq(ga), lower[j], hgrn_norm_g[j], s0t, sl)
                gr, gi = _hyena_filter(sl, hy_w1[j], hy_b1[j], hy_freq1[j], hy_w2[j], hy_b2[j],
                                       hy_freq2[j], hy_w3[j], hy_bias[j])
                mix_b = _hyena(per_seq(hy), hy_conv_w[j], hy_conv_b[j], gr, gi, sl,
                               nseq=4 if s["ctx"] else 2)
                if s["ctx"]:
                    new_state.append(jnp.swapaxes(s_fin, -1, -2))
            else:
                qc, qd, kc, vc, kd, vd = _qk_proj(x, m, norm1_g[l], w_in, c_qnorm_g[j], c_knorm_g[j],
                                                  d_qnorm_g[j], d_knorm_g[j], sl, use_rope=not s["ctx"],
                                                  tm=min(sl, 512), slabs=1 if s["ctx"] else 2)
                if s["ctx"]:
                    caches = (None, None, None, None)
                    new_ck.append(kc)
                    new_cv.append(vc)
                    new_dk.append(kd)
                    new_dv.append(vd)
                else:
                    caches = (cache_c_k[:, j : j + 1], cache_c_v[:, j : j + 1],
                              cache_d_k[:, j : j + 1], cache_d_v[:, j : j + 1])
                hps_c, hps_d, tq = (KV_C, H_D, sl) if s["ctx"] else (1, 1, sl)
                mix_a = _attention(qc, kc, vc, caches[0], caches[1], _gqa_kernel, (), KV_C,
                                   (H_C // KV_C) * HD_C, sl, hps_c, tq)
                diff = functools.partial(_diff_kernel, out_scale=1.0 - lam_init)
                mix_b = _attention(qd, kd, vd, caches[2], caches[3], diff,
                                   (lam_row, d_subln_g[j].reshape(1, DV_D)), H_D, 2 * DK_D, sl,
                                   hps_d, tq)
            grp = lambda t: t.reshape(ngrp, ROW_GROUP, t.shape[-1])
            x1, h2, gates = _mix_out(x, grp(mix_a), grp(mix_b), m, norm2_g[l], w_out, w_router, b_router)
            s["x"] = _moe(h2, gates, x1, m, l, moe_w_gate, moe_w_up, moe_w_down)

    y_ctx = streams[0]["x"].reshape(n_ctx, seq, D_MODEL)
    y_lat = streams[1]["x"]
    return (y_ctx, y_lat, jnp.stack(new_state, axis=1), jnp.stack(new_ck, axis=1),
            jnp.stack(new_cv, axis=1), jnp.stack(new_dk, axis=1), jnp.stack(new_dv, axis=1))
```

```python
import functools
import math

import numpy as np
import jax
import jax.numpy as jnp
from jax import lax
from jax.experimental import pallas as pl
from jax.experimental.pallas import tpu as pltpu

F32 = jnp.float32
BF16 = jnp.bfloat16

D_MODEL = 1024
EPS = 1e-6
LOG2E = 1.0 / math.log(2.0)
GRID_W = 64
ROPE_THETA = 10000.0
H_A = 4
DK_A = 128
W_A = 512
CHUNK = 128
SUB = 8
LEVELS = (16, 32, 64, 128)
C_B = 512
FILTER_BANDS = 16
DECAY_MIN = math.log(1e-2) / 1.5
DECAY_MAX = math.log(1e-2) / 0.3
H_C = 4
KV_C = 2
HD_C = 128
H_D = 4
DK_D = 64
DV_D = 128
N_GROUPS = 4
EXP_PER_GROUP = 4
N_EXPERTS = 16
D_EXPERT = 512

ROUTE_LANE = N_EXPERTS + N_GROUPS
MOE_TILE = 512
MOE_CHUNK = 16
MOE_TILE_CHUNKS = 2 * MOE_TILE // MOE_CHUNK + N_EXPERTS
MOE_TILE_ROWS = MOE_TILE_CHUNKS * MOE_CHUNK
MOE_BLOCK_CHUNKS = 16
MOE_ROW_W = D_MODEL + 128
MOE_W_PIECES = 3
ATT_SLAB = 256

LANES = 128
ROW_GROUP = 1024
VMEM_LIMIT_BYTES = 56 * 1024 * 1024


def _cparams(*sem):
    return pltpu.CompilerParams(dimension_semantics=sem, vmem_limit_bytes=VMEM_LIMIT_BYTES)


def _split_bf16(x):
    hi = x.astype(BF16)
    lo = (x - hi.astype(F32)).astype(BF16)
    return hi, lo


def _dot(a, b):
    return jnp.dot(a, b, preferred_element_type=F32)


def _dot3(a, b):
    ah, al = _split_bf16(a)
    bh, bl = _split_bf16(b)
    return _dot(ah, bh) + _dot(al, bh) + _dot(ah, bl)


def _dot_nt(a, b):
    return lax.dot_general(a, b, (((1,), (1,)), ((), ())), preferred_element_type=F32)


def _dot_tn(a, b):
    return lax.dot_general(a, b, (((0,), (0,)), ((), ())), preferred_element_type=F32)


def _silu(x):
    return x * jax.nn.sigmoid(x)


def _rms(x, eps=EPS):
    return x * lax.rsqrt(jnp.mean(x * x, axis=-1, keepdims=True) + eps)


def _adaln_kernel(c_ref, w_ref, b_ref, o_ref):
    s = _silu(c_ref[...])
    o_ref[0] = _dot(s.astype(BF16), w_ref[0].astype(BF16)) + b_ref[0]


def _adaln(cond, w_mod, b_mod):
    depth, _, n = w_mod.shape
    rows = cond.shape[0]
    tn = 1536
    return pl.pallas_call(
        _adaln_kernel,
        grid=(depth, n // tn),
        in_specs=[
            pl.BlockSpec((rows, D_MODEL), lambda l, j: (0, 0)),
            pl.BlockSpec((1, D_MODEL, tn), lambda l, j: (l, 0, j)),
            pl.BlockSpec((1, 1, tn), lambda l, j: (l, 0, j)),
        ],
        out_specs=pl.BlockSpec((1, rows, tn), lambda l, j: (l, 0, j)),
        out_shape=jax.ShapeDtypeStruct((depth, rows, n), F32),
        compiler_params=_cparams("arbitrary", "arbitrary"),
        name="adaln",
    )(cond, w_mod, b_mod.reshape(depth, 1, n))


def _norm_proj_kernel(x_ref, m_ref, g_ref, w_ref, *o_refs, splits):
    m = m_ref[0]
    h = _rms(x_ref[0]) * g_ref[...] * (1.0 + m[1:2]) + m[0:1]
    hb = h.astype(BF16)
    for o_ref, (a, b) in zip(o_refs, splits):
        o_ref[0] = _dot(hb, w_ref[:, a:b])


def _norm_proj(x, mods, g, w_bf16, splits, tm=512):
    ngrp = x.shape[0]
    n = w_bf16.shape[1]
    kern = functools.partial(_norm_proj_kernel, splits=splits)
    return pl.pallas_call(
        kern,
        grid=(ngrp, ROW_GROUP // tm),
        in_specs=[
            pl.BlockSpec((1, tm, D_MODEL), lambda b, i: (b, i, 0)),
            pl.BlockSpec((1, 6, D_MODEL), lambda b, i: (b, 0, 0)),
            pl.BlockSpec((1, D_MODEL), lambda b, i: (0, 0)),
            pl.BlockSpec((D_MODEL, n), lambda b, i: (0, 0)),
        ],
        out_specs=[pl.BlockSpec((1, tm, hi - lo), lambda b, i: (b, i, 0)) for lo, hi in splits],
        out_shape=[jax.ShapeDtypeStruct((ngrp, ROW_GROUP, hi - lo), F32) for lo, hi in splits],
        compiler_params=_cparams("arbitrary", "arbitrary"),
        name="norm_proj",
    )(x, mods, g.reshape(1, D_MODEL), w_bf16)


def _hgrn_constants():
    c = CHUNK
    nblk = 1 + len(LEVELS)
    w = np.zeros((2, nblk * c, c), np.float32)
    for t in range(c):
        w[0, t, : t + 1] = 1.0
    for li, r in enumerate(LEVELS):
        for t in range(c):
            mid = (t // r) * r + r // 2
            if t >= mid:
                w[0, (li + 1) * c + t, mid : t + 1] = 1.0
            else:
                w[0, (li + 1) * c + t, t + 1 : mid] = 1.0
    for blk in range(nblk):
        w[1, blk * c : (blk + 1) * c] = w[0, blk * c : (blk + 1) * c][::-1, ::-1]
    m = np.zeros((2, nblk, c, c), np.float32)
    for li, r in enumerate(LEVELS):
        for t in range(c):
            for s in range(c):
                if t // r == s // r and (t % r) >= r // 2 and (s % r) < r // 2:
                    m[0, li, t, s] = 1.0
    for t in range(c):
        for s in range(c):
            if t // SUB == s // SUB and s <= t:
                m[0, nblk - 1, t, s] = 1.0
    m[1] = m[0][:, ::-1, ::-1]
    sel = np.zeros((SUB * DK_A, c), np.float32)
    for i in range(SUB):
        sel[i * DK_A : (i + 1) * DK_A, i::SUB] = 1.0
    return w, m, sel


def _hgrn_chunk(direction, sq, r0, qa_ref, gate_ref, ia_ref, lb, wcum, masks, sel, states):
    rows = pl.ds(r0, CHUNK)
    q = _silu(qa_ref[sq, rows, :])
    v = ia_ref[sq, rows, :]
    f = lb + (1.0 - lb) * jax.nn.sigmoid(gate_ref[sq, rows, :])
    kk = jnp.maximum(1.0 - f, 0.0)
    lf = jnp.log(f) * LOG2E
    lk = jnp.log(kk) * LOG2E
    lf_hi, lf_lo = _split_bf16(lf)
    z = _dot(wcum, lf_hi) + _dot(wcum, lf_lo)
    e = z[0:CHUNK]
    edge = e[CHUNK - 1 : CHUNK] if direction == 0 else e[0:1]
    q_in = (q * jnp.exp2(e)).astype(BF16)
    k_st = (kk * jnp.exp2(edge - e)).astype(BF16)
    st_decay = jnp.exp2(edge)
    q_lv, k_lv = [], []
    for li in range(len(LEVELS)):
        ez = jnp.exp2(z[(li + 1) * CHUNK : (li + 2) * CHUNK])
        q_lv.append((q * ez).astype(BF16))
        k_lv.append((kk * ez).astype(BF16))
    nsub = CHUNK // SUB
    c3 = (e - lk).reshape(nsub, SUB, W_A)
    pair = []
    for i in range(SUB):
        cb = jnp.broadcast_to(c3[:, i : i + 1, :], (nsub, SUB, W_A)).reshape(CHUNK, W_A)
        pair.append((q * jnp.exp2(jnp.minimum(e - cb, 0.0))).astype(BF16))
    vb = v.astype(BF16)
    outs, new_states = [], []
    for h in range(H_A):
        hs = slice(h * DK_A, (h + 1) * DK_A)
        sc = _dot(jnp.concatenate([p[:, hs] for p in pair], axis=1), sel) * masks[len(LEVELS)]
        for li in range(len(LEVELS)):
            sc = sc + _dot_nt(q_lv[li][:, hs], k_lv[li][:, hs]) * masks[li]
        st = states[h]
        outs.append(_dot(sc.astype(BF16), vb[:, hs]) + _dot_nt(q_in[:, hs], st.astype(BF16)))
        new_states.append(st_decay[:, hs] * st + _dot_tn(vb[:, hs], k_st[:, hs]))
    return jnp.concatenate(outs, axis=1), tuple(new_states)


def _hgrn_kernel(qa_ref, ff_ref, fb_ref, ia_ref, ga_ref, lb_ref, ng_ref, s0_ref, w_ref, m_ref,
                 sel_ref, o_ref, sout_ref, ob_ref, *, seq_len):
    nchunk = seq_len // CHUNK
    nseq = o_ref.shape[0]
    lb = lb_ref[...]
    sel = sel_ref[...]
    nmask = 1 + len(LEVELS)

    def scan(i, states):
        rf = pl.multiple_of(i * CHUNK, CHUNK)
        rb = pl.multiple_of((nchunk - 1 - i) * CHUNK, CHUNK)
        new_states = []
        for sq in range(nseq):
            o_f, st_f = _hgrn_chunk(0, sq, rf, qa_ref, ff_ref, ia_ref, lb, w_ref[0],
                                    [m_ref[0, j] for j in range(nmask)], sel, states[sq][0])
            o_b, st_b = _hgrn_chunk(1, sq, rb, qa_ref, fb_ref, ia_ref, lb, w_ref[1],
                                    [m_ref[1, j] for j in range(nmask)], sel, states[sq][1])
            o_ref[sq, pl.ds(rf, CHUNK), :] = o_f
            ob_ref[sq, pl.ds(rb, CHUNK), :] = o_b
            new_states.append((st_f, st_b))
        return tuple(new_states)

    init = tuple(tuple(tuple(s0_ref[sq, d, h] for h in range(H_A)) for d in range(2))
                 for sq in range(nseq))
    final = lax.fori_loop(0, nchunk, scan, init)

    ng = jnp.concatenate([ng_ref[...]] * H_A, axis=1)

    def gate(i, carry):
        rows = pl.ds(pl.multiple_of(i * CHUNK, CHUNK), CHUNK)
        for sq in range(nseq):
            tot = o_ref[sq, rows, :] + ob_ref[sq, rows, :]
            normed = jnp.concatenate(
                [_rms(tot[:, h * DK_A : (h + 1) * DK_A]) for h in range(H_A)], axis=1)
            o_ref[sq, rows, :] = normed * ng * _silu(ga_ref[sq, rows, :])
        return carry

    lax.fori_loop(0, nchunk, gate, 0)
    for sq in range(nseq):
        for d in range(2):
            for h in range(H_A):
                sout_ref[sq, d, h] = final[sq][d][h]


def _hgrn(qa, ff, fb, ia, ga, lb, ng, s0t, seq_len, nseq):
    bsz = qa.shape[0]
    w, m, sel = _hgrn_constants()
    seq = pl.BlockSpec((nseq, seq_len, W_A), lambda b: (b, 0, 0))
    full = lambda shape: pl.BlockSpec(shape, lambda b: (0,) * len(shape))
    st_spec = pl.BlockSpec((nseq, 2, H_A, DK_A, DK_A), lambda b: (b, 0, 0, 0, 0))
    return pl.pallas_call(
        functools.partial(_hgrn_kernel, seq_len=seq_len),
        grid=(bsz // nseq,),
        in_specs=[seq, seq, seq, seq, seq, full((1, W_A)), full((1, DK_A)), st_spec,
                  full(w.shape), full(m.shape), full(sel.shape)],
        out_specs=[seq, st_spec],
        out_shape=[jax.ShapeDtypeStruct((bsz, seq_len, W_A), F32),
                   jax.ShapeDtypeStruct((bsz, 2, H_A, DK_A, DK_A), F32)],
        scratch_shapes=[pltpu.VMEM((nseq, seq_len, W_A), F32)],
        compiler_params=_cparams("arbitrary"),
        name="hgrn",
    )(qa, ff, fb, ia, ga, lb.reshape(1, W_A), ng.reshape(1, DK_A), s0t,
      jnp.asarray(w, BF16), jnp.asarray(m, F32), jnp.asarray(sel, BF16))


def _dft_constants(seq_len):
    n = 2 * seq_len
    t = np.arange(seq_len, dtype=np.int64)
    wt = (np.arange(seq_len, dtype=np.int64)[:, None] * t[None, :]) % n
    ang = 2.0 * np.pi * wt.astype(np.float64) / n
    cos, sin = np.cos(ang), np.sin(ang)
    nyq = np.where(t % 2 == 0, 1.0, -1.0)
    sin_p = sin.copy()
    sin_p[0] = nyq
    fwd = np.concatenate([cos, sin_p], axis=0)
    icos = 2.0 * cos.T / n
    icos[:, 0] = 1.0 / n
    isin = 2.0 * sin.T / n
    isin[:, 0] = nyq / n
    inv = np.concatenate([icos, isin], axis=1)
    return fwd, inv


def _filter_embedding(seq_len):
    t = np.linspace(0.0, 1.0, seq_len)[:, None]
    w = 2.0 * np.pi * np.arange(seq_len) / seq_len
    f = np.linspace(1e-4, FILTER_BANDS - 1, FILTER_BANDS)
    ang = w[:, None] * f[None, :]
    z = np.concatenate([t, np.cos(ang), -np.sin(ang)], axis=-1)
    zp = np.zeros((seq_len, LANES), np.float64)
    zp[:, : z.shape[1]] = z
    deltas = np.abs(np.linspace(DECAY_MIN, DECAY_MAX, C_B))
    window = np.exp(-t * deltas[None, :])
    return zp.astype(np.float32), window.astype(np.float32)


def _hyena_filter_kernel(z_ref, win_ref, w1_ref, b1_ref, fr1_ref, w2_ref, b2_ref, fr2_ref, w3_ref,
                         hb_ref, fh_ref, fl_ref, gr_ref, gi_ref, *, seq_len):
    h = jnp.sin(fr1_ref[...] * (_dot3(z_ref[...], w1_ref[...]) + b1_ref[...]))
    h = jnp.sin(fr2_ref[...] * (_dot3(h, w2_ref[...]) + b2_ref[...]))
    h = _dot3(h, w3_ref[...])
    win = win_ref[...]
    hf = h[:, :C_B] * win
    hbk = h[:, C_B:] * win

    def dft(x):
        xh, xl = _split_bf16(x)
        return _dot(fh_ref[...], xh) + _dot(fh_ref[...], xl) + _dot(fl_ref[...], xh)

    p_sum = dft(hf + hbk)
    p_dif = dft(hf - hbk)
    row0 = lax.broadcasted_iota(jnp.int32, (seq_len, C_B), 0) == 0
    gr_ref[...] = p_sum[:seq_len] + hb_ref[...]
    gi_ref[...] = jnp.where(row0, p_sum[seq_len:] + hb_ref[...], p_dif[seq_len:])


def _hyena_filter(seq_len, w1, b1, fr1, w2, b2, fr2, w3, hbias):
    zemb, window = _filter_embedding(seq_len)
    fwd, _ = _dft_constants(seq_len)
    f_hi, f_lo = _split_bf16(jnp.asarray(fwd, F32))

    def pad(a, rows, cols):
        return jnp.zeros((rows, cols), F32).at[: a.shape[0], : a.shape[1]].set(a)

    args = (jnp.asarray(zemb), jnp.asarray(window), pad(w1, LANES, LANES), pad(b1[None], 1, LANES),
            pad(fr1[None], 1, LANES), pad(w2, LANES, LANES), pad(b2[None], 1, LANES),
            pad(fr2[None], 1, LANES), pad(w3, LANES, 2 * C_B), hbias.reshape(1, C_B), f_hi, f_lo)
    return pl.pallas_call(
        functools.partial(_hyena_filter_kernel, seq_len=seq_len),
        out_shape=[jax.ShapeDtypeStruct((seq_len, C_B), F32)] * 2,
        compiler_params=pltpu.CompilerParams(vmem_limit_bytes=VMEM_LIMIT_BYTES),
        name="hyena_filter",
    )(*args)


def _hyena_kernel(x0_ref, x1_ref, v_ref, cw_ref, cb_ref, gr_ref, gi_ref, f_ref, fi_ref, o_ref, *,
                  seq_len):
    tc = o_ref.shape[2]
    row = lax.broadcasted_iota(jnp.int32, (seq_len, tc), 0)

    def short_conv(u_ref, j, sq):
        u = u_ref[sq]
        prev = jnp.where(row == 0, 0.0, pltpu.roll(u, 1, 0))
        nxt = jnp.where(row == seq_len - 1, 0.0, pltpu.roll(u, seq_len - 1, 0))
        cw = cw_ref[j]
        return cw[0:1] * prev + cw[1:2] * u + cw[2:3] * nxt + cb_ref[j]

    gr, gi = gr_ref[...], gi_ref[...]
    row0 = row == 0
    for sq in range(o_ref.shape[0]):
        x0 = short_conv(x0_ref, 0, sq)
        z = short_conv(v_ref, 2, sq) * short_conv(x1_ref, 1, sq)
        p = _dot(f_ref[...], z.astype(BF16))
        a, b = p[:seq_len], p[seq_len:]
        bgi = b * gi
        yr = a * gr - jnp.where(row0, 0.0, bgi)
        yq = jnp.where(row0, bgi, a * gi + b * gr)
        y = _dot(fi_ref[...], jnp.concatenate([yr, yq], axis=0).astype(BF16))
        o_ref[sq] = y * x0


def _hyena(hy, conv_w, conv_b, gr, gi, seq_len, nseq, tc=256):
    bsz = hy.shape[0]
    nct = C_B // tc
    fwd, inv = _dft_constants(seq_len)
    cw = conv_w.reshape(3, 3, C_B).transpose(1, 0, 2)
    cb = conv_b.reshape(3, 1, C_B)
    part = lambda k: pl.BlockSpec((nseq, seq_len, tc), lambda b, j, k=k: (b, 0, k * nct + j))
    return pl.pallas_call(
        functools.partial(_hyena_kernel, seq_len=seq_len),
        grid=(bsz // nseq, nct),
        in_specs=[part(0), part(1), part(2),
                  pl.BlockSpec((3, 3, tc), lambda b, j: (0, 0, j)),
                  pl.BlockSpec((3, 1, tc), lambda b, j: (0, 0, j)),
                  pl.BlockSpec((seq_len, tc), lambda b, j: (0, j)),
                  pl.BlockSpec((seq_len, tc), lambda b, j: (0, j)),
                  pl.BlockSpec((2 * seq_len, seq_len), lambda b, j: (0, 0)),
                  pl.BlockSpec((seq_len, 2 * seq_len), lambda b, j: (0, 0))],
        out_specs=pl.BlockSpec((nseq, seq_len, tc), lambda b, j: (b, 0, j)),
        out_shape=jax.ShapeDtypeStruct((bsz, seq_len, C_B), F32),
        compiler_params=_cparams("arbitrary", "arbitrary"),
        name="hyena",
    )(hy, hy, hy, cw, cb, gr, gi, jnp.asarray(fwd, F32).astype(BF16),
      jnp.asarray(inv, F32).astype(BF16))


def _rope_tables(seq_len, dim):
    rows = seq_len // GRID_W
    row_idx = np.repeat(np.arange(rows), GRID_W).astype(np.float64)
    col_idx = np.tile(np.arange(GRID_W), rows).astype(np.float64)
    half = dim // 2
    inv = ROPE_THETA ** (-np.arange(0, half, 2, dtype=np.float64) / half)
    ang = np.concatenate([row_idx[:, None] * inv, col_idx[:, None] * inv], axis=-1)
    cos = np.repeat(np.cos(ang), 2, axis=1)
    sin = np.repeat(np.sin(ang), 2, axis=1)
    sin[:, 0::2] *= -1.0
    reps = LANES // dim
    return (np.tile(cos, (1, reps)).astype(np.float32), np.tile(sin, (1, reps)).astype(np.float32))


def _lane_group_matrices():
    i = np.arange(2 * LANES)
    same = lambda width: (i[:, None] // width == i[None, :] // width).astype(np.float32)
    swap = (i[:, None] == (i[None, :] ^ 1)).astype(np.float32)
    return same(HD_C), same(DK_D), swap


def _qk_proj_kernel(x_ref, m_ref, g_ref, w_ref, cqg_ref, ckg_ref, dqg_ref, dkg_ref, grp_c_ref,
                    grp_d_ref, swap_ref, *rest, use_rope, slabs):
    if use_rope:
        cc_ref, sc_ref, cd_ref, sd_ref = rest[:4]
        rest = rest[4:]
    qc_ref, qd_ref, kc_ref, vc_ref, kd_ref, vd_ref = rest
    m = m_ref[0]
    pair = 2 * LANES
    assert pair == KV_C * HD_C
    two = lambda r: jnp.concatenate([r, r], axis=1)
    rows_per_slab = x_ref.shape[1] // slabs

    for slab in range(slabs):
        rows = pl.ds(slab * rows_per_slab, rows_per_slab)
        hb = (_rms(x_ref[0, rows, :]) * g_ref[...] * (1.0 + m[1:2]) + m[0:1]).astype(BF16)
        cols = lambda start, n: _dot(hb, w_ref[:, start * LANES : (start + n) * LANES])

        def norm(x, grp_ref, width, g_ref, cos_ref, sin_ref):
            ms = _dot((x * x).astype(BF16), grp_ref[...]) * (1.0 / width)
            y = x * lax.rsqrt(ms + EPS) * two(g_ref[...])
            if not use_rope:
                return y
            swapped = _dot(y.astype(BF16), swap_ref[...])
            return y * two(cos_ref[rows, :]) + swapped * two(sin_ref[rows, :])

        norm_c = lambda x, g_ref: norm(x, grp_c_ref, HD_C, g_ref, cc_ref if use_rope else None,
                                       sc_ref if use_rope else None)
        norm_d = lambda x, g_ref: norm(x, grp_d_ref, DK_D, g_ref, cd_ref if use_rope else None,
                                       sd_ref if use_rope else None)
        for i in range(H_C // 2):
            qc_ref[0, rows, i * pair : (i + 1) * pair] = norm_c(cols(2 * i, 2), cqg_ref)
        kc = norm_c(cols(H_C, KV_C), ckg_ref)
        vc = cols(H_C + KV_C, KV_C)
        for j in range(KV_C):
            kc_ref[0, j, rows, :] = kc[:, j * LANES : (j + 1) * LANES]
            vc_ref[0, j, rows, :] = vc[:, j * LANES : (j + 1) * LANES]
        base = H_C + 2 * KV_C
        for i in range(H_D // 2):
            qd_ref[0, rows, i * pair : (i + 1) * pair] = norm_d(cols(base + 2 * i, 2), dqg_ref)
            kd = norm_d(cols(base + H_D + 2 * i, 2), dkg_ref)
            vd = cols(base + 2 * H_D + 2 * i, 2)
            for j in range(2):
                kd_ref[0, 2 * i + j, rows, :] = kd[:, j * LANES : (j + 1) * LANES]
                vd_ref[0, 2 * i + j, rows, :] = vd[:, j * LANES : (j + 1) * LANES]


def _qk_proj(x, mods, g, w_bf16, cqg, ckg, dqg, dkg, seq_len, use_rope, tm, slabs):
    ngrp = x.shape[0]
    spg = ROW_GROUP // seq_len
    bsz = ngrp * spg
    assert seq_len % tm == 0
    seq_of = lambda gi, i: gi * spg + (i * tm) // seq_len
    blk_of = lambda i: ((i * tm) % seq_len) // tm
    vec = lambda: pl.BlockSpec((1, LANES), lambda gi, i: (0, 0))
    mat = lambda: pl.BlockSpec((2 * LANES, 2 * LANES), lambda gi, i: (0, 0))
    tab = lambda: pl.BlockSpec((tm, LANES), lambda gi, i: (blk_of(i), 0))
    in_specs = [pl.BlockSpec((1, tm, D_MODEL), lambda gi, i: (gi, i, 0)),
                pl.BlockSpec((1, 6, D_MODEL), lambda gi, i: (gi, 0, 0)),
                pl.BlockSpec((1, D_MODEL), lambda gi, i: (0, 0)),
                pl.BlockSpec(w_bf16.shape, lambda gi, i: (0, 0)),
                vec(), vec(), vec(), vec(), mat(), mat(), mat()]
    args = [x, mods, g.reshape(1, D_MODEL), w_bf16, cqg.reshape(1, HD_C), ckg.reshape(1, HD_C),
            jnp.tile(dqg.reshape(1, DK_D), (1, 2)), jnp.tile(dkg.reshape(1, DK_D), (1, 2))]
    args += [jnp.asarray(mm, BF16) for mm in _lane_group_matrices()]
    if use_rope:
        in_specs += [tab(), tab(), tab(), tab()]
        args += [jnp.asarray(t) for t in _rope_tables(seq_len, HD_C) + _rope_tables(seq_len, DK_D)]
    tok = lambda w: pl.BlockSpec((1, tm, w), lambda gi, i: (seq_of(gi, i), blk_of(i), 0))
    head = lambda nh: pl.BlockSpec((1, nh, tm, LANES), lambda gi, i: (seq_of(gi, i), 0, blk_of(i), 0))
    tok_shape = lambda w: jax.ShapeDtypeStruct((bsz, seq_len, w), F32)
    head_shape = lambda nh: jax.ShapeDtypeStruct((bsz, nh, seq_len, LANES), F32)
    return pl.pallas_call(
        functools.partial(_qk_proj_kernel, use_rope=use_rope, slabs=slabs),
        grid=(ngrp, ROW_GROUP // tm),
        in_specs=in_specs,
        out_specs=[tok(H_C * HD_C), tok(H_D * 2 * DK_D), head(KV_C), head(KV_C), head(H_D), head(H_D)],
        out_shape=[tok_shape(H_C * HD_C), tok_shape(H_D * 2 * DK_D), head_shape(KV_C),
                   head_shape(KV_C), head_shape(H_D), head_shape(H_D)],
        compiler_params=_cparams("arbitrary", "arbitrary"),
        name="qk_proj",
    )(*args)


def _softmax_pv(q_list, kv_list):
    outs = []
    for q in q_list:
        scores = [_dot_nt(q, k) for k, _ in kv_list]
        mx = scores[0].max(axis=1, keepdims=True)
        for s in scores[1:]:
            mx = jnp.maximum(mx, s.max(axis=1, keepdims=True))
        den = 0.0
        acc = 0.0
        for s, (_, v) in zip(scores, kv_list):
            pexp = jnp.exp2(s - mx)
            den = den + pexp.sum(axis=1, keepdims=True)
            acc = acc + _dot(pexp.astype(BF16), v)
        outs.append(acc / den)
    return outs


def _head_kv(j, k_ref, v_ref, cache_refs):
    kv = [(r_k[0, 0, j].astype(BF16), r_v[0, 0, j].astype(BF16)) for r_k, r_v in cache_refs]
    kv.append((k_ref[0, j].astype(BF16), v_ref[0, j].astype(BF16)))
    return kv


def _per_slab(q_ref, o_ref, cols, fn):
    for r0 in range(0, q_ref.shape[1], ATT_SLAB):
        rows = pl.ds(r0, min(ATT_SLAB, q_ref.shape[1]))
        o_ref[0, rows, cols] = fn(q_ref[0, rows, cols])


def _gqa_kernel(q_ref, k_ref, v_ref, *rest, has_cache):
    cache_refs = [rest[:2]] if has_cache else []
    o_ref = rest[-1]
    g_c = H_C // KV_C
    width = g_c * HD_C
    for j in range(k_ref.shape[1]):
        kv = _head_kv(j, k_ref, v_ref, cache_refs)

        def head(q):
            q = q * (HD_C ** -0.5 * LOG2E)
            qs = [q[:, g * HD_C : (g + 1) * HD_C].astype(BF16) for g in range(g_c)]
            return jnp.concatenate(_softmax_pv(qs, kv), axis=1)

        _per_slab(q_ref, o_ref, slice(j * width, (j + 1) * width), head)


def _diff_kernel(q_ref, k_ref, v_ref, lam_ref, sg_ref, *rest, has_cache, out_scale):
    cache_refs = [rest[:2]] if has_cache else []
    o_ref = rest[-1]
    width = 2 * DK_D
    for j in range(k_ref.shape[1]):
        kv = _head_kv(j, k_ref, v_ref, cache_refs)

        def head(q):
            q = q * (DK_D ** -0.5 * LOG2E)
            low = lax.broadcasted_iota(jnp.int32, q.shape, 1) < DK_D
            qs = [jnp.where(low, q, 0.0).astype(BF16), jnp.where(low, 0.0, q).astype(BF16)]
            o1, o2 = _softmax_pv(qs, kv)
            return _rms(o1 - lam_ref[...] * o2) * sg_ref[...] * out_scale

        _per_slab(q_ref, o_ref, slice(j * width, (j + 1) * width), head)


def _attention(q, k, v, cache_k, cache_v, kernel, extra_args, n_heads, q_width, seq_len, hps, tq):
    bsz = q.shape[0]
    in_specs = [pl.BlockSpec((1, tq, hps * q_width), lambda b, h, i: (b, i, h)),
                pl.BlockSpec((1, hps, seq_len, LANES), lambda b, h, i: (b, h, 0, 0)),
                pl.BlockSpec((1, hps, seq_len, LANES), lambda b, h, i: (b, h, 0, 0))]
    in_specs += [pl.BlockSpec((1, LANES), lambda b, h, i: (0, 0)) for _ in extra_args]
    args = [q, k, v, *extra_args]
    if cache_k is not None:
        past = cache_k.shape[3]
        spec = lambda: pl.BlockSpec((1, 1, hps, past, LANES), lambda b, h, i: (b, 0, h, 0, 0))
        in_specs += [spec(), spec()]
        args += [cache_k, cache_v]
    return pl.pallas_call(
        functools.partial(kernel, has_cache=cache_k is not None),
        grid=(bsz, n_heads // hps, seq_len // tq),
        in_specs=in_specs,
        out_specs=pl.BlockSpec((1, tq, hps * q_width), lambda b, h, i: (b, i, h)),
        out_shape=jax.ShapeDtypeStruct((bsz, seq_len, n_heads * q_width), F32),
        compiler_params=_cparams("arbitrary", "arbitrary", "arbitrary"),
        name="attention",
    )(*args)


def _route(lg):
    lane = lax.broadcasted_iota(jnp.int32, lg.shape, 1).astype(F32)
    neg = -1e30
    is_g = (lane >= N_EXPERTS) & (lane < N_EXPERTS + N_GROUPS)
    gl = jnp.where(is_g, lg, neg)
    gmax = gl.max(axis=1, keepdims=True)
    g_p = 1.0 / jnp.where(is_g, jnp.exp(gl - gmax), 0.0).sum(axis=1, keepdims=True)
    g_i = jnp.where(gl == gmax, lane - N_EXPERTS, 1e9).min(axis=1, keepdims=True)
    in_grp = (lane < N_EXPERTS) & (jnp.floor(lane * (1.0 / EXP_PER_GROUP)) == g_i)
    el = jnp.where(in_grp, lg, neg)
    m1 = el.max(axis=1, keepdims=True)
    i1 = jnp.where(in_grp & (el == m1), lane, 1e9).min(axis=1, keepdims=True)
    el2 = jnp.where(lane == i1, neg, el)
    m2 = el2.max(axis=1, keepdims=True)
    i2 = jnp.where(in_grp & (el2 == m2) & (lane != i1), lane, 1e9).min(axis=1, keepdims=True)
    r = jnp.exp(m2 - m1)
    w1 = g_p / (1.0 + r)
    rec = jnp.where(lane == ROUTE_LANE, i1, 0.0)
    rec = jnp.where(lane == ROUTE_LANE + 1, i2, rec)
    rec = jnp.where(lane == ROUTE_LANE + 2, w1, rec)
    return jnp.where(lane == ROUTE_LANE + 3, w1 * r, rec)


def _mix_out_kernel(x_ref, a_ref, b_ref, m_ref, g2_ref, w_ref, wr_ref, br_ref, tri_ref,
                    x1_ref, xs_ref, pos_ref, cnt_ref, *, parts):
    wa = a_ref.shape[2]
    m = m_ref[0]
    rows_per_part = x_ref.shape[1] // parts
    h2_parts, route_parts = [], []
    for part in range(parts):
        rows = pl.ds(part * rows_per_part, rows_per_part)
        o = (_dot(a_ref[0, rows, :].astype(BF16), w_ref[:wa])
             + _dot(b_ref[0, rows, :].astype(BF16), w_ref[wa:]))
        x1 = x_ref[0, rows, :] + m[2:3] * o
        x1_ref[0, rows, :] = x1
        h2 = _rms(x1) * g2_ref[...] * (1.0 + m[4:5]) + m[3:4]
        h2_parts.append(h2.astype(BF16))
        r = _dot(jnp.concatenate(_split_bf16(h2), axis=0), wr_ref[...])
        hi_rows, lo_rows = r[:rows_per_part], r[rows_per_part:]
        lg = hi_rows[:, :LANES] + hi_rows[:, LANES:] + lo_rows[:, :LANES] + br_ref[...]
        route_parts.append(_route(lg))
    xs, pos, cnt = _moe_sort_tile(jnp.concatenate(h2_parts, axis=0),
                                  jnp.concatenate(route_parts, axis=0), tri_ref[...])
    xs_ref[...] = xs
    pos_ref[0] = pos
    cnt_ref[0] = cnt


def _mix_out(x, a, b, mods, g2, w_out_bf16, w_router, b_router, parts=2):
    ngrp = x.shape[0]
    per = ROW_GROUP // MOE_TILE
    ntile = ngrp * per
    wa, wb = a.shape[2], b.shape[2]
    wr = jnp.concatenate(_split_bf16(w_router), axis=1)
    tri = np.tril(np.ones((MOE_TILE, MOE_TILE), np.float32), -1)
    src = lambda j: jnp.minimum(j, ntile - 1)
    row = lambda w: pl.BlockSpec((1, MOE_TILE, w), lambda j: (src(j) // per, src(j) % per, 0))
    full = lambda shape: pl.BlockSpec(shape, lambda j: (0,) * len(shape))
    return pl.pallas_call(
        functools.partial(_mix_out_kernel, parts=parts),
        grid=(ntile + 1,),
        in_specs=[row(D_MODEL), row(wa), row(wb),
                  pl.BlockSpec((1, 6, D_MODEL), lambda j: (src(j) // per, 0, 0)),
                  full((1, D_MODEL)), full((wa + wb, D_MODEL)),
                  full((D_MODEL, 2 * LANES)), full((1, LANES)), full((MOE_TILE, MOE_TILE))],
        out_specs=[pl.BlockSpec((1, MOE_TILE, D_MODEL), lambda j: (j, 0, 0)),
                   pl.BlockSpec((MOE_TILE_ROWS, MOE_ROW_W), lambda j: (j, 0)),
                   pl.BlockSpec((1, MOE_TILE, LANES), lambda j: (j, 0, 0)),
                   pl.BlockSpec((1, 8, LANES), lambda j: (j, 0, 0))],
        out_shape=[jax.ShapeDtypeStruct((ntile + 1, MOE_TILE, D_MODEL), F32),
                   jax.ShapeDtypeStruct(((ntile + 1) * MOE_TILE_ROWS, MOE_ROW_W), BF16),
                   jax.ShapeDtypeStruct((ntile + 1, MOE_TILE, LANES), F32),
                   jax.ShapeDtypeStruct((ntile + 1, 8, LANES), F32)],
        compiler_params=_cparams("arbitrary"),
        name="mix_out",
    )(x, a, b, mods, g2.reshape(1, D_MODEL), w_out_bf16, wr, b_router, jnp.asarray(tri, BF16))


def _lane_col(x, lane, k):
    return jnp.where(lane == k, x, 0.0).sum(axis=1, keepdims=True)


def _moe_sort_tile(h, r, tri):
    lane = lax.broadcasted_iota(jnp.int32, r.shape, 1).astype(F32)
    i1, i2, w1, w2 = [_lane_col(r, lane, ROUTE_LANE + k) for k in range(4)]
    oh1 = lane == i1
    oh2 = lane == i2
    oh = jnp.where(oh1 | oh2, 1.0, 0.0)
    rank = _dot(tri, oh.astype(BF16))
    cnt = oh.sum(axis=0, keepdims=True)
    chunks = jnp.floor((cnt + (MOE_CHUNK - 1)) * (1.0 / MOE_CHUNK))
    li = lax.broadcasted_iota(jnp.int32, (LANES, LANES), 0)
    lj = lax.broadcasted_iota(jnp.int32, (LANES, LANES), 1)
    before = jnp.where(li < lj, 1.0, 0.0).astype(BF16)
    seg = _dot(jnp.broadcast_to(chunks, (8, LANES)).astype(BF16), before)[0:1]
    base = seg * MOE_CHUNK + rank
    pos1 = jnp.where(oh1, base, 0.0).sum(axis=1, keepdims=True)
    pos2 = jnp.where(oh2, base, 0.0).sum(axis=1, keepdims=True)
    riota = lax.broadcasted_iota(jnp.int32, (r.shape[0], MOE_TILE_ROWS), 1).astype(F32)
    p = jnp.where((riota == pos1) | (riota == pos2), 1.0, 0.0).astype(BF16)

    aux = jnp.where(lane == 2 * MOE_W_PIECES, i1, 0.0)
    for k, w in enumerate((w1, w2)):
        rest = w
        for piece in range(MOE_W_PIECES):
            part = rest.astype(BF16).astype(F32)
            aux = jnp.where(lane == k * MOE_W_PIECES + piece, part, aux)
            rest = rest - part
    row = jnp.concatenate([h, aux.astype(BF16)], axis=1)
    xs = _dot_tn(p, row).astype(BF16)
    pos = jnp.where(lane == 0, pos1, jnp.where(lane == 1, pos2, 0.0))
    return xs, pos, jnp.broadcast_to(cnt, (8, LANES))


def _moe_tables(cnt, ntile):
    nblk = ntile * MOE_TILE_CHUNKS // MOE_BLOCK_CHUNKS + N_EXPERTS
    chunks = (cnt + MOE_CHUNK - 1) // MOE_CHUNK
    seg_start = jnp.cumsum(chunks, axis=1) - chunks
    tile_prefix = jnp.cumsum(chunks, axis=0) - chunks
    per_expert = chunks.sum(axis=0)
    blocks = (per_expert + MOE_BLOCK_CHUNKS - 1) // MOE_BLOCK_CHUNKS
    blk_end = jnp.cumsum(blocks)
    n_used = blk_end[-1]
    b = jnp.arange(nblk + 1, dtype=jnp.int32)
    blk_e = jnp.sum(b[:, None] >= blk_end[None, :], axis=1).astype(jnp.int32)
    last_e = jnp.sum((n_used - 1) >= blk_end).astype(jnp.int32)
    blk_e = jnp.clip(jnp.where(b < n_used, blk_e, last_e), 0, N_EXPERTS - 1)
    oh_e = (blk_e[:, None] == jnp.arange(N_EXPERTS)[None, :]).astype(jnp.int32)
    pick = lambda per_tile: jnp.sum(oh_e[:, :, None] * per_tile.T[None], axis=1)
    seg_e, pre_e, chunks_e = pick(seg_start), pick(tile_prefix), pick(chunks)
    first_blk = jnp.sum(oh_e * (blk_end - blocks)[None, :], axis=1)
    k = (b - first_blk)[:, None] * MOE_BLOCK_CHUNKS + jnp.arange(MOE_BLOCK_CHUNKS)[None, :]
    k3 = k[:, :, None]
    in_tile = (pre_e[:, None, :] <= k3) & (k3 < (pre_e + chunks_e)[:, None, :])
    tile_base = (jnp.arange(ntile) * MOE_TILE_CHUNKS)[None, :] + seg_e - pre_e
    src = jnp.sum(jnp.where(in_tile, tile_base[:, None, :] + k3, 0), axis=-1)
    valid = jnp.any(in_tile, axis=-1) & (b < n_used)[:, None]
    slot_c = jnp.arange(MOE_BLOCK_CHUNKS)[None, :]
    n_read = MOE_TILE_CHUNKS - 2 * MOE_BLOCK_CHUNKS
    assert n_read > 0
    spare = ntile * MOE_TILE_CHUNKS
    gather = jnp.where(valid, src, spare + 2 * MOE_BLOCK_CHUNKS + slot_c % n_read).astype(jnp.int32)
    scatter = jnp.where(valid, src, spare + (b % 2)[:, None] * MOE_BLOCK_CHUNKS + slot_c).astype(jnp.int32)
    blk_start = jnp.concatenate([blk_end - blocks, n_used[None]]).astype(jnp.int32)
    return blk_start, gather.reshape(-1), scatter.reshape(-1)


def _moe_expert_kernel(start_ref, gather_ref, scatter_ref, xs_hbm, wg_ref, wu_ref, wd_ref, ys_hbm,
                       lhs, obuf, wgb, wub, wdb, in_sem, out_sem):
    e = pl.program_id(0)
    ne = pl.num_programs(0)
    b0 = start_ref[e]
    b1 = start_ref[e + 1]
    n = start_ref[ne]

    def chunk_copy(blk, slot, c, gather):
        rows = pl.ds(c * MOE_CHUNK, MOE_CHUNK)
        if gather:
            idx = gather_ref[blk * MOE_BLOCK_CHUNKS + c]
            return pltpu.make_async_copy(xs_hbm.at[idx], lhs.at[slot, rows], in_sem.at[slot])
        idx = scatter_ref[blk * MOE_BLOCK_CHUNKS + c]
        dst = ys_hbm.at[idx, pl.ds(0, MOE_CHUNK), pl.ds(0, D_MODEL)]
        return pltpu.make_async_copy(obuf.at[slot, rows], dst, out_sem.at[slot])

    def for_chunks(blk, slot, gather, start):
        for c in range(MOE_BLOCK_CHUNKS):
            cp = chunk_copy(blk, slot, c, gather)
            if start:
                cp.start()
            else:
                cp.wait()

    @pl.when((e == 0) & (n > 0))
    def _():
        for_chunks(0, 0, True, True)

    @pl.when(b1 > b0)
    def _():
        wgb[...] = wg_ref[0, 0].astype(BF16)
        wub[...] = wu_ref[0, 0].astype(BF16)
        wdb[...] = wd_ref[0, 0].astype(BF16)
        e_f32 = e.astype(F32)

        def block(b, carry):
            slot = b % 2

            @pl.when(b >= 2)
            def _():
                for_chunks(b - 2, slot, False, False)

            for_chunks(b, slot, True, False)
            for_chunks(b + 1, 1 - slot, True, True)
            xa = lhs[slot]
            x = xa[:, :D_MODEL]
            aux = xa[:, D_MODEL:].astype(F32)
            lane = lax.broadcasted_iota(jnp.int32, aux.shape, 1)
            first = lane < MOE_W_PIECES
            w_first = jnp.where(first, aux, 0.0).sum(axis=1, keepdims=True)
            w_second = jnp.where(first | (lane >= 2 * MOE_W_PIECES), 0.0, aux).sum(axis=1, keepdims=True)
            e_first = _lane_col(aux, lane, 2 * MOE_W_PIECES)
            w = jnp.where(e_first == e_f32, w_first, w_second)
            hid = _silu(_dot(x, wgb[...])) * _dot(x, wub[...]) * w
            obuf[slot] = _dot(hid.astype(BF16), wdb[...]).astype(BF16)
            for_chunks(b, slot, False, True)
            return carry

        lax.fori_loop(b0, b1, block, 0)

    @pl.when(e == ne - 1)
    def _():
        @pl.when(n >= 2)
        def _():
            for_chunks(n - 2, n % 2, False, False)

        @pl.when(n >= 1)
        def _():
            for_chunks(n, n % 2, True, False)
            for_chunks(n - 1, (n - 1) % 2, False, False)


def _moe_experts(xs, blk_start, gather, scatter, layer, w_gate, w_up, w_down):
    nchunk = xs.shape[0] // MOE_CHUNK
    rows_per_blk = MOE_BLOCK_CHUNKS * MOE_CHUNK
    wspec = lambda shape: pl.BlockSpec((1, 1) + shape, lambda e, st, g, s: (layer, e, 0, 0))
    hbm = pl.BlockSpec(memory_space=pl.ANY)
    ys = pl.pallas_call(
        _moe_expert_kernel,
        grid_spec=pltpu.PrefetchScalarGridSpec(
            num_scalar_prefetch=3,
            grid=(N_EXPERTS,),
            in_specs=[hbm, wspec((D_MODEL, D_EXPERT)), wspec((D_MODEL, D_EXPERT)),
                      wspec((D_EXPERT, D_MODEL))],
            out_specs=hbm,
            scratch_shapes=[pltpu.VMEM((2, rows_per_blk, MOE_ROW_W), BF16),
                            pltpu.VMEM((2, rows_per_blk, D_MODEL), BF16),
                            pltpu.VMEM((D_MODEL, D_EXPERT), BF16),
                            pltpu.VMEM((D_MODEL, D_EXPERT), BF16),
                            pltpu.VMEM((D_EXPERT, D_MODEL), BF16),
                            pltpu.SemaphoreType.DMA((2,)),
                            pltpu.SemaphoreType.DMA((2,))]),
        out_shape=jax.ShapeDtypeStruct((nchunk, MOE_CHUNK, MOE_ROW_W), BF16),
        input_output_aliases={3: 0},
        compiler_params=_cparams("arbitrary"),
        name="moe_experts",
    )(blk_start, gather, scatter, xs.reshape(nchunk, MOE_CHUNK, MOE_ROW_W), w_gate, w_up, w_down)
    return ys.reshape(nchunk * MOE_CHUNK, MOE_ROW_W)


def _moe_combine_kernel(ys_ref, pos_ref, x_ref, m_ref, o_ref):
    pos = pos_ref[0]
    lane = lax.broadcasted_iota(jnp.int32, pos.shape, 1)
    pos1 = _lane_col(pos, lane, 0)
    pos2 = _lane_col(pos, lane, 1)
    riota = lax.broadcasted_iota(jnp.int32, (pos.shape[0], MOE_TILE_ROWS), 1).astype(F32)
    p = jnp.where((riota == pos1) | (riota == pos2), 1.0, 0.0).astype(BF16)
    o_ref[0] = x_ref[0] + m_ref[0][5:6] * _dot(p, ys_ref[...])


def _moe_combine(ys, pos, x1, mods):
    ngrp = mods.shape[0]
    per = ROW_GROUP // MOE_TILE
    tile = pl.BlockSpec((1, MOE_TILE, D_MODEL), lambda j: (j, 0, 0))
    return pl.pallas_call(
        _moe_combine_kernel,
        grid=(ngrp * per,),
        in_specs=[pl.BlockSpec((MOE_TILE_ROWS, D_MODEL), lambda j: (j, 0)),
                  pl.BlockSpec((1, MOE_TILE, LANES), lambda j: (j, 0, 0)),
                  tile, pl.BlockSpec((1, 6, D_MODEL), lambda j: (j // per, 0, 0))],
        out_specs=pl.BlockSpec((1, MOE_TILE, D_MODEL), lambda j: (j // per, j % per, 0)),
        out_shape=jax.ShapeDtypeStruct((ngrp, ROW_GROUP, D_MODEL), F32),
        compiler_params=_cparams("arbitrary"),
        name="moe_combine",
    )(ys, pos, x1, mods)


def _moe(xs, pos, cnt, x1, mods, layer, w_gate, w_up, w_down):
    ntile = mods.shape[0] * (ROW_GROUP // MOE_TILE)
    cnt = cnt[:ntile, 0, :N_EXPERTS].astype(jnp.int32)
    blk_start, gather, scatter = _moe_tables(cnt, ntile)
    ys = _moe_experts(xs, blk_start, gather, scatter, layer, w_gate, w_up, w_down)
    return _moe_combine(ys, pos, x1, mods)


def kernel(x_prompt, x_sample, state_hgrn, cache_c_k, cache_c_v, cache_d_k, cache_d_v, c, c_ctx, norm1_g, norm2_g, w_mod, b_mod, even_w_in, even_w_out, hgrn_lower, hgrn_norm_g, hy_conv_w, hy_conv_b, hy_w1, hy_b1, hy_freq1, hy_w2, hy_b2, hy_freq2, hy_w3, hy_bias, odd_w_in, odd_w_out, c_qnorm_g, c_knorm_g, d_qnorm_g, d_knorm_g, d_lambda_q1, d_lambda_k1, d_lambda_q2, d_lambda_k2, d_subln_g, moe_w_grp, moe_b_grp, moe_w_rt, moe_b_rt, moe_w_gate, moe_w_up, moe_w_down):
    depth = w_mod.shape[0]
    n_ctx, seq, _ = x_prompt.shape
    n_lat, dec_seq, _ = x_sample.shape
    g_ctx = n_ctx * seq // ROW_GROUP
    g_lat = n_lat * dec_seq // ROW_GROUP
    assert dec_seq == ROW_GROUP and ROW_GROUP % seq == 0

    cond = jnp.zeros((16, D_MODEL), F32).at[0].set(c_ctx).at[1 : 1 + n_lat].set(c)
    mods = _adaln(cond, w_mod, b_mod).reshape(depth, 16, 6, D_MODEL)
    lower = jnp.cumsum(jax.nn.softmax(hgrn_lower.astype(F32), axis=0), axis=0)

    streams = [
        dict(x=x_prompt.reshape(g_ctx, ROW_GROUP, D_MODEL), ngrp=g_ctx, bsz=n_ctx, seq=seq, ctx=True),
        dict(x=x_sample, ngrp=g_lat, bsz=n_lat, seq=dec_seq, ctx=False),
    ]
    new_state, new_ck, new_cv, new_dk, new_dv = [], [], [], [], []

    for l in range(depth):
        j = l // 2
        w_router = jnp.zeros((D_MODEL, LANES), F32)
        w_router = w_router.at[:, :N_EXPERTS].set(moe_w_rt[l])
        w_router = w_router.at[:, N_EXPERTS : N_EXPERTS + N_GROUPS].set(moe_w_grp[l])
        b_router = jnp.zeros((1, LANES), F32)
        b_router = b_router.at[0, :N_EXPERTS].set(moe_b_rt[l])
        b_router = b_router.at[0, N_EXPERTS : N_EXPERTS + N_GROUPS].set(moe_b_grp[l])
        if l % 2 == 0:
            w_in = even_w_in[j].astype(BF16)
            w_out = even_w_out[j].astype(BF16)
        else:
            w_in = odd_w_in[j].astype(BF16)
            w_out = odd_w_out[j].astype(BF16)
            lam_init = 0.8 - 0.6 * math.exp(-0.3 * l)
            lam = (jnp.exp(jnp.sum(d_lambda_q1[j] * d_lambda_k1[j]))
                   - jnp.exp(jnp.sum(d_lambda_q2[j] * d_lambda_k2[j])) + lam_init)
            lam_row = jnp.full((1, LANES), lam, F32)

        for s in streams:
            ngrp, bsz, sl = s["ngrp"], s["bsz"], s["seq"]
            if s["ctx"]:
                m = jnp.broadcast_to(mods[l, 0][None], (ngrp, 6, D_MODEL))
            else:
                m = mods[l, 1 : 1 + ngrp]
            x = s["x"]
            if l % 2 == 0:
                wa = H_A * DK_A
                splits = [(0, wa), (wa, 2 * wa), (2 * wa, 3 * wa), (3 * wa, 3 * wa + W_A),
                          (3 * wa + W_A, 3 * wa + 2 * W_A), (3 * wa + 2 * W_A, w_in.shape[1])]
                qa, ffa, fba, ia, ga, hy = _norm_proj(x, m, norm1_g[l], w_in, splits)
                per_seq = lambda t: t.reshape(bsz, sl, t.shape[-1])
                if s["ctx"]:
                    s0t = jnp.zeros((bsz, 2, H_A, DK_A, DK_A), F32)
                else:
                    s0t = jnp.swapaxes(state_hgrn[:, j].astype(F32), -1, -2)
                mix_a, s_fin = _hgrn(per_seq(qa), per_seq(ffa), per_seq(fba), per_seq(ia),
                                     per_seq(ga), lower[j], hgrn_norm_g[j], s0t, sl,
                                     nseq=2 if s["ctx"] else 1)
                gr, gi = _hyena_filter(sl, hy_w1[j], hy_b1[j], hy_freq1[j], hy_w2[j], hy_b2[j],
                                       hy_freq2[j], hy_w3[j], hy_bias[j])
                mix_b = _hyena(per_seq(hy), hy_conv_w[j], hy_conv_b[j], gr, gi, sl,
                               nseq=4 if s["ctx"] else 2)
                if s["ctx"]:
                    new_state.append(jnp.swapaxes(s_fin, -1, -2))
            else:
                qc, qd, kc, vc, kd, vd = _qk_proj(x, m, norm1_g[l], w_in, c_qnorm_g[j], c_knorm_g[j],
                                                  d_qnorm_g[j], d_knorm_g[j], sl, use_rope=not s["ctx"],
                                                  tm=min(sl, 512), slabs=1 if s["ctx"] else 2)
                if s["ctx"]:
                    caches = (None, None, None, None)
                    new_ck.append(kc)
                    new_cv.append(vc)
                    new_dk.append(kd)
                    new_dv.append(vd)
                else:
                    caches = (cache_c_k[:, j : j + 1], cache_c_v[:, j : j + 1],
                              cache_d_k[:, j : j + 1], cache_d_v[:, j : j + 1])
                hps_c, hps_d, tq = (KV_C, H_D, sl) if s["ctx"] else (1, 1, sl)
                mix_a = _attention(qc, kc, vc, caches[0], caches[1], _gqa_kernel, (), KV_C,
                                   (H_C // KV_C) * HD_C, sl, hps_c, tq)
                diff = functools.partial(_diff_kernel, out_scale=1.0 - lam_init)
                mix_b = _attention(qd, kd, vd, caches[2], caches[3], diff,
                                   (lam_row, d_subln_g[j].reshape(1, DV_D)), H_D, 2 * DK_D, sl,
                                   hps_d, tq)
            grp = lambda t: t.reshape(ngrp, ROW_GROUP, t.shape[-1])
            x1, xs, pos, cnt = _mix_out(x, grp(mix_a), grp(mix_b), m, norm2_g[l], w_out, w_router,
                                        b_router)
            s["x"] = _moe(xs, pos, cnt, x1, m, l, moe_w_gate, moe_w_up, moe_w_down)

    y_ctx = streams[0]["x"].reshape(n_ctx, seq, D_MODEL)
    y_lat = streams[1]["x"]
    return (y_ctx, y_lat, jnp.stack(new_state, axis=1), jnp.stack(new_ck, axis=1),
            jnp.stack(new_cv, axis=1), jnp.stack(new_dk, axis=1), jnp.stack(new_dv, axis=1))
```

```python
import functools
import math

import numpy as np
import jax
import jax.numpy as jnp
from jax import lax
from jax.experimental import pallas as pl
from jax.experimental.pallas import tpu as pltpu

F32 = jnp.float32
BF16 = jnp.bfloat16

D_MODEL = 1024
EPS = 1e-6
LOG2E = 1.0 / math.log(2.0)
GRID_W = 64
ROPE_THETA = 10000.0
H_A = 4
DK_A = 128
W_A = 512
CHUNK = 128
SUB = 8
LEVELS = (16, 32, 64, 128)
C_B = 512
FILTER_BANDS = 16
DECAY_MIN = math.log(1e-2) / 1.5
DECAY_MAX = math.log(1e-2) / 0.3
H_C = 4
KV_C = 2
HD_C = 128
H_D = 4
DK_D = 64
DV_D = 128
N_GROUPS = 4
EXP_PER_GROUP = 4
N_EXPERTS = 16
D_EXPERT = 512

ROUTE_LANE = N_EXPERTS + N_GROUPS
MOE_TILE = 512
MOE_CHUNK = 16
MOE_TILE_CHUNKS = 2 * MOE_TILE // MOE_CHUNK + N_EXPERTS
MOE_TILE_ROWS = MOE_TILE_CHUNKS * MOE_CHUNK
MOE_BLOCK_CHUNKS = 16
MOE_ROW_W = D_MODEL + 128
MOE_W_PIECES = 3
ATT_SLAB = 256

LANES = 128
ROW_GROUP = 1024
VMEM_LIMIT_BYTES = 56 * 1024 * 1024


def _cparams(*sem):
    return pltpu.CompilerParams(dimension_semantics=sem, vmem_limit_bytes=VMEM_LIMIT_BYTES)


def _split_bf16(x):
    hi = x.astype(BF16)
    lo = (x - hi.astype(F32)).astype(BF16)
    return hi, lo


def _dot(a, b):
    return jnp.dot(a, b, preferred_element_type=F32)


def _dot3(a, b):
    ah, al = _split_bf16(a)
    bh, bl = _split_bf16(b)
    return _dot(ah, bh) + _dot(al, bh) + _dot(ah, bl)


def _dot_nt(a, b):
    return lax.dot_general(a, b, (((1,), (1,)), ((), ())), preferred_element_type=F32)


def _dot_tn(a, b):
    return lax.dot_general(a, b, (((0,), (0,)), ((), ())), preferred_element_type=F32)


def _silu(x):
    return x * jax.nn.sigmoid(x)


def _rms(x, eps=EPS):
    return x * lax.rsqrt(jnp.mean(x * x, axis=-1, keepdims=True) + eps)


def _adaln_kernel(c_ref, w_ref, b_ref, o_ref):
    s = _silu(c_ref[...])
    o_ref[0] = _dot(s.astype(BF16), w_ref[0].astype(BF16)) + b_ref[0]


def _adaln(cond, w_mod, b_mod):
    depth, _, n = w_mod.shape
    rows = cond.shape[0]
    tn = 1536
    return pl.pallas_call(
        _adaln_kernel,
        grid=(depth, n // tn),
        in_specs=[
            pl.BlockSpec((rows, D_MODEL), lambda l, j: (0, 0)),
            pl.BlockSpec((1, D_MODEL, tn), lambda l, j: (l, 0, j)),
            pl.BlockSpec((1, 1, tn), lambda l, j: (l, 0, j)),
        ],
        out_specs=pl.BlockSpec((1, rows, tn), lambda l, j: (l, 0, j)),
        out_shape=jax.ShapeDtypeStruct((depth, rows, n), F32),
        compiler_params=_cparams("arbitrary", "arbitrary"),
        name="adaln",
    )(cond, w_mod, b_mod.reshape(depth, 1, n))


def _norm_proj_kernel(x_ref, m_ref, g_ref, w_ref, *o_refs, splits):
    m = m_ref[0]
    h = _rms(x_ref[0]) * g_ref[...] * (1.0 + m[1:2]) + m[0:1]
    hb = h.astype(BF16)
    for o_ref, (a, b) in zip(o_refs, splits):
        o_ref[0] = _dot(hb, w_ref[:, a:b])


def _norm_proj(x, mods, g, w_bf16, splits, tm=512):
    ngrp = x.shape[0]
    n = w_bf16.shape[1]
    kern = functools.partial(_norm_proj_kernel, splits=splits)
    return pl.pallas_call(
        kern,
        grid=(ngrp, ROW_GROUP // tm),
        in_specs=[
            pl.BlockSpec((1, tm, D_MODEL), lambda b, i: (b, i, 0)),
            pl.BlockSpec((1, 6, D_MODEL), lambda b, i: (b, 0, 0)),
            pl.BlockSpec((1, D_MODEL), lambda b, i: (0, 0)),
            pl.BlockSpec((D_MODEL, n), lambda b, i: (0, 0)),
        ],
        out_specs=[pl.BlockSpec((1, tm, hi - lo), lambda b, i: (b, i, 0)) for lo, hi in splits],
        out_shape=[jax.ShapeDtypeStruct((ngrp, ROW_GROUP, hi - lo), F32) for lo, hi in splits],
        compiler_params=_cparams("arbitrary", "arbitrary"),
        name="norm_proj",
    )(x, mods, g.reshape(1, D_MODEL), w_bf16)


def _hgrn_constants():
    c = CHUNK
    nblk = 1 + len(LEVELS)
    w = np.zeros((2, nblk * c, c), np.float32)
    for t in range(c):
        w[0, t, : t + 1] = 1.0
    for li, r in enumerate(LEVELS):
        for t in range(c):
            mid = (t // r) * r + r // 2
            if t >= mid:
                w[0, (li + 1) * c + t, mid : t + 1] = 1.0
            else:
                w[0, (li + 1) * c + t, t + 1 : mid] = 1.0
    for blk in range(nblk):
        w[1, blk * c : (blk + 1) * c] = w[0, blk * c : (blk + 1) * c][::-1, ::-1]
    m = np.zeros((2, nblk, c, c), np.float32)
    for li, r in enumerate(LEVELS):
        for t in range(c):
            for s in range(c):
                if t // r == s // r and (t % r) >= r // 2 and (s % r) < r // 2:
                    m[0, li, t, s] = 1.0
    for t in range(c):
        for s in range(c):
            if t // SUB == s // SUB and s <= t:
                m[0, nblk - 1, t, s] = 1.0
    m[1] = m[0][:, ::-1, ::-1]
    sel = np.zeros((SUB * DK_A, c), np.float32)
    for i in range(SUB):
        sel[i * DK_A : (i + 1) * DK_A, i::SUB] = 1.0
    return w, m, sel


def _hgrn_chunk(direction, sq, r0, qa_ref, gate_ref, ia_ref, lb, wcum, masks, sel, states):
    rows = pl.ds(r0, CHUNK)
    q = _silu(qa_ref[sq, rows, :])
    v = ia_ref[sq, rows, :]
    f = lb + (1.0 - lb) * jax.nn.sigmoid(gate_ref[sq, rows, :])
    kk = jnp.maximum(1.0 - f, 0.0)
    lf = jnp.log(f) * LOG2E
    lk = jnp.log(kk) * LOG2E
    lf_hi, lf_lo = _split_bf16(lf)
    z = _dot(wcum, lf_hi) + _dot(wcum, lf_lo)
    e = z[0:CHUNK]
    edge = e[CHUNK - 1 : CHUNK] if direction == 0 else e[0:1]
    q_in = (q * jnp.exp2(e)).astype(BF16)
    k_st = (kk * jnp.exp2(edge - e)).astype(BF16)
    st_decay = jnp.exp2(edge)
    q_lv, k_lv = [], []
    for li in range(len(LEVELS)):
        ez = jnp.exp2(z[(li + 1) * CHUNK : (li + 2) * CHUNK])
        q_lv.append((q * ez).astype(BF16))
        k_lv.append((kk * ez).astype(BF16))
    nsub = CHUNK // SUB
    c3 = (e - lk).reshape(nsub, SUB, W_A)
    pair = []
    for i in range(SUB):
        cb = jnp.broadcast_to(c3[:, i : i + 1, :], (nsub, SUB, W_A)).reshape(CHUNK, W_A)
        pair.append((q * jnp.exp2(jnp.minimum(e - cb, 0.0))).astype(BF16))
    vb = v.astype(BF16)
    outs, new_states = [], []
    for h in range(H_A):
        hs = slice(h * DK_A, (h + 1) * DK_A)
        sc = _dot(jnp.concatenate([p[:, hs] for p in pair], axis=1), sel) * masks[len(LEVELS)]
        for li in range(len(LEVELS)):
            sc = sc + _dot_nt(q_lv[li][:, hs], k_lv[li][:, hs]) * masks[li]
        st = states[h]
        outs.append(_dot(sc.astype(BF16), vb[:, hs]) + _dot_nt(q_in[:, hs], st.astype(BF16)))
        new_states.append(st_decay[:, hs] * st + _dot_tn(vb[:, hs], k_st[:, hs]))
    return jnp.concatenate(outs, axis=1), tuple(new_states)


def _hgrn_kernel(qa_ref, ff_ref, fb_ref, ia_ref, ga_ref, lb_ref, ng_ref, s0_ref, w_ref, m_ref,
                 sel_ref, o_ref, sout_ref, ob_ref, *, seq_len):
    nchunk = seq_len // CHUNK
    nseq = o_ref.shape[0]
    lb = lb_ref[...]
    sel = sel_ref[...]
    nmask = 1 + len(LEVELS)

    def scan(i, states):
        rf = pl.multiple_of(i * CHUNK, CHUNK)
        rb = pl.multiple_of((nchunk - 1 - i) * CHUNK, CHUNK)
        new_states = []
        for sq in range(nseq):
            o_f, st_f = _hgrn_chunk(0, sq, rf, qa_ref, ff_ref, ia_ref, lb, w_ref[0],
                                    [m_ref[0, j] for j in range(nmask)], sel, states[sq][0])
            o_b, st_b = _hgrn_chunk(1, sq, rb, qa_ref, fb_ref, ia_ref, lb, w_ref[1],
                                    [m_ref[1, j] for j in range(nmask)], sel, states[sq][1])
            o_ref[sq, pl.ds(rf, CHUNK), :] = o_f
            ob_ref[sq, pl.ds(rb, CHUNK), :] = o_b
            new_states.append((st_f, st_b))
        return tuple(new_states)

    init = tuple(tuple(tuple(s0_ref[sq, d, h] for h in range(H_A)) for d in range(2))
                 for sq in range(nseq))
    final = lax.fori_loop(0, nchunk, scan, init)

    ng = jnp.concatenate([ng_ref[...]] * H_A, axis=1)

    def gate(i, carry):
        rows = pl.ds(pl.multiple_of(i * CHUNK, CHUNK), CHUNK)
        for sq in range(nseq):
            tot = o_ref[sq, rows, :] + ob_ref[sq, rows, :]
            normed = jnp.concatenate(
                [_rms(tot[:, h * DK_A : (h + 1) * DK_A]) for h in range(H_A)], axis=1)
            o_ref[sq, rows, :] = normed * ng * _silu(ga_ref[sq, rows, :])
        return carry

    lax.fori_loop(0, nchunk, gate, 0)
    for sq in range(nseq):
        for d in range(2):
            for h in range(H_A):
                sout_ref[sq, d, h] = final[sq][d][h]


def _hgrn(qa, ff, fb, ia, ga, lb, ng, s0t, seq_len, nseq):
    bsz = qa.shape[0]
    w, m, sel = _hgrn_constants()
    seq = pl.BlockSpec((nseq, seq_len, W_A), lambda b: (b, 0, 0))
    full = lambda shape: pl.BlockSpec(shape, lambda b: (0,) * len(shape))
    st_spec = pl.BlockSpec((nseq, 2, H_A, DK_A, DK_A), lambda b: (b, 0, 0, 0, 0))
    return pl.pallas_call(
        functools.partial(_hgrn_kernel, seq_len=seq_len),
        grid=(bsz // nseq,),
        in_specs=[seq, seq, seq, seq, seq, full((1, W_A)), full((1, DK_A)), st_spec,
                  full(w.shape), full(m.shape), full(sel.shape)],
        out_specs=[seq, st_spec],
        out_shape=[jax.ShapeDtypeStruct((bsz, seq_len, W_A), F32),
                   jax.ShapeDtypeStruct((bsz, 2, H_A, DK_A, DK_A), F32)],
        scratch_shapes=[pltpu.VMEM((nseq, seq_len, W_A), F32)],
        compiler_params=_cparams("arbitrary"),
        name="hgrn",
    )(qa, ff, fb, ia, ga, lb.reshape(1, W_A), ng.reshape(1, DK_A), s0t,
      jnp.asarray(w, BF16), jnp.asarray(m, F32), jnp.asarray(sel, BF16))


def _dft_constants(seq_len):
    n = 2 * seq_len
    t = np.arange(seq_len, dtype=np.int64)
    wt = (np.arange(seq_len, dtype=np.int64)[:, None] * t[None, :]) % n
    ang = 2.0 * np.pi * wt.astype(np.float64) / n
    cos, sin = np.cos(ang), np.sin(ang)
    nyq = np.where(t % 2 == 0, 1.0, -1.0)
    sin_p = sin.copy()
    sin_p[0] = nyq
    fwd = np.concatenate([cos, sin_p], axis=0)
    icos = 2.0 * cos.T / n
    icos[:, 0] = 1.0 / n
    isin = 2.0 * sin.T / n
    isin[:, 0] = nyq / n
    inv = np.concatenate([icos, isin], axis=1)
    return fwd, inv


def _filter_embedding(seq_len):
    t = np.linspace(0.0, 1.0, seq_len)[:, None]
    w = 2.0 * np.pi * np.arange(seq_len) / seq_len
    f = np.linspace(1e-4, FILTER_BANDS - 1, FILTER_BANDS)
    ang = w[:, None] * f[None, :]
    z = np.concatenate([t, np.cos(ang), -np.sin(ang)], axis=-1)
    zp = np.zeros((seq_len, LANES), np.float64)
    zp[:, : z.shape[1]] = z
    deltas = np.abs(np.linspace(DECAY_MIN, DECAY_MAX, C_B))
    window = np.exp(-t * deltas[None, :])
    return zp.astype(np.float32), window.astype(np.float32)


def _hyena_filter_kernel(z_ref, win_ref, w1_ref, b1_ref, fr1_ref, w2_ref, b2_ref, fr2_ref, w3_ref,
                         hb_ref, fh_ref, fl_ref, gr_ref, gi_ref, *, seq_len):
    h = jnp.sin(fr1_ref[...] * (_dot3(z_ref[...], w1_ref[...]) + b1_ref[...]))
    h = jnp.sin(fr2_ref[...] * (_dot3(h, w2_ref[...]) + b2_ref[...]))
    h = _dot3(h, w3_ref[...])
    win = win_ref[...]
    hf = h[:, :C_B] * win
    hbk = h[:, C_B:] * win

    def dft(x):
        xh, xl = _split_bf16(x)
        return _dot(fh_ref[...], xh) + _dot(fh_ref[...], xl) + _dot(fl_ref[...], xh)

    p_sum = dft(hf + hbk)
    p_dif = dft(hf - hbk)
    row0 = lax.broadcasted_iota(jnp.int32, (seq_len, C_B), 0) == 0
    gr_ref[...] = p_sum[:seq_len] + hb_ref[...]
    gi_ref[...] = jnp.where(row0, p_sum[seq_len:] + hb_ref[...], p_dif[seq_len:])


def _hyena_filter(seq_len, w1, b1, fr1, w2, b2, fr2, w3, hbias):
    zemb, window = _filter_embedding(seq_len)
    fwd, _ = _dft_constants(seq_len)
    f_hi, f_lo = _split_bf16(jnp.asarray(fwd, F32))

    def pad(a, rows, cols):
        return jnp.zeros((rows, cols), F32).at[: a.shape[0], : a.shape[1]].set(a)

    args = (jnp.asarray(zemb), jnp.asarray(window), pad(w1, LANES, LANES), pad(b1[None], 1, LANES),
            pad(fr1[None], 1, LANES), pad(w2, LANES, LANES), pad(b2[None], 1, LANES),
            pad(fr2[None], 1, LANES), pad(w3, LANES, 2 * C_B), hbias.reshape(1, C_B), f_hi, f_lo)
    return pl.pallas_call(
        functools.partial(_hyena_filter_kernel, seq_len=seq_len),
        out_shape=[jax.ShapeDtypeStruct((seq_len, C_B), F32)] * 2,
        compiler_params=pltpu.CompilerParams(vmem_limit_bytes=VMEM_LIMIT_BYTES),
        name="hyena_filter",
    )(*args)


def _hyena_kernel(x0_ref, x1_ref, v_ref, cw_ref, cb_ref, gr_ref, gi_ref, f_ref, fi_ref, o_ref, *,
                  seq_len):
    tc = o_ref.shape[2]
    row = lax.broadcasted_iota(jnp.int32, (seq_len, tc), 0)

    def short_conv(u_ref, j, sq):
        u = u_ref[sq]
        prev = jnp.where(row == 0, 0.0, pltpu.roll(u, 1, 0))
        nxt = jnp.where(row == seq_len - 1, 0.0, pltpu.roll(u, seq_len - 1, 0))
        cw = cw_ref[j]
        return cw[0:1] * prev + cw[1:2] * u + cw[2:3] * nxt + cb_ref[j]

    gr, gi = gr_ref[...], gi_ref[...]
    row0 = row == 0
    for sq in range(o_ref.shape[0]):
        x0 = short_conv(x0_ref, 0, sq)
        z = short_conv(v_ref, 2, sq) * short_conv(x1_ref, 1, sq)
        p = _dot(f_ref[...], z.astype(BF16))
        a, b = p[:seq_len], p[seq_len:]
        bgi = b * gi
        yr = a * gr - jnp.where(row0, 0.0, bgi)
        yq = jnp.where(row0, bgi, a * gi + b * gr)
        y = _dot(fi_ref[...], jnp.concatenate([yr, yq], axis=0).astype(BF16))
        o_ref[sq] = y * x0


def _hyena(hy, conv_w, conv_b, gr, gi, seq_len, nseq, tc=256):
    bsz = hy.shape[0]
    nct = C_B // tc
    fwd, inv = _dft_constants(seq_len)
    cw = conv_w.reshape(3, 3, C_B).transpose(1, 0, 2)
    cb = conv_b.reshape(3, 1, C_B)
    part = lambda k: pl.BlockSpec((nseq, seq_len, tc), lambda b, j, k=k: (b, 0, k * nct + j))
    return pl.pallas_call(
        functools.partial(_hyena_kernel, seq_len=seq_len),
        grid=(bsz // nseq, nct),
        in_specs=[part(0), part(1), part(2),
                  pl.BlockSpec((3, 3, tc), lambda b, j: (0, 0, j)),
                  pl.BlockSpec((3, 1, tc), lambda b, j: (0, 0, j)),
                  pl.BlockSpec((seq_len, tc), lambda b, j: (0, j)),
                  pl.BlockSpec((seq_len, tc), lambda b, j: (0, j)),
                  pl.BlockSpec((2 * seq_len, seq_len), lambda b, j: (0, 0)),
                  pl.BlockSpec((seq_len, 2 * seq_len), lambda b, j: (0, 0))],
        out_specs=pl.BlockSpec((nseq, seq_len, tc), lambda b, j: (b, 0, j)),
        out_shape=jax.ShapeDtypeStruct((bsz, seq_len, C_B), F32),
        compiler_params=_cparams("arbitrary", "arbitrary"),
        name="hyena",
    )(hy, hy, hy, cw, cb, gr, gi, jnp.asarray(fwd, F32).astype(BF16),
      jnp.asarray(inv, F32).astype(BF16))


def _rope_tables(seq_len, dim):
    rows = seq_len // GRID_W
    row_idx = np.repeat(np.arange(rows), GRID_W).astype(np.float64)
    col_idx = np.tile(np.arange(GRID_W), rows).astype(np.float64)
    half = dim // 2
    inv = ROPE_THETA ** (-np.arange(0, half, 2, dtype=np.float64) / half)
    ang = np.concatenate([row_idx[:, None] * inv, col_idx[:, None] * inv], axis=-1)
    cos = np.repeat(np.cos(ang), 2, axis=1)
    sin = np.repeat(np.sin(ang), 2, axis=1)
    sin[:, 0::2] *= -1.0
    reps = LANES // dim
    return (np.tile(cos, (1, reps)).astype(np.float32), np.tile(sin, (1, reps)).astype(np.float32))


def _lane_group_matrices():
    i = np.arange(2 * LANES)
    same = lambda width: (i[:, None] // width == i[None, :] // width).astype(np.float32)
    return same(HD_C), same(DK_D)


def _qk_proj_kernel(x_ref, m_ref, g_ref, w_ref, cqg_ref, ckg_ref, dqg_ref, dkg_ref, grp_c_ref,
                    grp_d_ref, *rest, use_rope, slabs):
    if use_rope:
        cc_ref, sc_ref, cd_ref, sd_ref = rest[:4]
        rest = rest[4:]
    qc_ref, qd_ref, kc_ref, vc_ref, kd_ref, vd_ref = rest
    m = m_ref[0]
    pair = 2 * LANES
    assert pair == KV_C * HD_C
    two = lambda r: jnp.concatenate([r, r], axis=1)
    rows_per_slab = x_ref.shape[1] // slabs

    for slab in range(slabs):
        rows = pl.ds(slab * rows_per_slab, rows_per_slab)
        hb = (_rms(x_ref[0, rows, :]) * g_ref[...] * (1.0 + m[1:2]) + m[0:1]).astype(BF16)
        cols = lambda start, n: _dot(hb, w_ref[:, start * LANES : (start + n) * LANES])

        def norm(x, grp_ref, width, g_ref, cos_ref, sin_ref):
            ms = _dot((x * x).astype(BF16), grp_ref[...]) * (1.0 / width)
            y = x * lax.rsqrt(ms + EPS) * two(g_ref[...])
            if not use_rope:
                return y
            even = (lax.broadcasted_iota(jnp.int32, y.shape, 1) % 2) == 0
            swapped = jnp.where(even, pltpu.roll(y, y.shape[1] - 1, 1), pltpu.roll(y, 1, 1))
            return y * two(cos_ref[rows, :]) + swapped * two(sin_ref[rows, :])

        norm_c = lambda x, g_ref: norm(x, grp_c_ref, HD_C, g_ref, cc_ref if use_rope else None,
                                       sc_ref if use_rope else None)
        norm_d = lambda x, g_ref: norm(x, grp_d_ref, DK_D, g_ref, cd_ref if use_rope else None,
                                       sd_ref if use_rope else None)
        for i in range(H_C // 2):
            qc_ref[0, rows, i * pair : (i + 1) * pair] = norm_c(cols(2 * i, 2), cqg_ref)
        kc = norm_c(cols(H_C, KV_C), ckg_ref)
        vc = cols(H_C + KV_C, KV_C)
        for j in range(KV_C):
            kc_ref[0, j, rows, :] = kc[:, j * LANES : (j + 1) * LANES]
            vc_ref[0, j, rows, :] = vc[:, j * LANES : (j + 1) * LANES]
        base = H_C + 2 * KV_C
        for i in range(H_D // 2):
            qd_ref[0, rows, i * pair : (i + 1) * pair] = norm_d(cols(base + 2 * i, 2), dqg_ref)
            kd = norm_d(cols(base + H_D + 2 * i, 2), dkg_ref)
            vd = cols(base + 2 * H_D + 2 * i, 2)
            for j in range(2):
                kd_ref[0, 2 * i + j, rows, :] = kd[:, j * LANES : (j + 1) * LANES]
                vd_ref[0, 2 * i + j, rows, :] = vd[:, j * LANES : (j + 1) * LANES]


def _qk_proj(x, mods, g, w_bf16, cqg, ckg, dqg, dkg, seq_len, use_rope, tm, slabs):
    ngrp = x.shape[0]
    spg = ROW_GROUP // seq_len
    bsz = ngrp * spg
    assert seq_len % tm == 0
    seq_of = lambda gi, i: gi * spg + (i * tm) // seq_len
    blk_of = lambda i: ((i * tm) % seq_len) // tm
    vec = lambda: pl.BlockSpec((1, LANES), lambda gi, i: (0, 0))
    mat = lambda: pl.BlockSpec((2 * LANES, 2 * LANES), lambda gi, i: (0, 0))
    tab = lambda: pl.BlockSpec((tm, LANES), lambda gi, i: (blk_of(i), 0))
    in_specs = [pl.BlockSpec((1, tm, D_MODEL), lambda gi, i: (gi, i, 0)),
                pl.BlockSpec((1, 6, D_MODEL), lambda gi, i: (gi, 0, 0)),
                pl.BlockSpec((1, D_MODEL), lambda gi, i: (0, 0)),
                pl.BlockSpec(w_bf16.shape, lambda gi, i: (0, 0)),
                vec(), vec(), vec(), vec(), mat(), mat()]
    args = [x, mods, g.reshape(1, D_MODEL), w_bf16, cqg.reshape(1, HD_C), ckg.reshape(1, HD_C),
            jnp.tile(dqg.reshape(1, DK_D), (1, 2)), jnp.tile(dkg.reshape(1, DK_D), (1, 2))]
    args += [jnp.asarray(mm, BF16) for mm in _lane_group_matrices()]
    if use_rope:
        in_specs += [tab(), tab(), tab(), tab()]
        args += [jnp.asarray(t) for t in _rope_tables(seq_len, HD_C) + _rope_tables(seq_len, DK_D)]
    tok = lambda w: pl.BlockSpec((1, tm, w), lambda gi, i: (seq_of(gi, i), blk_of(i), 0))
    head = lambda nh: pl.BlockSpec((1, nh, tm, LANES), lambda gi, i: (seq_of(gi, i), 0, blk_of(i), 0))
    tok_shape = lambda w: jax.ShapeDtypeStruct((bsz, seq_len, w), F32)
    head_shape = lambda nh: jax.ShapeDtypeStruct((bsz, nh, seq_len, LANES), F32)
    return pl.pallas_call(
        functools.partial(_qk_proj_kernel, use_rope=use_rope, slabs=slabs),
        grid=(ngrp, ROW_GROUP // tm),
        in_specs=in_specs,
        out_specs=[tok(H_C * HD_C), tok(H_D * 2 * DK_D), head(KV_C), head(KV_C), head(H_D), head(H_D)],
        out_shape=[tok_shape(H_C * HD_C), tok_shape(H_D * 2 * DK_D), head_shape(KV_C),
                   head_shape(KV_C), head_shape(H_D), head_shape(H_D)],
        compiler_params=_cparams("arbitrary", "arbitrary"),
        name="qk_proj",
    )(*args)


def _softmax_pv(q_list, kv_list):
    outs = []
    for q in q_list:
        scores = [_dot_nt(q, k) for k, _ in kv_list]
        mx = scores[0].max(axis=1, keepdims=True)
        for s in scores[1:]:
            mx = jnp.maximum(mx, s.max(axis=1, keepdims=True))
        den = 0.0
        acc = 0.0
        for s, (_, v) in zip(scores, kv_list):
            pexp = jnp.exp2(s - mx)
            den = den + pexp.sum(axis=1, keepdims=True)
            acc = acc + _dot(pexp.astype(BF16), v)
        outs.append(acc / den)
    return outs


def _head_kv(j, k_ref, v_ref, cache_refs):
    kv = [(r_k[0, 0, j].astype(BF16), r_v[0, 0, j].astype(BF16)) for r_k, r_v in cache_refs]
    kv.append((k_ref[0, j].astype(BF16), v_ref[0, j].astype(BF16)))
    return kv


def _per_slab(q_ref, o_ref, cols, fn):
    for r0 in range(0, q_ref.shape[1], ATT_SLAB):
        rows = pl.ds(r0, min(ATT_SLAB, q_ref.shape[1]))
        o_ref[0, rows, cols] = fn(q_ref[0, rows, cols])


def _gqa_kernel(q_ref, k_ref, v_ref, *rest, has_cache):
    cache_refs = [rest[:2]] if has_cache else []
    o_ref = rest[-1]
    g_c = H_C // KV_C
    width = g_c * HD_C
    for j in range(k_ref.shape[1]):
        kv = _head_kv(j, k_ref, v_ref, cache_refs)

        def head(q):
            q = q * (HD_C ** -0.5 * LOG2E)
            qs = [q[:, g * HD_C : (g + 1) * HD_C].astype(BF16) for g in range(g_c)]
            return jnp.concatenate(_softmax_pv(qs, kv), axis=1)

        _per_slab(q_ref, o_ref, slice(j * width, (j + 1) * width), head)


def _diff_kernel(q_ref, k_ref, v_ref, lam_ref, sg_ref, *rest, has_cache, out_scale):
    cache_refs = [rest[:2]] if has_cache else []
    o_ref = rest[-1]
    width = 2 * DK_D
    for j in range(k_ref.shape[1]):
        kv = _head_kv(j, k_ref, v_ref, cache_refs)

        def head(q):
            q = q * (DK_D ** -0.5 * LOG2E)
            low = lax.broadcasted_iota(jnp.int32, q.shape, 1) < DK_D
            qs = [jnp.where(low, q, 0.0).astype(BF16), jnp.where(low, 0.0, q).astype(BF16)]
            o1, o2 = _softmax_pv(qs, kv)
            return _rms(o1 - lam_ref[...] * o2) * sg_ref[...] * out_scale

        _per_slab(q_ref, o_ref, slice(j * width, (j + 1) * width), head)


def _attention(q, k, v, cache_k, cache_v, kernel, extra_args, n_heads, q_width, seq_len, hps, tq):
    bsz = q.shape[0]
    in_specs = [pl.BlockSpec((1, tq, hps * q_width), lambda b, h, i: (b, i, h)),
                pl.BlockSpec((1, hps, seq_len, LANES), lambda b, h, i: (b, h, 0, 0)),
                pl.BlockSpec((1, hps, seq_len, LANES), lambda b, h, i: (b, h, 0, 0))]
    in_specs += [pl.BlockSpec((1, LANES), lambda b, h, i: (0, 0)) for _ in extra_args]
    args = [q, k, v, *extra_args]
    if cache_k is not None:
        past = cache_k.shape[3]
        spec = lambda: pl.BlockSpec((1, 1, hps, past, LANES), lambda b, h, i: (b, 0, h, 0, 0))
        in_specs += [spec(), spec()]
        args += [cache_k, cache_v]
    return pl.pallas_call(
        functools.partial(kernel, has_cache=cache_k is not None),
        grid=(bsz, n_heads // hps, seq_len // tq),
        in_specs=in_specs,
        out_specs=pl.BlockSpec((1, tq, hps * q_width), lambda b, h, i: (b, i, h)),
        out_shape=jax.ShapeDtypeStruct((bsz, seq_len, n_heads * q_width), F32),
        compiler_params=_cparams("arbitrary", "arbitrary", "arbitrary"),
        name="attention",
    )(*args)


def _route(lg):
    lane = lax.broadcasted_iota(jnp.int32, lg.shape, 1).astype(F32)
    neg = -1e30
    is_g = (lane >= N_EXPERTS) & (lane < N_EXPERTS + N_GROUPS)
    gl = jnp.where(is_g, lg, neg)
    gmax = gl.max(axis=1, keepdims=True)
    g_p = 1.0 / jnp.where(is_g, jnp.exp(gl - gmax), 0.0).sum(axis=1, keepdims=True)
    g_i = jnp.where(gl == gmax, lane - N_EXPERTS, 1e9).min(axis=1, keepdims=True)
    in_grp = (lane < N_EXPERTS) & (jnp.floor(lane * (1.0 / EXP_PER_GROUP)) == g_i)
    el = jnp.where(in_grp, lg, neg)
    m1 = el.max(axis=1, keepdims=True)
    i1 = jnp.where(in_grp & (el == m1), lane, 1e9).min(axis=1, keepdims=True)
    el2 = jnp.where(lane == i1, neg, el)
    m2 = el2.max(axis=1, keepdims=True)
    i2 = jnp.where(in_grp & (el2 == m2) & (lane != i1), lane, 1e9).min(axis=1, keepdims=True)
    r = jnp.exp(m2 - m1)
    w1 = g_p / (1.0 + r)
    rec = jnp.where(lane == ROUTE_LANE, i1, 0.0)
    rec = jnp.where(lane == ROUTE_LANE + 1, i2, rec)
    rec = jnp.where(lane == ROUTE_LANE + 2, w1, rec)
    return jnp.where(lane == ROUTE_LANE + 3, w1 * r, rec)


def _mix_out_kernel(x_ref, a_ref, b_ref, m_ref, g2_ref, w_ref, wr_ref, br_ref, tri_ref,
                    x1_ref, xs_ref, pos_ref, cnt_ref, *, parts):
    wa = a_ref.shape[2]
    m = m_ref[0]
    rows_per_part = x_ref.shape[1] // parts
    h2_parts, route_parts = [], []
    for part in range(parts):
        rows = pl.ds(part * rows_per_part, rows_per_part)
        o = (_dot(a_ref[0, rows, :].astype(BF16), w_ref[:wa])
             + _dot(b_ref[0, rows, :].astype(BF16), w_ref[wa:]))
        x1 = x_ref[0, rows, :] + m[2:3] * o
        x1_ref[0, rows, :] = x1
        h2 = _rms(x1) * g2_ref[...] * (1.0 + m[4:5]) + m[3:4]
        h2_parts.append(h2.astype(BF16))
        r = _dot(jnp.concatenate(_split_bf16(h2), axis=0), wr_ref[...])
        hi_rows, lo_rows = r[:rows_per_part], r[rows_per_part:]
        lg = hi_rows[:, :LANES] + hi_rows[:, LANES:] + lo_rows[:, :LANES] + br_ref[...]
        route_parts.append(_route(lg))
    xs, pos, cnt = _moe_sort_tile(jnp.concatenate(h2_parts, axis=0),
                                  jnp.concatenate(route_parts, axis=0), tri_ref[...])
    xs_ref[...] = xs
    pos_ref[0] = pos
    cnt_ref[0] = cnt


def _mix_out(x, a, b, mods, g2, w_out_bf16, w_router, b_router, parts=2):
    ngrp = x.shape[0]
    per = ROW_GROUP // MOE_TILE
    ntile = ngrp * per
    wa, wb = a.shape[2], b.shape[2]
    wr = jnp.concatenate(_split_bf16(w_router), axis=1)
    tri = np.tril(np.ones((MOE_TILE, MOE_TILE), np.float32), -1)
    src = lambda j: jnp.minimum(j, ntile - 1)
    row = lambda w: pl.BlockSpec((1, MOE_TILE, w), lambda j: (src(j) // per, src(j) % per, 0))
    full = lambda shape: pl.BlockSpec(shape, lambda j: (0,) * len(shape))
    return pl.pallas_call(
        functools.partial(_mix_out_kernel, parts=parts),
        grid=(ntile + 1,),
        in_specs=[row(D_MODEL), row(wa), row(wb),
                  pl.BlockSpec((1, 6, D_MODEL), lambda j: (src(j) // per, 0, 0)),
                  full((1, D_MODEL)), full((wa + wb, D_MODEL)),
                  full((D_MODEL, 2 * LANES)), full((1, LANES)), full((MOE_TILE, MOE_TILE))],
        out_specs=[pl.BlockSpec((1, MOE_TILE, D_MODEL), lambda j: (j, 0, 0)),
                   pl.BlockSpec((MOE_TILE_ROWS, MOE_ROW_W), lambda j: (j, 0)),
                   pl.BlockSpec((1, MOE_TILE, LANES), lambda j: (j, 0, 0)),
                   pl.BlockSpec((1, 8, LANES), lambda j: (j, 0, 0))],
        out_shape=[jax.ShapeDtypeStruct((ntile + 1, MOE_TILE, D_MODEL), F32),
                   jax.ShapeDtypeStruct(((ntile + 1) * MOE_TILE_ROWS, MOE_ROW_W), BF16),
                   jax.ShapeDtypeStruct((ntile + 1, MOE_TILE, LANES), F32),
                   jax.ShapeDtypeStruct((ntile + 1, 8, LANES), F32)],
        compiler_params=_cparams("arbitrary"),
        name="mix_out",
    )(x, a, b, mods, g2.reshape(1, D_MODEL), w_out_bf16, wr, b_router, jnp.asarray(tri, BF16))


def _lane_col(x, lane, k):
    return jnp.where(lane == k, x, 0.0).sum(axis=1, keepdims=True)


def _moe_sort_tile(h, r, tri):
    lane = lax.broadcasted_iota(jnp.int32, r.shape, 1).astype(F32)
    i1, i2, w1, w2 = [_lane_col(r, lane, ROUTE_LANE + k) for k in range(4)]
    oh1 = lane == i1
    oh2 = lane == i2
    oh = jnp.where(oh1 | oh2, 1.0, 0.0)
    rank = _dot(tri, oh.astype(BF16))
    cnt = oh.sum(axis=0, keepdims=True)
    chunks = jnp.floor((cnt + (MOE_CHUNK - 1)) * (1.0 / MOE_CHUNK))
    li = lax.broadcasted_iota(jnp.int32, (LANES, LANES), 0)
    lj = lax.broadcasted_iota(jnp.int32, (LANES, LANES), 1)
    before = jnp.where(li < lj, 1.0, 0.0).astype(BF16)
    seg = _dot(jnp.broadcast_to(chunks, (8, LANES)).astype(BF16), before)[0:1]
    base = seg * MOE_CHUNK + rank
    pos1 = jnp.where(oh1, base, 0.0).sum(axis=1, keepdims=True)
    pos2 = jnp.where(oh2, base, 0.0).sum(axis=1, keepdims=True)
    riota = lax.broadcasted_iota(jnp.int32, (r.shape[0], MOE_TILE_ROWS), 1).astype(F32)
    p = jnp.where((riota == pos1) | (riota == pos2), 1.0, 0.0).astype(BF16)

    aux = jnp.where(lane == 2 * MOE_W_PIECES, i1, 0.0)
    for k, w in enumerate((w1, w2)):
        rest = w
        for piece in range(MOE_W_PIECES):
            part = rest.astype(BF16).astype(F32)
            aux = jnp.where(lane == k * MOE_W_PIECES + piece, part, aux)
            rest = rest - part
    row = jnp.concatenate([h, aux.astype(BF16)], axis=1)
    xs = _dot_tn(p, row).astype(BF16)
    pos = jnp.where(lane == 0, pos1, jnp.where(lane == 1, pos2, 0.0))
    return xs, pos, jnp.broadcast_to(cnt, (8, LANES))


def _moe_tables(cnt, ntile):
    nblk = ntile * MOE_TILE_CHUNKS // MOE_BLOCK_CHUNKS + N_EXPERTS
    chunks = (cnt + MOE_CHUNK - 1) // MOE_CHUNK
    seg_start = jnp.cumsum(chunks, axis=1) - chunks
    tile_prefix = jnp.cumsum(chunks, axis=0) - chunks
    per_expert = chunks.sum(axis=0)
    blocks = (per_expert + MOE_BLOCK_CHUNKS - 1) // MOE_BLOCK_CHUNKS
    blk_end = jnp.cumsum(blocks)
    n_used = blk_end[-1]
    b = jnp.arange(nblk + 1, dtype=jnp.int32)
    blk_e = jnp.sum(b[:, None] >= blk_end[None, :], axis=1).astype(jnp.int32)
    last_e = jnp.sum((n_used - 1) >= blk_end).astype(jnp.int32)
    blk_e = jnp.clip(jnp.where(b < n_used, blk_e, last_e), 0, N_EXPERTS - 1)
    oh_e = (blk_e[:, None] == jnp.arange(N_EXPERTS)[None, :]).astype(jnp.int32)
    pick = lambda per_tile: jnp.sum(oh_e[:, :, None] * per_tile.T[None], axis=1)
    seg_e, pre_e, chunks_e = pick(seg_start), pick(tile_prefix), pick(chunks)
    first_blk = jnp.sum(oh_e * (blk_end - blocks)[None, :], axis=1)
    k = (b - first_blk)[:, None] * MOE_BLOCK_CHUNKS + jnp.arange(MOE_BLOCK_CHUNKS)[None, :]
    k3 = k[:, :, None]
    in_tile = (pre_e[:, None, :] <= k3) & (k3 < (pre_e + chunks_e)[:, None, :])
    tile_base = (jnp.arange(ntile) * MOE_TILE_CHUNKS)[None, :] + seg_e - pre_e
    src = jnp.sum(jnp.where(in_tile, tile_base[:, None, :] + k3, 0), axis=-1)
    valid = jnp.any(in_tile, axis=-1) & (b < n_used)[:, None]
    slot_c = jnp.arange(MOE_BLOCK_CHUNKS)[None, :]
    n_read = MOE_TILE_CHUNKS - 2 * MOE_BLOCK_CHUNKS
    assert n_read > 0
    spare = ntile * MOE_TILE_CHUNKS
    gather = jnp.where(valid, src, spare + 2 * MOE_BLOCK_CHUNKS + slot_c % n_read).astype(jnp.int32)
    scatter = jnp.where(valid, src, spare + (b % 2)[:, None] * MOE_BLOCK_CHUNKS + slot_c).astype(jnp.int32)
    blk_start = jnp.concatenate([blk_end - blocks, n_used[None]]).astype(jnp.int32)
    return blk_start, gather.reshape(-1), scatter.reshape(-1)


def _moe_expert_kernel(start_ref, gather_ref, scatter_ref, xs_hbm, wg_ref, wu_ref, wd_ref, ys_hbm,
                       lhs, obuf, wgb, wub, wdb, in_sem, out_sem):
    e = pl.program_id(0)
    ne = pl.num_programs(0)
    b0 = start_ref[e]
    b1 = start_ref[e + 1]
    n = start_ref[ne]

    def chunk_copy(blk, slot, c, gather):
        rows = pl.ds(c * MOE_CHUNK, MOE_CHUNK)
        if gather:
            idx = gather_ref[blk * MOE_BLOCK_CHUNKS + c]
            return pltpu.make_async_copy(xs_hbm.at[idx], lhs.at[slot, rows], in_sem.at[slot])
        idx = scatter_ref[blk * MOE_BLOCK_CHUNKS + c]
        dst = ys_hbm.at[idx, pl.ds(0, MOE_CHUNK), pl.ds(0, D_MODEL)]
        return pltpu.make_async_copy(obuf.at[slot, rows], dst, out_sem.at[slot])

    def for_chunks(blk, slot, gather, start):
        for c in range(MOE_BLOCK_CHUNKS):
            cp = chunk_copy(blk, slot, c, gather)
            if start:
                cp.start()
            else:
                cp.wait()

    @pl.when((e == 0) & (n > 0))
    def _():
        for_chunks(0, 0, True, True)

    @pl.when(b1 > b0)
    def _():
        wgb[...] = wg_ref[0, 0].astype(BF16)
        wub[...] = wu_ref[0, 0].astype(BF16)
        wdb[...] = wd_ref[0, 0].astype(BF16)
        e_f32 = e.astype(F32)

        def block(b, carry):
            slot = b % 2

            @pl.when(b >= 2)
            def _():
                for_chunks(b - 2, slot, False, False)

            for_chunks(b, slot, True, False)
            for_chunks(b + 1, 1 - slot, True, True)
            xa = lhs[slot]
            x = xa[:, :D_MODEL]
            aux = xa[:, D_MODEL:].astype(F32)
            lane = lax.broadcasted_iota(jnp.int32, aux.shape, 1)
            first = lane < MOE_W_PIECES
            w_first = jnp.where(first, aux, 0.0).sum(axis=1, keepdims=True)
            w_second = jnp.where(first | (lane >= 2 * MOE_W_PIECES), 0.0, aux).sum(axis=1, keepdims=True)
            e_first = _lane_col(aux, lane, 2 * MOE_W_PIECES)
            w = jnp.where(e_first == e_f32, w_first, w_second)
            hid = _silu(_dot(x, wgb[...])) * _dot(x, wub[...]) * w
            obuf[slot] = _dot(hid.astype(BF16), wdb[...]).astype(BF16)
            for_chunks(b, slot, False, True)
            return carry

        lax.fori_loop(b0, b1, block, 0)

    @pl.when(e == ne - 1)
    def _():
        @pl.when(n >= 2)
        def _():
            for_chunks(n - 2, n % 2, False, False)

        @pl.when(n >= 1)
        def _():
            for_chunks(n, n % 2, True, False)
            for_chunks(n - 1, (n - 1) % 2, False, False)


def _moe_experts(xs, blk_start, gather, scatter, layer, w_gate, w_up, w_down):
    nchunk = xs.shape[0] // MOE_CHUNK
    rows_per_blk = MOE_BLOCK_CHUNKS * MOE_CHUNK
    wspec = lambda shape: pl.BlockSpec((1, 1) + shape, lambda e, st, g, s: (layer, e, 0, 0))
    hbm = pl.BlockSpec(memory_space=pl.ANY)
    ys = pl.pallas_call(
        _moe_expert_kernel,
        grid_spec=pltpu.PrefetchScalarGridSpec(
            num_scalar_prefetch=3,
            grid=(N_EXPERTS,),
            in_specs=[hbm, wspec((D_MODEL, D_EXPERT)), wspec((D_MODEL, D_EXPERT)),
                      wspec((D_EXPERT, D_MODEL))],
            out_specs=hbm,
            scratch_shapes=[pltpu.VMEM((2, rows_per_blk, MOE_ROW_W), BF16),
                            pltpu.VMEM((2, rows_per_blk, D_MODEL), BF16),
                            pltpu.VMEM((D_MODEL, D_EXPERT), BF16),
                            pltpu.VMEM((D_MODEL, D_EXPERT), BF16),
                            pltpu.VMEM((D_EXPERT, D_MODEL), BF16),
                            pltpu.SemaphoreType.DMA((2,)),
                            pltpu.SemaphoreType.DMA((2,))]),
        out_shape=jax.ShapeDtypeStruct((nchunk, MOE_CHUNK, MOE_ROW_W), BF16),
        input_output_aliases={3: 0},
        compiler_params=_cparams("arbitrary"),
        name="moe_experts",
    )(blk_start, gather, scatter, xs.reshape(nchunk, MOE_CHUNK, MOE_ROW_W), w_gate, w_up, w_down)
    return ys.reshape(nchunk * MOE_CHUNK, MOE_ROW_W)


def _moe_combine_kernel(ys_ref, pos_ref, x_ref, m_ref, o_ref):
    pos = pos_ref[0]
    lane = lax.broadcasted_iota(jnp.int32, pos.shape, 1)
    pos1 = _lane_col(pos, lane, 0)
    pos2 = _lane_col(pos, lane, 1)
    riota = lax.broadcasted_iota(jnp.int32, (pos.shape[0], MOE_TILE_ROWS), 1).astype(F32)
    p = jnp.where((riota == pos1) | (riota == pos2), 1.0, 0.0).astype(BF16)
    o_ref[0] = x_ref[0] + m_ref[0][5:6] * _dot(p, ys_ref[...])


def _moe_combine(ys, pos, x1, mods):
    ngrp = mods.shape[0]
    per = ROW_GROUP // MOE_TILE
    tile = pl.BlockSpec((1, MOE_TILE, D_MODEL), lambda j: (j, 0, 0))
    return pl.pallas_call(
        _moe_combine_kernel,
        grid=(ngrp * per,),
        in_specs=[pl.BlockSpec((MOE_TILE_ROWS, D_MODEL), lambda j: (j, 0)),
                  pl.BlockSpec((1, MOE_TILE, LANES), lambda j: (j, 0, 0)),
                  tile, pl.BlockSpec((1, 6, D_MODEL), lambda j: (j // per, 0, 0))],
        out_specs=pl.BlockSpec((1, MOE_TILE, D_MODEL), lambda j: (j // per, j % per, 0)),
        out_shape=jax.ShapeDtypeStruct((ngrp, ROW_GROUP, D_MODEL), F32),
        compiler_params=_cparams("arbitrary"),
        name="moe_combine",
    )(ys, pos, x1, mods)


def _moe(xs, pos, cnt, x1, mods, layer, w_gate, w_up, w_down):
    ntile = mods.shape[0] * (ROW_GROUP // MOE_TILE)
    cnt = cnt[:ntile, 0, :N_EXPERTS].astype(jnp.int32)
    blk_start, gather, scatter = _moe_tables(cnt, ntile)
    ys = _moe_experts(xs, blk_start, gather, scatter, layer, w_gate, w_up, w_down)
    return _moe_combine(ys, pos, x1, mods)


def kernel(x_prompt, x_sample, state_hgrn, cache_c_k, cache_c_v, cache_d_k, cache_d_v, c, c_ctx, norm1_g, norm2_g, w_mod, b_mod, even_w_in, even_w_out, hgrn_lower, hgrn_norm_g, hy_conv_w, hy_conv_b, hy_w1, hy_b1, hy_freq1, hy_w2, hy_b2, hy_freq2, hy_w3, hy_bias, odd_w_in, odd_w_out, c_qnorm_g, c_knorm_g, d_qnorm_g, d_knorm_g, d_lambda_q1, d_lambda_k1, d_lambda_q2, d_lambda_k2, d_subln_g, moe_w_grp, moe_b_grp, moe_w_rt, moe_b_rt, moe_w_gate, moe_w_up, moe_w_down):
    depth = w_mod.shape[0]
    n_ctx, seq, _ = x_prompt.shape
    n_lat, dec_seq, _ = x_sample.shape
    g_ctx = n_ctx * seq // ROW_GROUP
    g_lat = n_lat * dec_seq // ROW_GROUP
    assert dec_seq == ROW_GROUP and ROW_GROUP % seq == 0

    cond = jnp.zeros((16, D_MODEL), F32).at[0].set(c_ctx).at[1 : 1 + n_lat].set(c)
    mods = _adaln(cond, w_mod, b_mod).reshape(depth, 16, 6, D_MODEL)
    lower = jnp.cumsum(jax.nn.softmax(hgrn_lower.astype(F32), axis=0), axis=0)

    streams = [
        dict(x=x_prompt.reshape(g_ctx, ROW_GROUP, D_MODEL), ngrp=g_ctx, bsz=n_ctx, seq=seq, ctx=True),
        dict(x=x_sample, ngrp=g_lat, bsz=n_lat, seq=dec_seq, ctx=False),
    ]
    new_state, new_ck, new_cv, new_dk, new_dv = [], [], [], [], []

    for l in range(depth):
        j = l // 2
        w_router = jnp.zeros((D_MODEL, LANES), F32)
        w_router = w_router.at[:, :N_EXPERTS].set(moe_w_rt[l])
        w_router = w_router.at[:, N_EXPERTS : N_EXPERTS + N_GROUPS].set(moe_w_grp[l])
        b_router = jnp.zeros((1, LANES), F32)
        b_router = b_router.at[0, :N_EXPERTS].set(moe_b_rt[l])
        b_router = b_router.at[0, N_EXPERTS : N_EXPERTS + N_GROUPS].set(moe_b_grp[l])
        if l % 2 == 0:
            w_in = even_w_in[j].astype(BF16)
            w_out = even_w_out[j].astype(BF16)
        else:
            w_in = odd_w_in[j].astype(BF16)
            w_out = odd_w_out[j].astype(BF16)
            lam_init = 0.8 - 0.6 * math.exp(-0.3 * l)
            lam = (jnp.exp(jnp.sum(d_lambda_q1[j] * d_lambda_k1[j]))
                   - jnp.exp(jnp.sum(d_lambda_q2[j] * d_lambda_k2[j])) + lam_init)
            lam_row = jnp.full((1, LANES), lam, F32)

        for s in streams:
            ngrp, bsz, sl = s["ngrp"], s["bsz"], s["seq"]
            if s["ctx"]:
                m = jnp.broadcast_to(mods[l, 0][None], (ngrp, 6, D_MODEL))
            else:
                m = mods[l, 1 : 1 + ngrp]
            x = s["x"]
            if l % 2 == 0:
                wa = H_A * DK_A
                splits = [(0, wa), (wa, 2 * wa), (2 * wa, 3 * wa), (3 * wa, 3 * wa + W_A),
                          (3 * wa + W_A, 3 * wa + 2 * W_A), (3 * wa + 2 * W_A, w_in.shape[1])]
                qa, ffa, fba, ia, ga, hy = _norm_proj(x, m, norm1_g[l], w_in, splits)
                per_seq = lambda t: t.reshape(bsz, sl, t.shape[-1])
                if s["ctx"]:
                    s0t = jnp.zeros((bsz, 2, H_A, DK_A, DK_A), F32)
                else:
                    s0t = jnp.swapaxes(state_hgrn[:, j].astype(F32), -1, -2)
                mix_a, s_fin = _hgrn(per_seq(qa), per_seq(ffa), per_seq(fba), per_seq(ia),
                                     per_seq(ga), lower[j], hgrn_norm_g[j], s0t, sl,
                                     nseq=2 if s["ctx"] else 1)
                gr, gi = _hyena_filter(sl, hy_w1[j], hy_b1[j], hy_freq1[j], hy_w2[j], hy_b2[j],
                                       hy_freq2[j], hy_w3[j], hy_bias[j])
                mix_b = _hyena(per_seq(hy), hy_conv_w[j], hy_conv_b[j], gr, gi, sl,
                               nseq=4 if s["ctx"] else 2)
                if s["ctx"]:
                    new_state.append(jnp.swapaxes(s_fin, -1, -2))
            else:
                qc, qd, kc, vc, kd, vd = _qk_proj(x, m, norm1_g[l], w_in, c_qnorm_g[j], c_knorm_g[j],
                                                  d_qnorm_g[j], d_knorm_g[j], sl, use_rope=not s["ctx"],
                                                  tm=min(sl, 512), slabs=1 if s["ctx"] else 2)
                if s["ctx"]:
                    caches = (None, None, None, None)
                    new_ck.append(kc)
                    new_cv.append(vc)
                    new_dk.append(kd)
                    new_dv.append(vd)
                else:
                    caches = (cache_c_k[:, j : j + 1], cache_c_v[:, j : j + 1],
                              cache_d_k[:, j : j + 1], cache_d_v[:, j : j + 1])
                hps_c, hps_d, tq = (KV_C, H_D, sl) if s["ctx"] else (1, 1, sl)
                mix_a = _attention(qc, kc, vc, caches[0], caches[1], _gqa_kernel, (), KV_C,
                                   (H_C // KV_C) * HD_C, sl, hps_c, tq)
                diff = functools.partial(_diff_kernel, out_scale=1.0 - lam_init)
                mix_b = _attention(qd, kd, vd, caches[2], caches[3], diff,
                                   (lam_row, d_subln_g[j].reshape(1, DV_D)), H_D, 2 * DK_D, sl,
                                   hps_d, tq)
            grp = lambda t: t.reshape(ngrp, ROW_GROUP, t.shape[-1])
            x1, xs, pos, cnt = _mix_out(x, grp(mix_a), grp(mix_b), m, norm2_g[l], w_out, w_router,
                                        b_router)
            s["x"] = _moe(xs, pos, cnt, x1, m, l, moe_w_gate, moe_w_up, moe_w_down)

    y_ctx = streams[0]["x"].reshape(n_ctx, seq, D_MODEL)
    y_lat = streams[1]["x"]
    return (y_ctx, y_lat, jnp.stack(new_state, axis=1), jnp.stack(new_ck, axis=1),
            jnp.stack(new_cv, axis=1), jnp.stack(new_dk, axis=1), jnp.stack(new_dv, axis=1))
```

```python
import functools
import math

import numpy as np
import jax
import jax.numpy as jnp
from jax import lax
from jax.experimental import pallas as pl
from jax.experimental.pallas import tpu as pltpu

F32 = jnp.float32
BF16 = jnp.bfloat16

D_MODEL = 1024
EPS = 1e-6
LOG2E = 1.0 / math.log(2.0)
GRID_W = 64
ROPE_THETA = 10000.0
H_A = 4
DK_A = 128
W_A = 512
CHUNK = 128
SUB = 8
LEVELS = (16, 32, 64, 128)
C_B = 512
FILTER_BANDS = 16
DECAY_MIN = math.log(1e-2) / 1.5
DECAY_MAX = math.log(1e-2) / 0.3
H_C = 4
KV_C = 2
HD_C = 128
H_D = 4
DK_D = 64
DV_D = 128
N_GROUPS = 4
EXP_PER_GROUP = 4
N_EXPERTS = 16
D_EXPERT = 512

ROUTE_LANE = N_EXPERTS + N_GROUPS
MOE_TILE = 512
MOE_CHUNK = 16
MOE_TILE_CHUNKS = 2 * MOE_TILE // MOE_CHUNK + N_EXPERTS
MOE_TILE_ROWS = MOE_TILE_CHUNKS * MOE_CHUNK
MOE_BLOCK_CHUNKS = 16
MOE_ROW_W = D_MODEL + 128
MOE_W_PIECES = 3
ATT_SLAB = 256

LANES = 128
ROW_GROUP = 1024
VMEM_LIMIT_BYTES = 56 * 1024 * 1024


def _cparams(*sem):
    return pltpu.CompilerParams(dimension_semantics=sem, vmem_limit_bytes=VMEM_LIMIT_BYTES)


def _split_bf16(x):
    hi = x.astype(BF16)
    lo = (x - hi.astype(F32)).astype(BF16)
    return hi, lo


def _dot(a, b):
    return jnp.dot(a, b, preferred_element_type=F32)


def _dot3(a, b):
    ah, al = _split_bf16(a)
    bh, bl = _split_bf16(b)
    return _dot(ah, bh) + _dot(al, bh) + _dot(ah, bl)


def _dot_nt(a, b):
    return lax.dot_general(a, b, (((1,), (1,)), ((), ())), preferred_element_type=F32)


def _dot_tn(a, b):
    return lax.dot_general(a, b, (((0,), (0,)), ((), ())), preferred_element_type=F32)


def _silu(x):
    return x * jax.nn.sigmoid(x)


def _rms(x, eps=EPS):
    return x * lax.rsqrt(jnp.mean(x * x, axis=-1, keepdims=True) + eps)


def _adaln_kernel(c_ref, w_ref, b_ref, o_ref):
    s = _silu(c_ref[...])
    o_ref[0] = _dot(s.astype(BF16), w_ref[0].astype(BF16)) + b_ref[0]


def _adaln(cond, w_mod, b_mod):
    depth, _, n = w_mod.shape
    rows = cond.shape[0]
    tn = 1536
    return pl.pallas_call(
        _adaln_kernel,
        grid=(depth, n // tn),
        in_specs=[
            pl.BlockSpec((rows, D_MODEL), lambda l, j: (0, 0)),
            pl.BlockSpec((1, D_MODEL, tn), lambda l, j: (l, 0, j)),
            pl.BlockSpec((1, 1, tn), lambda l, j: (l, 0, j)),
        ],
        out_specs=pl.BlockSpec((1, rows, tn), lambda l, j: (l, 0, j)),
        out_shape=jax.ShapeDtypeStruct((depth, rows, n), F32),
        compiler_params=_cparams("arbitrary", "arbitrary"),
        name="adaln",
    )(cond, w_mod, b_mod.reshape(depth, 1, n))


def _norm_proj_kernel(x_ref, m_ref, g_ref, w_ref, *o_refs, splits):
    m = m_ref[0]
    h = _rms(x_ref[0]) * g_ref[...] * (1.0 + m[1:2]) + m[0:1]
    hb = h.astype(BF16)
    for o_ref, (a, b) in zip(o_refs, splits):
        o_ref[0] = _dot(hb, w_ref[:, a:b])


def _norm_proj(x, mods, g, w_bf16, splits, tm=512):
    ngrp = x.shape[0]
    n = w_bf16.shape[1]
    kern = functools.partial(_norm_proj_kernel, splits=splits)
    return pl.pallas_call(
        kern,
        grid=(ngrp, ROW_GROUP // tm),
        in_specs=[
            pl.BlockSpec((1, tm, D_MODEL), lambda b, i: (b, i, 0)),
            pl.BlockSpec((1, 6, D_MODEL), lambda b, i: (b, 0, 0)),
            pl.BlockSpec((1, D_MODEL), lambda b, i: (0, 0)),
            pl.BlockSpec((D_MODEL, n), lambda b, i: (0, 0)),
        ],
        out_specs=[pl.BlockSpec((1, tm, hi - lo), lambda b, i: (b, i, 0)) for lo, hi in splits],
        out_shape=[jax.ShapeDtypeStruct((ngrp, ROW_GROUP, hi - lo), F32) for lo, hi in splits],
        compiler_params=_cparams("arbitrary", "arbitrary"),
        name="norm_proj",
    )(x, mods, g.reshape(1, D_MODEL), w_bf16)


def _hgrn_constants():
    c = CHUNK
    nblk = 1 + len(LEVELS)
    w = np.zeros((2, nblk * c, c), np.float32)
    for t in range(c):
        w[0, t, : t + 1] = 1.0
    for li, r in enumerate(LEVELS):
        for t in range(c):
            mid = (t // r) * r + r // 2
            if t >= mid:
                w[0, (li + 1) * c + t, mid : t + 1] = 1.0
            else:
                w[0, (li + 1) * c + t, t + 1 : mid] = 1.0
    for blk in range(nblk):
        w[1, blk * c : (blk + 1) * c] = w[0, blk * c : (blk + 1) * c][::-1, ::-1]
    m = np.zeros((2, nblk, c, c), np.float32)
    for li, r in enumerate(LEVELS):
        for t in range(c):
            for s in range(c):
                if t // r == s // r and (t % r) >= r // 2 and (s % r) < r // 2:
                    m[0, li, t, s] = 1.0
    for t in range(c):
        for s in range(c):
            if t // SUB == s // SUB and s <= t:
                m[0, nblk - 1, t, s] = 1.0
    m[1] = m[0][:, ::-1, ::-1]
    sel = np.zeros((SUB * DK_A, c), np.float32)
    for i in range(SUB):
        sel[i * DK_A : (i + 1) * DK_A, i::SUB] = 1.0
    return w, m, sel


def _hgrn_chunk(direction, sq, r0, qa_ref, gate_ref, ia_ref, lb, wcum, masks, sel, states):
    rows = pl.ds(r0, CHUNK)
    q = _silu(qa_ref[sq, rows, :])
    v = ia_ref[sq, rows, :]
    f = lb + (1.0 - lb) * jax.nn.sigmoid(gate_ref[sq, rows, :])
    kk = jnp.maximum(1.0 - f, 0.0)
    lf = jnp.log(f) * LOG2E
    lk = jnp.log(kk) * LOG2E
    lf_hi, lf_lo = _split_bf16(lf)
    z = _dot(wcum, lf_hi)
    e = z[0:CHUNK] + _dot(wcum[0:CHUNK], lf_lo)
    edge = e[CHUNK - 1 : CHUNK] if direction == 0 else e[0:1]
    q_in = (q * jnp.exp2(e)).astype(BF16)
    k_st = (kk * jnp.exp2(edge - e)).astype(BF16)
    st_decay = jnp.exp2(edge)
    q_lv, k_lv = [], []
    for li in range(len(LEVELS)):
        ez = jnp.exp2(z[(li + 1) * CHUNK : (li + 2) * CHUNK])
        q_lv.append((q * ez).astype(BF16))
        k_lv.append((kk * ez).astype(BF16))
    nsub = CHUNK // SUB
    c3 = (e - lk).reshape(nsub, SUB, W_A)
    pair = []
    for i in range(SUB):
        cb = jnp.broadcast_to(c3[:, i : i + 1, :], (nsub, SUB, W_A)).reshape(CHUNK, W_A)
        pair.append((q * jnp.exp2(jnp.minimum(e - cb, 0.0))).astype(BF16))
    vb = v.astype(BF16)
    outs, new_states = [], []
    for h in range(H_A):
        hs = slice(h * DK_A, (h + 1) * DK_A)
        sc = _dot(jnp.concatenate([p[:, hs] for p in pair], axis=1), sel) * masks[len(LEVELS)]
        for li in range(len(LEVELS)):
            sc = sc + _dot_nt(q_lv[li][:, hs], k_lv[li][:, hs]) * masks[li]
        st = states[h]
        outs.append(_dot(sc.astype(BF16), vb[:, hs]) + _dot_nt(q_in[:, hs], st.astype(BF16)))
        new_states.append(st_decay[:, hs] * st + _dot_tn(vb[:, hs], k_st[:, hs]))
    return jnp.concatenate(outs, axis=1), tuple(new_states)


def _hgrn_kernel(qa_ref, ff_ref, fb_ref, ia_ref, ga_ref, lb_ref, ng_ref, s0_ref, w_ref, m_ref,
                 sel_ref, o_ref, sout_ref, ob_ref, *, seq_len):
    nchunk = seq_len // CHUNK
    nseq = o_ref.shape[0]
    lb = lb_ref[...]
    sel = sel_ref[...]
    nmask = 1 + len(LEVELS)

    def scan(i, states):
        rf = pl.multiple_of(i * CHUNK, CHUNK)
        rb = pl.multiple_of((nchunk - 1 - i) * CHUNK, CHUNK)
        new_states = []
        for sq in range(nseq):
            o_f, st_f = _hgrn_chunk(0, sq, rf, qa_ref, ff_ref, ia_ref, lb, w_ref[0],
                                    [m_ref[0, j] for j in range(nmask)], sel, states[sq][0])
            o_b, st_b = _hgrn_chunk(1, sq, rb, qa_ref, fb_ref, ia_ref, lb, w_ref[1],
                                    [m_ref[1, j] for j in range(nmask)], sel, states[sq][1])
            o_ref[sq, pl.ds(rf, CHUNK), :] = o_f
            ob_ref[sq, pl.ds(rb, CHUNK), :] = o_b
            new_states.append((st_f, st_b))
        return tuple(new_states)

    init = tuple(tuple(tuple(s0_ref[sq, d, h] for h in range(H_A)) for d in range(2))
                 for sq in range(nseq))
    final = lax.fori_loop(0, nchunk, scan, init)

    ng = jnp.concatenate([ng_ref[...]] * H_A, axis=1)

    def gate(i, carry):
        rows = pl.ds(pl.multiple_of(i * CHUNK, CHUNK), CHUNK)
        for sq in range(nseq):
            tot = o_ref[sq, rows, :] + ob_ref[sq, rows, :]
            normed = jnp.concatenate(
                [_rms(tot[:, h * DK_A : (h + 1) * DK_A]) for h in range(H_A)], axis=1)
            o_ref[sq, rows, :] = normed * ng * _silu(ga_ref[sq, rows, :])
        return carry

    lax.fori_loop(0, nchunk, gate, 0)
    for sq in range(nseq):
        for d in range(2):
            for h in range(H_A):
                sout_ref[sq, d, h] = final[sq][d][h]


def _hgrn(qa, ff, fb, ia, ga, lb, ng, s0t, seq_len, nseq):
    bsz = qa.shape[0]
    w, m, sel = _hgrn_constants()
    seq = pl.BlockSpec((nseq, seq_len, W_A), lambda b: (b, 0, 0))
    full = lambda shape: pl.BlockSpec(shape, lambda b: (0,) * len(shape))
    st_spec = pl.BlockSpec((nseq, 2, H_A, DK_A, DK_A), lambda b: (b, 0, 0, 0, 0))
    return pl.pallas_call(
        functools.partial(_hgrn_kernel, seq_len=seq_len),
        grid=(bsz // nseq,),
        in_specs=[seq, seq, seq, seq, seq, full((1, W_A)), full((1, DK_A)), st_spec,
                  full(w.shape), full(m.shape), full(sel.shape)],
        out_specs=[seq, st_spec],
        out_shape=[jax.ShapeDtypeStruct((bsz, seq_len, W_A), F32),
                   jax.ShapeDtypeStruct((bsz, 2, H_A, DK_A, DK_A), F32)],
        scratch_shapes=[pltpu.VMEM((nseq, seq_len, W_A), F32)],
        compiler_params=_cparams("arbitrary"),
        name="hgrn",
    )(qa, ff, fb, ia, ga, lb.reshape(1, W_A), ng.reshape(1, DK_A), s0t,
      jnp.asarray(w, BF16), jnp.asarray(m, F32), jnp.asarray(sel, BF16))


def _dft_constants(seq_len):
    n = 2 * seq_len
    t = np.arange(seq_len, dtype=np.int64)
    wt = (np.arange(seq_len, dtype=np.int64)[:, None] * t[None, :]) % n
    ang = 2.0 * np.pi * wt.astype(np.float64) / n
    cos, sin = np.cos(ang), np.sin(ang)
    nyq = np.where(t % 2 == 0, 1.0, -1.0)
    sin_p = sin.copy()
    sin_p[0] = nyq
    fwd = np.concatenate([cos, sin_p], axis=0)
    icos = 2.0 * cos.T / n
    icos[:, 0] = 1.0 / n
    isin = 2.0 * sin.T / n
    isin[:, 0] = nyq / n
    inv = np.concatenate([icos, isin], axis=1)
    return fwd, inv


def _filter_embedding(seq_len):
    t = np.linspace(0.0, 1.0, seq_len)[:, None]
    w = 2.0 * np.pi * np.arange(seq_len) / seq_len
    f = np.linspace(1e-4, FILTER_BANDS - 1, FILTER_BANDS)
    ang = w[:, None] * f[None, :]
    z = np.concatenate([t, np.cos(ang), -np.sin(ang)], axis=-1)
    zp = np.zeros((seq_len, LANES), np.float64)
    zp[:, : z.shape[1]] = z
    deltas = np.abs(np.linspace(DECAY_MIN, DECAY_MAX, C_B))
    window = np.exp(-t * deltas[None, :])
    return zp.astype(np.float32), window.astype(np.float32)


def _hyena_filter_kernel(z_ref, win_ref, w1_ref, b1_ref, fr1_ref, w2_ref, b2_ref, fr2_ref, w3_ref,
                         hb_ref, fh_ref, fl_ref, gr_ref, gi_ref, *, seq_len):
    h = jnp.sin(fr1_ref[...] * (_dot3(z_ref[...], w1_ref[...]) + b1_ref[...]))
    h = jnp.sin(fr2_ref[...] * (_dot3(h, w2_ref[...]) + b2_ref[...]))
    h = _dot3(h, w3_ref[...])
    win = win_ref[...]
    hf = h[:, :C_B] * win
    hbk = h[:, C_B:] * win

    def dft(x):
        xh, xl = _split_bf16(x)
        return _dot(fh_ref[...], xh) + _dot(fh_ref[...], xl) + _dot(fl_ref[...], xh)

    p_sum = dft(hf + hbk)
    p_dif = dft(hf - hbk)
    row0 = lax.broadcasted_iota(jnp.int32, (seq_len, C_B), 0) == 0
    gr_ref[...] = p_sum[:seq_len] + hb_ref[...]
    gi_ref[...] = jnp.where(row0, p_sum[seq_len:] + hb_ref[...], p_dif[seq_len:])


def _hyena_filter(seq_len, w1, b1, fr1, w2, b2, fr2, w3, hbias):
    zemb, window = _filter_embedding(seq_len)
    fwd, _ = _dft_constants(seq_len)
    f_hi, f_lo = _split_bf16(jnp.asarray(fwd, F32))

    def pad(a, rows, cols):
        return jnp.zeros((rows, cols), F32).at[: a.shape[0], : a.shape[1]].set(a)

    args = (jnp.asarray(zemb), jnp.asarray(window), pad(w1, LANES, LANES), pad(b1[None], 1, LANES),
            pad(fr1[None], 1, LANES), pad(w2, LANES, LANES), pad(b2[None], 1, LANES),
            pad(fr2[None], 1, LANES), pad(w3, LANES, 2 * C_B), hbias.reshape(1, C_B), f_hi, f_lo)
    return pl.pallas_call(
        functools.partial(_hyena_filter_kernel, seq_len=seq_len),
        out_shape=[jax.ShapeDtypeStruct((seq_len, C_B), F32)] * 2,
        compiler_params=pltpu.CompilerParams(vmem_limit_bytes=VMEM_LIMIT_BYTES),
        name="hyena_filter",
    )(*args)


def _hyena_kernel(x0_ref, x1_ref, v_ref, cw_ref, cb_ref, gr_ref, gi_ref, f_ref, fi_ref, o_ref, *,
                  seq_len):
    tc = o_ref.shape[2]
    row = lax.broadcasted_iota(jnp.int32, (seq_len, tc), 0)

    def short_conv(u_ref, j, sq):
        u = u_ref[sq]
        prev = jnp.where(row == 0, 0.0, pltpu.roll(u, 1, 0))
        nxt = jnp.where(row == seq_len - 1, 0.0, pltpu.roll(u, seq_len - 1, 0))
        cw = cw_ref[j]
        return cw[0:1] * prev + cw[1:2] * u + cw[2:3] * nxt + cb_ref[j]

    gr, gi = gr_ref[...], gi_ref[...]
    row0 = row == 0
    for sq in range(o_ref.shape[0]):
        x0 = short_conv(x0_ref, 0, sq)
        z = short_conv(v_ref, 2, sq) * short_conv(x1_ref, 1, sq)
        p = _dot(f_ref[...], z.astype(BF16))
        a, b = p[:seq_len], p[seq_len:]
        bgi = b * gi
        yr = a * gr - jnp.where(row0, 0.0, bgi)
        yq = jnp.where(row0, bgi, a * gi + b * gr)
        y = _dot(fi_ref[...], jnp.concatenate([yr, yq], axis=0).astype(BF16))
        o_ref[sq] = y * x0


def _hyena(hy, conv_w, conv_b, gr, gi, seq_len, nseq, tc=256):
    bsz = hy.shape[0]
    nct = C_B // tc
    fwd, inv = _dft_constants(seq_len)
    cw = conv_w.reshape(3, 3, C_B).transpose(1, 0, 2)
    cb = conv_b.reshape(3, 1, C_B)
    part = lambda k: pl.BlockSpec((nseq, seq_len, tc), lambda b, j, k=k: (b, 0, k * nct + j))
    return pl.pallas_call(
        functools.partial(_hyena_kernel, seq_len=seq_len),
        grid=(bsz // nseq, nct),
        in_specs=[part(0), part(1), part(2),
                  pl.BlockSpec((3, 3, tc), lambda b, j: (0, 0, j)),
                  pl.BlockSpec((3, 1, tc), lambda b, j: (0, 0, j)),
                  pl.BlockSpec((seq_len, tc), lambda b, j: (0, j)),
                  pl.BlockSpec((seq_len, tc), lambda b, j: (0, j)),
                  pl.BlockSpec((2 * seq_len, seq_len), lambda b, j: (0, 0)),
                  pl.BlockSpec((seq_len, 2 * seq_len), lambda b, j: (0, 0))],
        out_specs=pl.BlockSpec((nseq, seq_len, tc), lambda b, j: (b, 0, j)),
        out_shape=jax.ShapeDtypeStruct((bsz, seq_len, C_B), F32),
        compiler_params=_cparams("arbitrary", "arbitrary"),
        name="hyena",
    )(hy, hy, hy, cw, cb, gr, gi, jnp.asarray(fwd, F32).astype(BF16),
      jnp.asarray(inv, F32).astype(BF16))


def _rope_tables(seq_len, dim):
    rows = seq_len // GRID_W
    row_idx = np.repeat(np.arange(rows), GRID_W).astype(np.float64)
    col_idx = np.tile(np.arange(GRID_W), rows).astype(np.float64)
    half = dim // 2
    inv = ROPE_THETA ** (-np.arange(0, half, 2, dtype=np.float64) / half)
    ang = np.concatenate([row_idx[:, None] * inv, col_idx[:, None] * inv], axis=-1)
    cos = np.repeat(np.cos(ang), 2, axis=1)
    sin = np.repeat(np.sin(ang), 2, axis=1)
    sin[:, 0::2] *= -1.0
    reps = LANES // dim
    return (np.tile(cos, (1, reps)).astype(np.float32), np.tile(sin, (1, reps)).astype(np.float32))


def _lane_group_matrices():
    i = np.arange(2 * LANES)
    same = lambda width: (i[:, None] // width == i[None, :] // width).astype(np.float32)
    return same(HD_C), same(DK_D)


def _qk_proj_kernel(x_ref, m_ref, g_ref, w_ref, cqg_ref, ckg_ref, dqg_ref, dkg_ref, grp_c_ref,
                    grp_d_ref, *rest, use_rope, slabs):
    if use_rope:
        cc_ref, sc_ref, cd_ref, sd_ref = rest[:4]
        rest = rest[4:]
    qc_ref, qd_ref, kc_ref, vc_ref, kd_ref, vd_ref = rest
    m = m_ref[0]
    pair = 2 * LANES
    assert pair == KV_C * HD_C
    two = lambda r: jnp.concatenate([r, r], axis=1)
    rows_per_slab = x_ref.shape[1] // slabs

    for slab in range(slabs):
        rows = pl.ds(slab * rows_per_slab, rows_per_slab)
        seq_rows = qc_ref.shape[1]
        oseq = (slab * rows_per_slab) // seq_rows
        orows = pl.ds((slab * rows_per_slab) % seq_rows, rows_per_slab)
        hb = (_rms(x_ref[0, rows, :]) * g_ref[...] * (1.0 + m[1:2]) + m[0:1]).astype(BF16)
        cols = lambda start, n: _dot(hb, w_ref[:, start * LANES : (start + n) * LANES])

        def norm(x, grp_ref, width, g_ref, cos_ref, sin_ref):
            ms = _dot((x * x).astype(BF16), grp_ref[...]) * (1.0 / width)
            y = x * lax.rsqrt(ms + EPS) * two(g_ref[...])
            if not use_rope:
                return y
            even = (lax.broadcasted_iota(jnp.int32, y.shape, 1) % 2) == 0
            swapped = jnp.where(even, pltpu.roll(y, y.shape[1] - 1, 1), pltpu.roll(y, 1, 1))
            return y * two(cos_ref[rows, :]) + swapped * two(sin_ref[rows, :])

        norm_c = lambda x, g_ref: norm(x, grp_c_ref, HD_C, g_ref, cc_ref if use_rope else None,
                                       sc_ref if use_rope else None)
        norm_d = lambda x, g_ref: norm(x, grp_d_ref, DK_D, g_ref, cd_ref if use_rope else None,
                                       sd_ref if use_rope else None)
        for i in range(H_C // 2):
            qc_ref[oseq, orows, i * pair : (i + 1) * pair] = norm_c(cols(2 * i, 2), cqg_ref)
        kc = norm_c(cols(H_C, KV_C), ckg_ref)
        vc = cols(H_C + KV_C, KV_C)
        for j in range(KV_C):
            kc_ref[oseq, j, orows, :] = kc[:, j * LANES : (j + 1) * LANES]
            vc_ref[oseq, j, orows, :] = vc[:, j * LANES : (j + 1) * LANES]
        base = H_C + 2 * KV_C
        for i in range(H_D // 2):
            qd_ref[oseq, orows, i * pair : (i + 1) * pair] = norm_d(cols(base + 2 * i, 2), dqg_ref)
            kd = norm_d(cols(base + H_D + 2 * i, 2), dkg_ref)
            vd = cols(base + 2 * H_D + 2 * i, 2)
            for j in range(2):
                kd_ref[oseq, 2 * i + j, orows, :] = kd[:, j * LANES : (j + 1) * LANES]
                vd_ref[oseq, 2 * i + j, orows, :] = vd[:, j * LANES : (j + 1) * LANES]


def _qk_proj(x, mods, g, w_bf16, cqg, ckg, dqg, dkg, seq_len, use_rope, tm, slabs):
    ngrp = x.shape[0]
    spg = ROW_GROUP // seq_len
    bsz = ngrp * spg
    seq_rows = min(tm, seq_len)
    spt = tm // seq_rows
    assert seq_len % seq_rows == 0 and (tm // slabs) <= seq_rows
    seq_blk = lambda gi, i: (gi * spg + (i * tm) // seq_len) // spt
    blk_of = lambda i: ((i * tm) % seq_len) // seq_rows
    vec = lambda: pl.BlockSpec((1, LANES), lambda gi, i: (0, 0))
    mat = lambda: pl.BlockSpec((2 * LANES, 2 * LANES), lambda gi, i: (0, 0))
    tab = lambda: pl.BlockSpec((tm, LANES), lambda gi, i: (blk_of(i), 0))
    in_specs = [pl.BlockSpec((1, tm, D_MODEL), lambda gi, i: (gi, i, 0)),
                pl.BlockSpec((1, 6, D_MODEL), lambda gi, i: (gi, 0, 0)),
                pl.BlockSpec((1, D_MODEL), lambda gi, i: (0, 0)),
                pl.BlockSpec(w_bf16.shape, lambda gi, i: (0, 0)),
                vec(), vec(), vec(), vec(), mat(), mat()]
    args = [x, mods, g.reshape(1, D_MODEL), w_bf16, cqg.reshape(1, HD_C), ckg.reshape(1, HD_C),
            jnp.tile(dqg.reshape(1, DK_D), (1, 2)), jnp.tile(dkg.reshape(1, DK_D), (1, 2))]
    args += [jnp.asarray(mm, BF16) for mm in _lane_group_matrices()]
    if use_rope:
        in_specs += [tab(), tab(), tab(), tab()]
        args += [jnp.asarray(t) for t in _rope_tables(seq_len, HD_C) + _rope_tables(seq_len, DK_D)]
    tok = lambda w: pl.BlockSpec((spt, seq_rows, w), lambda gi, i: (seq_blk(gi, i), blk_of(i), 0))
    head = lambda nh: pl.BlockSpec((spt, nh, seq_rows, LANES),
                                   lambda gi, i: (seq_blk(gi, i), 0, blk_of(i), 0))
    tok_shape = lambda w: jax.ShapeDtypeStruct((bsz, seq_len, w), F32)
    head_shape = lambda nh: jax.ShapeDtypeStruct((bsz, nh, seq_len, LANES), F32)
    return pl.pallas_call(
        functools.partial(_qk_proj_kernel, use_rope=use_rope, slabs=slabs),
        grid=(ngrp, ROW_GROUP // tm),
        in_specs=in_specs,
        out_specs=[tok(H_C * HD_C), tok(H_D * 2 * DK_D), head(KV_C), head(KV_C), head(H_D), head(H_D)],
        out_shape=[tok_shape(H_C * HD_C), tok_shape(H_D * 2 * DK_D), head_shape(KV_C),
                   head_shape(KV_C), head_shape(H_D), head_shape(H_D)],
        compiler_params=_cparams("arbitrary", "arbitrary"),
        name="qk_proj",
    )(*args)


def _softmax_pv(q_list, kv_list):
    outs = []
    for q in q_list:
        scores = [_dot_nt(q, k) for k, _ in kv_list]
        mx = scores[0].max(axis=1, keepdims=True)
        for s in scores[1:]:
            mx = jnp.maximum(mx, s.max(axis=1, keepdims=True))
        den = 0.0
        acc = 0.0
        for s, (_, v) in zip(scores, kv_list):
            pexp = jnp.exp2(s - mx)
            den = den + pexp.sum(axis=1, keepdims=True)
            acc = acc + _dot(pexp.astype(BF16), v)
        outs.append(acc / den)
    return outs


def _head_kv(j, k_ref, v_ref, cache_refs):
    kv = [(r_k[0, 0, j].astype(BF16), r_v[0, 0, j].astype(BF16)) for r_k, r_v in cache_refs]
    kv.append((k_ref[0, j].astype(BF16), v_ref[0, j].astype(BF16)))
    return kv


def _per_slab(q_ref, o_ref, cols, fn):
    for r0 in range(0, q_ref.shape[1], ATT_SLAB):
        rows = pl.ds(r0, min(ATT_SLAB, q_ref.shape[1]))
        o_ref[0, rows, cols] = fn(q_ref[0, rows, cols])


def _gqa_kernel(q_ref, k_ref, v_ref, *rest, has_cache):
    cache_refs = [rest[:2]] if has_cache else []
    o_ref = rest[-1]
    g_c = H_C // KV_C
    width = g_c * HD_C
    for j in range(k_ref.shape[1]):
        kv = _head_kv(j, k_ref, v_ref, cache_refs)

        def head(q):
            q = q * (HD_C ** -0.5 * LOG2E)
            qs = [q[:, g * HD_C : (g + 1) * HD_C].astype(BF16) for g in range(g_c)]
            return jnp.concatenate(_softmax_pv(qs, kv), axis=1)

        _per_slab(q_ref, o_ref, slice(j * width, (j + 1) * width), head)


def _diff_kernel(q_ref, k_ref, v_ref, lam_ref, sg_ref, *rest, has_cache, out_scale):
    cache_refs = [rest[:2]] if has_cache else []
    o_ref = rest[-1]
    width = 2 * DK_D
    for j in range(k_ref.shape[1]):
        kv = _head_kv(j, k_ref, v_ref, cache_refs)

        def head(q):
            q = q * (DK_D ** -0.5 * LOG2E)
            low = lax.broadcasted_iota(jnp.int32, q.shape, 1) < DK_D
            qs = [jnp.where(low, q, 0.0).astype(BF16), jnp.where(low, 0.0, q).astype(BF16)]
            o1, o2 = _softmax_pv(qs, kv)
            return _rms(o1 - lam_ref[...] * o2) * sg_ref[...] * out_scale

        _per_slab(q_ref, o_ref, slice(j * width, (j + 1) * width), head)


def _attention(q, k, v, cache_k, cache_v, kernel, extra_args, n_heads, q_width, seq_len, hps, tq):
    bsz = q.shape[0]
    in_specs = [pl.BlockSpec((1, tq, hps * q_width), lambda b, h, i: (b, i, h)),
                pl.BlockSpec((1, hps, seq_len, LANES), lambda b, h, i: (b, h, 0, 0)),
                pl.BlockSpec((1, hps, seq_len, LANES), lambda b, h, i: (b, h, 0, 0))]
    in_specs += [pl.BlockSpec((1, LANES), lambda b, h, i: (0, 0)) for _ in extra_args]
    args = [q, k, v, *extra_args]
    if cache_k is not None:
        past = cache_k.shape[3]
        spec = lambda: pl.BlockSpec((1, 1, hps, past, LANES), lambda b, h, i: (b, 0, h, 0, 0))
        in_specs += [spec(), spec()]
        args += [cache_k, cache_v]
    return pl.pallas_call(
        functools.partial(kernel, has_cache=cache_k is not None),
        grid=(bsz, n_heads // hps, seq_len // tq),
        in_specs=in_specs,
        out_specs=pl.BlockSpec((1, tq, hps * q_width), lambda b, h, i: (b, i, h)),
        out_shape=jax.ShapeDtypeStruct((bsz, seq_len, n_heads * q_width), F32),
        compiler_params=_cparams("arbitrary", "arbitrary", "arbitrary"),
        name="attention",
    )(*args)


def _route(lg):
    lane = lax.broadcasted_iota(jnp.int32, lg.shape, 1).astype(F32)
    neg = -1e30
    is_g = (lane >= N_EXPERTS) & (lane < N_EXPERTS + N_GROUPS)
    gl = jnp.where(is_g, lg, neg)
    gmax = gl.max(axis=1, keepdims=True)
    g_p = 1.0 / jnp.where(is_g, jnp.exp(gl - gmax), 0.0).sum(axis=1, keepdims=True)
    g_i = jnp.where(gl == gmax, lane - N_EXPERTS, 1e9).min(axis=1, keepdims=True)
    in_grp = (lane < N_EXPERTS) & (jnp.floor(lane * (1.0 / EXP_PER_GROUP)) == g_i)
    el = jnp.where(in_grp, lg, neg)
    m1 = el.max(axis=1, keepdims=True)
    i1 = jnp.where(in_grp & (el == m1), lane, 1e9).min(axis=1, keepdims=True)
    el2 = jnp.where(lane == i1, neg, el)
    m2 = el2.max(axis=1, keepdims=True)
    i2 = jnp.where(in_grp & (el2 == m2) & (lane != i1), lane, 1e9).min(axis=1, keepdims=True)
    r = jnp.exp(m2 - m1)
    w1 = g_p / (1.0 + r)
    rec = jnp.where(lane == ROUTE_LANE, i1, 0.0)
    rec = jnp.where(lane == ROUTE_LANE + 1, i2, rec)
    rec = jnp.where(lane == ROUTE_LANE + 2, w1, rec)
    return jnp.where(lane == ROUTE_LANE + 3, w1 * r, rec)


def _mix_out_kernel(x_ref, a_ref, b_ref, m_ref, g2_ref, w_ref, wr_ref, br_ref, tri_ref,
                    x1_ref, xs_ref, pos_ref, cnt_ref, *, parts):
    wa = a_ref.shape[2]
    m = m_ref[0]
    rows_per_part = x_ref.shape[1] // parts
    h2_parts, route_parts = [], []
    for part in range(parts):
        rows = pl.ds(part * rows_per_part, rows_per_part)
        o = (_dot(a_ref[0, rows, :].astype(BF16), w_ref[:wa])
             + _dot(b_ref[0, rows, :].astype(BF16), w_ref[wa:]))
        x1 = x_ref[0, rows, :] + m[2:3] * o
        x1_ref[0, rows, :] = x1
        h2 = _rms(x1) * g2_ref[...] * (1.0 + m[4:5]) + m[3:4]
        h2_parts.append(h2.astype(BF16))
        r = _dot(jnp.concatenate(_split_bf16(h2), axis=0), wr_ref[...])
        hi_rows, lo_rows = r[:rows_per_part], r[rows_per_part:]
        lg = hi_rows[:, :LANES] + hi_rows[:, LANES:] + lo_rows[:, :LANES] + br_ref[...]
        route_parts.append(_route(lg))
    xs, pos, cnt = _moe_sort_tile(jnp.concatenate(h2_parts, axis=0),
                                  jnp.concatenate(route_parts, axis=0), tri_ref[...])
    xs_ref[...] = xs
    pos_ref[0] = pos
    cnt_ref[0] = cnt


def _mix_out(x, a, b, mods, g2, w_out_bf16, w_router, b_router, parts=2):
    ngrp = x.shape[0]
    per = ROW_GROUP // MOE_TILE
    ntile = ngrp * per
    wa, wb = a.shape[2], b.shape[2]
    wr = jnp.concatenate(_split_bf16(w_router), axis=1)
    tri = np.tril(np.ones((MOE_TILE, MOE_TILE), np.float32), -1)
    src = lambda j: jnp.minimum(j, ntile - 1)
    row = lambda w: pl.BlockSpec((1, MOE_TILE, w), lambda j: (src(j) // per, src(j) % per, 0))
    full = lambda shape: pl.BlockSpec(shape, lambda j: (0,) * len(shape))
    return pl.pallas_call(
        functools.partial(_mix_out_kernel, parts=parts),
        grid=(ntile + 1,),
        in_specs=[row(D_MODEL), row(wa), row(wb),
                  pl.BlockSpec((1, 6, D_MODEL), lambda j: (src(j) // per, 0, 0)),
                  full((1, D_MODEL)), full((wa + wb, D_MODEL)),
                  full((D_MODEL, 2 * LANES)), full((1, LANES)), full((MOE_TILE, MOE_TILE))],
        out_specs=[pl.BlockSpec((1, MOE_TILE, D_MODEL), lambda j: (j, 0, 0)),
                   pl.BlockSpec((MOE_TILE_ROWS, MOE_ROW_W), lambda j: (j, 0)),
                   pl.BlockSpec((1, MOE_TILE, LANES), lambda j: (j, 0, 0)),
                   pl.BlockSpec((1, 8, LANES), lambda j: (j, 0, 0))],
        out_shape=[jax.ShapeDtypeStruct((ntile + 1, MOE_TILE, D_MODEL), F32),
                   jax.ShapeDtypeStruct(((ntile + 1) * MOE_TILE_ROWS, MOE_ROW_W), BF16),
                   jax.ShapeDtypeStruct((ntile + 1, MOE_TILE, LANES), F32),
                   jax.ShapeDtypeStruct((ntile + 1, 8, LANES), F32)],
        compiler_params=_cparams("arbitrary"),
        name="mix_out",
    )(x, a, b, mods, g2.reshape(1, D_MODEL), w_out_bf16, wr, b_router, jnp.asarray(tri, BF16))


def _lane_col(x, lane, k):
    return jnp.where(lane == k, x, 0.0).sum(axis=1, keepdims=True)


def _moe_sort_tile(h, r, tri):
    lane = lax.broadcasted_iota(jnp.int32, r.shape, 1).astype(F32)
    i1, i2, w1, w2 = [_lane_col(r, lane, ROUTE_LANE + k) for k in range(4)]
    oh1 = lane == i1
    oh2 = lane == i2
    oh = jnp.where(oh1 | oh2, 1.0, 0.0)
    rank = _dot(tri, oh.astype(BF16))
    cnt = oh.sum(axis=0, keepdims=True)
    chunks = jnp.floor((cnt + (MOE_CHUNK - 1)) * (1.0 / MOE_CHUNK))
    li = lax.broadcasted_iota(jnp.int32, (LANES, LANES), 0)
    lj = lax.broadcasted_iota(jnp.int32, (LANES, LANES), 1)
    before = jnp.where(li < lj, 1.0, 0.0).astype(BF16)
    seg = _dot(jnp.broadcast_to(chunks, (8, LANES)).astype(BF16), before)[0:1]
    base = seg * MOE_CHUNK + rank
    pos1 = jnp.where(oh1, base, 0.0).sum(axis=1, keepdims=True)
    pos2 = jnp.where(oh2, base, 0.0).sum(axis=1, keepdims=True)
    riota = lax.broadcasted_iota(jnp.int32, (r.shape[0], MOE_TILE_ROWS), 1).astype(F32)
    p = jnp.where((riota == pos1) | (riota == pos2), 1.0, 0.0).astype(BF16)

    aux = jnp.where(lane == 2 * MOE_W_PIECES, i1, 0.0)
    for k, w in enumerate((w1, w2)):
        rest = w
        for piece in range(MOE_W_PIECES):
            part = rest.astype(BF16).astype(F32)
            aux = jnp.where(lane == k * MOE_W_PIECES + piece, part, aux)
            rest = rest - part
    row = jnp.concatenate([h, aux.astype(BF16)], axis=1)
    xs = _dot_tn(p, row).astype(BF16)
    pos = jnp.where(lane == 0, pos1, jnp.where(lane == 1, pos2, 0.0))
    return xs, pos, jnp.broadcast_to(cnt, (8, LANES))


def _moe_tables(cnt, ntile):
    nblk = ntile * MOE_TILE_CHUNKS // MOE_BLOCK_CHUNKS + N_EXPERTS
    chunks = (cnt + MOE_CHUNK - 1) // MOE_CHUNK
    seg_start = jnp.cumsum(chunks, axis=1) - chunks
    tile_prefix = jnp.cumsum(chunks, axis=0) - chunks
    per_expert = chunks.sum(axis=0)
    blocks = (per_expert + MOE_BLOCK_CHUNKS - 1) // MOE_BLOCK_CHUNKS
    blk_end = jnp.cumsum(blocks)
    n_used = blk_end[-1]
    b = jnp.arange(nblk + 1, dtype=jnp.int32)
    blk_e = jnp.sum(b[:, None] >= blk_end[None, :], axis=1).astype(jnp.int32)
    last_e = jnp.sum((n_used - 1) >= blk_end).astype(jnp.int32)
    blk_e = jnp.clip(jnp.where(b < n_used, blk_e, last_e), 0, N_EXPERTS - 1)
    oh_e = (blk_e[:, None] == jnp.arange(N_EXPERTS)[None, :]).astype(jnp.int32)
    pick = lambda per_tile: jnp.sum(oh_e[:, :, None] * per_tile.T[None], axis=1)
    seg_e, pre_e, chunks_e = pick(seg_start), pick(tile_prefix), pick(chunks)
    first_blk = jnp.sum(oh_e * (blk_end - blocks)[None, :], axis=1)
    k = (b - first_blk)[:, None] * MOE_BLOCK_CHUNKS + jnp.arange(MOE_BLOCK_CHUNKS)[None, :]
    k3 = k[:, :, None]
    in_tile = (pre_e[:, None, :] <= k3) & (k3 < (pre_e + chunks_e)[:, None, :])
    tile_base = (jnp.arange(ntile) * MOE_TILE_CHUNKS)[None, :] + seg_e - pre_e
    src = jnp.sum(jnp.where(in_tile, tile_base[:, None, :] + k3, 0), axis=-1)
    valid = jnp.any(in_tile, axis=-1) & (b < n_used)[:, None]
    slot_c = jnp.arange(MOE_BLOCK_CHUNKS)[None, :]
    n_read = MOE_TILE_CHUNKS - 2 * MOE_BLOCK_CHUNKS
    assert n_read > 0
    spare = ntile * MOE_TILE_CHUNKS
    gather = jnp.where(valid, src, spare + 2 * MOE_BLOCK_CHUNKS + slot_c % n_read).astype(jnp.int32)
    scatter = jnp.where(valid, src, spare + (b % 2)[:, None] * MOE_BLOCK_CHUNKS + slot_c).astype(jnp.int32)
    blk_start = jnp.concatenate([blk_end - blocks, n_used[None]]).astype(jnp.int32)
    return blk_start, gather.reshape(-1), scatter.reshape(-1)


def _moe_expert_kernel(start_ref, gather_ref, scatter_ref, xs_hbm, wg_ref, wu_ref, wd_ref, ys_hbm,
                       lhs, obuf, wgb, wub, wdb, in_sem, out_sem):
    e = pl.program_id(0)
    ne = pl.num_programs(0)
    b0 = start_ref[e]
    b1 = start_ref[e + 1]
    n = start_ref[ne]

    def chunk_copy(blk, slot, c, gather):
        rows = pl.ds(c * MOE_CHUNK, MOE_CHUNK)
        if gather:
            idx = gather_ref[blk * MOE_BLOCK_CHUNKS + c]
            return pltpu.make_async_copy(xs_hbm.at[idx], lhs.at[slot, rows], in_sem.at[slot])
        idx = scatter_ref[blk * MOE_BLOCK_CHUNKS + c]
        dst = ys_hbm.at[idx, pl.ds(0, MOE_CHUNK), pl.ds(0, D_MODEL)]
        return pltpu.make_async_copy(obuf.at[slot, rows], dst, out_sem.at[slot])

    def for_chunks(blk, slot, gather, start):
        for c in range(MOE_BLOCK_CHUNKS):
            cp = chunk_copy(blk, slot, c, gather)
            if start:
                cp.start()
            else:
                cp.wait()

    @pl.when((e == 0) & (n > 0))
    def _():
        for_chunks(0, 0, True, True)

    @pl.when(b1 > b0)
    def _():
        wgb[...] = wg_ref[0, 0].astype(BF16)
        wub[...] = wu_ref[0, 0].astype(BF16)
        wdb[...] = wd_ref[0, 0].astype(BF16)
        e_f32 = e.astype(F32)

        def block(b, carry):
            slot = b % 2

            @pl.when(b >= 2)
            def _():
                for_chunks(b - 2, slot, False, False)

            for_chunks(b, slot, True, False)
            for_chunks(b + 1, 1 - slot, True, True)
            xa = lhs[slot]
            x = xa[:, :D_MODEL]
            aux = xa[:, D_MODEL:].astype(F32)
            lane = lax.broadcasted_iota(jnp.int32, aux.shape, 1)
            first = lane < MOE_W_PIECES
            w_first = jnp.where(first, aux, 0.0).sum(axis=1, keepdims=True)
            w_second = jnp.where(first | (lane >= 2 * MOE_W_PIECES), 0.0, aux).sum(axis=1, keepdims=True)
            e_first = _lane_col(aux, lane, 2 * MOE_W_PIECES)
            w = jnp.where(e_first == e_f32, w_first, w_second)
            hid = _silu(_dot(x, wgb[...])) * _dot(x, wub[...]) * w
            obuf[slot] = _dot(hid.astype(BF16), wdb[...]).astype(BF16)
            for_chunks(b, slot, False, True)
            return carry

        lax.fori_loop(b0, b1, block, 0)

    @pl.when(e == ne - 1)
    def _():
        @pl.when(n >= 2)
        def _():
            for_chunks(n - 2, n % 2, False, False)

        @pl.when(n >= 1)
        def _():
            for_chunks(n, n % 2, True, False)
            for_chunks(n - 1, (n - 1) % 2, False, False)


def _moe_experts(xs, blk_start, gather, scatter, layer, w_gate, w_up, w_down):
    nchunk = xs.shape[0] // MOE_CHUNK
    rows_per_blk = MOE_BLOCK_CHUNKS * MOE_CHUNK
    wspec = lambda shape: pl.BlockSpec((1, 1) + shape, lambda e, st, g, s: (layer, e, 0, 0))
    hbm = pl.BlockSpec(memory_space=pl.ANY)
    ys = pl.pallas_call(
        _moe_expert_kernel,
        grid_spec=pltpu.PrefetchScalarGridSpec(
            num_scalar_prefetch=3,
            grid=(N_EXPERTS,),
            in_specs=[hbm, wspec((D_MODEL, D_EXPERT)), wspec((D_MODEL, D_EXPERT)),
                      wspec((D_EXPERT, D_MODEL))],
            out_specs=hbm,
            scratch_shapes=[pltpu.VMEM((2, rows_per_blk, MOE_ROW_W), BF16),
                            pltpu.VMEM((2, rows_per_blk, D_MODEL), BF16),
                            pltpu.VMEM((D_MODEL, D_EXPERT), BF16),
                            pltpu.VMEM((D_MODEL, D_EXPERT), BF16),
                            pltpu.VMEM((D_EXPERT, D_MODEL), BF16),
                            pltpu.SemaphoreType.DMA((2,)),
                            pltpu.SemaphoreType.DMA((2,))]),
        out_shape=jax.ShapeDtypeStruct((nchunk, MOE_CHUNK, MOE_ROW_W), BF16),
        input_output_aliases={3: 0},
        compiler_params=_cparams("arbitrary"),
        name="moe_experts",
    )(blk_start, gather, scatter, xs.reshape(nchunk, MOE_CHUNK, MOE_ROW_W), w_gate, w_up, w_down)
    return ys.reshape(nchunk * MOE_CHUNK, MOE_ROW_W)


def _moe_combine_kernel(ys_ref, pos_ref, x_ref, m_ref, o_ref):
    pos = pos_ref[0]
    lane = lax.broadcasted_iota(jnp.int32, pos.shape, 1)
    pos1 = _lane_col(pos, lane, 0)
    pos2 = _lane_col(pos, lane, 1)
    riota = lax.broadcasted_iota(jnp.int32, (pos.shape[0], MOE_TILE_ROWS), 1).astype(F32)
    p = jnp.where((riota == pos1) | (riota == pos2), 1.0, 0.0).astype(BF16)
    o_ref[0] = x_ref[0] + m_ref[0][5:6] * _dot(p, ys_ref[...])


def _moe_combine(ys, pos, x1, mods):
    ngrp = mods.shape[0]
    per = ROW_GROUP // MOE_TILE
    tile = pl.BlockSpec((1, MOE_TILE, D_MODEL), lambda j: (j, 0, 0))
    return pl.pallas_call(
        _moe_combine_kernel,
        grid=(ngrp * per,),
        in_specs=[pl.BlockSpec((MOE_TILE_ROWS, D_MODEL), lambda j: (j, 0)),
                  pl.BlockSpec((1, MOE_TILE, LANES), lambda j: (j, 0, 0)),
                  tile, pl.BlockSpec((1, 6, D_MODEL), lambda j: (j // per, 0, 0))],
        out_specs=pl.BlockSpec((1, MOE_TILE, D_MODEL), lambda j: (j // per, j % per, 0)),
        out_shape=jax.ShapeDtypeStruct((ngrp, ROW_GROUP, D_MODEL), F32),
        compiler_params=_cparams("arbitrary"),
        name="moe_combine",
    )(ys, pos, x1, mods)


def _moe(xs, pos, cnt, x1, mods, layer, w_gate, w_up, w_down):
    ntile = mods.shape[0] * (ROW_GROUP // MOE_TILE)
    cnt = cnt[:ntile, 0, :N_EXPERTS].astype(jnp.int32)
    blk_start, gather, scatter = _moe_tables(cnt, ntile)
    ys = _moe_experts(xs, blk_start, gather, scatter, layer, w_gate, w_up, w_down)
    return _moe_combine(ys, pos, x1, mods)


def kernel(x_prompt, x_sample, state_hgrn, cache_c_k, cache_c_v, cache_d_k, cache_d_v, c, c_ctx, norm1_g, norm2_g, w_mod, b_mod, even_w_in, even_w_out, hgrn_lower, hgrn_norm_g, hy_conv_w, hy_conv_b, hy_w1, hy_b1, hy_freq1, hy_w2, hy_b2, hy_freq2, hy_w3, hy_bias, odd_w_in, odd_w_out, c_qnorm_g, c_knorm_g, d_qnorm_g, d_knorm_g, d_lambda_q1, d_lambda_k1, d_lambda_q2, d_lambda_k2, d_subln_g, moe_w_grp, moe_b_grp, moe_w_rt, moe_b_rt, moe_w_gate, moe_w_up, moe_w_down):
    depth = w_mod.shape[0]
    n_ctx, seq, _ = x_prompt.shape
    n_lat, dec_seq, _ = x_sample.shape
    g_ctx = n_ctx * seq // ROW_GROUP
    g_lat = n_lat * dec_seq // ROW_GROUP
    assert dec_seq == ROW_GROUP and ROW_GROUP % seq == 0

    cond = jnp.zeros((16, D_MODEL), F32).at[0].set(c_ctx).at[1 : 1 + n_lat].set(c)
    mods = _adaln(cond, w_mod, b_mod).reshape(depth, 16, 6, D_MODEL)
    lower = jnp.cumsum(jax.nn.softmax(hgrn_lower.astype(F32), axis=0), axis=0)

    streams = [
        dict(x=x_prompt.reshape(g_ctx, ROW_GROUP, D_MODEL), ngrp=g_ctx, bsz=n_ctx, seq=seq, ctx=True),
        dict(x=x_sample, ngrp=g_lat, bsz=n_lat, seq=dec_seq, ctx=False),
    ]
    new_state, new_ck, new_cv, new_dk, new_dv = [], [], [], [], []

    for l in range(depth):
        j = l // 2
        w_router = jnp.zeros((D_MODEL, LANES), F32)
        w_router = w_router.at[:, :N_EXPERTS].set(moe_w_rt[l])
        w_router = w_router.at[:, N_EXPERTS : N_EXPERTS + N_GROUPS].set(moe_w_grp[l])
        b_router = jnp.zeros((1, LANES), F32)
        b_router = b_router.at[0, :N_EXPERTS].set(moe_b_rt[l])
        b_router = b_router.at[0, N_EXPERTS : N_EXPERTS + N_GROUPS].set(moe_b_grp[l])
        if l % 2 == 0:
            w_in = even_w_in[j].astype(BF16)
            w_out = even_w_out[j].astype(BF16)
        else:
            w_in = odd_w_in[j].astype(BF16)
            w_out = odd_w_out[j].astype(BF16)
            lam_init = 0.8 - 0.6 * math.exp(-0.3 * l)
            lam = (jnp.exp(jnp.sum(d_lambda_q1[j] * d_lambda_k1[j]))
                   - jnp.exp(jnp.sum(d_lambda_q2[j] * d_lambda_k2[j])) + lam_init)
            lam_row = jnp.full((1, LANES), lam, F32)

        for s in streams:
            ngrp, bsz, sl = s["ngrp"], s["bsz"], s["seq"]
            if s["ctx"]:
                m = jnp.broadcast_to(mods[l, 0][None], (ngrp, 6, D_MODEL))
            else:
                m = mods[l, 1 : 1 + ngrp]
            x = s["x"]
            if l % 2 == 0:
                wa = H_A * DK_A
                splits = [(0, wa), (wa, 2 * wa), (2 * wa, 3 * wa), (3 * wa, 3 * wa + W_A),
                          (3 * wa + W_A, 3 * wa + 2 * W_A), (3 * wa + 2 * W_A, w_in.shape[1])]
                qa, ffa, fba, ia, ga, hy = _norm_proj(x, m, norm1_g[l], w_in, splits)
                per_seq = lambda t: t.reshape(bsz, sl, t.shape[-1])
                if s["ctx"]:
                    s0t = jnp.zeros((bsz, 2, H_A, DK_A, DK_A), F32)
                else:
                    s0t = jnp.swapaxes(state_hgrn[:, j].astype(F32), -1, -2)
                mix_a, s_fin = _hgrn(per_seq(qa), per_seq(ffa), per_seq(fba), per_seq(ia),
                                     per_seq(ga), lower[j], hgrn_norm_g[j], s0t, sl,
                                     nseq=2 if s["ctx"] else 1)
                gr, gi = _hyena_filter(sl, hy_w1[j], hy_b1[j], hy_freq1[j], hy_w2[j], hy_b2[j],
                                       hy_freq2[j], hy_w3[j], hy_bias[j])
                mix_b = _hyena(per_seq(hy), hy_conv_w[j], hy_conv_b[j], gr, gi, sl,
                               nseq=4 if s["ctx"] else 2)
                if s["ctx"]:
                    new_state.append(jnp.swapaxes(s_fin, -1, -2))
            else:
                qc, qd, kc, vc, kd, vd = _qk_proj(x, m, norm1_g[l], w_in, c_qnorm_g[j], c_knorm_g[j],
                                                  d_qnorm_g[j], d_knorm_g[j], sl, use_rope=not s["ctx"],
                                                  tm=512, slabs=2)
                if s["ctx"]:
                    caches = (None, None, None, None)
                    new_ck.append(kc)
                    new_cv.append(vc)
                    new_dk.append(kd)
                    new_dv.append(vd)
                else:
                    caches = (cache_c_k[:, j : j + 1], cache_c_v[:, j : j + 1],
                              cache_d_k[:, j : j + 1], cache_d_v[:, j : j + 1])
                hps_c, hps_d, tq = (KV_C, H_D, sl) if s["ctx"] else (1, 1, sl)
                mix_a = _attention(qc, kc, vc, caches[0], caches[1], _gqa_kernel, (), KV_C,
                                   (H_C // KV_C) * HD_C, sl, hps_c, tq)
                diff = functools.partial(_diff_kernel, out_scale=1.0 - lam_init)
                mix_b = _attention(qd, kd, vd, caches[2], caches[3], diff,
                                   (lam_row, d_subln_g[j].reshape(1, DV_D)), H_D, 2 * DK_D, sl,
                                   hps_d, tq)
            grp = lambda t: t.reshape(ngrp, ROW_GROUP, t.shape[-1])
            x1, xs, pos, cnt = _mix_out(x, grp(mix_a), grp(mix_b), m, norm2_g[l], w_out, w_router,
                                        b_router)
            s["x"] = _moe(xs, pos, cnt, x1, m, l, moe_w_gate, moe_w_up, moe_w_down)

    y_ctx = streams[0]["x"].reshape(n_ctx, seq, D_MODEL)
    y_lat = streams[1]["x"]
    return (y_ctx, y_lat, jnp.stack(new_state, axis=1), jnp.stack(new_ck, axis=1),
            jnp.stack(new_cv, axis=1), jnp.stack(new_dk, axis=1), jnp.stack(new_dv, axis=1))
```

```python
import functools
import math

import numpy as np
import jax
import jax.numpy as jnp
from jax import lax
from jax.experimental import pallas as pl
from jax.experimental.pallas import tpu as pltpu

F32 = jnp.float32
BF16 = jnp.bfloat16

D_MODEL = 1024
EPS = 1e-6
LOG2E = 1.0 / math.log(2.0)
GRID_W = 64
ROPE_THETA = 10000.0
H_A = 4
DK_A = 128
W_A = 512
CHUNK = 128
SUB = 8
LEVELS = (16, 32, 64, 128)
C_B = 512
FILTER_BANDS = 16
DECAY_MIN = math.log(1e-2) / 1.5
DECAY_MAX = math.log(1e-2) / 0.3
H_C = 4
KV_C = 2
HD_C = 128
H_D = 4
DK_D = 64
DV_D = 128
N_GROUPS = 4
EXP_PER_GROUP = 4
N_EXPERTS = 16
D_EXPERT = 512

ROUTE_LANE = N_EXPERTS + N_GROUPS
MOE_TILE = 512
MOE_CHUNK = 16
MOE_TILE_CHUNKS = 2 * MOE_TILE // MOE_CHUNK + N_EXPERTS
MOE_TILE_ROWS = MOE_TILE_CHUNKS * MOE_CHUNK
MOE_BLOCK_CHUNKS = 16
MOE_ROW_W = D_MODEL + 128
MOE_W_PIECES = 3
ATT_SLAB = 256

LANES = 128
ROW_GROUP = 1024
VMEM_LIMIT_BYTES = 56 * 1024 * 1024


def _cparams(*sem):
    return pltpu.CompilerParams(dimension_semantics=sem, vmem_limit_bytes=VMEM_LIMIT_BYTES)


def _split_bf16(x):
    hi = x.astype(BF16)
    lo = (x - hi.astype(F32)).astype(BF16)
    return hi, lo


def _dot(a, b):
    return jnp.dot(a, b, preferred_element_type=F32)


def _dot3(a, b):
    ah, al = _split_bf16(a)
    bh, bl = _split_bf16(b)
    return _dot(ah, bh) + _dot(al, bh) + _dot(ah, bl)


def _dot_nt(a, b):
    return lax.dot_general(a, b, (((1,), (1,)), ((), ())), preferred_element_type=F32)


def _dot_tn(a, b):
    return lax.dot_general(a, b, (((0,), (0,)), ((), ())), preferred_element_type=F32)


def _silu(x):
    return x * jax.nn.sigmoid(x)


def _rms(x, eps=EPS):
    return x * lax.rsqrt(jnp.mean(x * x, axis=-1, keepdims=True) + eps)


def _adaln_kernel(c_ref, w_ref, b_ref, o_ref):
    s = _silu(c_ref[...])
    o_ref[0] = _dot(s.astype(BF16), w_ref[0].astype(BF16)) + b_ref[0]


def _adaln(cond, w_mod, b_mod):
    depth, _, n = w_mod.shape
    rows = cond.shape[0]
    tn = 1536
    return pl.pallas_call(
        _adaln_kernel,
        grid=(depth, n // tn),
        in_specs=[
            pl.BlockSpec((rows, D_MODEL), lambda l, j: (0, 0)),
            pl.BlockSpec((1, D_MODEL, tn), lambda l, j: (l, 0, j)),
            pl.BlockSpec((1, 1, tn), lambda l, j: (l, 0, j)),
        ],
        out_specs=pl.BlockSpec((1, rows, tn), lambda l, j: (l, 0, j)),
        out_shape=jax.ShapeDtypeStruct((depth, rows, n), F32),
        compiler_params=_cparams("arbitrary", "arbitrary"),
        name="adaln",
    )(cond, w_mod, b_mod.reshape(depth, 1, n))


def _norm_proj_kernel(x_ref, m_ref, g_ref, w_ref, *o_refs, splits):
    m = m_ref[0]
    h = _rms(x_ref[0]) * g_ref[...] * (1.0 + m[1:2]) + m[0:1]
    hb = h.astype(BF16)
    for o_ref, (a, b) in zip(o_refs, splits):
        o_ref[0] = _dot(hb, w_ref[:, a:b])


def _norm_proj(x, mods, g, w_bf16, splits, tm=512):
    ngrp = x.shape[0]
    n = w_bf16.shape[1]
    kern = functools.partial(_norm_proj_kernel, splits=splits)
    return pl.pallas_call(
        kern,
        grid=(ngrp, ROW_GROUP // tm),
        in_specs=[
            pl.BlockSpec((1, tm, D_MODEL), lambda b, i: (b, i, 0)),
            pl.BlockSpec((1, 6, D_MODEL), lambda b, i: (b, 0, 0)),
            pl.BlockSpec((1, D_MODEL), lambda b, i: (0, 0)),
            pl.BlockSpec((D_MODEL, n), lambda b, i: (0, 0)),
        ],
        out_specs=[pl.BlockSpec((1, tm, hi - lo), lambda b, i: (b, i, 0)) for lo, hi in splits],
        out_shape=[jax.ShapeDtypeStruct((ngrp, ROW_GROUP, hi - lo), F32) for lo, hi in splits],
        compiler_params=_cparams("arbitrary", "arbitrary"),
        name="norm_proj",
    )(x, mods, g.reshape(1, D_MODEL), w_bf16)


def _hgrn_constants():
    c = CHUNK
    nblk = 1 + len(LEVELS)
    w = np.zeros((2, nblk * c, c), np.float32)
    for t in range(c):
        w[0, t, : t + 1] = 1.0
    for li, r in enumerate(LEVELS):
        for t in range(c):
            mid = (t // r) * r + r // 2
            if t >= mid:
                w[0, (li + 1) * c + t, mid : t + 1] = 1.0
            else:
                w[0, (li + 1) * c + t, t + 1 : mid] = 1.0
    for blk in range(nblk):
        w[1, blk * c : (blk + 1) * c] = w[0, blk * c : (blk + 1) * c][::-1, ::-1]
    m = np.zeros((2, nblk, c, c), np.float32)
    for li, r in enumerate(LEVELS):
        for t in range(c):
            for s in range(c):
                if t // r == s // r and (t % r) >= r // 2 and (s % r) < r // 2:
                    m[0, li, t, s] = 1.0
    for t in range(c):
        for s in range(c):
            if t // SUB == s // SUB and s <= t:
                m[0, nblk - 1, t, s] = 1.0
    m[1] = m[0][:, ::-1, ::-1]
    sel = np.zeros((SUB * DK_A, c), np.float32)
    for i in range(SUB):
        sel[i * DK_A : (i + 1) * DK_A, i::SUB] = 1.0
    return w, m, sel


def _hgrn_chunk(direction, sq, r0, qa_ref, gate_ref, ia_ref, lb, wcum, masks, sel, states):
    rows = pl.ds(r0, CHUNK)
    q = _silu(qa_ref[sq, rows, :])
    v = ia_ref[sq, rows, :]
    f = lb + (1.0 - lb) * jax.nn.sigmoid(gate_ref[sq, rows, :])
    kk = jnp.maximum(1.0 - f, 0.0)
    lf = jnp.log(f) * LOG2E
    lk = jnp.log(kk) * LOG2E
    lf_hi, lf_lo = _split_bf16(lf)
    z = _dot(wcum, lf_hi)
    e = z[0:CHUNK] + _dot(wcum[0:CHUNK], lf_lo)
    edge = e[CHUNK - 1 : CHUNK] if direction == 0 else e[0:1]
    q_in = (q * jnp.exp2(e)).astype(BF16)
    k_st = (kk * jnp.exp2(edge - e)).astype(BF16)
    st_decay = jnp.exp2(edge)
    q_lv, k_lv = [], []
    for li in range(len(LEVELS)):
        ez = jnp.exp2(z[(li + 1) * CHUNK : (li + 2) * CHUNK])
        q_lv.append((q * ez).astype(BF16))
        k_lv.append((kk * ez).astype(BF16))
    nsub = CHUNK // SUB
    c3 = (e - lk).reshape(nsub, SUB, W_A)
    pair = []
    for i in range(SUB):
        cb = jnp.broadcast_to(c3[:, i : i + 1, :], (nsub, SUB, W_A)).reshape(CHUNK, W_A)
        pair.append((q * jnp.exp2(jnp.minimum(e - cb, 0.0))).astype(BF16))
    vb = v.astype(BF16)
    outs, new_states = [], []
    for h in range(H_A):
        hs = slice(h * DK_A, (h + 1) * DK_A)
        sc = _dot(jnp.concatenate([p[:, hs] for p in pair], axis=1), sel) * masks[len(LEVELS)]
        for li in range(len(LEVELS)):
            sc = sc + _dot_nt(q_lv[li][:, hs], k_lv[li][:, hs]) * masks[li]
        st = states[h]
        outs.append(_dot(sc.astype(BF16), vb[:, hs]) + _dot_nt(q_in[:, hs], st.astype(BF16)))
        new_states.append(st_decay[:, hs] * st + _dot_tn(vb[:, hs], k_st[:, hs]))
    return jnp.concatenate(outs, axis=1), tuple(new_states)


def _hgrn_kernel(qa_ref, ff_ref, fb_ref, ia_ref, ga_ref, lb_ref, ng_ref, w_ref, m_ref, sel_ref,
                 *rest, seq_len, has_state):
    s0_ref = rest[0] if has_state else None
    o_ref, sout_ref, ob_ref = rest[-3:]
    nchunk = seq_len // CHUNK
    nseq = o_ref.shape[0]
    lb = lb_ref[...]
    sel = sel_ref[...]
    nmask = 1 + len(LEVELS)

    def scan(i, states):
        rf = pl.multiple_of(i * CHUNK, CHUNK)
        rb = pl.multiple_of((nchunk - 1 - i) * CHUNK, CHUNK)
        new_states = []
        for sq in range(nseq):
            o_f, st_f = _hgrn_chunk(0, sq, rf, qa_ref, ff_ref, ia_ref, lb, w_ref[0],
                                    [m_ref[0, j] for j in range(nmask)], sel, states[sq][0])
            o_b, st_b = _hgrn_chunk(1, sq, rb, qa_ref, fb_ref, ia_ref, lb, w_ref[1],
                                    [m_ref[1, j] for j in range(nmask)], sel, states[sq][1])
            o_ref[sq, pl.ds(rf, CHUNK), :] = o_f
            ob_ref[sq, pl.ds(rb, CHUNK), :] = o_b
            new_states.append((st_f, st_b))
        return tuple(new_states)

    start = lambda sq, d, h: s0_ref[sq, d, h].T if has_state else jnp.zeros((DK_A, DK_A), F32)
    init = tuple(tuple(tuple(start(sq, d, h) for h in range(H_A)) for d in range(2))
                 for sq in range(nseq))
    final = lax.fori_loop(0, nchunk, scan, init)

    ng = jnp.concatenate([ng_ref[...]] * H_A, axis=1)

    def gate(i, carry):
        rows = pl.ds(pl.multiple_of(i * CHUNK, CHUNK), CHUNK)
        for sq in range(nseq):
            tot = o_ref[sq, rows, :] + ob_ref[sq, rows, :]
            normed = jnp.concatenate(
                [_rms(tot[:, h * DK_A : (h + 1) * DK_A]) for h in range(H_A)], axis=1)
            o_ref[sq, rows, :] = normed * ng * _silu(ga_ref[sq, rows, :])
        return carry

    lax.fori_loop(0, nchunk, gate, 0)
    for sq in range(nseq):
        for d in range(2):
            for h in range(H_A):
                sout_ref[sq, d, h] = final[sq][d][h].T


def _hgrn(qa, ff, fb, ia, ga, lb, ng, s0, seq_len, nseq):
    bsz = qa.shape[0]
    w, m, sel = _hgrn_constants()
    seq = pl.BlockSpec((nseq, seq_len, W_A), lambda b: (b, 0, 0))
    full = lambda shape: pl.BlockSpec(shape, lambda b: (0,) * len(shape))
    st_spec = pl.BlockSpec((nseq, 2, H_A, DK_A, DK_A), lambda b: (b, 0, 0, 0, 0))
    return pl.pallas_call(
        functools.partial(_hgrn_kernel, seq_len=seq_len, has_state=s0 is not None),
        grid=(bsz // nseq,),
        in_specs=[seq, seq, seq, seq, seq, full((1, W_A)), full((1, DK_A)),
                  full(w.shape), full(m.shape), full(sel.shape)] + ([st_spec] if s0 is not None else []),
        out_specs=[seq, st_spec],
        out_shape=[jax.ShapeDtypeStruct((bsz, seq_len, W_A), F32),
                   jax.ShapeDtypeStruct((bsz, 2, H_A, DK_A, DK_A), F32)],
        scratch_shapes=[pltpu.VMEM((nseq, seq_len, W_A), F32)],
        compiler_params=_cparams("arbitrary"),
        name="hgrn",
    )(qa, ff, fb, ia, ga, lb.reshape(1, W_A), ng.reshape(1, DK_A),
      jnp.asarray(w, BF16), jnp.asarray(m, F32), jnp.asarray(sel, BF16),
      *([s0] if s0 is not None else []))


def _dft_constants(seq_len):
    n = 2 * seq_len
    t = np.arange(seq_len, dtype=np.int64)
    wt = (np.arange(seq_len, dtype=np.int64)[:, None] * t[None, :]) % n
    ang = 2.0 * np.pi * wt.astype(np.float64) / n
    cos, sin = np.cos(ang), np.sin(ang)
    nyq = np.where(t % 2 == 0, 1.0, -1.0)
    sin_p = sin.copy()
    sin_p[0] = nyq
    fwd = np.concatenate([cos, sin_p], axis=0)
    icos = 2.0 * cos.T / n
    icos[:, 0] = 1.0 / n
    isin = 2.0 * sin.T / n
    isin[:, 0] = nyq / n
    inv = np.concatenate([icos, isin], axis=1)
    return fwd, inv


def _filter_embedding(seq_len):
    t = np.linspace(0.0, 1.0, seq_len)[:, None]
    w = 2.0 * np.pi * np.arange(seq_len) / seq_len
    f = np.linspace(1e-4, FILTER_BANDS - 1, FILTER_BANDS)
    ang = w[:, None] * f[None, :]
    z = np.concatenate([t, np.cos(ang), -np.sin(ang)], axis=-1)
    zp = np.zeros((seq_len, LANES), np.float64)
    zp[:, : z.shape[1]] = z
    deltas = np.abs(np.linspace(DECAY_MIN, DECAY_MAX, C_B))
    window = np.exp(-t * deltas[None, :])
    return zp.astype(np.float32), window.astype(np.float32)


def _hyena_filter_kernel(z_ref, win_ref, w1_ref, b1_ref, fr1_ref, w2_ref, b2_ref, fr2_ref, w3_ref,
                         hb_ref, fh_ref, fl_ref, gr_ref, gi_ref, *, seq_len):
    h = jnp.sin(fr1_ref[...] * (_dot3(z_ref[...], w1_ref[...]) + b1_ref[...]))
    h = jnp.sin(fr2_ref[...] * (_dot3(h, w2_ref[...]) + b2_ref[...]))
    h = _dot3(h, w3_ref[...])
    win = win_ref[...]
    hf = h[:, :C_B] * win
    hbk = h[:, C_B:] * win

    def dft(x):
        xh, xl = _split_bf16(x)
        return _dot(fh_ref[...], xh) + _dot(fh_ref[...], xl) + _dot(fl_ref[...], xh)

    p_sum = dft(hf + hbk)
    p_dif = dft(hf - hbk)
    row0 = lax.broadcasted_iota(jnp.int32, (seq_len, C_B), 0) == 0
    gr_ref[...] = p_sum[:seq_len] + hb_ref[...]
    gi_ref[...] = jnp.where(row0, p_sum[seq_len:] + hb_ref[...], p_dif[seq_len:])


def _hyena_filter(seq_len, w1, b1, fr1, w2, b2, fr2, w3, hbias):
    zemb, window = _filter_embedding(seq_len)
    fwd, _ = _dft_constants(seq_len)
    f_hi, f_lo = _split_bf16(jnp.asarray(fwd, F32))

    def pad(a, rows, cols):
        return jnp.zeros((rows, cols), F32).at[: a.shape[0], : a.shape[1]].set(a)

    args = (jnp.asarray(zemb), jnp.asarray(window), pad(w1, LANES, LANES), pad(b1[None], 1, LANES),
            pad(fr1[None], 1, LANES), pad(w2, LANES, LANES), pad(b2[None], 1, LANES),
            pad(fr2[None], 1, LANES), pad(w3, LANES, 2 * C_B), hbias.reshape(1, C_B), f_hi, f_lo)
    return pl.pallas_call(
        functools.partial(_hyena_filter_kernel, seq_len=seq_len),
        out_shape=[jax.ShapeDtypeStruct((seq_len, C_B), F32)] * 2,
        compiler_params=pltpu.CompilerParams(vmem_limit_bytes=VMEM_LIMIT_BYTES),
        name="hyena_filter",
    )(*args)


def _hyena_kernel(x0_ref, x1_ref, v_ref, cw_ref, cb_ref, gr_ref, gi_ref, f_ref, fi_ref, o_ref, *,
                  seq_len):
    tc = o_ref.shape[2]
    row = lax.broadcasted_iota(jnp.int32, (seq_len, tc), 0)

    def short_conv(u_ref, j, sq):
        u = u_ref[sq]
        prev = jnp.where(row == 0, 0.0, pltpu.roll(u, 1, 0))
        nxt = jnp.where(row == seq_len - 1, 0.0, pltpu.roll(u, seq_len - 1, 0))
        cw = cw_ref[j]
        return cw[0:1] * prev + cw[1:2] * u + cw[2:3] * nxt + cb_ref[j]

    gr, gi = gr_ref[...], gi_ref[...]
    row0 = row == 0
    for sq in range(o_ref.shape[0]):
        x0 = short_conv(x0_ref, 0, sq)
        z = short_conv(v_ref, 2, sq) * short_conv(x1_ref, 1, sq)
        p = _dot(f_ref[...], z.astype(BF16))
        a, b = p[:seq_len], p[seq_len:]
        bgi = b * gi
        yr = a * gr - jnp.where(row0, 0.0, bgi)
        yq = jnp.where(row0, bgi, a * gi + b * gr)
        y = _dot(fi_ref[...], jnp.concatenate([yr, yq], axis=0).astype(BF16))
        o_ref[sq] = y * x0


def _hyena(hy, conv_w, conv_b, gr, gi, seq_len, nseq, tc=256):
    bsz = hy.shape[0]
    nct = C_B // tc
    fwd, inv = _dft_constants(seq_len)
    cw = conv_w.reshape(3, 3, C_B).transpose(1, 0, 2)
    cb = conv_b.reshape(3, 1, C_B)
    part = lambda k: pl.BlockSpec((nseq, seq_len, tc), lambda b, j, k=k: (b, 0, k * nct + j))
    return pl.pallas_call(
        functools.partial(_hyena_kernel, seq_len=seq_len),
        grid=(bsz // nseq, nct),
        in_specs=[part(0), part(1), part(2),
                  pl.BlockSpec((3, 3, tc), lambda b, j: (0, 0, j)),
                  pl.BlockSpec((3, 1, tc), lambda b, j: (0, 0, j)),
                  pl.BlockSpec((seq_len, tc), lambda b, j: (0, j)),
                  pl.BlockSpec((seq_len, tc), lambda b, j: (0, j)),
                  pl.BlockSpec((2 * seq_len, seq_len), lambda b, j: (0, 0)),
                  pl.BlockSpec((seq_len, 2 * seq_len), lambda b, j: (0, 0))],
        out_specs=pl.BlockSpec((nseq, seq_len, tc), lambda b, j: (b, 0, j)),
        out_shape=jax.ShapeDtypeStruct((bsz, seq_len, C_B), F32),
        compiler_params=_cparams("arbitrary", "arbitrary"),
        name="hyena",
    )(hy, hy, hy, cw, cb, gr, gi, jnp.asarray(fwd, F32).astype(BF16),
      jnp.asarray(inv, F32).astype(BF16))


def _rope_tables(seq_len, dim):
    rows = seq_len // GRID_W
    row_idx = np.repeat(np.arange(rows), GRID_W).astype(np.float64)
    col_idx = np.tile(np.arange(GRID_W), rows).astype(np.float64)
    half = dim // 2
    inv = ROPE_THETA ** (-np.arange(0, half, 2, dtype=np.float64) / half)
    ang = np.concatenate([row_idx[:, None] * inv, col_idx[:, None] * inv], axis=-1)
    cos = np.repeat(np.cos(ang), 2, axis=1)
    sin = np.repeat(np.sin(ang), 2, axis=1)
    sin[:, 0::2] *= -1.0
    reps = LANES // dim
    return (np.tile(cos, (1, reps)).astype(np.float32), np.tile(sin, (1, reps)).astype(np.float32))


def _lane_group_matrices():
    i = np.arange(2 * LANES)
    same = lambda width: (i[:, None] // width == i[None, :] // width).astype(np.float32)
    return same(HD_C), same(DK_D)


def _qk_proj_kernel(x_ref, m_ref, g_ref, w_ref, cqg_ref, ckg_ref, dqg_ref, dkg_ref, grp_c_ref,
                    grp_d_ref, *rest, use_rope, slabs):
    if use_rope:
        cc_ref, sc_ref, cd_ref, sd_ref = rest[:4]
        rest = rest[4:]
    qc_ref, qd_ref, kc_ref, vc_ref, kd_ref, vd_ref = rest
    m = m_ref[0]
    pair = 2 * LANES
    assert pair == KV_C * HD_C
    two = lambda r: jnp.concatenate([r, r], axis=1)
    rows_per_slab = x_ref.shape[1] // slabs

    for slab in range(slabs):
        rows = pl.ds(slab * rows_per_slab, rows_per_slab)
        seq_rows = qc_ref.shape[1]
        oseq = (slab * rows_per_slab) // seq_rows
        orows = pl.ds((slab * rows_per_slab) % seq_rows, rows_per_slab)
        hb = (_rms(x_ref[0, rows, :]) * g_ref[...] * (1.0 + m[1:2]) + m[0:1]).astype(BF16)
        cols = lambda start, n: _dot(hb, w_ref[:, start * LANES : (start + n) * LANES])

        def norm(x, grp_ref, width, g_ref, cos_ref, sin_ref):
            ms = _dot((x * x).astype(BF16), grp_ref[...]) * (1.0 / width)
            y = x * lax.rsqrt(ms + EPS) * two(g_ref[...])
            if not use_rope:
                return y
            even = (lax.broadcasted_iota(jnp.int32, y.shape, 1) % 2) == 0
            swapped = jnp.where(even, pltpu.roll(y, y.shape[1] - 1, 1), pltpu.roll(y, 1, 1))
            return y * two(cos_ref[rows, :]) + swapped * two(sin_ref[rows, :])

        norm_c = lambda x, g_ref: norm(x, grp_c_ref, HD_C, g_ref, cc_ref if use_rope else None,
                                       sc_ref if use_rope else None)
        norm_d = lambda x, g_ref: norm(x, grp_d_ref, DK_D, g_ref, cd_ref if use_rope else None,
                                       sd_ref if use_rope else None)
        for i in range(H_C // 2):
            qc_ref[oseq, orows, i * pair : (i + 1) * pair] = norm_c(cols(2 * i, 2), cqg_ref)
        kc = norm_c(cols(H_C, KV_C), ckg_ref)
        vc = cols(H_C + KV_C, KV_C)
        for j in range(KV_C):
            kc_ref[oseq, j, orows, :] = kc[:, j * LANES : (j + 1) * LANES]
            vc_ref[oseq, j, orows, :] = vc[:, j * LANES : (j + 1) * LANES]
        base = H_C + 2 * KV_C
        for i in range(H_D // 2):
            qd_ref[oseq, orows, i * pair : (i + 1) * pair] = norm_d(cols(base + 2 * i, 2), dqg_ref)
            kd = norm_d(cols(base + H_D + 2 * i, 2), dkg_ref)
            vd = cols(base + 2 * H_D + 2 * i, 2)
            for j in range(2):
                kd_ref[oseq, 2 * i + j, orows, :] = kd[:, j * LANES : (j + 1) * LANES]
                vd_ref[oseq, 2 * i + j, orows, :] = vd[:, j * LANES : (j + 1) * LANES]


def _qk_proj(x, mods, g, w_bf16, cqg, ckg, dqg, dkg, seq_len, use_rope, tm, slabs):
    ngrp = x.shape[0]
    spg = ROW_GROUP // seq_len
    bsz = ngrp * spg
    seq_rows = min(tm, seq_len)
    spt = tm // seq_rows
    assert seq_len % seq_rows == 0 and (tm // slabs) <= seq_rows
    seq_blk = lambda gi, i: (gi * spg + (i * tm) // seq_len) // spt
    blk_of = lambda i: ((i * tm) % seq_len) // seq_rows
    vec = lambda: pl.BlockSpec((1, LANES), lambda gi, i: (0, 0))
    mat = lambda: pl.BlockSpec((2 * LANES, 2 * LANES), lambda gi, i: (0, 0))
    tab = lambda: pl.BlockSpec((tm, LANES), lambda gi, i: (blk_of(i), 0))
    in_specs = [pl.BlockSpec((1, tm, D_MODEL), lambda gi, i: (gi, i, 0)),
                pl.BlockSpec((1, 6, D_MODEL), lambda gi, i: (gi, 0, 0)),
                pl.BlockSpec((1, D_MODEL), lambda gi, i: (0, 0)),
                pl.BlockSpec(w_bf16.shape, lambda gi, i: (0, 0)),
                vec(), vec(), vec(), vec(), mat(), mat()]
    args = [x, mods, g.reshape(1, D_MODEL), w_bf16, cqg.reshape(1, HD_C), ckg.reshape(1, HD_C),
            jnp.tile(dqg.reshape(1, DK_D), (1, 2)), jnp.tile(dkg.reshape(1, DK_D), (1, 2))]
    args += [jnp.asarray(mm, BF16) for mm in _lane_group_matrices()]
    if use_rope:
        in_specs += [tab(), tab(), tab(), tab()]
        args += [jnp.asarray(t) for t in _rope_tables(seq_len, HD_C) + _rope_tables(seq_len, DK_D)]
    tok = lambda w: pl.BlockSpec((spt, seq_rows, w), lambda gi, i: (seq_blk(gi, i), blk_of(i), 0))
    head = lambda nh: pl.BlockSpec((spt, nh, seq_rows, LANES),
                                   lambda gi, i: (seq_blk(gi, i), 0, blk_of(i), 0))
    tok_shape = lambda w: jax.ShapeDtypeStruct((bsz, seq_len, w), F32)
    head_shape = lambda nh: jax.ShapeDtypeStruct((bsz, nh, seq_len, LANES), F32)
    return pl.pallas_call(
        functools.partial(_qk_proj_kernel, use_rope=use_rope, slabs=slabs),
        grid=(ngrp, ROW_GROUP // tm),
        in_specs=in_specs,
        out_specs=[tok(H_C * HD_C), tok(H_D * 2 * DK_D), head(KV_C), head(KV_C), head(H_D), head(H_D)],
        out_shape=[tok_shape(H_C * HD_C), tok_shape(H_D * 2 * DK_D), head_shape(KV_C),
                   head_shape(KV_C), head_shape(H_D), head_shape(H_D)],
        compiler_params=_cparams("arbitrary", "arbitrary"),
        name="qk_proj",
    )(*args)


def _softmax_pv(q_list, kv_list):
    outs = []
    for q in q_list:
        scores = [_dot_nt(q, k) for k, _ in kv_list]
        mx = scores[0].max(axis=1, keepdims=True)
        for s in scores[1:]:
            mx = jnp.maximum(mx, s.max(axis=1, keepdims=True))
        den = 0.0
        acc = 0.0
        for s, (_, v) in zip(scores, kv_list):
            pexp = jnp.exp2(s - mx)
            den = den + pexp.sum(axis=1, keepdims=True)
            acc = acc + _dot(pexp.astype(BF16), v)
        outs.append(acc / den)
    return outs


def _head_kv(j, k_ref, v_ref, cache_refs):
    kv = [(r_k[0, 0, j].astype(BF16), r_v[0, 0, j].astype(BF16)) for r_k, r_v in cache_refs]
    kv.append((k_ref[0, j].astype(BF16), v_ref[0, j].astype(BF16)))
    return kv


def _per_slab(q_ref, o_ref, cols, fn):
    for r0 in range(0, q_ref.shape[1], ATT_SLAB):
        rows = pl.ds(r0, min(ATT_SLAB, q_ref.shape[1]))
        o_ref[0, rows, cols] = fn(q_ref[0, rows, cols])


def _gqa_kernel(q_ref, k_ref, v_ref, *rest, has_cache):
    cache_refs = [rest[:2]] if has_cache else []
    o_ref = rest[-1]
    g_c = H_C // KV_C
    width = g_c * HD_C
    for j in range(k_ref.shape[1]):
        kv = _head_kv(j, k_ref, v_ref, cache_refs)

        def head(q):
            q = q * (HD_C ** -0.5 * LOG2E)
            qs = [q[:, g * HD_C : (g + 1) * HD_C].astype(BF16) for g in range(g_c)]
            return jnp.concatenate(_softmax_pv(qs, kv), axis=1)

        _per_slab(q_ref, o_ref, slice(j * width, (j + 1) * width), head)


def _diff_kernel(q_ref, k_ref, v_ref, lam_ref, sg_ref, *rest, has_cache, out_scale):
    cache_refs = [rest[:2]] if has_cache else []
    o_ref = rest[-1]
    width = 2 * DK_D
    for j in range(k_ref.shape[1]):
        kv = _head_kv(j, k_ref, v_ref, cache_refs)

        def head(q):
            q = q * (DK_D ** -0.5 * LOG2E)
            low = lax.broadcasted_iota(jnp.int32, q.shape, 1) < DK_D
            qs = [jnp.where(low, q, 0.0).astype(BF16), jnp.where(low, 0.0, q).astype(BF16)]
            o1, o2 = _softmax_pv(qs, kv)
            return _rms(o1 - lam_ref[...] * o2) * sg_ref[...] * out_scale

        _per_slab(q_ref, o_ref, slice(j * width, (j + 1) * width), head)


def _attention(q, k, v, cache_k, cache_v, kernel, extra_args, n_heads, q_width, seq_len, hps, tq):
    bsz = q.shape[0]
    in_specs = [pl.BlockSpec((1, tq, hps * q_width), lambda b, h, i: (b, i, h)),
                pl.BlockSpec((1, hps, seq_len, LANES), lambda b, h, i: (b, h, 0, 0)),
                pl.BlockSpec((1, hps, seq_len, LANES), lambda b, h, i: (b, h, 0, 0))]
    in_specs += [pl.BlockSpec((1, LANES), lambda b, h, i: (0, 0)) for _ in extra_args]
    args = [q, k, v, *extra_args]
    if cache_k is not None:
        past = cache_k.shape[3]
        spec = lambda: pl.BlockSpec((1, 1, hps, past, LANES), lambda b, h, i: (b, 0, h, 0, 0))
        in_specs += [spec(), spec()]
        args += [cache_k, cache_v]
    return pl.pallas_call(
        functools.partial(kernel, has_cache=cache_k is not None),
        grid=(bsz, n_heads // hps, seq_len // tq),
        in_specs=in_specs,
        out_specs=pl.BlockSpec((1, tq, hps * q_width), lambda b, h, i: (b, i, h)),
        out_shape=jax.ShapeDtypeStruct((bsz, seq_len, n_heads * q_width), F32),
        compiler_params=_cparams("arbitrary", "arbitrary", "arbitrary"),
        name="attention",
    )(*args)


def _route(lg):
    lane = lax.broadcasted_iota(jnp.int32, lg.shape, 1).astype(F32)
    neg = -1e30
    is_g = (lane >= N_EXPERTS) & (lane < N_EXPERTS + N_GROUPS)
    gl = jnp.where(is_g, lg, neg)
    gmax = gl.max(axis=1, keepdims=True)
    g_p = 1.0 / jnp.where(is_g, jnp.exp(gl - gmax), 0.0).sum(axis=1, keepdims=True)
    g_i = jnp.where(gl == gmax, lane - N_EXPERTS, 1e9).min(axis=1, keepdims=True)
    in_grp = (lane < N_EXPERTS) & (jnp.floor(lane * (1.0 / EXP_PER_GROUP)) == g_i)
    el = jnp.where(in_grp, lg, neg)
    m1 = el.max(axis=1, keepdims=True)
    i1 = jnp.where(in_grp & (el == m1), lane, 1e9).min(axis=1, keepdims=True)
    el2 = jnp.where(lane == i1, neg, el)
    m2 = el2.max(axis=1, keepdims=True)
    i2 = jnp.where(in_grp & (el2 == m2) & (lane != i1), lane, 1e9).min(axis=1, keepdims=True)
    r = jnp.exp(m2 - m1)
    w1 = g_p / (1.0 + r)
    rec = jnp.where(lane == ROUTE_LANE, i1, 0.0)
    rec = jnp.where(lane == ROUTE_LANE + 1, i2, rec)
    rec = jnp.where(lane == ROUTE_LANE + 2, w1, rec)
    return jnp.where(lane == ROUTE_LANE + 3, w1 * r, rec)


def _mix_out_kernel(x_ref, a_ref, b_ref, m_ref, g2_ref, w_ref, wr_ref, br_ref, tri_ref,
                    x1_ref, xs_ref, pos_ref, cnt_ref, *, parts):
    wa = a_ref.shape[2]
    m = m_ref[0]
    rows_per_part = x_ref.shape[1] // parts
    h2_parts, route_parts = [], []
    for part in range(parts):
        rows = pl.ds(part * rows_per_part, rows_per_part)
        o = (_dot(a_ref[0, rows, :].astype(BF16), w_ref[:wa])
             + _dot(b_ref[0, rows, :].astype(BF16), w_ref[wa:]))
        x1 = x_ref[0, rows, :] + m[2:3] * o
        x1_ref[0, rows, :] = x1
        h2 = _rms(x1) * g2_ref[...] * (1.0 + m[4:5]) + m[3:4]
        h2_parts.append(h2.astype(BF16))
        r = _dot(jnp.concatenate(_split_bf16(h2), axis=0), wr_ref[...])
        hi_rows, lo_rows = r[:rows_per_part], r[rows_per_part:]
        lg = hi_rows[:, :LANES] + hi_rows[:, LANES:] + lo_rows[:, :LANES] + br_ref[...]
        route_parts.append(_route(lg))
    xs, pos, cnt = _moe_sort_tile(jnp.concatenate(h2_parts, axis=0),
                                  jnp.concatenate(route_parts, axis=0), tri_ref[...])
    xs_ref[...] = xs
    pos_ref[0] = pos
    cnt_ref[0] = cnt


def _mix_out(x, a, b, mods, g2, w_out_bf16, w_router, b_router, parts=2):
    ngrp = x.shape[0]
    per = ROW_GROUP // MOE_TILE
    ntile = ngrp * per
    wa, wb = a.shape[2], b.shape[2]
    wr = jnp.concatenate(_split_bf16(w_router), axis=1)
    tri = np.tril(np.ones((MOE_TILE, MOE_TILE), np.float32), -1)
    src = lambda j: jnp.minimum(j, ntile - 1)
    row = lambda w: pl.BlockSpec((1, MOE_TILE, w), lambda j: (src(j) // per, src(j) % per, 0))
    full = lambda shape: pl.BlockSpec(shape, lambda j: (0,) * len(shape))
    return pl.pallas_call(
        functools.partial(_mix_out_kernel, parts=parts),
        grid=(ntile + 1,),
        in_specs=[row(D_MODEL), row(wa), row(wb),
                  pl.BlockSpec((1, 6, D_MODEL), lambda j: (src(j) // per, 0, 0)),
                  full((1, D_MODEL)), full((wa + wb, D_MODEL)),
                  full((D_MODEL, 2 * LANES)), full((1, LANES)), full((MOE_TILE, MOE_TILE))],
        out_specs=[pl.BlockSpec((1, MOE_TILE, D_MODEL), lambda j: (j, 0, 0)),
                   pl.BlockSpec((MOE_TILE_ROWS, MOE_ROW_W), lambda j: (j, 0)),
                   pl.BlockSpec((1, MOE_TILE, LANES), lambda j: (j, 0, 0)),
                   pl.BlockSpec((1, 8, LANES), lambda j: (j, 0, 0))],
        out_shape=[jax.ShapeDtypeStruct((ntile + 1, MOE_TILE, D_MODEL), F32),
                   jax.ShapeDtypeStruct(((ntile + 1) * MOE_TILE_ROWS, MOE_ROW_W), BF16),
                   jax.ShapeDtypeStruct((ntile + 1, MOE_TILE, LANES), F32),
                   jax.ShapeDtypeStruct((ntile + 1, 8, LANES), F32)],
        compiler_params=_cparams("arbitrary"),
        name="mix_out",
    )(x, a, b, mods, g2.reshape(1, D_MODEL), w_out_bf16, wr, b_router, jnp.asarray(tri, BF16))


def _lane_col(x, lane, k):
    return jnp.where(lane == k, x, 0.0).sum(axis=1, keepdims=True)


def _moe_sort_tile(h, r, tri):
    lane = lax.broadcasted_iota(jnp.int32, r.shape, 1).astype(F32)
    i1, i2, w1, w2 = [_lane_col(r, lane, ROUTE_LANE + k) for k in range(4)]
    oh1 = lane == i1
    oh2 = lane == i2
    oh = jnp.where(oh1 | oh2, 1.0, 0.0)
    rank = _dot(tri, oh.astype(BF16))
    cnt = oh.sum(axis=0, keepdims=True)
    chunks = jnp.floor((cnt + (MOE_CHUNK - 1)) * (1.0 / MOE_CHUNK))
    li = lax.broadcasted_iota(jnp.int32, (LANES, LANES), 0)
    lj = lax.broadcasted_iota(jnp.int32, (LANES, LANES), 1)
    before = jnp.where(li < lj, 1.0, 0.0).astype(BF16)
    seg = _dot(jnp.broadcast_to(chunks, (8, LANES)).astype(BF16), before)[0:1]
    base = seg * MOE_CHUNK + rank
    pos1 = jnp.where(oh1, base, 0.0).sum(axis=1, keepdims=True)
    pos2 = jnp.where(oh2, base, 0.0).sum(axis=1, keepdims=True)
    riota = lax.broadcasted_iota(jnp.int32, (r.shape[0], MOE_TILE_ROWS), 1).astype(F32)
    p = jnp.where((riota == pos1) | (riota == pos2), 1.0, 0.0).astype(BF16)

    aux = jnp.where(lane == 2 * MOE_W_PIECES, i1, 0.0)
    for k, w in enumerate((w1, w2)):
        rest = w
        for piece in range(MOE_W_PIECES):
            part = rest.astype(BF16).astype(F32)
            aux = jnp.where(lane == k * MOE_W_PIECES + piece, part, aux)
            rest = rest - part
    row = jnp.concatenate([h, aux.astype(BF16)], axis=1)
    xs = _dot_tn(p, row).astype(BF16)
    pos = jnp.where(lane == 0, pos1, jnp.where(lane == 1, pos2, 0.0))
    return xs, pos, jnp.broadcast_to(cnt, (8, LANES))


def _moe_tables(cnt, ntile):
    nblk = ntile * MOE_TILE_CHUNKS // MOE_BLOCK_CHUNKS + N_EXPERTS
    chunks = (cnt + MOE_CHUNK - 1) // MOE_CHUNK
    seg_start = jnp.cumsum(chunks, axis=1) - chunks
    tile_prefix = jnp.cumsum(chunks, axis=0) - chunks
    per_expert = chunks.sum(axis=0)
    blocks = (per_expert + MOE_BLOCK_CHUNKS - 1) // MOE_BLOCK_CHUNKS
    blk_end = jnp.cumsum(blocks)
    n_used = blk_end[-1]
    b = jnp.arange(nblk + 1, dtype=jnp.int32)
    blk_e = jnp.sum(b[:, None] >= blk_end[None, :], axis=1).astype(jnp.int32)
    last_e = jnp.sum((n_used - 1) >= blk_end).astype(jnp.int32)
    blk_e = jnp.clip(jnp.where(b < n_used, blk_e, last_e), 0, N_EXPERTS - 1)
    oh_e = (blk_e[:, None] == jnp.arange(N_EXPERTS)[None, :]).astype(jnp.int32)
    pick = lambda per_tile: jnp.sum(oh_e[:, :, None] * per_tile.T[None], axis=1)
    seg_e, pre_e, chunks_e = pick(seg_start), pick(tile_prefix), pick(chunks)
    first_blk = jnp.sum(oh_e * (blk_end - blocks)[None, :], axis=1)
    k = (b - first_blk)[:, None] * MOE_BLOCK_CHUNKS + jnp.arange(MOE_BLOCK_CHUNKS)[None, :]
    k3 = k[:, :, None]
    in_tile = (pre_e[:, None, :] <= k3) & (k3 < (pre_e + chunks_e)[:, None, :])
    tile_base = (jnp.arange(ntile) * MOE_TILE_CHUNKS)[None, :] + seg_e - pre_e
    src = jnp.sum(jnp.where(in_tile, tile_base[:, None, :] + k3, 0), axis=-1)
    valid = jnp.any(in_tile, axis=-1) & (b < n_used)[:, None]
    slot_c = jnp.arange(MOE_BLOCK_CHUNKS)[None, :]
    n_read = MOE_TILE_CHUNKS - 2 * MOE_BLOCK_CHUNKS
    assert n_read > 0
    spare = ntile * MOE_TILE_CHUNKS
    gather = jnp.where(valid, src, spare + 2 * MOE_BLOCK_CHUNKS + slot_c % n_read).astype(jnp.int32)
    scatter = jnp.where(valid, src, spare + (b % 2)[:, None] * MOE_BLOCK_CHUNKS + slot_c).astype(jnp.int32)
    blk_start = jnp.concatenate([blk_end - blocks, n_used[None]]).astype(jnp.int32)
    return blk_start, gather.reshape(-1), scatter.reshape(-1)


def _moe_expert_kernel(start_ref, gather_ref, scatter_ref, xs_hbm, wg_ref, wu_ref, wd_ref, ys_hbm,
                       lhs, obuf, wgb, wub, wdb, in_sem, out_sem):
    e = pl.program_id(0)
    ne = pl.num_programs(0)
    b0 = start_ref[e]
    b1 = start_ref[e + 1]
    n = start_ref[ne]

    def chunk_copy(blk, slot, c, gather):
        rows = pl.ds(c * MOE_CHUNK, MOE_CHUNK)
        if gather:
            idx = gather_ref[blk * MOE_BLOCK_CHUNKS + c]
            return pltpu.make_async_copy(xs_hbm.at[idx], lhs.at[slot, rows], in_sem.at[slot])
        idx = scatter_ref[blk * MOE_BLOCK_CHUNKS + c]
        dst = ys_hbm.at[idx, pl.ds(0, MOE_CHUNK), pl.ds(0, D_MODEL)]
        return pltpu.make_async_copy(obuf.at[slot, rows], dst, out_sem.at[slot])

    def for_chunks(blk, slot, gather, start):
        for c in range(MOE_BLOCK_CHUNKS):
            cp = chunk_copy(blk, slot, c, gather)
            if start:
                cp.start()
            else:
                cp.wait()

    @pl.when((e == 0) & (n > 0))
    def _():
        for_chunks(0, 0, True, True)

    @pl.when(b1 > b0)
    def _():
        wgb[...] = wg_ref[0, 0].astype(BF16)
        wub[...] = wu_ref[0, 0].astype(BF16)
        wdb[...] = wd_ref[0, 0].astype(BF16)
        e_f32 = e.astype(F32)

        def block(b, carry):
            slot = b % 2

            @pl.when(b >= 2)
            def _():
                for_chunks(b - 2, slot, False, False)

            for_chunks(b, slot, True, False)
            for_chunks(b + 1, 1 - slot, True, True)
            xa = lhs[slot]
            x = xa[:, :D_MODEL]
            aux = xa[:, D_MODEL:].astype(F32)
            lane = lax.broadcasted_iota(jnp.int32, aux.shape, 1)
            first = lane < MOE_W_PIECES
            w_first = jnp.where(first, aux, 0.0).sum(axis=1, keepdims=True)
            w_second = jnp.where(first | (lane >= 2 * MOE_W_PIECES), 0.0, aux).sum(axis=1, keepdims=True)
            e_first = _lane_col(aux, lane, 2 * MOE_W_PIECES)
            w = jnp.where(e_first == e_f32, w_first, w_second)
            hid = _silu(_dot(x, wgb[...])) * _dot(x, wub[...]) * w
            obuf[slot] = _dot(hid.astype(BF16), wdb[...]).astype(BF16)
            for_chunks(b, slot, False, True)
            return carry

        lax.fori_loop(b0, b1, block, 0)

    @pl.when(e == ne - 1)
    def _():
        @pl.when(n >= 2)
        def _():
            for_chunks(n - 2, n % 2, False, False)

        @pl.when(n >= 1)
        def _():
            for_chunks(n, n % 2, True, False)
            for_chunks(n - 1, (n - 1) % 2, False, False)


def _moe_experts(xs, blk_start, gather, scatter, layer, w_gate, w_up, w_down):
    nchunk = xs.shape[0] // MOE_CHUNK
    rows_per_blk = MOE_BLOCK_CHUNKS * MOE_CHUNK
    wspec = lambda shape: pl.BlockSpec((1, 1) + shape, lambda e, st, g, s: (layer, e, 0, 0))
    hbm = pl.BlockSpec(memory_space=pl.ANY)
    ys = pl.pallas_call(
        _moe_expert_kernel,
        grid_spec=pltpu.PrefetchScalarGridSpec(
            num_scalar_prefetch=3,
            grid=(N_EXPERTS,),
            in_specs=[hbm, wspec((D_MODEL, D_EXPERT)), wspec((D_MODEL, D_EXPERT)),
                      wspec((D_EXPERT, D_MODEL))],
            out_specs=hbm,
            scratch_shapes=[pltpu.VMEM((2, rows_per_blk, MOE_ROW_W), BF16),
                            pltpu.VMEM((2, rows_per_blk, D_MODEL), BF16),
                            pltpu.VMEM((D_MODEL, D_EXPERT), BF16),
                            pltpu.VMEM((D_MODEL, D_EXPERT), BF16),
                            pltpu.VMEM((D_EXPERT, D_MODEL), BF16),
                            pltpu.SemaphoreType.DMA((2,)),
                            pltpu.SemaphoreType.DMA((2,))]),
        out_shape=jax.ShapeDtypeStruct((nchunk, MOE_CHUNK, MOE_ROW_W), BF16),
        input_output_aliases={3: 0},
        compiler_params=_cparams("arbitrary"),
        name="moe_experts",
    )(blk_start, gather, scatter, xs.reshape(nchunk, MOE_CHUNK, MOE_ROW_W), w_gate, w_up, w_down)
    return ys.reshape(nchunk * MOE_CHUNK, MOE_ROW_W)


def _moe_combine_kernel(ys_ref, pos_ref, x_ref, m_ref, o_ref):
    pos = pos_ref[0]
    lane = lax.broadcasted_iota(jnp.int32, pos.shape, 1)
    pos1 = _lane_col(pos, lane, 0)
    pos2 = _lane_col(pos, lane, 1)
    riota = lax.broadcasted_iota(jnp.int32, (pos.shape[0], MOE_TILE_ROWS), 1).astype(F32)
    p = jnp.where((riota == pos1) | (riota == pos2), 1.0, 0.0).astype(BF16)
    o_ref[0] = x_ref[0] + m_ref[0][5:6] * _dot(p, ys_ref[...])


def _moe_combine(ys, pos, x1, mods):
    ngrp = mods.shape[0]
    per = ROW_GROUP // MOE_TILE
    tile = pl.BlockSpec((1, MOE_TILE, D_MODEL), lambda j: (j, 0, 0))
    return pl.pallas_call(
        _moe_combine_kernel,
        grid=(ngrp * per,),
        in_specs=[pl.BlockSpec((MOE_TILE_ROWS, D_MODEL), lambda j: (j, 0)),
                  pl.BlockSpec((1, MOE_TILE, LANES), lambda j: (j, 0, 0)),
                  tile, pl.BlockSpec((1, 6, D_MODEL), lambda j: (j // per, 0, 0))],
        out_specs=pl.BlockSpec((1, MOE_TILE, D_MODEL), lambda j: (j // per, j % per, 0)),
        out_shape=jax.ShapeDtypeStruct((ngrp, ROW_GROUP, D_MODEL), F32),
        compiler_params=_cparams("arbitrary"),
        name="moe_combine",
    )(ys, pos, x1, mods)


def _moe(xs, pos, cnt, x1, mods, layer, w_gate, w_up, w_down):
    ntile = mods.shape[0] * (ROW_GROUP // MOE_TILE)
    cnt = cnt[:ntile, 0, :N_EXPERTS].astype(jnp.int32)
    blk_start, gather, scatter = _moe_tables(cnt, ntile)
    ys = _moe_experts(xs, blk_start, gather, scatter, layer, w_gate, w_up, w_down)
    return _moe_combine(ys, pos, x1, mods)


def kernel(x_prompt, x_sample, state_hgrn, cache_c_k, cache_c_v, cache_d_k, cache_d_v, c, c_ctx, norm1_g, norm2_g, w_mod, b_mod, even_w_in, even_w_out, hgrn_lower, hgrn_norm_g, hy_conv_w, hy_conv_b, hy_w1, hy_b1, hy_freq1, hy_w2, hy_b2, hy_freq2, hy_w3, hy_bias, odd_w_in, odd_w_out, c_qnorm_g, c_knorm_g, d_qnorm_g, d_knorm_g, d_lambda_q1, d_lambda_k1, d_lambda_q2, d_lambda_k2, d_subln_g, moe_w_grp, moe_b_grp, moe_w_rt, moe_b_rt, moe_w_gate, moe_w_up, moe_w_down):
    depth = w_mod.shape[0]
    n_ctx, seq, _ = x_prompt.shape
    n_lat, dec_seq, _ = x_sample.shape
    g_ctx = n_ctx * seq // ROW_GROUP
    g_lat = n_lat * dec_seq // ROW_GROUP
    assert dec_seq == ROW_GROUP and ROW_GROUP % seq == 0

    cond = jnp.zeros((16, D_MODEL), F32).at[0].set(c_ctx).at[1 : 1 + n_lat].set(c)
    mods = _adaln(cond, w_mod, b_mod).reshape(depth, 16, 6, D_MODEL)
    lower = jnp.cumsum(jax.nn.softmax(hgrn_lower.astype(F32), axis=0), axis=0)

    streams = [
        dict(x=x_prompt.reshape(g_ctx, ROW_GROUP, D_MODEL), ngrp=g_ctx, bsz=n_ctx, seq=seq, ctx=True),
        dict(x=x_sample, ngrp=g_lat, bsz=n_lat, seq=dec_seq, ctx=False),
    ]
    new_state, new_ck, new_cv, new_dk, new_dv = [], [], [], [], []

    for l in range(depth):
        j = l // 2
        w_router = jnp.zeros((D_MODEL, LANES), F32)
        w_router = w_router.at[:, :N_EXPERTS].set(moe_w_rt[l])
        w_router = w_router.at[:, N_EXPERTS : N_EXPERTS + N_GROUPS].set(moe_w_grp[l])
        b_router = jnp.zeros((1, LANES), F32)
        b_router = b_router.at[0, :N_EXPERTS].set(moe_b_rt[l])
        b_router = b_router.at[0, N_EXPERTS : N_EXPERTS + N_GROUPS].set(moe_b_grp[l])
        if l % 2 == 0:
            w_in = even_w_in[j].astype(BF16)
            w_out = even_w_out[j].astype(BF16)
        else:
            w_in = odd_w_in[j].astype(BF16)
            w_out = odd_w_out[j].astype(BF16)
            lam_init = 0.8 - 0.6 * math.exp(-0.3 * l)
            lam = (jnp.exp(jnp.sum(d_lambda_q1[j] * d_lambda_k1[j]))
                   - jnp.exp(jnp.sum(d_lambda_q2[j] * d_lambda_k2[j])) + lam_init)
            lam_row = jnp.full((1, LANES), lam, F32)

        for s in streams:
            ngrp, bsz, sl = s["ngrp"], s["bsz"], s["seq"]
            if s["ctx"]:
                m = jnp.broadcast_to(mods[l, 0][None], (ngrp, 6, D_MODEL))
            else:
                m = mods[l, 1 : 1 + ngrp]
            x = s["x"]
            if l % 2 == 0:
                wa = H_A * DK_A
                splits = [(0, wa), (wa, 2 * wa), (2 * wa, 3 * wa), (3 * wa, 3 * wa + W_A),
                          (3 * wa + W_A, 3 * wa + 2 * W_A), (3 * wa + 2 * W_A, w_in.shape[1])]
                qa, ffa, fba, ia, ga, hy = _norm_proj(x, m, norm1_g[l], w_in, splits)
                per_seq = lambda t: t.reshape(bsz, sl, t.shape[-1])
                s0 = None if s["ctx"] else state_hgrn[:, j].astype(F32)
                mix_a, s_fin = _hgrn(per_seq(qa), per_seq(ffa), per_seq(fba), per_seq(ia),
                                     per_seq(ga), lower[j], hgrn_norm_g[j], s0, sl,
                                     nseq=2 if s["ctx"] else 1)
                gr, gi = _hyena_filter(sl, hy_w1[j], hy_b1[j], hy_freq1[j], hy_w2[j], hy_b2[j],
                                       hy_freq2[j], hy_w3[j], hy_bias[j])
                mix_b = _hyena(per_seq(hy), hy_conv_w[j], hy_conv_b[j], gr, gi, sl,
                               nseq=4 if s["ctx"] else 2)
                if s["ctx"]:
                    new_state.append(s_fin)
            else:
                qc, qd, kc, vc, kd, vd = _qk_proj(x, m, norm1_g[l], w_in, c_qnorm_g[j], c_knorm_g[j],
                                                  d_qnorm_g[j], d_knorm_g[j], sl, use_rope=not s["ctx"],
                                                  tm=512, slabs=2)
                if s["ctx"]:
                    caches = (None, None, None, None)
                    new_ck.append(kc)
                    new_cv.append(vc)
                    new_dk.append(kd)
                    new_dv.append(vd)
                else:
                    caches = (cache_c_k[:, j : j + 1], cache_c_v[:, j : j + 1],
                              cache_d_k[:, j : j + 1], cache_d_v[:, j : j + 1])
                hps_c, hps_d, tq = (KV_C, H_D, sl) if s["ctx"] else (1, 1, sl)
                mix_a = _attention(qc, kc, vc, caches[0], caches[1], _gqa_kernel, (), KV_C,
                                   (H_C // KV_C) * HD_C, sl, hps_c, tq)
                diff = functools.partial(_diff_kernel, out_scale=1.0 - lam_init)
                mix_b = _attention(qd, kd, vd, caches[2], caches[3], diff,
                                   (lam_row, d_subln_g[j].reshape(1, DV_D)), H_D, 2 * DK_D, sl,
                                   hps_d, tq)
            grp = lambda t: t.reshape(ngrp, ROW_GROUP, t.shape[-1])
            x1, xs, pos, cnt = _mix_out(x, grp(mix_a), grp(mix_b), m, norm2_g[l], w_out, w_router,
                                        b_router)
            s["x"] = _moe(xs, pos, cnt, x1, m, l, moe_w_gate, moe_w_up, moe_w_down)

    y_ctx = streams[0]["x"].reshape(n_ctx, seq, D_MODEL)
    y_lat = streams[1]["x"]
    return (y_ctx, y_lat, jnp.stack(new_state, axis=1), jnp.stack(new_ck, axis=1),
            jnp.stack(new_cv, axis=1), jnp.stack(new_dk, axis=1), jnp.stack(new_dv, axis=1))
```

```python
import functools
import itertools
import math

import numpy as np
import jax
import jax.numpy as jnp
from jax import lax
from jax.experimental import pallas as pl
from jax.experimental.pallas import tpu as pltpu

F32 = jnp.float32
BF16 = jnp.bfloat16

D_MODEL = 1024
EPS = 1e-6
LOG2E = 1.0 / math.log(2.0)
GRID_W = 64
ROPE_THETA = 10000.0
H_A = 4
DK_A = 128
W_A = 512
CHUNK = 128
SUB = 8
LEVELS = (16, 32, 64, 128)
C_B = 512
FILTER_BANDS = 16
DECAY_MIN = math.log(1e-2) / 1.5
DECAY_MAX = math.log(1e-2) / 0.3
H_C = 4
KV_C = 2
HD_C = 128
H_D = 4
DK_D = 64
DV_D = 128
N_GROUPS = 4
EXP_PER_GROUP = 4
N_EXPERTS = 16
D_EXPERT = 512

ROUTE_LANE = N_EXPERTS + N_GROUPS
MOE_TILE = 512
MOE_CHUNK = 16
MOE_TILE_CHUNKS = 2 * MOE_TILE // MOE_CHUNK + N_EXPERTS
MOE_TILE_ROWS = MOE_TILE_CHUNKS * MOE_CHUNK
MOE_BLOCK_CHUNKS = 16
MOE_ROW_W = D_MODEL + 128
MOE_W_PIECES = 3
ATT_SLAB = 256

LANES = 128
ROW_GROUP = 1024
VMEM_LIMIT_BYTES = 56 * 1024 * 1024


def _cparams(*sem):
    return pltpu.CompilerParams(dimension_semantics=sem, vmem_limit_bytes=VMEM_LIMIT_BYTES)


def _split_bf16(x):
    hi = x.astype(BF16)
    lo = (x - hi.astype(F32)).astype(BF16)
    return hi, lo


def _dot(a, b):
    return jnp.dot(a, b, preferred_element_type=F32)


def _dot3(a, b):
    ah, al = _split_bf16(a)
    bh, bl = _split_bf16(b)
    return _dot(ah, bh) + _dot(al, bh) + _dot(ah, bl)


def _dot_nt(a, b):
    return lax.dot_general(a, b, (((1,), (1,)), ((), ())), preferred_element_type=F32)


def _dot_tn(a, b):
    return lax.dot_general(a, b, (((0,), (0,)), ((), ())), preferred_element_type=F32)


def _silu(x):
    return x * jax.nn.sigmoid(x)


def _rms(x, eps=EPS):
    return x * lax.rsqrt(jnp.mean(x * x, axis=-1, keepdims=True) + eps)


def _adaln_kernel(c_ref, w_ref, b_ref, o_ref):
    s = _silu(c_ref[...])
    o_ref[0] = _dot(s.astype(BF16), w_ref[0].astype(BF16)) + b_ref[0]


def _adaln(cond, w_mod, b_mod):
    depth, _, n = w_mod.shape
    rows = cond.shape[0]
    tn = 1536
    return pl.pallas_call(
        _adaln_kernel,
        grid=(depth, n // tn),
        in_specs=[
            pl.BlockSpec((rows, D_MODEL), lambda l, j: (0, 0)),
            pl.BlockSpec((1, D_MODEL, tn), lambda l, j: (l, 0, j)),
            pl.BlockSpec((1, 1, tn), lambda l, j: (l, 0, j)),
        ],
        out_specs=pl.BlockSpec((1, rows, tn), lambda l, j: (l, 0, j)),
        out_shape=jax.ShapeDtypeStruct((depth, rows, n), F32),
        compiler_params=_cparams("arbitrary", "arbitrary"),
        name="adaln",
    )(cond, w_mod, b_mod.reshape(depth, 1, n))


def _norm_proj_kernel(x_ref, m_ref, g_ref, w_ref, *o_refs, splits):
    m = m_ref[0]
    h = _rms(x_ref[0]) * g_ref[...] * (1.0 + m[1:2]) + m[0:1]
    hb = h.astype(BF16)
    for o_ref, (a, b) in zip(o_refs, splits):
        o_ref[0] = _dot(hb, w_ref[:, a:b])


def _norm_proj(x, mods, g, w_bf16, splits, tm=512):
    ngrp = x.shape[0]
    n = w_bf16.shape[1]
    kern = functools.partial(_norm_proj_kernel, splits=splits)
    return pl.pallas_call(
        kern,
        grid=(ngrp, ROW_GROUP // tm),
        in_specs=[
            pl.BlockSpec((1, tm, D_MODEL), lambda b, i: (b, i, 0)),
            pl.BlockSpec((1, 6, D_MODEL), lambda b, i: (b, 0, 0)),
            pl.BlockSpec((1, D_MODEL), lambda b, i: (0, 0)),
            pl.BlockSpec((D_MODEL, n), lambda b, i: (0, 0)),
        ],
        out_specs=[pl.BlockSpec((1, tm, hi - lo), lambda b, i: (b, i, 0)) for lo, hi in splits],
        out_shape=[jax.ShapeDtypeStruct((ngrp, ROW_GROUP, hi - lo), F32) for lo, hi in splits],
        compiler_params=_cparams("arbitrary", "arbitrary"),
        name="norm_proj",
    )(x, mods, g.reshape(1, D_MODEL), w_bf16)


def _hgrn_constants():
    c = CHUNK
    nblk = 1 + len(LEVELS)
    w = np.zeros((2, nblk * c, c), np.float32)
    for t in range(c):
        w[0, t, : t + 1] = 1.0
    for li, r in enumerate(LEVELS):
        for t in range(c):
            mid = (t // r) * r + r // 2
            if t >= mid:
                w[0, (li + 1) * c + t, mid : t + 1] = 1.0
            else:
                w[0, (li + 1) * c + t, t + 1 : mid] = 1.0
    for blk in range(nblk):
        w[1, blk * c : (blk + 1) * c] = w[0, blk * c : (blk + 1) * c][::-1, ::-1]
    m = np.zeros((2, nblk, c, c), np.float32)
    for li, r in enumerate(LEVELS):
        for t in range(c):
            for s in range(c):
                if t // r == s // r and (t % r) >= r // 2 and (s % r) < r // 2:
                    m[0, li, t, s] = 1.0
    for t in range(c):
        for s in range(c):
            if t // SUB == s // SUB and s <= t:
                m[0, nblk - 1, t, s] = 1.0
    m[1] = m[0][:, ::-1, ::-1]
    sel = np.zeros((SUB * DK_A, c), np.float32)
    for i in range(SUB):
        sel[i * DK_A : (i + 1) * DK_A, i::SUB] = 1.0
    return w, m, sel


def _hgrn_chunk(direction, sq, r0, qa_ref, gate_ref, ia_ref, lb, wcum, masks, sel, states):
    rows = pl.ds(r0, CHUNK)
    q = _silu(qa_ref[sq, rows, :])
    v = ia_ref[sq, rows, :]
    f = lb + (1.0 - lb) * jax.nn.sigmoid(gate_ref[sq, rows, :])
    kk = jnp.maximum(1.0 - f, 0.0)
    lf = jnp.log(f) * LOG2E
    lk = jnp.log(kk) * LOG2E
    lf_hi, lf_lo = _split_bf16(lf)
    z = _dot(wcum, lf_hi)
    e = z[0:CHUNK] + _dot(wcum[0:CHUNK], lf_lo)
    edge = e[CHUNK - 1 : CHUNK] if direction == 0 else e[0:1]
    q_in = (q * jnp.exp2(e)).astype(BF16)
    k_st = (kk * jnp.exp2(edge - e)).astype(BF16)
    st_decay = jnp.exp2(edge)
    q_lv, k_lv = [], []
    for li in range(len(LEVELS)):
        ez = jnp.exp2(z[(li + 1) * CHUNK : (li + 2) * CHUNK])
        q_lv.append((q * ez).astype(BF16))
        k_lv.append((kk * ez).astype(BF16))
    nsub = CHUNK // SUB
    c3 = (e - lk).reshape(nsub, SUB, W_A)
    pair = []
    for i in range(SUB):
        cb = jnp.broadcast_to(c3[:, i : i + 1, :], (nsub, SUB, W_A)).reshape(CHUNK, W_A)
        pair.append((q * jnp.exp2(jnp.minimum(e - cb, 0.0))).astype(BF16))
    vb = v.astype(BF16)
    outs, new_states = [], []
    for h in range(H_A):
        hs = slice(h * DK_A, (h + 1) * DK_A)
        sc = _dot(jnp.concatenate([p[:, hs] for p in pair], axis=1), sel) * masks[len(LEVELS)]
        for li in range(len(LEVELS)):
            sc = sc + _dot_nt(q_lv[li][:, hs], k_lv[li][:, hs]) * masks[li]
        st = states[h]
        outs.append(_dot(sc.astype(BF16), vb[:, hs]) + _dot_nt(q_in[:, hs], st.astype(BF16)))
        new_states.append(st_decay[:, hs] * st + _dot_tn(vb[:, hs], k_st[:, hs]))
    return jnp.concatenate(outs, axis=1), tuple(new_states)


def _hgrn_kernel(qa_ref, ff_ref, fb_ref, ia_ref, ga_ref, lb_ref, ng_ref, w_ref, m_ref, sel_ref,
                 *rest, seq_len, has_state):
    s0_ref = rest[0] if has_state else None
    o_ref, sout_ref, ob_ref = rest[-3:]
    nchunk = seq_len // CHUNK
    nseq = o_ref.shape[0]
    lb = lb_ref[...]
    sel = sel_ref[...]
    nmask = 1 + len(LEVELS)

    def scan(i, states):
        rf = pl.multiple_of(i * CHUNK, CHUNK)
        rb = pl.multiple_of((nchunk - 1 - i) * CHUNK, CHUNK)
        new_states = []
        for sq in range(nseq):
            o_f, st_f = _hgrn_chunk(0, sq, rf, qa_ref, ff_ref, ia_ref, lb, w_ref[0],
                                    [m_ref[0, j] for j in range(nmask)], sel, states[sq][0])
            o_b, st_b = _hgrn_chunk(1, sq, rb, qa_ref, fb_ref, ia_ref, lb, w_ref[1],
                                    [m_ref[1, j] for j in range(nmask)], sel, states[sq][1])
            o_ref[sq, pl.ds(rf, CHUNK), :] = o_f
            ob_ref[sq, pl.ds(rb, CHUNK), :] = o_b
            new_states.append((st_f, st_b))
        return tuple(new_states)

    start = lambda sq, d, h: s0_ref[sq, d, h].T if has_state else jnp.zeros((DK_A, DK_A), F32)
    init = tuple(tuple(tuple(start(sq, d, h) for h in range(H_A)) for d in range(2))
                 for sq in range(nseq))
    final = lax.fori_loop(0, nchunk, scan, init)

    ng = jnp.concatenate([ng_ref[...]] * H_A, axis=1)

    def gate(i, carry):
        rows = pl.ds(pl.multiple_of(i * CHUNK, CHUNK), CHUNK)
        for sq in range(nseq):
            tot = o_ref[sq, rows, :] + ob_ref[sq, rows, :]
            normed = jnp.concatenate(
                [_rms(tot[:, h * DK_A : (h + 1) * DK_A]) for h in range(H_A)], axis=1)
            o_ref[sq, rows, :] = normed * ng * _silu(ga_ref[sq, rows, :])
        return carry

    lax.fori_loop(0, nchunk, gate, 0)
    for sq in range(nseq):
        for d in range(2):
            for h in range(H_A):
                sout_ref[sq, d, h] = final[sq][d][h].T


def _hgrn(qa, ff, fb, ia, ga, lb, ng, s0, seq_len, nseq):
    bsz = qa.shape[0]
    w, m, sel = _hgrn_constants()
    seq = pl.BlockSpec((nseq, seq_len, W_A), lambda b: (b, 0, 0))
    full = lambda shape: pl.BlockSpec(shape, lambda b: (0,) * len(shape))
    st_spec = pl.BlockSpec((nseq, 2, H_A, DK_A, DK_A), lambda b: (b, 0, 0, 0, 0))
    return pl.pallas_call(
        functools.partial(_hgrn_kernel, seq_len=seq_len, has_state=s0 is not None),
        grid=(bsz // nseq,),
        in_specs=[seq, seq, seq, seq, seq, full((1, W_A)), full((1, DK_A)),
                  full(w.shape), full(m.shape), full(sel.shape)] + ([st_spec] if s0 is not None else []),
        out_specs=[seq, st_spec],
        out_shape=[jax.ShapeDtypeStruct((bsz, seq_len, W_A), F32),
                   jax.ShapeDtypeStruct((bsz, 2, H_A, DK_A, DK_A), F32)],
        scratch_shapes=[pltpu.VMEM((nseq, seq_len, W_A), F32)],
        compiler_params=_cparams("arbitrary"),
        name="hgrn",
    )(qa, ff, fb, ia, ga, lb.reshape(1, W_A), ng.reshape(1, DK_A),
      jnp.asarray(w, BF16), jnp.asarray(m, F32), jnp.asarray(sel, BF16),
      *([s0] if s0 is not None else []))


def _dft_constants(seq_len):
    n = 2 * seq_len
    t = np.arange(seq_len, dtype=np.int64)
    wt = (np.arange(seq_len, dtype=np.int64)[:, None] * t[None, :]) % n
    ang = 2.0 * np.pi * wt.astype(np.float64) / n
    cos, sin = np.cos(ang), np.sin(ang)
    nyq = np.where(t % 2 == 0, 1.0, -1.0)
    sin_p = sin.copy()
    sin_p[0] = nyq
    fwd = np.concatenate([cos, sin_p], axis=0)
    icos = 2.0 * cos.T / n
    icos[:, 0] = 1.0 / n
    isin = 2.0 * sin.T / n
    isin[:, 0] = nyq / n
    inv = np.concatenate([icos, isin], axis=1)
    return fwd, inv


def _filter_embedding(seq_len):
    t = np.linspace(0.0, 1.0, seq_len)[:, None]
    w = 2.0 * np.pi * np.arange(seq_len) / seq_len
    f = np.linspace(1e-4, FILTER_BANDS - 1, FILTER_BANDS)
    ang = w[:, None] * f[None, :]
    z = np.concatenate([t, np.cos(ang), -np.sin(ang)], axis=-1)
    zp = np.zeros((seq_len, LANES), np.float64)
    zp[:, : z.shape[1]] = z
    deltas = np.abs(np.linspace(DECAY_MIN, DECAY_MAX, C_B))
    window = np.exp(-t * deltas[None, :])
    return zp.astype(np.float32), window.astype(np.float32)


def _hyena_filter_kernel(z_ref, win_ref, w1_ref, b1_ref, fr1_ref, w2_ref, b2_ref, fr2_ref, w3_ref,
                         hb_ref, fh_ref, fl_ref, gr_ref, gi_ref, *, seq_len):
    h = jnp.sin(fr1_ref[...] * (_dot3(z_ref[...], w1_ref[...]) + b1_ref[...]))
    h = jnp.sin(fr2_ref[...] * (_dot3(h, w2_ref[...]) + b2_ref[...]))
    h = _dot3(h, w3_ref[...])
    win = win_ref[...]
    hf = h[:, :C_B] * win
    hbk = h[:, C_B:] * win

    def dft(x):
        xh, xl = _split_bf16(x)
        return _dot(fh_ref[...], xh) + _dot(fh_ref[...], xl) + _dot(fl_ref[...], xh)

    p_sum = dft(hf + hbk)
    p_dif = dft(hf - hbk)
    row0 = lax.broadcasted_iota(jnp.int32, (seq_len, C_B), 0) == 0
    gr_ref[...] = p_sum[:seq_len] + hb_ref[...]
    gi_ref[...] = jnp.where(row0, p_sum[seq_len:] + hb_ref[...], p_dif[seq_len:])


def _hyena_filter(seq_len, w1, b1, fr1, w2, b2, fr2, w3, hbias):
    zemb, window = _filter_embedding(seq_len)
    fwd, _ = _dft_constants(seq_len)
    f_hi, f_lo = _split_bf16(jnp.asarray(fwd, F32))

    def pad(a, rows, cols):
        return jnp.zeros((rows, cols), F32).at[: a.shape[0], : a.shape[1]].set(a)

    args = (jnp.asarray(zemb), jnp.asarray(window), pad(w1, LANES, LANES), pad(b1[None], 1, LANES),
            pad(fr1[None], 1, LANES), pad(w2, LANES, LANES), pad(b2[None], 1, LANES),
            pad(fr2[None], 1, LANES), pad(w3, LANES, 2 * C_B), hbias.reshape(1, C_B), f_hi, f_lo)
    return pl.pallas_call(
        functools.partial(_hyena_filter_kernel, seq_len=seq_len),
        out_shape=[jax.ShapeDtypeStruct((seq_len, C_B), F32)] * 2,
        compiler_params=pltpu.CompilerParams(vmem_limit_bytes=VMEM_LIMIT_BYTES),
        name="hyena_filter",
    )(*args)


def _hyena_kernel(x0_ref, x1_ref, v_ref, cw_ref, cb_ref, gr_ref, gi_ref, f_ref, fi_ref, o_ref, *,
                  seq_len):
    tc = o_ref.shape[2]
    row = lax.broadcasted_iota(jnp.int32, (seq_len, tc), 0)

    def short_conv(u_ref, j, sq):
        u = u_ref[sq]
        prev = jnp.where(row == 0, 0.0, pltpu.roll(u, 1, 0))
        nxt = jnp.where(row == seq_len - 1, 0.0, pltpu.roll(u, seq_len - 1, 0))
        cw = cw_ref[j]
        return cw[0:1] * prev + cw[1:2] * u + cw[2:3] * nxt + cb_ref[j]

    gr, gi = gr_ref[...], gi_ref[...]
    row0 = row == 0
    for sq in range(o_ref.shape[0]):
        x0 = short_conv(x0_ref, 0, sq)
        z = short_conv(v_ref, 2, sq) * short_conv(x1_ref, 1, sq)
        p = _dot(f_ref[...], z.astype(BF16))
        a, b = p[:seq_len], p[seq_len:]
        bgi = b * gi
        yr = a * gr - jnp.where(row0, 0.0, bgi)
        yq = jnp.where(row0, bgi, a * gi + b * gr)
        y = _dot(fi_ref[...], jnp.concatenate([yr, yq], axis=0).astype(BF16))
        o_ref[sq] = y * x0


def _hyena(hy, conv_w, conv_b, gr, gi, seq_len, nseq, tc=256):
    bsz = hy.shape[0]
    nct = C_B // tc
    fwd, inv = _dft_constants(seq_len)
    cw = conv_w.reshape(3, 3, C_B).transpose(1, 0, 2)
    cb = conv_b.reshape(3, 1, C_B)
    part = lambda k: pl.BlockSpec((nseq, seq_len, tc), lambda b, j, k=k: (b, 0, k * nct + j))
    return pl.pallas_call(
        functools.partial(_hyena_kernel, seq_len=seq_len),
        grid=(bsz // nseq, nct),
        in_specs=[part(0), part(1), part(2),
                  pl.BlockSpec((3, 3, tc), lambda b, j: (0, 0, j)),
                  pl.BlockSpec((3, 1, tc), lambda b, j: (0, 0, j)),
                  pl.BlockSpec((seq_len, tc), lambda b, j: (0, j)),
                  pl.BlockSpec((seq_len, tc), lambda b, j: (0, j)),
                  pl.BlockSpec((2 * seq_len, seq_len), lambda b, j: (0, 0)),
                  pl.BlockSpec((seq_len, 2 * seq_len), lambda b, j: (0, 0))],
        out_specs=pl.BlockSpec((nseq, seq_len, tc), lambda b, j: (b, 0, j)),
        out_shape=jax.ShapeDtypeStruct((bsz, seq_len, C_B), F32),
        compiler_params=_cparams("arbitrary", "arbitrary"),
        name="hyena",
    )(hy, hy, hy, cw, cb, gr, gi, jnp.asarray(fwd, F32).astype(BF16),
      jnp.asarray(inv, F32).astype(BF16))


def _rope_tables(seq_len, dim):
    rows = seq_len // GRID_W
    row_idx = np.repeat(np.arange(rows), GRID_W).astype(np.float64)
    col_idx = np.tile(np.arange(GRID_W), rows).astype(np.float64)
    half = dim // 2
    inv = ROPE_THETA ** (-np.arange(0, half, 2, dtype=np.float64) / half)
    ang = np.concatenate([row_idx[:, None] * inv, col_idx[:, None] * inv], axis=-1)
    cos = np.repeat(np.cos(ang), 2, axis=1)
    sin = np.repeat(np.sin(ang), 2, axis=1)
    sin[:, 0::2] *= -1.0
    reps = LANES // dim
    return (np.tile(cos, (1, reps)).astype(np.float32), np.tile(sin, (1, reps)).astype(np.float32))


def _lane_group_matrices():
    i = np.arange(2 * LANES)
    same = lambda width: (i[:, None] // width == i[None, :] // width).astype(np.float32)
    return same(HD_C), same(DK_D)


def _qk_proj_kernel(x_ref, m_ref, g_ref, w_ref, cqg_ref, ckg_ref, dqg_ref, dkg_ref, grp_c_ref,
                    grp_d_ref, *rest, use_rope, slabs):
    if use_rope:
        cc_ref, sc_ref, cd_ref, sd_ref = rest[:4]
        rest = rest[4:]
    qc_ref, qd_ref, kc_ref, vc_ref, kd_ref, vd_ref = rest
    m = m_ref[0]
    pair = 2 * LANES
    assert pair == KV_C * HD_C
    two = lambda r: jnp.concatenate([r, r], axis=1)
    rows_per_slab = x_ref.shape[1] // slabs

    for slab in range(slabs):
        rows = pl.ds(slab * rows_per_slab, rows_per_slab)
        seq_rows = qc_ref.shape[1]
        oseq = (slab * rows_per_slab) // seq_rows
        orows = pl.ds((slab * rows_per_slab) % seq_rows, rows_per_slab)
        hb = (_rms(x_ref[0, rows, :]) * g_ref[...] * (1.0 + m[1:2]) + m[0:1]).astype(BF16)
        cols = lambda start, n: _dot(hb, w_ref[:, start * LANES : (start + n) * LANES])

        def norm(x, grp_ref, width, g_ref, cos_ref, sin_ref):
            ms = _dot((x * x).astype(BF16), grp_ref[...]) * (1.0 / width)
            y = x * lax.rsqrt(ms + EPS) * two(g_ref[...])
            if not use_rope:
                return y
            even = (lax.broadcasted_iota(jnp.int32, y.shape, 1) % 2) == 0
            swapped = jnp.where(even, pltpu.roll(y, y.shape[1] - 1, 1), pltpu.roll(y, 1, 1))
            return y * two(cos_ref[rows, :]) + swapped * two(sin_ref[rows, :])

        norm_c = lambda x, g_ref: norm(x, grp_c_ref, HD_C, g_ref, cc_ref if use_rope else None,
                                       sc_ref if use_rope else None)
        norm_d = lambda x, g_ref: norm(x, grp_d_ref, DK_D, g_ref, cd_ref if use_rope else None,
                                       sd_ref if use_rope else None)
        for i in range(H_C // 2):
            qc_ref[oseq, orows, i * pair : (i + 1) * pair] = norm_c(cols(2 * i, 2), cqg_ref)
        kc = norm_c(cols(H_C, KV_C), ckg_ref)
        vc = cols(H_C + KV_C, KV_C)
        for j in range(KV_C):
            kc_ref[oseq, j, orows, :] = kc[:, j * LANES : (j + 1) * LANES]
            vc_ref[oseq, j, orows, :] = vc[:, j * LANES : (j + 1) * LANES]
        base = H_C + 2 * KV_C
        for i in range(H_D // 2):
            qd_ref[oseq, orows, i * pair : (i + 1) * pair] = norm_d(cols(base + 2 * i, 2), dqg_ref)
            kd = norm_d(cols(base + H_D + 2 * i, 2), dkg_ref)
            vd = cols(base + 2 * H_D + 2 * i, 2)
            for j in range(2):
                kd_ref[oseq, 2 * i + j, orows, :] = kd[:, j * LANES : (j + 1) * LANES]
                vd_ref[oseq, 2 * i + j, orows, :] = vd[:, j * LANES : (j + 1) * LANES]


def _qk_proj(x, mods, g, w_bf16, cqg, ckg, dqg, dkg, seq_len, use_rope, tm, slabs):
    ngrp = x.shape[0]
    spg = ROW_GROUP // seq_len
    bsz = ngrp * spg
    seq_rows = min(tm, seq_len)
    spt = tm // seq_rows
    assert seq_len % seq_rows == 0 and (tm // slabs) <= seq_rows
    seq_blk = lambda gi, i: (gi * spg + (i * tm) // seq_len) // spt
    blk_of = lambda i: ((i * tm) % seq_len) // seq_rows
    vec = lambda: pl.BlockSpec((1, LANES), lambda gi, i: (0, 0))
    mat = lambda: pl.BlockSpec((2 * LANES, 2 * LANES), lambda gi, i: (0, 0))
    tab = lambda: pl.BlockSpec((tm, LANES), lambda gi, i: (blk_of(i), 0))
    in_specs = [pl.BlockSpec((1, tm, D_MODEL), lambda gi, i: (gi, i, 0)),
                pl.BlockSpec((1, 6, D_MODEL), lambda gi, i: (gi, 0, 0)),
                pl.BlockSpec((1, D_MODEL), lambda gi, i: (0, 0)),
                pl.BlockSpec(w_bf16.shape, lambda gi, i: (0, 0)),
                vec(), vec(), vec(), vec(), mat(), mat()]
    args = [x, mods, g.reshape(1, D_MODEL), w_bf16, cqg.reshape(1, HD_C), ckg.reshape(1, HD_C),
            jnp.tile(dqg.reshape(1, DK_D), (1, 2)), jnp.tile(dkg.reshape(1, DK_D), (1, 2))]
    args += [jnp.asarray(mm, BF16) for mm in _lane_group_matrices()]
    if use_rope:
        in_specs += [tab(), tab(), tab(), tab()]
        args += [jnp.asarray(t) for t in _rope_tables(seq_len, HD_C) + _rope_tables(seq_len, DK_D)]
    tok = lambda w: pl.BlockSpec((spt, seq_rows, w), lambda gi, i: (seq_blk(gi, i), blk_of(i), 0))
    head = lambda nh: pl.BlockSpec((spt, nh, seq_rows, LANES),
                                   lambda gi, i: (seq_blk(gi, i), 0, blk_of(i), 0))
    tok_shape = lambda w: jax.ShapeDtypeStruct((bsz, seq_len, w), F32)
    head_shape = lambda nh: jax.ShapeDtypeStruct((bsz, nh, seq_len, LANES), F32)
    return pl.pallas_call(
        functools.partial(_qk_proj_kernel, use_rope=use_rope, slabs=slabs),
        grid=(ngrp, ROW_GROUP // tm),
        in_specs=in_specs,
        out_specs=[tok(H_C * HD_C), tok(H_D * 2 * DK_D), head(KV_C), head(KV_C), head(H_D), head(H_D)],
        out_shape=[tok_shape(H_C * HD_C), tok_shape(H_D * 2 * DK_D), head_shape(KV_C),
                   head_shape(KV_C), head_shape(H_D), head_shape(H_D)],
        compiler_params=_cparams("arbitrary", "arbitrary"),
        name="qk_proj",
    )(*args)


def _softmax_pv(q_list, kv_list):
    outs = []
    for q in q_list:
        scores = [_dot_nt(q, k) for k, _ in kv_list]
        mx = scores[0].max(axis=1, keepdims=True)
        for s in scores[1:]:
            mx = jnp.maximum(mx, s.max(axis=1, keepdims=True))
        den = 0.0
        acc = 0.0
        for s, (_, v) in zip(scores, kv_list):
            pexp = jnp.exp2(s - mx)
            den = den + pexp.sum(axis=1, keepdims=True)
            acc = acc + _dot(pexp.astype(BF16), v)
        outs.append(acc / den)
    return outs


def _head_kv(sq, j, k_ref, v_ref, cache_refs):
    kv = [(r_k[sq, 0, j].astype(BF16), r_v[sq, 0, j].astype(BF16)) for r_k, r_v in cache_refs]
    kv.append((k_ref[sq, j].astype(BF16), v_ref[sq, j].astype(BF16)))
    return kv


def _per_slab(sq, q_ref, o_ref, cols, fn):
    for r0 in range(0, q_ref.shape[1], ATT_SLAB):
        rows = pl.ds(r0, min(ATT_SLAB, q_ref.shape[1]))
        o_ref[sq, rows, cols] = fn(q_ref[sq, rows, cols])


def _gqa_kernel(q_ref, k_ref, v_ref, *rest, has_cache):
    cache_refs = [rest[:2]] if has_cache else []
    o_ref = rest[-1]
    g_c = H_C // KV_C
    width = g_c * HD_C
    for sq, j in itertools.product(range(k_ref.shape[0]), range(k_ref.shape[1])):
        kv = _head_kv(sq, j, k_ref, v_ref, cache_refs)

        def head(q, kv=kv):
            q = q * (HD_C ** -0.5 * LOG2E)
            qs = [q[:, g * HD_C : (g + 1) * HD_C].astype(BF16) for g in range(g_c)]
            return jnp.concatenate(_softmax_pv(qs, kv), axis=1)

        _per_slab(sq, q_ref, o_ref, slice(j * width, (j + 1) * width), head)


def _diff_kernel(q_ref, k_ref, v_ref, lam_ref, sg_ref, *rest, has_cache, out_scale):
    cache_refs = [rest[:2]] if has_cache else []
    o_ref = rest[-1]
    width = 2 * DK_D
    for sq, j in itertools.product(range(k_ref.shape[0]), range(k_ref.shape[1])):
        kv = _head_kv(sq, j, k_ref, v_ref, cache_refs)

        def head(q, kv=kv):
            q = q * (DK_D ** -0.5 * LOG2E)
            low = lax.broadcasted_iota(jnp.int32, q.shape, 1) < DK_D
            qs = [jnp.where(low, q, 0.0).astype(BF16), jnp.where(low, 0.0, q).astype(BF16)]
            o1, o2 = _softmax_pv(qs, kv)
            return _rms(o1 - lam_ref[...] * o2) * sg_ref[...] * out_scale

        _per_slab(sq, q_ref, o_ref, slice(j * width, (j + 1) * width), head)


def _attention(q, k, v, cache_k, cache_v, kernel, extra_args, n_heads, q_width, seq_len, hps, tq,
               nseq):
    bsz = q.shape[0]
    in_specs = [pl.BlockSpec((nseq, tq, hps * q_width), lambda b, h, i: (b, i, h)),
                pl.BlockSpec((nseq, hps, seq_len, LANES), lambda b, h, i: (b, h, 0, 0)),
                pl.BlockSpec((nseq, hps, seq_len, LANES), lambda b, h, i: (b, h, 0, 0))]
    in_specs += [pl.BlockSpec((1, LANES), lambda b, h, i: (0, 0)) for _ in extra_args]
    args = [q, k, v, *extra_args]
    if cache_k is not None:
        past = cache_k.shape[3]
        spec = lambda: pl.BlockSpec((nseq, 1, hps, past, LANES), lambda b, h, i: (b, 0, h, 0, 0))
        in_specs += [spec(), spec()]
        args += [cache_k, cache_v]
    return pl.pallas_call(
        functools.partial(kernel, has_cache=cache_k is not None),
        grid=(bsz // nseq, n_heads // hps, seq_len // tq),
        in_specs=in_specs,
        out_specs=pl.BlockSpec((nseq, tq, hps * q_width), lambda b, h, i: (b, i, h)),
        out_shape=jax.ShapeDtypeStruct((bsz, seq_len, n_heads * q_width), F32),
        compiler_params=_cparams("arbitrary", "arbitrary", "arbitrary"),
        name="attention",
    )(*args)


def _route(lg):
    lane = lax.broadcasted_iota(jnp.int32, lg.shape, 1).astype(F32)
    neg = -1e30
    is_g = (lane >= N_EXPERTS) & (lane < N_EXPERTS + N_GROUPS)
    gl = jnp.where(is_g, lg, neg)
    gmax = gl.max(axis=1, keepdims=True)
    g_p = 1.0 / jnp.where(is_g, jnp.exp(gl - gmax), 0.0).sum(axis=1, keepdims=True)
    g_i = jnp.where(gl == gmax, lane - N_EXPERTS, 1e9).min(axis=1, keepdims=True)
    in_grp = (lane < N_EXPERTS) & (jnp.floor(lane * (1.0 / EXP_PER_GROUP)) == g_i)
    el = jnp.where(in_grp, lg, neg)
    m1 = el.max(axis=1, keepdims=True)
    i1 = jnp.where(in_grp & (el == m1), lane, 1e9).min(axis=1, keepdims=True)
    el2 = jnp.where(lane == i1, neg, el)
    m2 = el2.max(axis=1, keepdims=True)
    i2 = jnp.where(in_grp & (el2 == m2) & (lane != i1), lane, 1e9).min(axis=1, keepdims=True)
    r = jnp.exp(m2 - m1)
    w1 = g_p / (1.0 + r)
    rec = jnp.where(lane == ROUTE_LANE, i1, 0.0)
    rec = jnp.where(lane == ROUTE_LANE + 1, i2, rec)
    rec = jnp.where(lane == ROUTE_LANE + 2, w1, rec)
    return jnp.where(lane == ROUTE_LANE + 3, w1 * r, rec)


def _mix_out_kernel(x_ref, a_ref, b_ref, m_ref, g2_ref, w_ref, wr_ref, br_ref, tri_ref,
                    x1_ref, xs_ref, pos_ref, cnt_ref, *, parts):
    wa = a_ref.shape[2]
    m = m_ref[0]
    rows_per_part = x_ref.shape[1] // parts
    h2_parts, route_parts = [], []
    for part in range(parts):
        rows = pl.ds(part * rows_per_part, rows_per_part)
        o = (_dot(a_ref[0, rows, :].astype(BF16), w_ref[:wa])
             + _dot(b_ref[0, rows, :].astype(BF16), w_ref[wa:]))
        x1 = x_ref[0, rows, :] + m[2:3] * o
        x1_ref[0, rows, :] = x1
        h2 = _rms(x1) * g2_ref[...] * (1.0 + m[4:5]) + m[3:4]
        h2_parts.append(h2.astype(BF16))
        r = _dot(jnp.concatenate(_split_bf16(h2), axis=0), wr_ref[...])
        hi_rows, lo_rows = r[:rows_per_part], r[rows_per_part:]
        lg = hi_rows[:, :LANES] + hi_rows[:, LANES:] + lo_rows[:, :LANES] + br_ref[...]
        route_parts.append(_route(lg))
    xs, pos, cnt = _moe_sort_tile(jnp.concatenate(h2_parts, axis=0),
                                  jnp.concatenate(route_parts, axis=0), tri_ref[...])
    xs_ref[...] = xs
    pos_ref[0] = pos
    cnt_ref[0] = cnt


def _mix_out(x, a, b, mods, g2, w_out_bf16, w_router, b_router, parts=2):
    ngrp = x.shape[0]
    per = ROW_GROUP // MOE_TILE
    ntile = ngrp * per
    wa, wb = a.shape[2], b.shape[2]
    wr = jnp.concatenate(_split_bf16(w_router), axis=1)
    tri = np.tril(np.ones((MOE_TILE, MOE_TILE), np.float32), -1)
    src = lambda j: jnp.minimum(j, ntile - 1)
    row = lambda w: pl.BlockSpec((1, MOE_TILE, w), lambda j: (src(j) // per, src(j) % per, 0))
    full = lambda shape: pl.BlockSpec(shape, lambda j: (0,) * len(shape))
    return pl.pallas_call(
        functools.partial(_mix_out_kernel, parts=parts),
        grid=(ntile + 1,),
        in_specs=[row(D_MODEL), row(wa), row(wb),
                  pl.BlockSpec((1, 6, D_MODEL), lambda j: (src(j) // per, 0, 0)),
                  full((1, D_MODEL)), full((wa + wb, D_MODEL)),
                  full((D_MODEL, 2 * LANES)), full((1, LANES)), full((MOE_TILE, MOE_TILE))],
        out_specs=[pl.BlockSpec((1, MOE_TILE, D_MODEL), lambda j: (j, 0, 0)),
                   pl.BlockSpec((MOE_TILE_ROWS, MOE_ROW_W), lambda j: (j, 0)),
                   pl.BlockSpec((1, MOE_TILE, LANES), lambda j: (j, 0, 0)),
                   pl.BlockSpec((1, 8, LANES), lambda j: (j, 0, 0))],
        out_shape=[jax.ShapeDtypeStruct((ntile + 1, MOE_TILE, D_MODEL), F32),
                   jax.ShapeDtypeStruct(((ntile + 1) * MOE_TILE_ROWS, MOE_ROW_W), BF16),
                   jax.ShapeDtypeStruct((ntile + 1, MOE_TILE, LANES), F32),
                   jax.ShapeDtypeStruct((ntile + 1, 8, LANES), F32)],
        compiler_params=_cparams("arbitrary"),
        name="mix_out",
    )(x, a, b, mods, g2.reshape(1, D_MODEL), w_out_bf16, wr, b_router, jnp.asarray(tri, BF16))


def _lane_col(x, lane, k):
    return jnp.where(lane == k, x, 0.0).sum(axis=1, keepdims=True)


def _moe_sort_tile(h, r, tri):
    lane = lax.broadcasted_iota(jnp.int32, r.shape, 1).astype(F32)
    i1, i2, w1, w2 = [_lane_col(r, lane, ROUTE_LANE + k) for k in range(4)]
    oh1 = lane == i1
    oh2 = lane == i2
    oh = jnp.where(oh1 | oh2, 1.0, 0.0)
    rank = _dot(tri, oh.astype(BF16))
    cnt = oh.sum(axis=0, keepdims=True)
    chunks = jnp.floor((cnt + (MOE_CHUNK - 1)) * (1.0 / MOE_CHUNK))
    li = lax.broadcasted_iota(jnp.int32, (LANES, LANES), 0)
    lj = lax.broadcasted_iota(jnp.int32, (LANES, LANES), 1)
    before = jnp.where(li < lj, 1.0, 0.0).astype(BF16)
    seg = _dot(jnp.broadcast_to(chunks, (8, LANES)).astype(BF16), before)[0:1]
    base = seg * MOE_CHUNK + rank
    pos1 = jnp.where(oh1, base, 0.0).sum(axis=1, keepdims=True)
    pos2 = jnp.where(oh2, base, 0.0).sum(axis=1, keepdims=True)
    riota = lax.broadcasted_iota(jnp.int32, (r.shape[0], MOE_TILE_ROWS), 1).astype(F32)
    p = jnp.where((riota == pos1) | (riota == pos2), 1.0, 0.0).astype(BF16)

    aux = jnp.where(lane == 2 * MOE_W_PIECES, i1, 0.0)
    for k, w in enumerate((w1, w2)):
        rest = w
        for piece in range(MOE_W_PIECES):
            part = rest.astype(BF16).astype(F32)
            aux = jnp.where(lane == k * MOE_W_PIECES + piece, part, aux)
            rest = rest - part
    row = jnp.concatenate([h, aux.astype(BF16)], axis=1)
    xs = _dot_tn(p, row).astype(BF16)
    pos = jnp.where(lane == 0, pos1, jnp.where(lane == 1, pos2, 0.0))
    return xs, pos, jnp.broadcast_to(cnt, (8, LANES))


def _moe_tables(cnt, ntile):
    nblk = ntile * MOE_TILE_CHUNKS // MOE_BLOCK_CHUNKS + N_EXPERTS
    chunks = (cnt + MOE_CHUNK - 1) // MOE_CHUNK
    seg_start = jnp.cumsum(chunks, axis=1) - chunks
    tile_prefix = jnp.cumsum(chunks, axis=0) - chunks
    per_expert = chunks.sum(axis=0)
    blocks = (per_expert + MOE_BLOCK_CHUNKS - 1) // MOE_BLOCK_CHUNKS
    blk_end = jnp.cumsum(blocks)
    n_used = blk_end[-1]
    b = jnp.arange(nblk + 1, dtype=jnp.int32)
    blk_e = jnp.sum(b[:, None] >= blk_end[None, :], axis=1).astype(jnp.int32)
    last_e = jnp.sum((n_used - 1) >= blk_end).astype(jnp.int32)
    blk_e = jnp.clip(jnp.where(b < n_used, blk_e, last_e), 0, N_EXPERTS - 1)
    oh_e = (blk_e[:, None] == jnp.arange(N_EXPERTS)[None, :]).astype(jnp.int32)
    pick = lambda per_tile: jnp.sum(oh_e[:, :, None] * per_tile.T[None], axis=1)
    seg_e, pre_e, chunks_e = pick(seg_start), pick(tile_prefix), pick(chunks)
    first_blk = jnp.sum(oh_e * (blk_end - blocks)[None, :], axis=1)
    k = (b - first_blk)[:, None] * MOE_BLOCK_CHUNKS + jnp.arange(MOE_BLOCK_CHUNKS)[None, :]
    k3 = k[:, :, None]
    in_tile = (pre_e[:, None, :] <= k3) & (k3 < (pre_e + chunks_e)[:, None, :])
    tile_base = (jnp.arange(ntile) * MOE_TILE_CHUNKS)[None, :] + seg_e - pre_e
    src = jnp.sum(jnp.where(in_tile, tile_base[:, None, :] + k3, 0), axis=-1)
    valid = jnp.any(in_tile, axis=-1) & (b < n_used)[:, None]
    slot_c = jnp.arange(MOE_BLOCK_CHUNKS)[None, :]
    n_read = MOE_TILE_CHUNKS - 2 * MOE_BLOCK_CHUNKS
    assert n_read > 0
    spare = ntile * MOE_TILE_CHUNKS
    gather = jnp.where(valid, src, spare + 2 * MOE_BLOCK_CHUNKS + slot_c % n_read).astype(jnp.int32)
    scatter = jnp.where(valid, src, spare + (b % 2)[:, None] * MOE_BLOCK_CHUNKS + slot_c).astype(jnp.int32)
    blk_start = jnp.concatenate([blk_end - blocks, n_used[None]]).astype(jnp.int32)
    return blk_start, gather.reshape(-1), scatter.reshape(-1)


def _moe_expert_kernel(start_ref, gather_ref, scatter_ref, xs_hbm, wg_ref, wu_ref, wd_ref, ys_hbm,
                       lhs, obuf, wgb, wub, wdb, in_sem, out_sem):
    e = pl.program_id(0)
    ne = pl.num_programs(0)
    b0 = start_ref[e]
    b1 = start_ref[e + 1]
    n = start_ref[ne]

    def chunk_copy(blk, slot, c, gather):
        rows = pl.ds(c * MOE_CHUNK, MOE_CHUNK)
        if gather:
            idx = gather_ref[blk * MOE_BLOCK_CHUNKS + c]
            return pltpu.make_async_copy(xs_hbm.at[idx], lhs.at[slot, rows], in_sem.at[slot])
        idx = scatter_ref[blk * MOE_BLOCK_CHUNKS + c]
        dst = ys_hbm.at[idx, pl.ds(0, MOE_CHUNK), pl.ds(0, D_MODEL)]
        return pltpu.make_async_copy(obuf.at[slot, rows], dst, out_sem.at[slot])

    def for_chunks(blk, slot, gather, start):
        for c in range(MOE_BLOCK_CHUNKS):
            cp = chunk_copy(blk, slot, c, gather)
            if start:
                cp.start()
            else:
                cp.wait()

    @pl.when((e == 0) & (n > 0))
    def _():
        for_chunks(0, 0, True, True)

    @pl.when(b1 > b0)
    def _():
        wgb[...] = wg_ref[0, 0].astype(BF16)
        wub[...] = wu_ref[0, 0].astype(BF16)
        wdb[...] = wd_ref[0, 0].astype(BF16)
        e_f32 = e.astype(F32)

        def block(b, carry):
            slot = b % 2

            @pl.when(b >= 2)
            def _():
                for_chunks(b - 2, slot, False, False)

            for_chunks(b, slot, True, False)
            for_chunks(b + 1, 1 - slot, True, True)
            xa = lhs[slot]
            x = xa[:, :D_MODEL]
            aux = xa[:, D_MODEL:].astype(F32)
            lane = lax.broadcasted_iota(jnp.int32, aux.shape, 1)
            first = lane < MOE_W_PIECES
            w_first = jnp.where(first, aux, 0.0).sum(axis=1, keepdims=True)
            w_second = jnp.where(first | (lane >= 2 * MOE_W_PIECES), 0.0, aux).sum(axis=1, keepdims=True)
            e_first = _lane_col(aux, lane, 2 * MOE_W_PIECES)
            w = jnp.where(e_first == e_f32, w_first, w_second)
            hid = _silu(_dot(x, wgb[...])) * _dot(x, wub[...]) * w
            obuf[slot] = _dot(hid.astype(BF16), wdb[...]).astype(BF16)
            for_chunks(b, slot, False, True)
            return carry

        lax.fori_loop(b0, b1, block, 0)

    @pl.when(e == ne - 1)
    def _():
        @pl.when(n >= 2)
        def _():
            for_chunks(n - 2, n % 2, False, False)

        @pl.when(n >= 1)
        def _():
            for_chunks(n, n % 2, True, False)
            for_chunks(n - 1, (n - 1) % 2, False, False)


def _moe_experts(xs, blk_start, gather, scatter, layer, w_gate, w_up, w_down):
    nchunk = xs.shape[0] // MOE_CHUNK
    rows_per_blk = MOE_BLOCK_CHUNKS * MOE_CHUNK
    wspec = lambda shape: pl.BlockSpec((1, 1) + shape, lambda e, st, g, s: (layer, e, 0, 0))
    hbm = pl.BlockSpec(memory_space=pl.ANY)
    ys = pl.pallas_call(
        _moe_expert_kernel,
        grid_spec=pltpu.PrefetchScalarGridSpec(
            num_scalar_prefetch=3,
            grid=(N_EXPERTS,),
            in_specs=[hbm, wspec((D_MODEL, D_EXPERT)), wspec((D_MODEL, D_EXPERT)),
                      wspec((D_EXPERT, D_MODEL))],
            out_specs=hbm,
            scratch_shapes=[pltpu.VMEM((2, rows_per_blk, MOE_ROW_W), BF16),
                            pltpu.VMEM((2, rows_per_blk, D_MODEL), BF16),
                            pltpu.VMEM((D_MODEL, D_EXPERT), BF16),
                            pltpu.VMEM((D_MODEL, D_EXPERT), BF16),
                            pltpu.VMEM((D_EXPERT, D_MODEL), BF16),
                            pltpu.SemaphoreType.DMA((2,)),
                            pltpu.SemaphoreType.DMA((2,))]),
        out_shape=jax.ShapeDtypeStruct((nchunk, MOE_CHUNK, MOE_ROW_W), BF16),
        input_output_aliases={3: 0},
        compiler_params=_cparams("arbitrary"),
        name="moe_experts",
    )(blk_start, gather, scatter, xs.reshape(nchunk, MOE_CHUNK, MOE_ROW_W), w_gate, w_up, w_down)
    return ys.reshape(nchunk * MOE_CHUNK, MOE_ROW_W)


def _moe_combine_kernel(ys_ref, pos_ref, x_ref, m_ref, o_ref):
    pos = pos_ref[0]
    lane = lax.broadcasted_iota(jnp.int32, pos.shape, 1)
    pos1 = _lane_col(pos, lane, 0)
    pos2 = _lane_col(pos, lane, 1)
    riota = lax.broadcasted_iota(jnp.int32, (pos.shape[0], MOE_TILE_ROWS), 1).astype(F32)
    p = jnp.where((riota == pos1) | (riota == pos2), 1.0, 0.0).astype(BF16)
    o_ref[0] = x_ref[0] + m_ref[0][5:6] * _dot(p, ys_ref[...])


def _moe_combine(ys, pos, x1, mods):
    ngrp = mods.shape[0]
    per = ROW_GROUP // MOE_TILE
    tile = pl.BlockSpec((1, MOE_TILE, D_MODEL), lambda j: (j, 0, 0))
    return pl.pallas_call(
        _moe_combine_kernel,
        grid=(ngrp * per,),
        in_specs=[pl.BlockSpec((MOE_TILE_ROWS, D_MODEL), lambda j: (j, 0)),
                  pl.BlockSpec((1, MOE_TILE, LANES), lambda j: (j, 0, 0)),
                  tile, pl.BlockSpec((1, 6, D_MODEL), lambda j: (j // per, 0, 0))],
        out_specs=pl.BlockSpec((1, MOE_TILE, D_MODEL), lambda j: (j // per, j % per, 0)),
        out_shape=jax.ShapeDtypeStruct((ngrp, ROW_GROUP, D_MODEL), F32),
        compiler_params=_cparams("arbitrary"),
        name="moe_combine",
    )(ys, pos, x1, mods)


def _moe(xs, pos, cnt, x1, mods, layer, w_gate, w_up, w_down):
    ntile = mods.shape[0] * (ROW_GROUP // MOE_TILE)
    cnt = cnt[:ntile, 0, :N_EXPERTS].astype(jnp.int32)
    blk_start, gather, scatter = _moe_tables(cnt, ntile)
    ys = _moe_experts(xs, blk_start, gather, scatter, layer, w_gate, w_up, w_down)
    return _moe_combine(ys, pos, x1, mods)


def kernel(x_prompt, x_sample, state_hgrn, cache_c_k, cache_c_v, cache_d_k, cache_d_v, c, c_ctx, norm1_g, norm2_g, w_mod, b_mod, even_w_in, even_w_out, hgrn_lower, hgrn_norm_g, hy_conv_w, hy_conv_b, hy_w1, hy_b1, hy_freq1, hy_w2, hy_b2, hy_freq2, hy_w3, hy_bias, odd_w_in, odd_w_out, c_qnorm_g, c_knorm_g, d_qnorm_g, d_knorm_g, d_lambda_q1, d_lambda_k1, d_lambda_q2, d_lambda_k2, d_subln_g, moe_w_grp, moe_b_grp, moe_w_rt, moe_b_rt, moe_w_gate, moe_w_up, moe_w_down):
    depth = w_mod.shape[0]
    n_ctx, seq, _ = x_prompt.shape
    n_lat, dec_seq, _ = x_sample.shape
    g_ctx = n_ctx * seq // ROW_GROUP
    g_lat = n_lat * dec_seq // ROW_GROUP
    assert dec_seq == ROW_GROUP and ROW_GROUP % seq == 0

    cond = jnp.zeros((16, D_MODEL), F32).at[0].set(c_ctx).at[1 : 1 + n_lat].set(c)
    mods = _adaln(cond, w_mod, b_mod).reshape(depth, 16, 6, D_MODEL)
    lower = jnp.cumsum(jax.nn.softmax(hgrn_lower.astype(F32), axis=0), axis=0)

    streams = [
        dict(x=x_prompt.reshape(g_ctx, ROW_GROUP, D_MODEL), ngrp=g_ctx, bsz=n_ctx, seq=seq, ctx=True),
        dict(x=x_sample, ngrp=g_lat, bsz=n_lat, seq=dec_seq, ctx=False),
    ]
    new_state, new_ck, new_cv, new_dk, new_dv = [], [], [], [], []

    for l in range(depth):
        j = l // 2
        w_router = jnp.zeros((D_MODEL, LANES), F32)
        w_router = w_router.at[:, :N_EXPERTS].set(moe_w_rt[l])
        w_router = w_router.at[:, N_EXPERTS : N_EXPERTS + N_GROUPS].set(moe_w_grp[l])
        b_router = jnp.zeros((1, LANES), F32)
        b_router = b_router.at[0, :N_EXPERTS].set(moe_b_rt[l])
        b_router = b_router.at[0, N_EXPERTS : N_EXPERTS + N_GROUPS].set(moe_b_grp[l])
        if l % 2 == 0:
            w_in = even_w_in[j].astype(BF16)
            w_out = even_w_out[j].astype(BF16)
        else:
            w_in = odd_w_in[j].astype(BF16)
            w_out = odd_w_out[j].astype(BF16)
            lam_init = 0.8 - 0.6 * math.exp(-0.3 * l)
            lam = (jnp.exp(jnp.sum(d_lambda_q1[j] * d_lambda_k1[j]))
                   - jnp.exp(jnp.sum(d_lambda_q2[j] * d_lambda_k2[j])) + lam_init)
            lam_row = jnp.full((1, LANES), lam, F32)

        for s in streams:
            ngrp, bsz, sl = s["ngrp"], s["bsz"], s["seq"]
            if s["ctx"]:
                m = jnp.broadcast_to(mods[l, 0][None], (ngrp, 6, D_MODEL))
            else:
                m = mods[l, 1 : 1 + ngrp]
            x = s["x"]
            if l % 2 == 0:
                wa = H_A * DK_A
                splits = [(0, wa), (wa, 2 * wa), (2 * wa, 3 * wa), (3 * wa, 3 * wa + W_A),
                          (3 * wa + W_A, 3 * wa + 2 * W_A), (3 * wa + 2 * W_A, w_in.shape[1])]
                qa, ffa, fba, ia, ga, hy = _norm_proj(x, m, norm1_g[l], w_in, splits)
                per_seq = lambda t: t.reshape(bsz, sl, t.shape[-1])
                s0 = None if s["ctx"] else state_hgrn[:, j].astype(F32)
                mix_a, s_fin = _hgrn(per_seq(qa), per_seq(ffa), per_seq(fba), per_seq(ia),
                                     per_seq(ga), lower[j], hgrn_norm_g[j], s0, sl,
                                     nseq=2 if s["ctx"] else 1)
                gr, gi = _hyena_filter(sl, hy_w1[j], hy_b1[j], hy_freq1[j], hy_w2[j], hy_b2[j],
                                       hy_freq2[j], hy_w3[j], hy_bias[j])
                mix_b = _hyena(per_seq(hy), hy_conv_w[j], hy_conv_b[j], gr, gi, sl,
                               nseq=4 if s["ctx"] else 2)
                if s["ctx"]:
                    new_state.append(s_fin)
            else:
                qc, qd, kc, vc, kd, vd = _qk_proj(x, m, norm1_g[l], w_in, c_qnorm_g[j], c_knorm_g[j],
                                                  d_qnorm_g[j], d_knorm_g[j], sl, use_rope=not s["ctx"],
                                                  tm=512, slabs=2)
                if s["ctx"]:
                    caches = (None, None, None, None)
                    new_ck.append(kc)
                    new_cv.append(vc)
                    new_dk.append(kd)
                    new_dv.append(vd)
                else:
                    caches = (cache_c_k[:, j : j + 1], cache_c_v[:, j : j + 1],
                              cache_d_k[:, j : j + 1], cache_d_v[:, j : j + 1])
                hps_c, hps_d, nseq = (KV_C, H_D, 2) if s["ctx"] else (1, 1, 1)
                mix_a = _attention(qc, kc, vc, caches[0], caches[1], _gqa_kernel, (), KV_C,
                                   (H_C // KV_C) * HD_C, sl, hps_c, sl, nseq)
                diff = functools.partial(_diff_kernel, out_scale=1.0 - lam_init)
                mix_b = _attention(qd, kd, vd, caches[2], caches[3], diff,
                                   (lam_row, d_subln_g[j].reshape(1, DV_D)), H_D, 2 * DK_D, sl,
                                   hps_d, sl, nseq)
            grp = lambda t: t.reshape(ngrp, ROW_GROUP, t.shape[-1])
            x1, xs, pos, cnt = _mix_out(x, grp(mix_a), grp(mix_b), m, norm2_g[l], w_out, w_router,
                                        b_router)
            s["x"] = _moe(xs, pos, cnt, x1, m, l, moe_w_gate, moe_w_up, moe_w_down)

    y_ctx = streams[0]["x"].reshape(n_ctx, seq, D_MODEL)
    y_lat = streams[1]["x"]
    return (y_ctx, y_lat, jnp.stack(new_state, axis=1), jnp.stack(new_ck, axis=1),
            jnp.stack(new_cv, axis=1), jnp.stack(new_dk, axis=1), jnp.stack(new_dv, axis=1))
```

```python
import functools
import itertools
import math

import numpy as np
import jax
import jax.numpy as jnp
from jax import lax
from jax.experimental import pallas as pl
from jax.experimental.pallas import tpu as pltpu

F32 = jnp.float32
BF16 = jnp.bfloat16

D_MODEL = 1024
EPS = 1e-6
LOG2E = 1.0 / math.log(2.0)
GRID_W = 64
ROPE_THETA = 10000.0
H_A = 4
DK_A = 128
W_A = 512
CHUNK = 128
SUB = 8
LEVELS = (16, 32, 64, 128)
C_B = 512
FILTER_BANDS = 16
DECAY_MIN = math.log(1e-2) / 1.5
DECAY_MAX = math.log(1e-2) / 0.3
H_C = 4
KV_C = 2
HD_C = 128
H_D = 4
DK_D = 64
DV_D = 128
N_GROUPS = 4
EXP_PER_GROUP = 4
N_EXPERTS = 16
D_EXPERT = 512

ROUTE_LANE = N_EXPERTS + N_GROUPS
MOE_TILE = 512
MOE_CHUNK = 16
MOE_TILE_CHUNKS = 2 * MOE_TILE // MOE_CHUNK + N_EXPERTS
MOE_TILE_ROWS = MOE_TILE_CHUNKS * MOE_CHUNK
MOE_BLOCK_CHUNKS = 16
MOE_ROW_W = D_MODEL + 128
MOE_W_PIECES = 3
ATT_SLAB = 256

LANES = 128
ROW_GROUP = 1024
VMEM_LIMIT_BYTES = 56 * 1024 * 1024


def _cparams(*sem):
    return pltpu.CompilerParams(dimension_semantics=sem, vmem_limit_bytes=VMEM_LIMIT_BYTES)


def _split_bf16(x):
    hi = x.astype(BF16)
    lo = (x - hi.astype(F32)).astype(BF16)
    return hi, lo


def _dot(a, b):
    return jnp.dot(a, b, preferred_element_type=F32)


def _dot3(a, b):
    ah, al = _split_bf16(a)
    bh, bl = _split_bf16(b)
    return _dot(ah, bh) + _dot(al, bh) + _dot(ah, bl)


def _dot_nt(a, b):
    return lax.dot_general(a, b, (((1,), (1,)), ((), ())), preferred_element_type=F32)


def _dot_tn(a, b):
    return lax.dot_general(a, b, (((0,), (0,)), ((), ())), preferred_element_type=F32)


def _silu(x):
    return x * jax.nn.sigmoid(x)


def _rms(x, eps=EPS):
    return x * lax.rsqrt(jnp.mean(x * x, axis=-1, keepdims=True) + eps)


def _adaln_kernel(c_ref, w_ref, b_ref, o_ref):
    s = _silu(c_ref[...])
    o_ref[0] = _dot(s.astype(BF16), w_ref[0].astype(BF16)) + b_ref[0]


def _adaln(cond, w_mod, b_mod):
    depth, _, n = w_mod.shape
    rows = cond.shape[0]
    tn = 1536
    return pl.pallas_call(
        _adaln_kernel,
        grid=(depth, n // tn),
        in_specs=[
            pl.BlockSpec((rows, D_MODEL), lambda l, j: (0, 0)),
            pl.BlockSpec((1, D_MODEL, tn), lambda l, j: (l, 0, j)),
            pl.BlockSpec((1, 1, tn), lambda l, j: (l, 0, j)),
        ],
        out_specs=pl.BlockSpec((1, rows, tn), lambda l, j: (l, 0, j)),
        out_shape=jax.ShapeDtypeStruct((depth, rows, n), F32),
        compiler_params=_cparams("arbitrary", "arbitrary"),
        name="adaln",
    )(cond, w_mod, b_mod.reshape(depth, 1, n))


def _norm_proj_kernel(x_ref, m_ref, g_ref, w_ref, *o_refs, splits):
    m = m_ref[0]
    h = _rms(x_ref[0]) * g_ref[...] * (1.0 + m[1:2]) + m[0:1]
    hb = h.astype(BF16)
    for o_ref, (a, b) in zip(o_refs, splits):
        o_ref[0] = _dot(hb, w_ref[:, a:b])


def _norm_proj(x, mods, g, w_bf16, splits, tm=512):
    ngrp = x.shape[0]
    n = w_bf16.shape[1]
    kern = functools.partial(_norm_proj_kernel, splits=splits)
    return pl.pallas_call(
        kern,
        grid=(ngrp, ROW_GROUP // tm),
        in_specs=[
            pl.BlockSpec((1, tm, D_MODEL), lambda b, i: (b, i, 0)),
            pl.BlockSpec((1, 6, D_MODEL), lambda b, i: (b, 0, 0)),
            pl.BlockSpec((1, D_MODEL), lambda b, i: (0, 0)),
            pl.BlockSpec((D_MODEL, n), lambda b, i: (0, 0)),
        ],
        out_specs=[pl.BlockSpec((1, tm, hi - lo), lambda b, i: (b, i, 0)) for lo, hi in splits],
        out_shape=[jax.ShapeDtypeStruct((ngrp, ROW_GROUP, hi - lo), F32) for lo, hi in splits],
        compiler_params=_cparams("arbitrary", "arbitrary"),
        name="norm_proj",
    )(x, mods, g.reshape(1, D_MODEL), w_bf16)


def _hgrn_constants():
    c = CHUNK
    nblk = 1 + len(LEVELS)
    w = np.zeros((2, nblk * c, c), np.float32)
    for t in range(c):
        w[0, t, : t + 1] = 1.0
    for li, r in enumerate(LEVELS):
        for t in range(c):
            mid = (t // r) * r + r // 2
            if t >= mid:
                w[0, (li + 1) * c + t, mid : t + 1] = 1.0
            else:
                w[0, (li + 1) * c + t, t + 1 : mid] = 1.0
    for blk in range(nblk):
        w[1, blk * c : (blk + 1) * c] = w[0, blk * c : (blk + 1) * c][::-1, ::-1]
    m = np.zeros((2, nblk, c, c), np.float32)
    for li, r in enumerate(LEVELS):
        for t in range(c):
            for s in range(c):
                if t // r == s // r and (t % r) >= r // 2 and (s % r) < r // 2:
                    m[0, li, t, s] = 1.0
    for t in range(c):
        for s in range(c):
            if t // SUB == s // SUB and s <= t:
                m[0, nblk - 1, t, s] = 1.0
    m[1] = m[0][:, ::-1, ::-1]
    sel = np.zeros((SUB * DK_A, c), np.float32)
    for i in range(SUB):
        sel[i * DK_A : (i + 1) * DK_A, i::SUB] = 1.0
    return w, m, sel


def _hgrn_chunk(direction, sq, r0, qa_ref, gate_ref, ia_ref, lb, wcum, masks, sel, states):
    rows = pl.ds(r0, CHUNK)
    q = _silu(qa_ref[sq, rows, :])
    v = ia_ref[sq, rows, :]
    f = lb + (1.0 - lb) * jax.nn.sigmoid(gate_ref[sq, rows, :])
    kk = jnp.maximum(1.0 - f, 0.0)
    lf = jnp.log(f) * LOG2E
    lk = jnp.log(kk) * LOG2E
    lf_hi, lf_lo = _split_bf16(lf)
    z = _dot(wcum, lf_hi)
    e = z[0:CHUNK] + _dot(wcum[0:CHUNK], lf_lo)
    edge = e[CHUNK - 1 : CHUNK] if direction == 0 else e[0:1]
    q_in = (q * jnp.exp2(e)).astype(BF16)
    k_st = (kk * jnp.exp2(edge - e)).astype(BF16)
    st_decay = jnp.exp2(edge)
    q_lv, k_lv = [], []
    for li in range(len(LEVELS)):
        ez = jnp.exp2(z[(li + 1) * CHUNK : (li + 2) * CHUNK])
        q_lv.append((q * ez).astype(BF16))
        k_lv.append((kk * ez).astype(BF16))
    nsub = CHUNK // SUB
    c3 = (e - lk).reshape(nsub, SUB, W_A)
    pair = []
    for i in range(SUB):
        cb = jnp.broadcast_to(c3[:, i : i + 1, :], (nsub, SUB, W_A)).reshape(CHUNK, W_A)
        pair.append((q * jnp.exp2(jnp.minimum(e - cb, 0.0))).astype(BF16))
    vb = v.astype(BF16)
    outs, new_states = [], []
    for h in range(H_A):
        hs = slice(h * DK_A, (h + 1) * DK_A)
        sc = _dot(jnp.concatenate([p[:, hs] for p in pair], axis=1), sel) * masks[len(LEVELS)]
        for li in range(len(LEVELS)):
            sc = sc + _dot_nt(q_lv[li][:, hs], k_lv[li][:, hs]) * masks[li]
        st = states[h]
        outs.append(_dot(sc.astype(BF16), vb[:, hs]) + _dot_nt(q_in[:, hs], st.astype(BF16)))
        new_states.append(st_decay[:, hs] * st + _dot_tn(vb[:, hs], k_st[:, hs]))
    return jnp.concatenate(outs, axis=1), tuple(new_states)


def _hgrn_kernel(qa_ref, ff_ref, fb_ref, ia_ref, ga_ref, lb_ref, ng_ref, w_ref, m_ref, sel_ref,
                 *rest, seq_len, has_state):
    s0_ref = rest[0] if has_state else None
    o_ref, sout_ref, ob_ref = rest[-3:]
    nchunk = seq_len // CHUNK
    nseq = o_ref.shape[0]
    lb = lb_ref[...]
    sel = sel_ref[...]
    nmask = 1 + len(LEVELS)

    def scan(i, states):
        rf = pl.multiple_of(i * CHUNK, CHUNK)
        rb = pl.multiple_of((nchunk - 1 - i) * CHUNK, CHUNK)
        new_states = []
        for sq in range(nseq):
            o_f, st_f = _hgrn_chunk(0, sq, rf, qa_ref, ff_ref, ia_ref, lb, w_ref[0],
                                    [m_ref[0, j] for j in range(nmask)], sel, states[sq][0])
            o_b, st_b = _hgrn_chunk(1, sq, rb, qa_ref, fb_ref, ia_ref, lb, w_ref[1],
                                    [m_ref[1, j] for j in range(nmask)], sel, states[sq][1])
            o_ref[sq, pl.ds(rf, CHUNK), :] = o_f
            ob_ref[sq, pl.ds(rb, CHUNK), :] = o_b
            new_states.append((st_f, st_b))
        return tuple(new_states)

    start = lambda sq, d, h: s0_ref[sq, d, h].T if has_state else jnp.zeros((DK_A, DK_A), F32)
    init = tuple(tuple(tuple(start(sq, d, h) for h in range(H_A)) for d in range(2))
                 for sq in range(nseq))
    final = lax.fori_loop(0, nchunk, scan, init)

    ng = jnp.concatenate([ng_ref[...]] * H_A, axis=1)

    def gate(i, carry):
        rows = pl.ds(pl.multiple_of(i * CHUNK, CHUNK), CHUNK)
        for sq in range(nseq):
            tot = o_ref[sq, rows, :] + ob_ref[sq, rows, :]
            normed = jnp.concatenate(
                [_rms(tot[:, h * DK_A : (h + 1) * DK_A]) for h in range(H_A)], axis=1)
            o_ref[sq, rows, :] = normed * ng * _silu(ga_ref[sq, rows, :])
        return carry

    lax.fori_loop(0, nchunk, gate, 0)
    for sq in range(nseq):
        for d in range(2):
            for h in range(H_A):
                sout_ref[sq, d, h] = final[sq][d][h].T


def _hgrn(qa, ff, fb, ia, ga, lb, ng, s0, seq_len, nseq):
    bsz = qa.shape[0]
    w, m, sel = _hgrn_constants()
    seq = pl.BlockSpec((nseq, seq_len, W_A), lambda b: (b, 0, 0))
    full = lambda shape: pl.BlockSpec(shape, lambda b: (0,) * len(shape))
    st_spec = pl.BlockSpec((nseq, 2, H_A, DK_A, DK_A), lambda b: (b, 0, 0, 0, 0))
    return pl.pallas_call(
        functools.partial(_hgrn_kernel, seq_len=seq_len, has_state=s0 is not None),
        grid=(bsz // nseq,),
        in_specs=[seq, seq, seq, seq, seq, full((1, W_A)), full((1, DK_A)),
                  full(w.shape), full(m.shape), full(sel.shape)] + ([st_spec] if s0 is not None else []),
        out_specs=[seq, st_spec],
        out_shape=[jax.ShapeDtypeStruct((bsz, seq_len, W_A), F32),
                   jax.ShapeDtypeStruct((bsz, 2, H_A, DK_A, DK_A), F32)],
        scratch_shapes=[pltpu.VMEM((nseq, seq_len, W_A), F32)],
        compiler_params=_cparams("arbitrary"),
        name="hgrn",
    )(qa, ff, fb, ia, ga, lb.reshape(1, W_A), ng.reshape(1, DK_A),
      jnp.asarray(w, BF16), jnp.asarray(m, F32), jnp.asarray(sel, BF16),
      *([s0] if s0 is not None else []))


def _dft_constants(seq_len):
    n = 2 * seq_len
    t = np.arange(seq_len, dtype=np.int64)
    wt = (np.arange(seq_len, dtype=np.int64)[:, None] * t[None, :]) % n
    ang = 2.0 * np.pi * wt.astype(np.float64) / n
    cos, sin = np.cos(ang), np.sin(ang)
    nyq = np.where(t % 2 == 0, 1.0, -1.0)
    sin_p = sin.copy()
    sin_p[0] = nyq
    fwd = np.concatenate([cos, sin_p], axis=0)
    icos = 2.0 * cos.T / n
    icos[:, 0] = 1.0 / n
    isin = 2.0 * sin.T / n
    isin[:, 0] = nyq / n
    inv = np.concatenate([icos, isin], axis=1)
    return fwd, inv


def _filter_embedding(seq_len):
    t = np.linspace(0.0, 1.0, seq_len)[:, None]
    w = 2.0 * np.pi * np.arange(seq_len) / seq_len
    f = np.linspace(1e-4, FILTER_BANDS - 1, FILTER_BANDS)
    ang = w[:, None] * f[None, :]
    z = np.concatenate([t, np.cos(ang), -np.sin(ang)], axis=-1)
    zp = np.zeros((seq_len, LANES), np.float64)
    zp[:, : z.shape[1]] = z
    deltas = np.abs(np.linspace(DECAY_MIN, DECAY_MAX, C_B))
    window = np.exp(-t * deltas[None, :])
    return zp.astype(np.float32), window.astype(np.float32)


def _hyena_filter_kernel(z_ref, win_ref, w1_ref, b1_ref, fr1_ref, w2_ref, b2_ref, fr2_ref, w3_ref,
                         hb_ref, fh_ref, fl_ref, gr_ref, gi_ref, *, seq_len):
    h = jnp.sin(fr1_ref[...] * (_dot3(z_ref[...], w1_ref[...]) + b1_ref[...]))
    h = jnp.sin(fr2_ref[...] * (_dot3(h, w2_ref[...]) + b2_ref[...]))
    h = _dot3(h, w3_ref[...])
    win = win_ref[...]
    hf = h[:, :C_B] * win
    hbk = h[:, C_B:] * win

    def dft(x):
        xh, xl = _split_bf16(x)
        return _dot(fh_ref[...], xh) + _dot(fh_ref[...], xl) + _dot(fl_ref[...], xh)

    p_sum = dft(hf + hbk)
    p_dif = dft(hf - hbk)
    row0 = lax.broadcasted_iota(jnp.int32, (seq_len, C_B), 0) == 0
    gr_ref[...] = p_sum[:seq_len] + hb_ref[...]
    gi_ref[...] = jnp.where(row0, p_sum[seq_len:] + hb_ref[...], p_dif[seq_len:])


def _hyena_filter(seq_len, w1, b1, fr1, w2, b2, fr2, w3, hbias):
    zemb, window = _filter_embedding(seq_len)
    fwd, _ = _dft_constants(seq_len)
    f_hi, f_lo = _split_bf16(jnp.asarray(fwd, F32))

    def pad(a, rows, cols):
        return jnp.zeros((rows, cols), F32).at[: a.shape[0], : a.shape[1]].set(a)

    args = (jnp.asarray(zemb), jnp.asarray(window), pad(w1, LANES, LANES), pad(b1[None], 1, LANES),
            pad(fr1[None], 1, LANES), pad(w2, LANES, LANES), pad(b2[None], 1, LANES),
            pad(fr2[None], 1, LANES), pad(w3, LANES, 2 * C_B), hbias.reshape(1, C_B), f_hi, f_lo)
    return pl.pallas_call(
        functools.partial(_hyena_filter_kernel, seq_len=seq_len),
        out_shape=[jax.ShapeDtypeStruct((seq_len, C_B), F32)] * 2,
        compiler_params=pltpu.CompilerParams(vmem_limit_bytes=VMEM_LIMIT_BYTES),
        name="hyena_filter",
    )(*args)


def _hyena_kernel(x0_ref, x1_ref, v_ref, cw_ref, cb_ref, gr_ref, gi_ref, f_ref, fi_ref, o_ref, *,
                  seq_len):
    tc = o_ref.shape[2]
    row = lax.broadcasted_iota(jnp.int32, (seq_len, tc), 0)

    def short_conv(u_ref, j, sq):
        u = u_ref[sq]
        prev = jnp.where(row == 0, 0.0, pltpu.roll(u, 1, 0))
        nxt = jnp.where(row == seq_len - 1, 0.0, pltpu.roll(u, seq_len - 1, 0))
        cw = cw_ref[j]
        return cw[0:1] * prev + cw[1:2] * u + cw[2:3] * nxt + cb_ref[j]

    gr, gi = gr_ref[...], gi_ref[...]
    row0 = row == 0
    for sq in range(o_ref.shape[0]):
        x0 = short_conv(x0_ref, 0, sq)
        z = short_conv(v_ref, 2, sq) * short_conv(x1_ref, 1, sq)
        p = _dot(f_ref[...], z.astype(BF16))
        a, b = p[:seq_len], p[seq_len:]
        bgi = b * gi
        yr = a * gr - jnp.where(row0, 0.0, bgi)
        yq = jnp.where(row0, bgi, a * gi + b * gr)
        y = _dot(fi_ref[...], jnp.concatenate([yr, yq], axis=0).astype(BF16))
        o_ref[sq] = y * x0


def _hyena(hy, conv_w, conv_b, gr, gi, seq_len, nseq, tc=256):
    bsz = hy.shape[0]
    nct = C_B // tc
    fwd, inv = _dft_constants(seq_len)
    cw = conv_w.reshape(3, 3, C_B).transpose(1, 0, 2)
    cb = conv_b.reshape(3, 1, C_B)
    part = lambda k: pl.BlockSpec((nseq, seq_len, tc), lambda b, j, k=k: (b, 0, k * nct + j))
    return pl.pallas_call(
        functools.partial(_hyena_kernel, seq_len=seq_len),
        grid=(bsz // nseq, nct),
        in_specs=[part(0), part(1), part(2),
                  pl.BlockSpec((3, 3, tc), lambda b, j: (0, 0, j)),
                  pl.BlockSpec((3, 1, tc), lambda b, j: (0, 0, j)),
                  pl.BlockSpec((seq_len, tc), lambda b, j: (0, j)),
                  pl.BlockSpec((seq_len, tc), lambda b, j: (0, j)),
                  pl.BlockSpec((2 * seq_len, seq_len), lambda b, j: (0, 0)),
                  pl.BlockSpec((seq_len, 2 * seq_len), lambda b, j: (0, 0))],
        out_specs=pl.BlockSpec((nseq, seq_len, tc), lambda b, j: (b, 0, j)),
        out_shape=jax.ShapeDtypeStruct((bsz, seq_len, C_B), F32),
        compiler_params=_cparams("arbitrary", "arbitrary"),
        name="hyena",
    )(hy, hy, hy, cw, cb, gr, gi, jnp.asarray(fwd, F32).astype(BF16),
      jnp.asarray(inv, F32).astype(BF16))


def _rope_tables(seq_len, dim):
    rows = seq_len // GRID_W
    row_idx = np.repeat(np.arange(rows), GRID_W).astype(np.float64)
    col_idx = np.tile(np.arange(GRID_W), rows).astype(np.float64)
    half = dim // 2
    inv = ROPE_THETA ** (-np.arange(0, half, 2, dtype=np.float64) / half)
    ang = np.concatenate([row_idx[:, None] * inv, col_idx[:, None] * inv], axis=-1)
    cos = np.repeat(np.cos(ang), 2, axis=1)
    sin = np.repeat(np.sin(ang), 2, axis=1)
    sin[:, 0::2] *= -1.0
    reps = LANES // dim
    return (np.tile(cos, (1, reps)).astype(np.float32), np.tile(sin, (1, reps)).astype(np.float32))


def _lane_group_matrices():
    i = np.arange(2 * LANES)
    same = lambda width: (i[:, None] // width == i[None, :] // width).astype(np.float32)
    return same(HD_C), same(DK_D)


def _qk_proj_kernel(x_ref, m_ref, g_ref, w_ref, cqg_ref, ckg_ref, dqg_ref, dkg_ref, grp_c_ref,
                    grp_d_ref, *rest, use_rope, slabs):
    if use_rope:
        cc_ref, sc_ref, cd_ref, sd_ref = rest[:4]
        rest = rest[4:]
    qc_ref, qd_ref, kc_ref, vc_ref, kd_ref, vd_ref = rest
    m = m_ref[0]
    pair = 2 * LANES
    assert pair == KV_C * HD_C
    two = lambda r: jnp.concatenate([r, r], axis=1)
    rows_per_slab = x_ref.shape[1] // slabs

    for slab in range(slabs):
        rows = pl.ds(slab * rows_per_slab, rows_per_slab)
        seq_rows = qc_ref.shape[1]
        oseq = (slab * rows_per_slab) // seq_rows
        orows = pl.ds((slab * rows_per_slab) % seq_rows, rows_per_slab)
        hb = (_rms(x_ref[0, rows, :]) * g_ref[...] * (1.0 + m[1:2]) + m[0:1]).astype(BF16)
        cols = lambda start, n: _dot(hb, w_ref[:, start * LANES : (start + n) * LANES])

        def norm(x, grp_ref, width, g_ref, cos_ref, sin_ref):
            ms = _dot((x * x).astype(BF16), grp_ref[...]) * (1.0 / width)
            y = x * lax.rsqrt(ms + EPS) * two(g_ref[...])
            if not use_rope:
                return y
            even = (lax.broadcasted_iota(jnp.int32, y.shape, 1) % 2) == 0
            swapped = jnp.where(even, pltpu.roll(y, y.shape[1] - 1, 1), pltpu.roll(y, 1, 1))
            return y * two(cos_ref[rows, :]) + swapped * two(sin_ref[rows, :])

        norm_c = lambda x, g_ref: norm(x, grp_c_ref, HD_C, g_ref, cc_ref if use_rope else None,
                                       sc_ref if use_rope else None)
        norm_d = lambda x, g_ref: norm(x, grp_d_ref, DK_D, g_ref, cd_ref if use_rope else None,
                                       sd_ref if use_rope else None)
        for i in range(H_C // 2):
            qc_ref[oseq, orows, i * pair : (i + 1) * pair] = norm_c(cols(2 * i, 2), cqg_ref)
        kc = norm_c(cols(H_C, KV_C), ckg_ref)
        vc = cols(H_C + KV_C, KV_C)
        for j in range(KV_C):
            kc_ref[oseq, j, orows, :] = kc[:, j * LANES : (j + 1) * LANES]
            vc_ref[oseq, j, orows, :] = vc[:, j * LANES : (j + 1) * LANES]
        base = H_C + 2 * KV_C
        for i in range(H_D // 2):
            qd_ref[oseq, orows, i * pair : (i + 1) * pair] = norm_d(cols(base + 2 * i, 2), dqg_ref)
            kd = norm_d(cols(base + H_D + 2 * i, 2), dkg_ref)
            vd = cols(base + 2 * H_D + 2 * i, 2)
            for j in range(2):
                kd_ref[oseq, 2 * i + j, orows, :] = kd[:, j * LANES : (j + 1) * LANES]
                vd_ref[oseq, 2 * i + j, orows, :] = vd[:, j * LANES : (j + 1) * LANES]


def _qk_proj(x, mods, g, w_bf16, cqg, ckg, dqg, dkg, seq_len, use_rope, tm, slabs):
    ngrp = x.shape[0]
    spg = ROW_GROUP // seq_len
    bsz = ngrp * spg
    seq_rows = min(tm, seq_len)
    spt = tm // seq_rows
    assert seq_len % seq_rows == 0 and (tm // slabs) <= seq_rows
    seq_blk = lambda gi, i: (gi * spg + (i * tm) // seq_len) // spt
    blk_of = lambda i: ((i * tm) % seq_len) // seq_rows
    vec = lambda: pl.BlockSpec((1, LANES), lambda gi, i: (0, 0))
    mat = lambda: pl.BlockSpec((2 * LANES, 2 * LANES), lambda gi, i: (0, 0))
    tab = lambda: pl.BlockSpec((tm, LANES), lambda gi, i: (blk_of(i), 0))
    in_specs = [pl.BlockSpec((1, tm, D_MODEL), lambda gi, i: (gi, i, 0)),
                pl.BlockSpec((1, 6, D_MODEL), lambda gi, i: (gi, 0, 0)),
                pl.BlockSpec((1, D_MODEL), lambda gi, i: (0, 0)),
                pl.BlockSpec(w_bf16.shape, lambda gi, i: (0, 0)),
                vec(), vec(), vec(), vec(), mat(), mat()]
    args = [x, mods, g.reshape(1, D_MODEL), w_bf16, cqg.reshape(1, HD_C), ckg.reshape(1, HD_C),
            jnp.tile(dqg.reshape(1, DK_D), (1, 2)), jnp.tile(dkg.reshape(1, DK_D), (1, 2))]
    args += [jnp.asarray(mm, BF16) for mm in _lane_group_matrices()]
    if use_rope:
        in_specs += [tab(), tab(), tab(), tab()]
        args += [jnp.asarray(t) for t in _rope_tables(seq_len, HD_C) + _rope_tables(seq_len, DK_D)]
    tok = lambda w: pl.BlockSpec((spt, seq_rows, w), lambda gi, i: (seq_blk(gi, i), blk_of(i), 0))
    head = lambda nh: pl.BlockSpec((spt, nh, seq_rows, LANES),
                                   lambda gi, i: (seq_blk(gi, i), 0, blk_of(i), 0))
    tok_shape = lambda w: jax.ShapeDtypeStruct((bsz, seq_len, w), F32)
    head_shape = lambda nh: jax.ShapeDtypeStruct((bsz, nh, seq_len, LANES), F32)
    return pl.pallas_call(
        functools.partial(_qk_proj_kernel, use_rope=use_rope, slabs=slabs),
        grid=(ngrp, ROW_GROUP // tm),
        in_specs=in_specs,
        out_specs=[tok(H_C * HD_C), tok(H_D * 2 * DK_D), head(KV_C), head(KV_C), head(H_D), head(H_D)],
        out_shape=[tok_shape(H_C * HD_C), tok_shape(H_D * 2 * DK_D), head_shape(KV_C),
                   head_shape(KV_C), head_shape(H_D), head_shape(H_D)],
        compiler_params=_cparams("arbitrary", "arbitrary"),
        name="qk_proj",
    )(*args)


def _softmax_pv(q_list, kv_list):
    outs = []
    for q in q_list:
        scores = [_dot_nt(q, k) for k, _ in kv_list]
        mx = scores[0].max(axis=1, keepdims=True)
        for s in scores[1:]:
            mx = jnp.maximum(mx, s.max(axis=1, keepdims=True))
        den = 0.0
        acc = 0.0
        for s, (_, v) in zip(scores, kv_list):
            pexp = jnp.exp2(s - mx)
            den = den + pexp.sum(axis=1, keepdims=True)
            acc = acc + _dot(pexp.astype(BF16), v)
        outs.append(acc / den)
    return outs


def _head_kv(sq, j, k_ref, v_ref, cache_refs):
    kv = [(r_k[sq, 0, j].astype(BF16), r_v[sq, 0, j].astype(BF16)) for r_k, r_v in cache_refs]
    kv.append((k_ref[sq, j].astype(BF16), v_ref[sq, j].astype(BF16)))
    return kv


def _per_slab(sq, q_ref, o_ref, cols, fn):
    for r0 in range(0, q_ref.shape[1], ATT_SLAB):
        rows = pl.ds(r0, min(ATT_SLAB, q_ref.shape[1]))
        o_ref[sq, rows, cols] = fn(q_ref[sq, rows, cols])


def _gqa_kernel(q_ref, k_ref, v_ref, *rest, has_cache):
    cache_refs = [rest[:2]] if has_cache else []
    o_ref = rest[-1]
    g_c = H_C // KV_C
    width = g_c * HD_C
    for sq, j in itertools.product(range(k_ref.shape[0]), range(k_ref.shape[1])):
        kv = _head_kv(sq, j, k_ref, v_ref, cache_refs)

        def head(q, kv=kv):
            q = q * (HD_C ** -0.5 * LOG2E)
            qs = [q[:, g * HD_C : (g + 1) * HD_C].astype(BF16) for g in range(g_c)]
            return jnp.concatenate(_softmax_pv(qs, kv), axis=1)

        _per_slab(sq, q_ref, o_ref, slice(j * width, (j + 1) * width), head)


def _diff_kernel(q_ref, k_ref, v_ref, lam_ref, sg_ref, *rest, has_cache, out_scale):
    cache_refs = [rest[:2]] if has_cache else []
    o_ref = rest[-1]
    width = 2 * DK_D
    for sq, j in itertools.product(range(k_ref.shape[0]), range(k_ref.shape[1])):
        kv = _head_kv(sq, j, k_ref, v_ref, cache_refs)

        def head(q, kv=kv):
            q = q * (DK_D ** -0.5 * LOG2E)
            low = lax.broadcasted_iota(jnp.int32, q.shape, 1) < DK_D
            qs = [jnp.where(low, q, 0.0).astype(BF16), jnp.where(low, 0.0, q).astype(BF16)]
            o1, o2 = _softmax_pv(qs, kv)
            return _rms(o1 - lam_ref[...] * o2) * sg_ref[...] * out_scale

        _per_slab(sq, q_ref, o_ref, slice(j * width, (j + 1) * width), head)


def _attention(q, k, v, cache_k, cache_v, kernel, extra_args, n_heads, q_width, seq_len, hps, tq,
               nseq):
    bsz = q.shape[0]
    in_specs = [pl.BlockSpec((nseq, tq, hps * q_width), lambda b, h, i: (b, i, h)),
                pl.BlockSpec((nseq, hps, seq_len, LANES), lambda b, h, i: (b, h, 0, 0)),
                pl.BlockSpec((nseq, hps, seq_len, LANES), lambda b, h, i: (b, h, 0, 0))]
    in_specs += [pl.BlockSpec((1, LANES), lambda b, h, i: (0, 0)) for _ in extra_args]
    args = [q, k, v, *extra_args]
    if cache_k is not None:
        past = cache_k.shape[3]
        spec = lambda: pl.BlockSpec((nseq, 1, hps, past, LANES), lambda b, h, i: (b, 0, h, 0, 0))
        in_specs += [spec(), spec()]
        args += [cache_k, cache_v]
    return pl.pallas_call(
        functools.partial(kernel, has_cache=cache_k is not None),
        grid=(bsz // nseq, n_heads // hps, seq_len // tq),
        in_specs=in_specs,
        out_specs=pl.BlockSpec((nseq, tq, hps * q_width), lambda b, h, i: (b, i, h)),
        out_shape=jax.ShapeDtypeStruct((bsz, seq_len, n_heads * q_width), F32),
        compiler_params=_cparams("arbitrary", "arbitrary", "arbitrary"),
        name="attention",
    )(*args)


def _route(lg):
    lane = lax.broadcasted_iota(jnp.int32, lg.shape, 1).astype(F32)
    neg = -1e30
    is_g = (lane >= N_EXPERTS) & (lane < N_EXPERTS + N_GROUPS)
    gl = jnp.where(is_g, lg, neg)
    gmax = gl.max(axis=1, keepdims=True)
    g_p = 1.0 / jnp.where(is_g, jnp.exp(gl - gmax), 0.0).sum(axis=1, keepdims=True)
    g_i = jnp.where(gl == gmax, lane - N_EXPERTS, 1e9).min(axis=1, keepdims=True)
    in_grp = (lane < N_EXPERTS) & (jnp.floor(lane * (1.0 / EXP_PER_GROUP)) == g_i)
    el = jnp.where(in_grp, lg, neg)
    m1 = el.max(axis=1, keepdims=True)
    i1 = jnp.where(in_grp & (el == m1), lane, 1e9).min(axis=1, keepdims=True)
    el2 = jnp.where(lane == i1, neg, el)
    m2 = el2.max(axis=1, keepdims=True)
    i2 = jnp.where(in_grp & (el2 == m2) & (lane != i1), lane, 1e9).min(axis=1, keepdims=True)
    r = jnp.exp(m2 - m1)
    w1 = g_p / (1.0 + r)
    rec = jnp.where(lane == ROUTE_LANE, i1, 0.0)
    rec = jnp.where(lane == ROUTE_LANE + 1, i2, rec)
    rec = jnp.where(lane == ROUTE_LANE + 2, w1, rec)
    return jnp.where(lane == ROUTE_LANE + 3, w1 * r, rec)


def _mix_out_kernel(x_ref, a_ref, b_ref, m_ref, g2_ref, w_ref, wr_ref, br_ref, tri_ref,
                    x1_ref, xs_ref, pos_ref, cnt_ref, *, parts):
    wa = a_ref.shape[2]
    m = m_ref[0]
    rows_per_part = x_ref.shape[1] // parts
    h2_parts, route_parts = [], []
    for part in range(parts):
        rows = pl.ds(part * rows_per_part, rows_per_part)
        o = (_dot(a_ref[0, rows, :].astype(BF16), w_ref[:wa])
             + _dot(b_ref[0, rows, :].astype(BF16), w_ref[wa:]))
        x1 = x_ref[0, rows, :] + m[2:3] * o
        x1_ref[0, rows, :] = x1
        h2 = _rms(x1) * g2_ref[...] * (1.0 + m[4:5]) + m[3:4]
        h2_parts.append(h2.astype(BF16))
        r = _dot(jnp.concatenate(_split_bf16(h2), axis=0), wr_ref[...])
        hi_rows, lo_rows = r[:rows_per_part], r[rows_per_part:]
        lg = hi_rows[:, :LANES] + hi_rows[:, LANES:] + lo_rows[:, :LANES] + br_ref[...]
        route_parts.append(_route(lg))
    xs, pos, cnt = _moe_sort_tile(jnp.concatenate(h2_parts, axis=0),
                                  jnp.concatenate(route_parts, axis=0), tri_ref[...])
    xs_ref[...] = xs
    pos_ref[0] = pos
    cnt_ref[0] = cnt


def _mix_out(x, a, b, mods, g2, w_out_bf16, w_router, b_router, parts=2):
    ngrp = x.shape[0]
    per = ROW_GROUP // MOE_TILE
    ntile = ngrp * per
    wa, wb = a.shape[2], b.shape[2]
    wr = jnp.concatenate(_split_bf16(w_router), axis=1)
    tri = np.tril(np.ones((MOE_TILE, MOE_TILE), np.float32), -1)
    src = lambda j: jnp.minimum(j, ntile - 1)
    row = lambda w: pl.BlockSpec((1, MOE_TILE, w), lambda j: (src(j) // per, src(j) % per, 0))
    full = lambda shape: pl.BlockSpec(shape, lambda j: (0,) * len(shape))
    return pl.pallas_call(
        functools.partial(_mix_out_kernel, parts=parts),
        grid=(ntile + 1,),
        in_specs=[row(D_MODEL), row(wa), row(wb),
                  pl.BlockSpec((1, 6, D_MODEL), lambda j: (src(j) // per, 0, 0)),
                  full((1, D_MODEL)), full((wa + wb, D_MODEL)),
                  full((D_MODEL, 2 * LANES)), full((1, LANES)), full((MOE_TILE, MOE_TILE))],
        out_specs=[pl.BlockSpec((1, MOE_TILE, D_MODEL), lambda j: (j, 0, 0)),
                   pl.BlockSpec((MOE_TILE_ROWS, MOE_ROW_W), lambda j: (j, 0)),
                   pl.BlockSpec((1, MOE_TILE, LANES), lambda j: (j, 0, 0)),
                   pl.BlockSpec((1, 8, LANES), lambda j: (j, 0, 0))],
        out_shape=[jax.ShapeDtypeStruct((ntile + 1, MOE_TILE, D_MODEL), F32),
                   jax.ShapeDtypeStruct(((ntile + 1) * MOE_TILE_ROWS, MOE_ROW_W), BF16),
                   jax.ShapeDtypeStruct((ntile + 1, MOE_TILE, LANES), F32),
                   jax.ShapeDtypeStruct((ntile + 1, 8, LANES), F32)],
        compiler_params=_cparams("arbitrary"),
        name="mix_out",
    )(x, a, b, mods, g2.reshape(1, D_MODEL), w_out_bf16, wr, b_router, jnp.asarray(tri, BF16))


def _lane_col(x, lane, k):
    return jnp.where(lane == k, x, 0.0).sum(axis=1, keepdims=True)


def _moe_sort_tile(h, r, tri):
    lane = lax.broadcasted_iota(jnp.int32, r.shape, 1).astype(F32)
    i1, i2, w1, w2 = [_lane_col(r, lane, ROUTE_LANE + k) for k in range(4)]
    oh1 = lane == i1
    oh2 = lane == i2
    oh = jnp.where(oh1 | oh2, 1.0, 0.0)
    rank = _dot(tri, oh.astype(BF16))
    cnt = oh.sum(axis=0, keepdims=True)
    chunks = jnp.floor((cnt + (MOE_CHUNK - 1)) * (1.0 / MOE_CHUNK))
    li = lax.broadcasted_iota(jnp.int32, (LANES, LANES), 0)
    lj = lax.broadcasted_iota(jnp.int32, (LANES, LANES), 1)
    before = jnp.where(li < lj, 1.0, 0.0).astype(BF16)
    seg = _dot(jnp.broadcast_to(chunks, (8, LANES)).astype(BF16), before)[0:1]
    base = seg * MOE_CHUNK + rank
    pos1 = jnp.where(oh1, base, 0.0).sum(axis=1, keepdims=True)
    pos2 = jnp.where(oh2, base, 0.0).sum(axis=1, keepdims=True)
    riota = lax.broadcasted_iota(jnp.int32, (r.shape[0], MOE_TILE_ROWS), 1).astype(F32)
    p = jnp.where((riota == pos1) | (riota == pos2), 1.0, 0.0).astype(BF16)

    aux = jnp.where(lane == 2 * MOE_W_PIECES, i1, 0.0)
    for k, w in enumerate((w1, w2)):
        rest = w
        for piece in range(MOE_W_PIECES):
            part = rest.astype(BF16).astype(F32)
            aux = jnp.where(lane == k * MOE_W_PIECES + piece, part, aux)
            rest = rest - part
    row = jnp.concatenate([h, aux.astype(BF16)], axis=1)
    xs = _dot_tn(p, row).astype(BF16)
    pos = jnp.where(lane == 0, pos1, jnp.where(lane == 1, pos2, 0.0))
    return xs, pos, jnp.broadcast_to(cnt, (8, LANES))


def _moe_tables(cnt, ntile):
    nblk = ntile * MOE_TILE_CHUNKS // MOE_BLOCK_CHUNKS + N_EXPERTS
    chunks = (cnt + MOE_CHUNK - 1) // MOE_CHUNK
    seg_start = jnp.cumsum(chunks, axis=1) - chunks
    tile_prefix = jnp.cumsum(chunks, axis=0) - chunks
    per_expert = chunks.sum(axis=0)
    blocks = (per_expert + MOE_BLOCK_CHUNKS - 1) // MOE_BLOCK_CHUNKS
    blk_end = jnp.cumsum(blocks)
    n_used = blk_end[-1]
    b = jnp.arange(nblk + 1, dtype=jnp.int32)
    blk_e = jnp.sum(b[:, None] >= blk_end[None, :], axis=1).astype(jnp.int32)
    last_e = jnp.sum((n_used - 1) >= blk_end).astype(jnp.int32)
    blk_e = jnp.clip(jnp.where(b < n_used, blk_e, last_e), 0, N_EXPERTS - 1)
    oh_e = (blk_e[:, None] == jnp.arange(N_EXPERTS)[None, :]).astype(jnp.int32)
    pick = lambda per_tile: jnp.sum(oh_e[:, :, None] * per_tile.T[None], axis=1)
    seg_e, pre_e, chunks_e = pick(seg_start), pick(tile_prefix), pick(chunks)
    first_blk = jnp.sum(oh_e * (blk_end - blocks)[None, :], axis=1)
    k = (b - first_blk)[:, None] * MOE_BLOCK_CHUNKS + jnp.arange(MOE_BLOCK_CHUNKS)[None, :]
    k3 = k[:, :, None]
    in_tile = (pre_e[:, None, :] <= k3) & (k3 < (pre_e + chunks_e)[:, None, :])
    tile_base = (jnp.arange(ntile) * MOE_TILE_CHUNKS)[None, :] + seg_e - pre_e
    src = jnp.sum(jnp.where(in_tile, tile_base[:, None, :] + k3, 0), axis=-1)
    valid = jnp.any(in_tile, axis=-1) & (b < n_used)[:, None]
    slot_c = jnp.arange(MOE_BLOCK_CHUNKS)[None, :]
    n_read = MOE_TILE_CHUNKS - 2 * MOE_BLOCK_CHUNKS
    assert n_read > 0
    spare = ntile * MOE_TILE_CHUNKS
    gather = jnp.where(valid, src, spare + 2 * MOE_BLOCK_CHUNKS + slot_c % n_read).astype(jnp.int32)
    scatter = jnp.where(valid, src, spare + (b % 2)[:, None] * MOE_BLOCK_CHUNKS + slot_c).astype(jnp.int32)
    blk_start = jnp.concatenate([blk_end - blocks, n_used[None]]).astype(jnp.int32)
    return blk_start, gather.reshape(-1), scatter.reshape(-1)


def _moe_expert_kernel(start_ref, gather_ref, scatter_ref, xs_hbm, wg_ref, wu_ref, wd_ref, ys_hbm,
                       lhs, obuf, wgb, wub, wdb, in_sem, out_sem):
    e = pl.program_id(0)
    ne = pl.num_programs(0)
    b0 = start_ref[e]
    b1 = start_ref[e + 1]
    n = start_ref[ne]

    def chunk_copy(blk, slot, c, gather):
        rows = pl.ds(c * MOE_CHUNK, MOE_CHUNK)
        if gather:
            idx = gather_ref[blk * MOE_BLOCK_CHUNKS + c]
            return pltpu.make_async_copy(xs_hbm.at[idx], lhs.at[slot, rows], in_sem.at[slot])
        idx = scatter_ref[blk * MOE_BLOCK_CHUNKS + c]
        dst = ys_hbm.at[idx, pl.ds(0, MOE_CHUNK), pl.ds(0, D_MODEL)]
        return pltpu.make_async_copy(obuf.at[slot, rows], dst, out_sem.at[slot])

    def for_chunks(blk, slot, gather, start):
        for c in range(MOE_BLOCK_CHUNKS):
            cp = chunk_copy(blk, slot, c, gather)
            if start:
                cp.start()
            else:
                cp.wait()

    @pl.when((e == 0) & (n > 0))
    def _():
        for_chunks(0, 0, True, True)

    @pl.when(b1 > b0)
    def _():
        wgb[...] = wg_ref[0, 0].astype(BF16)
        wub[...] = wu_ref[0, 0].astype(BF16)
        wdb[...] = wd_ref[0, 0].astype(BF16)
        e_f32 = e.astype(F32)

        def block(b, carry):
            slot = b % 2

            @pl.when(b >= 2)
            def _():
                for_chunks(b - 2, slot, False, False)

            for_chunks(b, slot, True, False)
            for_chunks(b + 1, 1 - slot, True, True)
            xa = lhs[slot]
            x = xa[:, :D_MODEL]
            aux = xa[:, D_MODEL:].astype(F32)
            lane = lax.broadcasted_iota(jnp.int32, aux.shape, 1)
            first = lane < MOE_W_PIECES
            w_first = jnp.where(first, aux, 0.0).sum(axis=1, keepdims=True)
            w_second = jnp.where(first | (lane >= 2 * MOE_W_PIECES), 0.0, aux).sum(axis=1, keepdims=True)
            e_first = _lane_col(aux, lane, 2 * MOE_W_PIECES)
            w = jnp.where(e_first == e_f32, w_first, w_second)
            hid = _silu(_dot(x, wgb[...])) * _dot(x, wub[...]) * w
            obuf[slot] = _dot(hid.astype(BF16), wdb[...]).astype(BF16)
            for_chunks(b, slot, False, True)
            return carry

        lax.fori_loop(b0, b1, block, 0)

    @pl.when(e == ne - 1)
    def _():
        @pl.when(n >= 2)
        def _():
            for_chunks(n - 2, n % 2, False, False)

        @pl.when(n >= 1)
        def _():
            for_chunks(n, n % 2, True, False)
            for_chunks(n - 1, (n - 1) % 2, False, False)


def _moe_experts(xs, blk_start, gather, scatter, layer, w_gate, w_up, w_down):
    nchunk = xs.shape[0] // MOE_CHUNK
    rows_per_blk = MOE_BLOCK_CHUNKS * MOE_CHUNK
    wspec = lambda shape: pl.BlockSpec((1, 1) + shape, lambda e, st, g, s: (layer, e, 0, 0))
    hbm = pl.BlockSpec(memory_space=pl.ANY)
    ys = pl.pallas_call(
        _moe_expert_kernel,
        grid_spec=pltpu.PrefetchScalarGridSpec(
            num_scalar_prefetch=3,
            grid=(N_EXPERTS,),
            in_specs=[hbm, wspec((D_MODEL, D_EXPERT)), wspec((D_MODEL, D_EXPERT)),
                      wspec((D_EXPERT, D_MODEL))],
            out_specs=hbm,
            scratch_shapes=[pltpu.VMEM((2, rows_per_blk, MOE_ROW_W), BF16),
                            pltpu.VMEM((2, rows_per_blk, D_MODEL), BF16),
                            pltpu.VMEM((D_MODEL, D_EXPERT), BF16),
                            pltpu.VMEM((D_MODEL, D_EXPERT), BF16),
                            pltpu.VMEM((D_EXPERT, D_MODEL), BF16),
                            pltpu.SemaphoreType.DMA((2,)),
                            pltpu.SemaphoreType.DMA((2,))]),
        out_shape=jax.ShapeDtypeStruct((nchunk, MOE_CHUNK, MOE_ROW_W), BF16),
        input_output_aliases={3: 0},
        compiler_params=_cparams("arbitrary"),
        name="moe_experts",
    )(blk_start, gather, scatter, xs.reshape(nchunk, MOE_CHUNK, MOE_ROW_W), w_gate, w_up, w_down)
    return ys.reshape(nchunk * MOE_CHUNK, MOE_ROW_W)


def _moe_combine_kernel(ys_ref, pos_ref, x_ref, m_ref, o_ref):
    pos = pos_ref[0]
    lane = lax.broadcasted_iota(jnp.int32, pos.shape, 1)
    pos1 = _lane_col(pos, lane, 0)
    pos2 = _lane_col(pos, lane, 1)
    riota = lax.broadcasted_iota(jnp.int32, (pos.shape[0], MOE_TILE_ROWS), 1).astype(F32)
    p = jnp.where((riota == pos1) | (riota == pos2), 1.0, 0.0).astype(BF16)
    o_ref[0] = x_ref[0] + m_ref[0][5:6] * _dot(p, ys_ref[...])


def _moe_combine(ys, pos, x1, mods):
    ngrp = mods.shape[0]
    per = ROW_GROUP // MOE_TILE
    tile = pl.BlockSpec((1, MOE_TILE, D_MODEL), lambda j: (j, 0, 0))
    return pl.pallas_call(
        _moe_combine_kernel,
        grid=(ngrp * per,),
        in_specs=[pl.BlockSpec((MOE_TILE_ROWS, D_MODEL), lambda j: (j, 0)),
                  pl.BlockSpec((1, MOE_TILE, LANES), lambda j: (j, 0, 0)),
                  tile, pl.BlockSpec((1, 6, D_MODEL), lambda j: (j // per, 0, 0))],
        out_specs=pl.BlockSpec((1, MOE_TILE, D_MODEL), lambda j: (j // per, j % per, 0)),
        out_shape=jax.ShapeDtypeStruct((ngrp, ROW_GROUP, D_MODEL), F32),
        compiler_params=_cparams("arbitrary"),
        name="moe_combine",
    )(ys, pos, x1, mods)


def _moe(xs, pos, cnt, x1, mods, layer, w_gate, w_up, w_down):
    ntile = mods.shape[0] * (ROW_GROUP // MOE_TILE)
    cnt = cnt[:ntile, 0, :N_EXPERTS].astype(jnp.int32)
    blk_start, gather, scatter = _moe_tables(cnt, ntile)
    ys = _moe_experts(xs, blk_start, gather, scatter, layer, w_gate, w_up, w_down)
    return _moe_combine(ys, pos, x1, mods)


def kernel(x_prompt, x_sample, state_hgrn, cache_c_k, cache_c_v, cache_d_k, cache_d_v, c, c_ctx, norm1_g, norm2_g, w_mod, b_mod, even_w_in, even_w_out, hgrn_lower, hgrn_norm_g, hy_conv_w, hy_conv_b, hy_w1, hy_b1, hy_freq1, hy_w2, hy_b2, hy_freq2, hy_w3, hy_bias, odd_w_in, odd_w_out, c_qnorm_g, c_knorm_g, d_qnorm_g, d_knorm_g, d_lambda_q1, d_lambda_k1, d_lambda_q2, d_lambda_k2, d_subln_g, moe_w_grp, moe_b_grp, moe_w_rt, moe_b_rt, moe_w_gate, moe_w_up, moe_w_down):
    depth = w_mod.shape[0]
    n_ctx, seq, _ = x_prompt.shape
    n_lat, dec_seq, _ = x_sample.shape
    g_ctx = n_ctx * seq // ROW_GROUP
    g_lat = n_lat * dec_seq // ROW_GROUP
    assert dec_seq == ROW_GROUP and ROW_GROUP % seq == 0

    cond = jnp.zeros((16, D_MODEL), F32).at[0].set(c_ctx).at[1 : 1 + n_lat].set(c)
    mods = _adaln(cond, w_mod, b_mod).reshape(depth, 16, 6, D_MODEL)
    lower = jnp.cumsum(jax.nn.softmax(hgrn_lower.astype(F32), axis=0), axis=0)

    streams = [
        dict(x=x_prompt.reshape(g_ctx, ROW_GROUP, D_MODEL), ngrp=g_ctx, bsz=n_ctx, seq=seq, ctx=True),
        dict(x=x_sample, ngrp=g_lat, bsz=n_lat, seq=dec_seq, ctx=False),
    ]
    new_state, new_ck, new_cv, new_dk, new_dv = [], [], [], [], []

    for l in range(depth):
        j = l // 2
        w_router = jnp.zeros((D_MODEL, LANES), F32)
        w_router = w_router.at[:, :N_EXPERTS].set(moe_w_rt[l])
        w_router = w_router.at[:, N_EXPERTS : N_EXPERTS + N_GROUPS].set(moe_w_grp[l])
        b_router = jnp.zeros((1, LANES), F32)
        b_router = b_router.at[0, :N_EXPERTS].set(moe_b_rt[l])
        b_router = b_router.at[0, N_EXPERTS : N_EXPERTS + N_GROUPS].set(moe_b_grp[l])
        if l % 2 == 0:
            w_in = even_w_in[j].astype(BF16)
            w_out = even_w_out[j].astype(BF16)
        else:
            w_in = odd_w_in[j].astype(BF16)
            w_out = odd_w_out[j].astype(BF16)
            lam_init = 0.8 - 0.6 * math.exp(-0.3 * l)
            lam = (jnp.exp(jnp.sum(d_lambda_q1[j] * d_lambda_k1[j]))
                   - jnp.exp(jnp.sum(d_lambda_q2[j] * d_lambda_k2[j])) + lam_init)
            lam_row = jnp.full((1, LANES), lam, F32)

        for s in streams:
            ngrp, bsz, sl = s["ngrp"], s["bsz"], s["seq"]
            if s["ctx"]:
                m = jnp.broadcast_to(mods[l, 0][None], (ngrp, 6, D_MODEL))
            else:
                m = mods[l, 1 : 1 + ngrp]
            x = s["x"]
            if l % 2 == 0:
                wa = H_A * DK_A
                splits = [(0, wa), (wa, 2 * wa), (2 * wa, 3 * wa), (3 * wa, 3 * wa + W_A),
                          (3 * wa + W_A, 3 * wa + 2 * W_A), (3 * wa + 2 * W_A, w_in.shape[1])]
                qa, ffa, fba, ia, ga, hy = _norm_proj(x, m, norm1_g[l], w_in, splits)
                per_seq = lambda t: t.reshape(bsz, sl, t.shape[-1])
                s0 = None if s["ctx"] else state_hgrn[:, j].astype(F32)
                mix_a, s_fin = _hgrn(per_seq(qa), per_seq(ffa), per_seq(fba), per_seq(ia),
                                     per_seq(ga), lower[j], hgrn_norm_g[j], s0, sl,
                                     nseq=2 if s["ctx"] else 1)
                gr, gi = _hyena_filter(sl, hy_w1[j], hy_b1[j], hy_freq1[j], hy_w2[j], hy_b2[j],
                                       hy_freq2[j], hy_w3[j], hy_bias[j])
                mix_b = _hyena(per_seq(hy), hy_conv_w[j], hy_conv_b[j], gr, gi, sl,
                               nseq=8 if s["ctx"] else 2)
                if s["ctx"]:
                    new_state.append(s_fin)
            else:
                qc, qd, kc, vc, kd, vd = _qk_proj(x, m, norm1_g[l], w_in, c_qnorm_g[j], c_knorm_g[j],
                                                  d_qnorm_g[j], d_knorm_g[j], sl, use_rope=not s["ctx"],
                                                  tm=512, slabs=2)
                if s["ctx"]:
                    caches = (None, None, None, None)
                    new_ck.append(kc)
                    new_cv.append(vc)
                    new_dk.append(kd)
                    new_dv.append(vd)
                else:
                    caches = (cache_c_k[:, j : j + 1], cache_c_v[:, j : j + 1],
                              cache_d_k[:, j : j + 1], cache_d_v[:, j : j + 1])
                hps_c, hps_d, nseq = (KV_C, H_D, 4) if s["ctx"] else (1, 1, 1)
                mix_a = _attention(qc, kc, vc, caches[0], caches[1], _gqa_kernel, (), KV_C,
                                   (H_C // KV_C) * HD_C, sl, hps_c, sl, nseq)
                diff = functools.partial(_diff_kernel, out_scale=1.0 - lam_init)
                mix_b = _attention(qd, kd, vd, caches[2], caches[3], diff,
                                   (lam_row, d_subln_g[j].reshape(1, DV_D)), H_D, 2 * DK_D, sl,
                                   hps_d, sl, nseq)
            grp = lambda t: t.reshape(ngrp, ROW_GROUP, t.shape[-1])
            x1, xs, pos, cnt = _mix_out(x, grp(mix_a), grp(mix_b), m, norm2_g[l], w_out, w_router,
                                        b_router)
            s["x"] = _moe(xs, pos, cnt, x1, m, l, moe_w_gate, moe_w_up, moe_w_down)

    y_ctx = streams[0]["x"].reshape(n_ctx, seq, D_MODEL)
    y_lat = streams[1]["x"]
    return (y_ctx, y_lat, jnp.stack(new_state, axis=1), jnp.stack(new_ck, axis=1),
            jnp.stack(new_cv, axis=1), jnp.stack(new_dk, axis=1), jnp.stack(new_dv, axis=1))
```
